```python
import math
import jax
import jax.numpy as jnp
from jax import lax
import numpy as np

D_MODEL = 2048
BATCH = 4
SEQ = 2048
DEPTH = 1
DEC_BATCH = 128
DEC_SEQ = 8
PAST_LEN = 16384
PAGE_SIZE = 128

S5_WIDTH = D_MODEL // 2
S5_GROUP = 16
S5_GROUPS = S5_WIDTH // S5_GROUP
S5_STATE = 64
GLA_HEADS = 4
GLA_DK = D_MODEL // 16
GLA_DV = D_MODEL // 8
GLA_RANK = 16
GLA_TAU = 16.0
GLA_CHUNK = 64
XA_HEADS = 4
XA_HEAD_DIM = D_MODEL // 8
XA_WIDTH = XA_HEADS * XA_HEAD_DIM
MEM_LEN = 256
N_BRANCH = 3
FFN_HIDDEN = ((8 * D_MODEL + 3 * 256 - 1) // (3 * 256)) * 256
IN_SPLITS = (S5_WIDTH, GLA_HEADS * GLA_DK, GLA_HEADS * GLA_DK, GLA_HEADS * GLA_DV,
             GLA_HEADS * GLA_DV, GLA_RANK, XA_WIDTH, N_BRANCH * D_MODEL)
IN_WIDTH = sum(IN_SPLITS)
RMS_EPS = 1e-6

kernel_name = 'hybrid_s5_gla_memattn_decoder_step'


def _split_points():
    return [int(v) for v in np.cumsum(IN_SPLITS)[:-1]]


def _rmsnorm(x, g):
    x32 = x.astype(jnp.float32)
    y = x32 * lax.rsqrt(jnp.mean(x32 * x32, axis=-1, keepdims=True) + RMS_EPS)
    return (y * g.astype(jnp.float32)).astype(x.dtype)


def _complex_affine_combine(earlier, later):
    a1r, a1i, b1r, b1i = earlier
    a2r, a2i, b2r, b2i = later
    return (a2r * a1r - a2i * a1i,
            a2r * a1i + a2i * a1r,
            a2r * b1r - a2i * b1i + b2r,
            a2r * b1i + a2i * b1r + b2i)


def _s5_branch(u, h0_re, h0_im, lam_re, lam_im, log_dt, b_re, b_im, c_re, c_im, d_skip, w_glu, b_glu):
    f32 = jnp.float32
    bt, length, _ = u.shape
    ug = u.astype(f32).reshape(bt, length, S5_GROUPS, S5_GROUP)
    lam_re = lam_re.astype(f32)
    lam_im = lam_im.astype(f32)
    dt = jnp.exp(log_dt.astype(f32))[:, None]
    mag = jnp.exp(lam_re * dt)
    a_re = mag * jnp.cos(lam_im * dt)
    a_im = mag * jnp.sin(lam_im * dt)
    den = lam_re * lam_re + lam_im * lam_im
    coef_re = ((a_re - 1.0) * lam_re + a_im * lam_im) / den
    coef_im = (a_im * lam_re - (a_re - 1.0) * lam_im) / den
    b_re = b_re.astype(f32)
    b_im = b_im.astype(f32)
    bbar_re = coef_re[..., None] * b_re - coef_im[..., None] * b_im
    bbar_im = coef_re[..., None] * b_im + coef_im[..., None] * b_re
    bu_re = jnp.einsum('blgc,gpc->blgp', ug, bbar_re)
    bu_im = jnp.einsum('blgc,gpc->blgp', ug, bbar_im)
    h0r = h0_re.astype(f32)
    h0i = h0_im.astype(f32)
    bu_re = bu_re.at[:, 0].add(a_re * h0r - a_im * h0i)
    bu_im = bu_im.at[:, 0].add(a_re * h0i + a_im * h0r)
    ar = jnp.broadcast_to(a_re, bu_re.shape)
    ai = jnp.broadcast_to(a_im, bu_re.shape)
    _, _, h_re, h_im = lax.associative_scan(_complex_affine_combine, (ar, ai, bu_re, bu_im), axis=1)
    y = (jnp.einsum('blgp,gcp->blgc', h_re, c_re.astype(f32))
         - jnp.einsum('blgp,gcp->blgc', h_im, c_im.astype(f32))
         + d_skip.astype(f32) * ug)
    y = jax.nn.gelu(y.reshape(bt, length, S5_WIDTH))
    y = y * jax.nn.sigmoid(y @ w_glu.astype(f32) + b_glu.astype(f32))
    return y.astype(u.dtype), h_re[:, -1].astype(h0_re.dtype), h_im[:, -1].astype(h0_im.dtype)


def _gla_chunked(q, k, v, log_a, s0):
    bt, length, heads, dk = q.shape
    dv = v.shape[-1]
    c = math.gcd(length, GLA_CHUNK)
    n = length // c

    def to_chunks(t):
        return t.reshape(bt, n, c, heads, t.shape[-1]).transpose(1, 0, 3, 2, 4)

    mask = jnp.tril(jnp.ones((c, c), dtype=bool))[:, :, None]

    def step(s, xs):
        qc, kc, vc, ac = xs
        b = jnp.cumsum(ac, axis=2)
        o_inter = jnp.einsum('bhid,bhde->bhie', qc * jnp.exp(b), s)
        diff = b[:, :, :, None, :] - b[:, :, None, :, :]
        decay = jnp.exp(jnp.where(mask, diff, -jnp.inf))
        att = jnp.einsum('bhid,bhjd,bhijd->bhij', qc, kc, decay)
        o = o_inter + jnp.einsum('bhij,bhje->bhie', att, vc)
        b_last = b[:, :, -1:, :]
        s_new = (jnp.exp(b_last[:, :, 0, :])[..., None] * s
                 + jnp.einsum('bhjd,bhje->bhde', kc * jnp.exp(b_last - b), vc))
        return s_new, o

    s_fin, o = lax.scan(step, s0, (to_chunks(q), to_chunks(k), to_chunks(v), to_chunks(log_a)))
    o = o.transpose(1, 0, 3, 2, 4).reshape(bt, length, heads, dv)
    return o, s_fin


def _gla_branch(q, k, v, r, a_low, s0, w_a2, b_a, g_norm):
    f32 = jnp.float32
    bt, length, _ = q.shape
    qh = q.astype(f32).reshape(bt, length, GLA_HEADS, GLA_DK) * (GLA_DK ** -0.5)
    kh = k.astype(f32).reshape(bt, length, GLA_HEADS, GLA_DK)
    vh = v.astype(f32).reshape(bt, length, GLA_HEADS, GLA_DV)
    log_a = jax.nn.log_sigmoid(a_low.astype(f32) @ w_a2.astype(f32) + b_a.astype(f32)) / GLA_TAU
    log_a = log_a.reshape(bt, length, GLA_HEADS, GLA_DK)
    o, s_fin = _gla_chunked(qh, kh, vh, log_a, s0.astype(f32))
    o = o * lax.rsqrt(jnp.mean(o * o, axis=-1, keepdims=True) + RMS_EPS)
    o = o.reshape(bt, length, GLA_HEADS * GLA_DV) * g_norm.astype(f32)
    o = o * jax.nn.silu(r.astype(f32))
    return o.astype(q.dtype), s_fin.astype(s0.dtype)


def _memory_attention(q, mem_k, mem_v):
    bt, length, _ = q.shape
    qh = q.astype(jnp.float32).reshape(bt, length, XA_HEADS, XA_HEAD_DIM)
    scores = jnp.einsum('blhd,bmhd->bhlm', qh, mem_k.astype(jnp.float32)) * (XA_HEAD_DIM ** -0.5)
    p = jax.nn.softmax(scores, axis=-1)
    o = jnp.einsum('bhlm,bmhd->blhd', p, mem_v.astype(jnp.float32)).reshape(bt, length, XA_WIDTH)
    return o.astype(q.dtype)


def _layer(x, mem_k, mem_v, s5_re0, s5_im0, gla_s0, p):
    bt, length, _ = x.shape
    h = _rmsnorm(x, p['norm_mix'])
    z = h @ p['w_in']
    u_s5, q_g, k_g, v_g, r_g, a_low, q_x, gate_logits = jnp.split(z, _split_points(), axis=-1)
    y_s5, s5_re, s5_im = _s5_branch(u_s5, s5_re0, s5_im0, p['s5_lam_re'], p['s5_lam_im'], p['s5_log_dt'],
                                    p['s5_b_re'], p['s5_b_im'], p['s5_c_re'], p['s5_c_im'], p['s5_d'],
                                    p['s5_w_glu'], p['s5_b_glu'])
    y_gla, gla_s = _gla_branch(q_g, k_g, v_g, r_g, a_low, gla_s0, p['gla_w_a2'], p['gla_b_a'], p['gla_norm'])
    y_x = _memory_attention(q_x, mem_k, mem_v)
    gates = jax.nn.sigmoid(gate_logits.astype(jnp.float32)).reshape(bt, length, N_BRANCH, D_MODEL)
    merged = (gates[:, :, 0] * (y_s5 @ p['w_br_s5'])
              + gates[:, :, 1] * (y_gla @ p['w_br_gla'])
              + gates[:, :, 2] * (y_x @ p['w_br_xattn'])).astype(x.dtype)
    x = x + merged @ p['w_out']
    hf = _rmsnorm(x, p['norm_ffn'])
    x = x + (jax.nn.silu(hf @ p['w_ffn_gate']) * (hf @ p['w_ffn_up'])) @ p['w_ffn_down']
    return x, s5_re, s5_im, gla_s


def setup_inputs(seed: int = 0) -> dict:
    key = jax.random.key(seed)
    ks = jax.random.split(key, 36)
    f32 = jnp.float32
    L = DEPTH
    G = S5_GROUPS
    P = S5_STATE

    def nrm(k, shape, scale):
        return jax.random.normal(k, shape, f32) * scale

    def gain(k, shape):
        return 1.0 + 0.01 * jax.random.normal(k, shape, f32)

    x_prompt = nrm(ks[0], (BATCH, SEQ, D_MODEL), 1.0)
    x_sample = nrm(ks[1], (DEC_BATCH, DEC_SEQ, D_MODEL), 1.0)
    mem_prompt = nrm(ks[2], (BATCH, MEM_LEN, D_MODEL), 1.0)
    state_s5_re = nrm(ks[3], (L, DEC_BATCH, G, P), 0.1)
    state_s5_im = nrm(ks[4], (L, DEC_BATCH, G, P), 0.1)
    state_gla = nrm(ks[5], (L, DEC_BATCH, GLA_HEADS, GLA_DK, GLA_DV), 0.5)
    cache_mem_k = nrm(ks[6], (L, DEC_BATCH, MEM_LEN, XA_HEADS, XA_HEAD_DIM), 1.0)
    cache_mem_v = nrm(ks[7], (L, DEC_BATCH, MEM_LEN, XA_HEADS, XA_HEAD_DIM), 1.0)
    norm_mix = gain(ks[8], (L, D_MODEL))
    w_in = nrm(ks[9], (L, D_MODEL, IN_WIDTH), D_MODEL ** -0.5)
    s5_lam_re = -0.5 + nrm(ks[10], (L, G, P), 0.01)
    s5_lam_im = math.pi * jnp.arange(P, dtype=f32) + nrm(ks[11], (L, G, P), 0.01)
    s5_log_dt = jax.random.uniform(ks[12], (L, G), f32, math.log(1e-3), math.log(1e-1))
    s5_b_re = nrm(ks[13], (L, G, P, S5_GROUP), (2 * S5_GROUP) ** -0.5)
    s5_b_im = nrm(ks[14], (L, G, P, S5_GROUP), (2 * S5_GROUP) ** -0.5)
    s5_c_re = nrm(ks[15], (L, G, S5_GROUP, P), P ** -0.5)
    s5_c_im = nrm(ks[16], (L, G, S5_GROUP, P), P ** -0.5)
    s5_d = nrm(ks[17], (L, G, S5_GROUP), 1.0)
    s5_w_glu = nrm(ks[18], (L, S5_WIDTH, S5_WIDTH), S5_WIDTH ** -0.5)
    s5_b_glu = nrm(ks[19], (L, S5_WIDTH), 0.01)
    gla_w_a2 = nrm(ks[20], (L, GLA_RANK, GLA_HEADS * GLA_DK), GLA_RANK ** -0.5)
    gla_b_a = nrm(ks[21], (L, GLA_HEADS * GLA_DK), 0.1)
    gla_norm = gain(ks[22], (L, GLA_HEADS * GLA_DV))
    mem_norm = gain(ks[23], (L, D_MODEL))
    w_mem_k = nrm(ks[24], (L, D_MODEL, XA_WIDTH), D_MODEL ** -0.5)
    w_mem_v = nrm(ks[25], (L, D_MODEL, XA_WIDTH), D_MODEL ** -0.5)
    w_br_s5 = nrm(ks[26], (L, S5_WIDTH, D_MODEL), S5_WIDTH ** -0.5)
    w_br_gla = nrm(ks[27], (L, GLA_HEADS * GLA_DV, D_MODEL), (GLA_HEADS * GLA_DV) ** -0.5)
    w_br_xattn = nrm(ks[28], (L, XA_WIDTH, D_MODEL), XA_WIDTH ** -0.5)
    w_out = nrm(ks[29], (L, D_MODEL, D_MODEL), D_MODEL ** -0.5)
    norm_ffn = gain(ks[30], (L, D_MODEL))
    w_ffn_gate = nrm(ks[31], (L, D_MODEL, FFN_HIDDEN), D_MODEL ** -0.5)
    w_ffn_up = nrm(ks[32], (L, D_MODEL, FFN_HIDDEN), D_MODEL ** -0.5)
    w_ffn_down = nrm(ks[33], (L, FFN_HIDDEN, D_MODEL), FFN_HIDDEN ** -0.5)
    norm_final = gain(ks[34], (D_MODEL,))
    return {'x_prompt': x_prompt, 'x_sample': x_sample, 'mem_prompt': mem_prompt,
            'state_s5_re': state_s5_re, 'state_s5_im': state_s5_im, 'state_gla': state_gla,
            'cache_mem_k': cache_mem_k, 'cache_mem_v': cache_mem_v,
            'norm_mix': norm_mix, 'w_in': w_in,
            's5_lam_re': s5_lam_re, 's5_lam_im': s5_lam_im, 's5_log_dt': s5_log_dt,
            's5_b_re': s5_b_re, 's5_b_im': s5_b_im, 's5_c_re': s5_c_re, 's5_c_im': s5_c_im,
            's5_d': s5_d, 's5_w_glu': s5_w_glu, 's5_b_glu': s5_b_glu,
            'gla_w_a2': gla_w_a2, 'gla_b_a': gla_b_a, 'gla_norm': gla_norm,
            'mem_norm': mem_norm, 'w_mem_k': w_mem_k, 'w_mem_v': w_mem_v,
            'w_br_s5': w_br_s5, 'w_br_gla': w_br_gla, 'w_br_xattn': w_br_xattn, 'w_out': w_out,
            'norm_ffn': norm_ffn, 'w_ffn_gate': w_ffn_gate, 'w_ffn_up': w_ffn_up, 'w_ffn_down': w_ffn_down,
            'norm_final': norm_final}


def reference(x_prompt, x_sample, mem_prompt, state_s5_re, state_s5_im, state_gla, cache_mem_k, cache_mem_v,
              norm_mix, w_in, s5_lam_re, s5_lam_im, s5_log_dt, s5_b_re, s5_b_im, s5_c_re, s5_c_im,
              s5_d, s5_w_glu, s5_b_glu, gla_w_a2, gla_b_a, gla_norm, mem_norm, w_mem_k, w_mem_v,
              w_br_s5, w_br_gla, w_br_xattn, w_out, norm_ffn, w_ffn_gate, w_ffn_up, w_ffn_down, norm_final):
    bp = x_prompt.shape[0]
    xp = x_prompt
    xs = x_sample
    p_s5_re, p_s5_im, p_gla, p_mk, p_mv = [], [], [], [], []
    s_s5_re, s_s5_im, s_gla = [], [], []
    for l in range(DEPTH):
        p = {'norm_mix': norm_mix[l], 'w_in': w_in[l],
             's5_lam_re': s5_lam_re[l], 's5_lam_im': s5_lam_im[l], 's5_log_dt': s5_log_dt[l],
             's5_b_re': s5_b_re[l], 's5_b_im': s5_b_im[l], 's5_c_re': s5_c_re[l], 's5_c_im': s5_c_im[l],
             's5_d': s5_d[l], 's5_w_glu': s5_w_glu[l], 's5_b_glu': s5_b_glu[l],
             'gla_w_a2': gla_w_a2[l], 'gla_b_a': gla_b_a[l], 'gla_norm': gla_norm[l],
             'w_br_s5': w_br_s5[l], 'w_br_gla': w_br_gla[l], 'w_br_xattn': w_br_xattn[l], 'w_out': w_out[l],
             'norm_ffn': norm_ffn[l], 'w_ffn_gate': w_ffn_gate[l], 'w_ffn_up': w_ffn_up[l],
             'w_ffn_down': w_ffn_down[l]}
        mem_h = _rmsnorm(mem_prompt, mem_norm[l])
        mk = (mem_h @ w_mem_k[l]).reshape(bp, mem_prompt.shape[1], XA_HEADS, XA_HEAD_DIM)
        mv = (mem_h @ w_mem_v[l]).reshape(bp, mem_prompt.shape[1], XA_HEADS, XA_HEAD_DIM)
        z_re = jnp.zeros((bp, S5_GROUPS, S5_STATE), dtype=x_prompt.dtype)
        z_im = jnp.zeros((bp, S5_GROUPS, S5_STATE), dtype=x_prompt.dtype)
        z_gla = jnp.zeros((bp, GLA_HEADS, GLA_DK, GLA_DV), dtype=x_prompt.dtype)
        xp, pr, pi, pg = _layer(xp, mk, mv, z_re, z_im, z_gla, p)
        xs, sr, si, sg = _layer(xs, cache_mem_k[l], cache_mem_v[l], state_s5_re[l], state_s5_im[l], state_gla[l], p)
        p_s5_re.append(pr)
        p_s5_im.append(pi)
        p_gla.append(pg)
        p_mk.append(mk)
        p_mv.append(mv)
        s_s5_re.append(sr)
        s_s5_im.append(si)
        s_gla.append(sg)
    y_prompt = _rmsnorm(xp, norm_final)
    y_sample = _rmsnorm(xs, norm_final)
    return (y_prompt, y_sample,
            jnp.stack(p_s5_re), jnp.stack(p_s5_im), jnp.stack(p_gla), jnp.stack(p_mk), jnp.stack(p_mv),
            jnp.stack(s_s5_re), jnp.stack(s_s5_im), jnp.stack(s_gla))
```

```python
import functools
import math

import jax
import jax.numpy as jnp
from jax import lax
from jax.experimental import pallas as pl
from jax.experimental.pallas import tpu as pltpu

F32 = jnp.float32
BF16 = jnp.bfloat16

D_MODEL = 2048
S5_WIDTH = 1024
S5_GROUP = 16
S5_GROUPS = 64
S5_STATE = 64
GLA_HEADS = 4
GLA_DK = 128
GLA_DV = 256
GLA_RANK = 16
GLA_TAU = 16.0
GLA_CHUNK = 64
XA_HEADS = 4
XA_HEAD_DIM = 256
XA_WIDTH = 1024
MEM_LEN = 256
FFN_HIDDEN = 5632
RMS_EPS = 1e-6

LANES = 128
SUBLANES = 8
VMEM_LIMIT = 56 * 1024 * 1024

COL_U = 0
COL_Q = 1024
COL_K = 1536
COL_V = 2048
COL_R = 3072
COL_GATE = 4096
COL_QX = 10240
Z_WIDTH = 11264

S5_GB = 8
S5_CH = S5_GB * S5_STATE
S5_NBLK = S5_GROUPS // S5_GB
S5_SEGS = 8


def _params(*sem):
    return pltpu.CompilerParams(dimension_semantics=sem, vmem_limit_bytes=VMEM_LIMIT)


def _rms(x, g):
    return x * lax.rsqrt(jnp.mean(x * x, axis=-1, keepdims=True) + RMS_EPS) * g


def _norm_matmul_kernel(x_ref, g_ref, w_ref, o_ref, h_ref):
    @pl.when(pl.program_id(1) == 0)
    def _():
        h_ref[...] = _rms(x_ref[...], g_ref[...]).astype(BF16)

    o_ref[...] = jnp.dot(h_ref[...], w_ref[...], preferred_element_type=F32)


def _norm_matmul2_kernel(x_ref, g_ref, w_ref, ws_ref, o_ref, os_ref, h_ref):
    @pl.when(pl.program_id(1) == 0)
    def _():
        h = _rms(x_ref[...], g_ref[...]).astype(BF16)
        h_ref[...] = h
        os_ref[...] = jnp.dot(h, ws_ref[...], preferred_element_type=F32)

    o_ref[...] = jnp.dot(h_ref[...], w_ref[...], preferred_element_type=F32)


def _norm_matmul(x, g, w, w_small=None, *, tm, tn):
    t, d = x.shape
    n = w.shape[1]
    grid = (t // tm, n // tn)
    in_specs = [pl.BlockSpec((tm, d), lambda i, j: (i, 0)),
                pl.BlockSpec((1, d), lambda i, j: (0, 0)),
                pl.BlockSpec((d, tn), lambda i, j: (0, j))]
    out_specs = pl.BlockSpec((tm, tn), lambda i, j: (i, j))
    out_shape = jax.ShapeDtypeStruct((t, n), F32)
    scratch = [pltpu.VMEM((tm, d), BF16)]
    if w_small is None:
        return pl.pallas_call(_norm_matmul_kernel, grid=grid, in_specs=in_specs, out_specs=out_specs,
                              out_shape=out_shape, scratch_shapes=scratch,
                              compiler_params=_params("parallel", "arbitrary"),
                              name="norm_matmul")(x, g, w)
    ns = w_small.shape[1]
    in_specs.append(pl.BlockSpec((d, ns), lambda i, j: (0, 0)))
    return pl.pallas_call(_norm_matmul2_kernel, grid=grid, in_specs=in_specs,
                          out_specs=[out_specs, pl.BlockSpec((tm, ns), lambda i, j: (i, 0))],
                          out_shape=[out_shape, jax.ShapeDtypeStruct((t, ns), F32)],
                          scratch_shapes=scratch,
                          compiler_params=_params("parallel", "arbitrary"),
                          name="in_proj")(x, g, w, w_small)


def _s5_disc_kernel(lr_ref, li_ref, ldt_ref, br_ref, bi_ref, ar_ref, ai_ref, bbr_ref, bbi_ref):
    lam_re = lr_ref[...]
    lam_im = li_ref[...]
    dt = jnp.exp(ldt_ref[...])
    mag = jnp.exp(lam_re * dt)
    a_re = mag * jnp.cos(lam_im * dt)
    a_im = mag * jnp.sin(lam_im * dt)
    den = lam_re * lam_re + lam_im * lam_im
    coef_re = ((a_re - 1.0) * lam_re + a_im * lam_im) / den
    coef_im = (a_im * lam_re - (a_re - 1.0) * lam_im) / den
    b_re = br_ref[...]
    b_im = bi_ref[...]
    ar_ref[...] = a_re
    ai_ref[...] = a_im
    bbr_ref[...] = coef_re * b_re - coef_im * b_im
    bbi_ref[...] = coef_re * b_im + coef_im * b_re


def _s5_discretise(lam_re, lam_im, log_dt, b_re, b_im):
    n = S5_GROUPS * S5_STATE
    col = lambda a: a.reshape(n, 1)
    ldt = jnp.broadcast_to(log_dt[:, None], (S5_GROUPS, S5_STATE))
    outs = pl.pallas_call(
        _s5_disc_kernel,
        out_shape=[jax.ShapeDtypeStruct((n, 1), F32), jax.ShapeDtypeStruct((n, 1), F32),
                   jax.ShapeDtypeStruct((n, S5_GROUP), F32), jax.ShapeDtypeStruct((n, S5_GROUP), F32)],
        compiler_params=pltpu.CompilerParams(vmem_limit_bytes=VMEM_LIMIT),
        name="s5_discretise",
    )(col(lam_re), col(lam_im), col(ldt), b_re.reshape(n, S5_GROUP), b_im.reshape(n, S5_GROUP))
    return outs


def _s5_block_weights(a_re, a_im, bb_re, bb_im, c_re, c_im, d_skip):
    eye = jnp.eye(S5_GB, dtype=F32)

    def b_blk(bb):
        bb = bb.reshape(S5_NBLK, S5_GB, S5_STATE, S5_GROUP)
        return jnp.einsum('kgpc,gh->kgchp', bb, eye).reshape(S5_NBLK, S5_GB * S5_GROUP, S5_CH)

    def c_blk(c):
        c = c.reshape(S5_NBLK, S5_GB, S5_GROUP, S5_STATE)
        return jnp.einsum('kgcp,gh->kgphc', c, eye).reshape(S5_NBLK, S5_CH, S5_GB * S5_GROUP)

    w_b = jnp.concatenate([b_blk(bb_re), b_blk(bb_im)], axis=-1).astype(BF16)
    w_c = jnp.concatenate([c_blk(c_re), -c_blk(c_im)], axis=1).astype(BF16)
    a = jnp.concatenate([a_re.reshape(S5_NBLK, 1, S5_CH), a_im.reshape(S5_NBLK, 1, S5_CH)], axis=-1)
    d = d_skip.reshape(S5_NBLK, 1, S5_GB * S5_GROUP)
    return w_b, w_c, a, d


def _s5_scan_kernel(u_ref, wb_ref, wc_ref, a_ref, d_ref, h0r_ref, h0i_ref,
                    y_ref, hfr_ref, hfi_ref, up_ref, bu_ref, yp_ref,
                    *, nseq, slen, segmented, lane_chunk):
    ch = S5_CH
    rows = nseq * slen

    def slab(t):
        return pl.ds(pl.multiple_of(t * nseq, nseq), nseq)

    if slen <= 8:
        for t in range(slen):
            up_ref[t * nseq:(t + 1) * nseq, :] = u_ref[pl.ds(t, nseq, stride=slen), :]
    else:
        def perm_in(t, c):
            up_ref[slab(t), :] = u_ref[pl.ds(t, nseq, stride=slen), :]
            return c
        lax.fori_loop(0, slen, perm_in, 0)

    up = up_ref[...]
    bu_ref[...] = jnp.dot(up.astype(BF16), wb_ref[0], preferred_element_type=F32)

    def run_scan(lo, h_re, h_im, store):
        w = h_re.shape[1]
        a_re = jnp.broadcast_to(a_ref[0, :, lo:lo + w], (nseq, w))
        a_im = jnp.broadcast_to(a_ref[0, :, ch + lo:ch + lo + w], (nseq, w))

        def step(t, carry):
            hr, hi = carry
            br = bu_ref[slab(t), lo:lo + w]
            bi = bu_ref[slab(t), ch + lo:ch + lo + w]
            nr = a_re * hr - a_im * hi + br
            ni = a_re * hi + a_im * hr + bi
            if store:
                bu_ref[slab(t), lo:lo + w] = nr
                bu_ref[slab(t), ch + lo:ch + lo + w] = ni
            return nr, ni

        if slen <= 8:
            carry = (h_re, h_im)
            for t in range(slen):
                carry = step(t, carry)
            return carry
        return lax.fori_loop(0, slen, step, (h_re, h_im), unroll=4)

    for lo in range(0, ch, lane_chunk):
        w = lane_chunk
        if segmented:
            zero = jnp.zeros((nseq, w), F32)
            e_re, e_im = run_scan(lo, zero, zero, store=False)
            p_re = a_ref[0, :, lo:lo + w]
            p_im = a_ref[0, :, ch + lo:ch + lo + w]
            for _ in range(int(math.log2(slen))):
                p_re, p_im = p_re * p_re - p_im * p_im, 2.0 * (p_re * p_im)
            cur_re = h0r_ref[0, :, lo:lo + w]
            cur_im = h0i_ref[0, :, lo:lo + w]
            init_re, init_im = [], []
            for s in range(nseq):
                init_re.append(cur_re)
                init_im.append(cur_im)
                nxt_re = p_re * cur_re - p_im * cur_im + e_re[s:s + 1, :]
                nxt_im = p_re * cur_im + p_im * cur_re + e_im[s:s + 1, :]
                cur_re, cur_im = nxt_re, nxt_im
            hfr_ref[0, :, lo:lo + w] = cur_re
            hfi_ref[0, :, lo:lo + w] = cur_im
            run_scan(lo, jnp.concatenate(init_re, axis=0), jnp.concatenate(init_im, axis=0), store=True)
        else:
            f_re, f_im = run_scan(lo, h0r_ref[0, :, lo:lo + w], h0i_ref[0, :, lo:lo + w], store=True)
            hfr_ref[0, :, lo:lo + w] = f_re
            hfi_ref[0, :, lo:lo + w] = f_im

    y = jnp.dot(bu_ref[...].astype(BF16), wc_ref[0], preferred_element_type=F32) + d_ref[0] * up
    yp_ref[...] = jax.nn.gelu(y, approximate=True)

    if slen <= 8:
        for t in range(slen):
            y_ref[pl.ds(t, nseq, stride=slen), :] = yp_ref[t * nseq:(t + 1) * nseq, :]
    else:
        def perm_out(t, c):
            y_ref[pl.ds(t, nseq, stride=slen), :] = yp_ref[slab(t), :]
            return c
        lax.fori_loop(0, slen, perm_out, 0)


def _s5_scan(z, w_b, w_c, a, d, h0_re, h0_im, *, batch, seq, segmented):
    if segmented:
        nseq, slen, nb, ns, lane_chunk = S5_SEGS, seq // S5_SEGS, batch, 1, S5_CH
        assert slen & (slen - 1) == 0
    else:
        nseq, slen, nb, ns, lane_chunk = batch, seq, 1, batch, LANES
    rows = nseq * slen
    gw = S5_GB * S5_GROUP
    kern = functools.partial(_s5_scan_kernel, nseq=nseq, slen=slen, segmented=segmented, lane_chunk=lane_chunk)
    state_spec = pl.BlockSpec((1, ns, S5_CH), lambda b, k: (b, 0, k))
    state_shape = jax.ShapeDtypeStruct((nb, ns, S5_GROUPS * S5_STATE), F32)
    return pl.pallas_call(
        kern, grid=(nb, S5_NBLK),
        in_specs=[pl.BlockSpec((rows, gw), lambda b, k: (b, COL_U // gw + k)),
                  pl.BlockSpec((1, gw, 2 * S5_CH), lambda b, k: (k, 0, 0)),
                  pl.BlockSpec((1, 2 * S5_CH, gw), lambda b, k: (k, 0, 0)),
                  pl.BlockSpec((1, 1, 2 * S5_CH), lambda b, k: (k, 0, 0)),
                  pl.BlockSpec((1, 1, gw), lambda b, k: (k, 0, 0)),
                  state_spec, state_spec],
        out_specs=[pl.BlockSpec((rows, gw), lambda b, k: (b, k)), state_spec, state_spec],
        out_shape=[jax.ShapeDtypeStruct((nb * rows, S5_WIDTH), F32), state_shape, state_shape],
        scratch_shapes=[pltpu.VMEM((rows, gw), F32), pltpu.VMEM((rows, 2 * S5_CH), F32),
                        pltpu.VMEM((rows, gw), F32)],
        compiler_params=_params("parallel", "parallel"),
        name="s5_scan",
    )(z, w_b, w_c, a, d, h0_re, h0_im)


def _gla_kernel(q_ref, k_ref, v_ref, r_ref, al_ref, wa_ref, ba_ref, gn_ref, s0_ref,
                y_ref, sf_ref, s_scr, *, nb, seq, chunk):
    nchunk = seq // chunk
    row_id = lax.broadcasted_iota(jnp.int32, (chunk, chunk), 0)
    col_id = lax.broadcasted_iota(jnp.int32, (chunk, chunk), 1)
    causal = row_id >= col_id
    tri = causal.astype(F32)
    ones = jnp.ones((chunk, GLA_DK), F32)
    tn_dims = (((0,), (0,)), ((), ()))
    nt_dims = (((1,), (1,)), ((), ()))

    def seq_body(bi, carry):
        s_scr[...] = s0_ref[bi, 0]

        def chunk_body(c, carry2):
            rows = pl.ds(pl.multiple_of(bi * seq + c * chunk, chunk), chunk)
            x = jnp.dot(al_ref[rows, :].astype(BF16), wa_ref[...], preferred_element_type=F32) + ba_ref[...]
            log_a = (jnp.minimum(x, 0.0) - jnp.log1p(jnp.exp(-jnp.abs(x)))) * (1.0 / GLA_TAU)
            b = jnp.dot(tri, log_a, precision=lax.Precision.HIGHEST, preferred_element_type=F32)
            b_last = b[chunk - 1:chunk, :]
            q = q_ref[rows, :] * (GLA_DK ** -0.5)
            k = k_ref[rows, :]
            v = v_ref[rows, :].astype(BF16)
            qb = (q * jnp.exp(b)).astype(BF16)
            kb = (k * jnp.exp(-b)).astype(BF16)
            s = s_scr[...]
            o = jnp.dot(qb, s.astype(BF16), preferred_element_type=F32)
            att = lax.dot_general(qb, kb, nt_dims, preferred_element_type=F32)
            att = jnp.where(causal, att, 0.0)
            o = o + jnp.dot(att.astype(BF16), v, preferred_element_type=F32)
            kd = (k * jnp.exp(b_last - b)).astype(BF16)
            upd = lax.dot_general(kd, v, tn_dims, preferred_element_type=F32)
            b_col = lax.dot_general(log_a, ones, tn_dims, precision=lax.Precision.HIGHEST,
                                    preferred_element_type=F32)
            e_col = jnp.exp(b_col)
            s_scr[...] = jnp.concatenate([s[:, :LANES] * e_col, s[:, LANES:] * e_col], axis=1) + upd
            o = o * lax.rsqrt(jnp.mean(o * o, axis=-1, keepdims=True) + RMS_EPS)
            o = o * gn_ref[...]
            r = r_ref[rows, :]
            y_ref[rows, :] = o * (r * jax.nn.sigmoid(r))
            return carry2

        lax.fori_loop(0, nchunk, chunk_body, 0)
        sf_ref[bi, 0] = s_scr[...]
        return carry

    lax.fori_loop(0, nb, seq_body, 0)


def _gla(z, alow, w_a2, b_a, g_norm, s0, *, batch, seq, nb):
    chunk = math.gcd(seq, GLA_CHUNK)
    rows = nb * seq
    kern = functools.partial(_gla_kernel, nb=nb, seq=seq, chunk=chunk)
    state_spec = pl.BlockSpec((nb, 1, GLA_DK, GLA_DV), lambda i, h: (i, h, 0, 0))
    return pl.pallas_call(
        kern, grid=(batch // nb, GLA_HEADS),
        in_specs=[pl.BlockSpec((rows, GLA_DK), lambda i, h: (i, COL_Q // GLA_DK + h)),
                  pl.BlockSpec((rows, GLA_DK), lambda i, h: (i, COL_K // GLA_DK + h)),
                  pl.BlockSpec((rows, GLA_DV), lambda i, h: (i, COL_V // GLA_DV + h)),
                  pl.BlockSpec((rows, GLA_DV), lambda i, h: (i, COL_R // GLA_DV + h)),
                  pl.BlockSpec((rows, LANES), lambda i, h: (i, 0)),
                  pl.BlockSpec((LANES, GLA_DK), lambda i, h: (0, h)),
                  pl.BlockSpec((1, GLA_DK), lambda i, h: (0, h)),
                  pl.BlockSpec((1, GLA_DV), lambda i, h: (0, h)),
                  state_spec],
        out_specs=[pl.BlockSpec((rows, GLA_DV), lambda i, h: (i, h)), state_spec],
        out_shape=[jax.ShapeDtypeStruct((batch * seq, GLA_HEADS * GLA_DV), F32),
                   jax.ShapeDtypeStruct((batch, GLA_HEADS, GLA_DK, GLA_DV), F32)],
        scratch_shapes=[pltpu.VMEM((GLA_DK, GLA_DV), F32)],
        compiler_params=_params("parallel", "parallel"),
        name="gla",
    )(z, z, z, z, alow, w_a2, b_a, g_norm, s0)


def _attn_kernel(q_ref, k_ref, v_ref, o_ref, *, nb, seq, heads):
    nt_dims = (((1,), (1,)), ((), ()))

    def body(bi, carry):
        rows = pl.ds(pl.multiple_of(bi * seq, SUBLANES), seq)
        for h in range(heads):
            cols = slice(h * XA_HEAD_DIM, (h + 1) * XA_HEAD_DIM)
            q = q_ref[rows, cols].astype(BF16)
            k = k_ref[bi, :, cols].astype(BF16)
            s = lax.dot_general(q, k, nt_dims, preferred_element_type=F32) * (XA_HEAD_DIM ** -0.5)
            p = jnp.exp(s - jnp.max(s, axis=-1, keepdims=True))
            p = p / jnp.sum(p, axis=-1, keepdims=True)
            o_ref[rows, cols] = jnp.dot(p.astype(BF16), v_ref[bi, :, cols].astype(BF16),
                                        preferred_element_type=F32)
        return carry

    lax.fori_loop(0, nb, body, 0)


def _attn_prompt(z, mem_k, mem_v, *, batch, seq, rc):
    nrc = seq // rc
    kern = functools.partial(_attn_kernel, nb=1, seq=rc, heads=1)
    kv_spec = pl.BlockSpec((1, MEM_LEN, XA_HEAD_DIM), lambda b, h, c: (b, 0, h))
    return pl.pallas_call(
        kern, grid=(batch, XA_HEADS, nrc),
        in_specs=[pl.BlockSpec((rc, XA_HEAD_DIM), lambda b, h, c: (b * nrc + c, COL_QX // XA_HEAD_DIM + h)),
                  kv_spec, kv_spec],
        out_specs=pl.BlockSpec((rc, XA_HEAD_DIM), lambda b, h, c: (b * nrc + c, h)),
        out_shape=jax.ShapeDtypeStruct((batch * seq, XA_WIDTH), F32),
        compiler_params=_params("parallel", "parallel", "parallel"),
        name="attn_prompt",
    )(z, mem_k, mem_v)


def _attn_sample(z, mem_k, mem_v, *, batch, seq, nb):
    rows = nb * seq
    kern = functools.partial(_attn_kernel, nb=nb, seq=seq, heads=XA_HEADS)
    kv_spec = pl.BlockSpec((nb, MEM_LEN, XA_WIDTH), lambda i: (i, 0, 0))
    return pl.pallas_call(
        kern, grid=(batch // nb,),
        in_specs=[pl.BlockSpec((rows, XA_WIDTH), lambda i: (i, COL_QX // XA_WIDTH)), kv_spec, kv_spec],
        out_specs=pl.BlockSpec((rows, XA_WIDTH), lambda i: (i, 0)),
        out_shape=jax.ShapeDtypeStruct((batch * seq, XA_WIDTH), F32),
        compiler_params=_params("parallel"),
        name="attn_sample",
    )(z, mem_k, mem_v)


def _merge_kernel(ys_ref, yg_ref, yx_ref, g0_ref, g1_ref, g2_ref, wglu_ref, bglu_ref,
                  w0_ref, w1_ref, w2_ref, o_ref, s5_scr, gla_scr, xa_scr):
    @pl.when(pl.program_id(1) == 0)
    def _():
        y = ys_ref[...]
        lin = jnp.dot(y.astype(BF16), wglu_ref[...], preferred_element_type=F32) + bglu_ref[...]
        s5_scr[...] = (y * jax.nn.sigmoid(lin)).astype(BF16)
        gla_scr[...] = yg_ref[...].astype(BF16)
        xa_scr[...] = yx_ref[...].astype(BF16)

    m = jax.nn.sigmoid(g0_ref[...]) * jnp.dot(s5_scr[...], w0_ref[...], preferred_element_type=F32)
    m = m + jax.nn.sigmoid(g1_ref[...]) * jnp.dot(gla_scr[...], w1_ref[...], preferred_element_type=F32)
    m = m + jax.nn.sigmoid(g2_ref[...]) * jnp.dot(xa_scr[...], w2_ref[...], preferred_element_type=F32)
    o_ref[...] = m.astype(o_ref.dtype)


def _merge(y_s5, y_gla, y_x, z, w_glu, b_glu, w_br_s5, w_br_gla, w_br_x, *, tm, tn):
    t = y_s5.shape[0]
    nj = D_MODEL // tn
    wide = pl.BlockSpec((tm, S5_WIDTH), lambda i, j: (i, 0))
    gate = lambda b: pl.BlockSpec((tm, tn), lambda i, j: (i, (COL_GATE + b * D_MODEL) // tn + j))
    w_br = pl.BlockSpec((S5_WIDTH, tn), lambda i, j: (0, j))
    return pl.pallas_call(
        _merge_kernel, grid=(t // tm, nj),
        in_specs=[wide, wide, wide, gate(0), gate(1), gate(2),
                  pl.BlockSpec((S5_WIDTH, S5_WIDTH), lambda i, j: (0, 0)),
                  pl.BlockSpec((1, S5_WIDTH), lambda i, j: (0, 0)),
                  w_br, w_br, w_br],
        out_specs=pl.BlockSpec((tm, tn), lambda i, j: (i, j)),
        out_shape=jax.ShapeDtypeStruct((t, D_MODEL), BF16),
        scratch_shapes=[pltpu.VMEM((tm, S5_WIDTH), BF16)] * 3,
        compiler_params=_params("parallel", "arbitrary"),
        name="merge",
    )(y_s5, y_gla, y_x, z, z, z, w_glu, b_glu, w_br_s5, w_br_gla, w_br_x)


def _out_proj_kernel(m_ref, w_ref, x_ref, o_ref):
    o_ref[...] = x_ref[...] + jnp.dot(m_ref[...], w_ref[...], preferred_element_type=F32)


def _out_proj(merged, w_out, x, *, tm, tn):
    t = x.shape[0]
    return pl.pallas_call(
        _out_proj_kernel, grid=(t // tm, D_MODEL // tn),
        in_specs=[pl.BlockSpec((tm, D_MODEL), lambda i, j: (i, 0)),
                  pl.BlockSpec((D_MODEL, tn), lambda i, j: (0, j)),
                  pl.BlockSpec((tm, tn), lambda i, j: (i, j))],
        out_specs=pl.BlockSpec((tm, tn), lambda i, j: (i, j)),
        out_shape=jax.ShapeDtypeStruct((t, D_MODEL), F32),
        compiler_params=_params("parallel", "parallel"),
        name="out_proj",
    )(merged, w_out, x)


def _ffn_kernel(x_ref, gf_ref, wg_ref, wu_ref, wd_ref, gl_ref, o_ref, h_scr, acc_scr):
    k = pl.program_id(1)

    @pl.when(k == 0)
    def _():
        h_scr[...] = _rms(x_ref[...], gf_ref[...]).astype(BF16)
        acc_scr[...] = jnp.zeros_like(acc_scr)

    h = h_scr[...]
    gate = jnp.dot(h, wg_ref[...], preferred_element_type=F32)
    up = jnp.dot(h, wu_ref[...], preferred_element_type=F32)
    act = (gate * jax.nn.sigmoid(gate) * up).astype(BF16)
    acc_scr[...] += jnp.dot(act, wd_ref[...], preferred_element_type=F32)

    @pl.when(k == pl.num_programs(1) - 1)
    def _():
        o_ref[...] = _rms(x_ref[...] + acc_scr[...], gl_ref[...])


def _ffn(x, g_ffn, w_gate, w_up, w_down, g_final, *, tm, th):
    t = x.shape[0]
    row = pl.BlockSpec((tm, D_MODEL), lambda i, k: (i, 0))
    vec = pl.BlockSpec((1, D_MODEL), lambda i, k: (0, 0))
    w_in = pl.BlockSpec((D_MODEL, th), lambda i, k: (0, k))
    return pl.pallas_call(
        _ffn_kernel, grid=(t // tm, FFN_HIDDEN // th),
        in_specs=[row, vec, w_in, w_in, pl.BlockSpec((th, D_MODEL), lambda i, k: (k, 0)), vec],
        out_specs=row,
        out_shape=jax.ShapeDtypeStruct((t, D_MODEL), F32),
        scratch_shapes=[pltpu.VMEM((tm, D_MODEL), BF16), pltpu.VMEM((tm, D_MODEL), F32)],
        compiler_params=_params("parallel", "arbitrary"),
        name="ffn",
    )(x, g_ffn, w_gate, w_up, w_down, g_final)


def _reorder_in_proj(w_in):
    u_to_r = w_in[:, 0:4096]
    a_low = w_in[:, 4096:4112]
    q_x = w_in[:, 4112:5136]
    gates = w_in[:, 5136:11280]
    w_main = jnp.concatenate([u_to_r, gates, q_x], axis=1).astype(BF16)
    w_alow = jnp.pad(a_low, ((0, 0), (0, LANES - GLA_RANK))).astype(BF16)
    return w_main, w_alow


def _layer(x, mem_k, mem_v, s5_re0, s5_im0, gla_s0, w, *, batch, seq, segmented, gla_nb):
    z, alow = _norm_matmul(x, w['norm_mix'], w['w_main'], w['w_alow'], tm=1024, tn=1024)
    y_s5, hf_re, hf_im = _s5_scan(z, w['s5_wb'], w['s5_wc'], w['s5_a'], w['s5_d'], s5_re0, s5_im0,
                                  batch=batch, seq=seq, segmented=segmented)
    y_gla, gla_s = _gla(z, alow, w['gla_w_a2'], w['gla_b_a'], w['gla_norm'], gla_s0,
                        batch=batch, seq=seq, nb=gla_nb)
    if segmented:
        y_x = _attn_prompt(z, mem_k, mem_v, batch=batch, seq=seq, rc=512)
    else:
        y_x = _attn_sample(z, mem_k, mem_v, batch=batch, seq=seq, nb=8)
    merged = _merge(y_s5, y_gla, y_x, z, w['s5_w_glu'], w['s5_b_glu'],
                    w['w_br_s5'], w['w_br_gla'], w['w_br_xattn'], tm=512, tn=512)
    x1 = _out_proj(merged, w['w_out'], x, tm=1024, tn=512)
    y = _ffn(x1, w['norm_ffn'], w['w_ffn_gate'], w['w_ffn_up'], w['w_ffn_down'], w['norm_final'],
             tm=512, th=512)
    return y, hf_re, hf_im, gla_s


def kernel(x_prompt, x_sample, mem_prompt, state_s5_re, state_s5_im, state_gla, cache_mem_k, cache_mem_v,
           norm_mix, w_in, s5_lam_re, s5_lam_im, s5_log_dt, s5_b_re, s5_b_im, s5_c_re, s5_c_im,
           s5_d, s5_w_glu, s5_b_glu, gla_w_a2, gla_b_a, gla_norm, mem_norm, w_mem_k, w_mem_v,
           w_br_s5, w_br_gla, w_br_xattn, w_out, norm_ffn, w_ffn_gate, w_ffn_up, w_ffn_down, norm_final):
    depth = w_in.shape[0]
    assert depth == 1
    bp, sp, d = x_prompt.shape
    bs, ss, _ = x_sample.shape
    n_state = S5_GROUPS * S5_STATE
    row = lambda v: v.reshape(1, -1)

    l = 0
    w_main, w_alow = _reorder_in_proj(w_in[l])
    a_re, a_im, bb_re, bb_im = _s5_discretise(s5_lam_re[l], s5_lam_im[l], s5_log_dt[l], s5_b_re[l], s5_b_im[l])
    s5_wb, s5_wc, s5_a, s5_dd = _s5_block_weights(a_re, a_im, bb_re, bb_im, s5_c_re[l], s5_c_im[l], s5_d[l])
    w = {
        'norm_mix': row(norm_mix[l]), 'w_main': w_main, 'w_alow': w_alow,
        's5_wb': s5_wb, 's5_wc': s5_wc, 's5_a': s5_a, 's5_d': s5_dd,
        's5_w_glu': s5_w_glu[l].astype(BF16), 's5_b_glu': row(s5_b_glu[l]),
        'gla_w_a2': jnp.pad(gla_w_a2[l], ((0, LANES - GLA_RANK), (0, 0))).astype(BF16),
        'gla_b_a': row(gla_b_a[l]), 'gla_norm': row(gla_norm[l]),
        'w_br_s5': w_br_s5[l].astype(BF16), 'w_br_gla': w_br_gla[l].astype(BF16),
        'w_br_xattn': w_br_xattn[l].astype(BF16), 'w_out': w_out[l].astype(BF16),
        'norm_ffn': row(norm_ffn[l]), 'w_ffn_gate': w_ffn_gate[l].astype(BF16),
        'w_ffn_up': w_ffn_up[l].astype(BF16), 'w_ffn_down': w_ffn_down[l].astype(BF16),
        'norm_final': row(norm_final),
    }

    w_mem = jnp.concatenate([w_mem_k[l], w_mem_v[l]], axis=1).astype(BF16)
    mem_kv = _norm_matmul(mem_prompt.reshape(bp * MEM_LEN, d), row(mem_norm[l]), w_mem, tm=512, tn=1024)
    mk = mem_kv[:, :XA_WIDTH].reshape(bp, MEM_LEN, XA_WIDTH)
    mv = mem_kv[:, XA_WIDTH:].reshape(bp, MEM_LEN, XA_WIDTH)
    zero_s5 = jnp.zeros((bp, 1, n_state), F32)
    zero_gla = jnp.zeros((bp, GLA_HEADS, GLA_DK, GLA_DV), F32)
    yp, p_re, p_im, p_gla = _layer(x_prompt.reshape(bp * sp, d), mk, mv, zero_s5, zero_s5, zero_gla, w,
                                   batch=bp, seq=sp, segmented=True, gla_nb=1)

    ys, s_re, s_im, s_gla = _layer(x_sample.reshape(bs * ss, d),
                                   cache_mem_k[l].reshape(bs, MEM_LEN, XA_WIDTH),
                                   cache_mem_v[l].reshape(bs, MEM_LEN, XA_WIDTH),
                                   state_s5_re[l].reshape(1, bs, n_state), state_s5_im[l].reshape(1, bs, n_state),
                                   state_gla[l], w, batch=bs, seq=ss, segmented=False, gla_nb=8)

    s5_shape_p = (1, bp, S5_GROUPS, S5_STATE)
    s5_shape_s = (1, bs, S5_GROUPS, S5_STATE)
    kv_shape = (1, bp, MEM_LEN, XA_HEADS, XA_HEAD_DIM)
    return (yp.reshape(bp, sp, d), ys.reshape(bs, ss, d),
            p_re.reshape(s5_shape_p), p_im.reshape(s5_shape_p), p_gla[None],
            mk.reshape(kv_shape), mv.reshape(kv_shape),
            s_re.reshape(s5_shape_s), s_im.reshape(s5_shape_s), s_gla[None])
```

```python
import functools
import math

import jax
import jax.numpy as jnp
from jax import lax
from jax.experimental import pallas as pl
from jax.experimental.pallas import tpu as pltpu

F32 = jnp.float32
BF16 = jnp.bfloat16

D_MODEL = 2048
S5_WIDTH = 1024
S5_GROUP = 16
S5_GROUPS = 64
S5_STATE = 64
GLA_HEADS = 4
GLA_DK = 128
GLA_DV = 256
GLA_RANK = 16
GLA_TAU = 16.0
GLA_CHUNK = 64
XA_HEADS = 4
XA_HEAD_DIM = 256
XA_WIDTH = 1024
MEM_LEN = 256
FFN_HIDDEN = 5632
RMS_EPS = 1e-6

LANES = 128
SUBLANES = 8
VMEM_LIMIT = 56 * 1024 * 1024

COL_U = 0
COL_Q = 1024
COL_K = 1536
COL_V = 2048
COL_R = 3072
COL_GATE = 4096
COL_QX = 10240
Z_WIDTH = 11264

S5_GB = 8
S5_CH = S5_GB * S5_STATE
S5_NBLK = S5_GROUPS // S5_GB
S5_SEGS = 8
GLA_TILE = 64


def _params(*sem):
    return pltpu.CompilerParams(dimension_semantics=sem, vmem_limit_bytes=VMEM_LIMIT)


def _rms(x, g):
    return x * lax.rsqrt(jnp.mean(x * x, axis=-1, keepdims=True) + RMS_EPS) * g


def _norm_matmul_kernel(x_ref, g_ref, w_ref, o_ref, h_ref):
    @pl.when(pl.program_id(1) == 0)
    def _():
        h_ref[...] = _rms(x_ref[...], g_ref[...]).astype(BF16)

    o_ref[...] = jnp.dot(h_ref[...], w_ref[...], preferred_element_type=F32)


def _norm_matmul2_kernel(x_ref, g_ref, w_ref, ws_ref, o_ref, os_ref, h_ref):
    @pl.when(pl.program_id(1) == 0)
    def _():
        h = _rms(x_ref[...], g_ref[...]).astype(BF16)
        h_ref[...] = h
        os_ref[...] = jnp.dot(h, ws_ref[...], preferred_element_type=F32)

    o_ref[...] = jnp.dot(h_ref[...], w_ref[...], preferred_element_type=F32)


def _norm_matmul(x, g, w, w_small=None, *, tm, tn):
    t, d = x.shape
    n = w.shape[1]
    grid = (t // tm, n // tn)
    in_specs = [pl.BlockSpec((tm, d), lambda i, j: (i, 0)),
                pl.BlockSpec((1, d), lambda i, j: (0, 0)),
                pl.BlockSpec((d, tn), lambda i, j: (0, j))]
    out_specs = pl.BlockSpec((tm, tn), lambda i, j: (i, j))
    out_shape = jax.ShapeDtypeStruct((t, n), F32)
    scratch = [pltpu.VMEM((tm, d), BF16)]
    if w_small is None:
        return pl.pallas_call(_norm_matmul_kernel, grid=grid, in_specs=in_specs, out_specs=out_specs,
                              out_shape=out_shape, scratch_shapes=scratch,
                              compiler_params=_params("parallel", "arbitrary"),
                              name="norm_matmul")(x, g, w)
    ns = w_small.shape[1]
    in_specs.append(pl.BlockSpec((d, ns), lambda i, j: (0, 0)))
    return pl.pallas_call(_norm_matmul2_kernel, grid=grid, in_specs=in_specs,
                          out_specs=[out_specs, pl.BlockSpec((tm, ns), lambda i, j: (i, 0))],
                          out_shape=[out_shape, jax.ShapeDtypeStruct((t, ns), F32)],
                          scratch_shapes=scratch,
                          compiler_params=_params("parallel", "arbitrary"),
                          name="in_proj")(x, g, w, w_small)


def _s5_disc_kernel(lr_ref, li_ref, ldt_ref, br_ref, bi_ref, ar_ref, ai_ref, bbr_ref, bbi_ref):
    lam_re = lr_ref[...]
    lam_im = li_ref[...]
    dt = jnp.exp(ldt_ref[...])
    mag = jnp.exp(lam_re * dt)
    a_re = mag * jnp.cos(lam_im * dt)
    a_im = mag * jnp.sin(lam_im * dt)
    den = lam_re * lam_re + lam_im * lam_im
    coef_re = ((a_re - 1.0) * lam_re + a_im * lam_im) / den
    coef_im = (a_im * lam_re - (a_re - 1.0) * lam_im) / den
    b_re = br_ref[...]
    b_im = bi_ref[...]
    ar_ref[...] = a_re
    ai_ref[...] = a_im
    bbr_ref[...] = coef_re * b_re - coef_im * b_im
    bbi_ref[...] = coef_re * b_im + coef_im * b_re


def _s5_discretise(lam_re, lam_im, log_dt, b_re, b_im):
    n = S5_GROUPS * S5_STATE
    col = lambda a: a.reshape(n, 1)
    ldt = jnp.broadcast_to(log_dt[:, None], (S5_GROUPS, S5_STATE))
    outs = pl.pallas_call(
        _s5_disc_kernel,
        out_shape=[jax.ShapeDtypeStruct((n, 1), F32), jax.ShapeDtypeStruct((n, 1), F32),
                   jax.ShapeDtypeStruct((n, S5_GROUP), F32), jax.ShapeDtypeStruct((n, S5_GROUP), F32)],
        compiler_params=pltpu.CompilerParams(vmem_limit_bytes=VMEM_LIMIT),
        name="s5_discretise",
    )(col(lam_re), col(lam_im), col(ldt), b_re.reshape(n, S5_GROUP), b_im.reshape(n, S5_GROUP))
    return outs


def _s5_block_weights(a_re, a_im, bb_re, bb_im, c_re, c_im, d_skip):
    eye = jnp.eye(S5_GB, dtype=F32)

    def b_blk(bb):
        bb = bb.reshape(S5_NBLK, S5_GB, S5_STATE, S5_GROUP)
        return jnp.einsum('kgpc,gh->kgchp', bb, eye).reshape(S5_NBLK, S5_GB * S5_GROUP, S5_CH)

    def c_blk(c):
        c = c.reshape(S5_NBLK, S5_GB, S5_GROUP, S5_STATE)
        return jnp.einsum('kgcp,gh->kgphc', c, eye).reshape(S5_NBLK, S5_CH, S5_GB * S5_GROUP)

    w_b = jnp.concatenate([b_blk(bb_re), b_blk(bb_im)], axis=-1).astype(BF16)
    w_c = jnp.concatenate([c_blk(c_re), -c_blk(c_im)], axis=1).astype(BF16)
    a = jnp.concatenate([a_re.reshape(S5_NBLK, 1, S5_CH), a_im.reshape(S5_NBLK, 1, S5_CH)], axis=-1)
    d = d_skip.reshape(S5_NBLK, 1, S5_GB * S5_GROUP)
    return w_b, w_c, a, d


def _s5_scan_kernel(u_ref, wb_ref, wc_ref, a_ref, d_ref, h0r_ref, h0i_ref,
                    y_ref, hfr_ref, hfi_ref, up_ref, bu_ref, yp_ref,
                    *, nseq, slen, segmented, lane_chunk):
    ch = S5_CH
    rows = nseq * slen

    def slab(t):
        return pl.ds(pl.multiple_of(t * nseq, nseq), nseq)

    if slen <= 8:
        for t in range(slen):
            up_ref[t * nseq:(t + 1) * nseq, :] = u_ref[pl.ds(t, nseq, stride=slen), :]
    else:
        def perm_in(t, c):
            up_ref[slab(t), :] = u_ref[pl.ds(t, nseq, stride=slen), :]
            return c
        lax.fori_loop(0, slen, perm_in, 0)

    up = up_ref[...]
    bu_ref[...] = jnp.dot(up.astype(BF16), wb_ref[0], preferred_element_type=F32)

    def run_scan(lo, h_re, h_im, store):
        w = h_re.shape[1]
        a_re = jnp.broadcast_to(a_ref[0, :, lo:lo + w], (nseq, w))
        a_im = jnp.broadcast_to(a_ref[0, :, ch + lo:ch + lo + w], (nseq, w))

        def step(t, carry):
            hr, hi = carry
            br = bu_ref[slab(t), lo:lo + w]
            bi = bu_ref[slab(t), ch + lo:ch + lo + w]
            nr = a_re * hr - a_im * hi + br
            ni = a_re * hi + a_im * hr + bi
            if store:
                bu_ref[slab(t), lo:lo + w] = nr
                bu_ref[slab(t), ch + lo:ch + lo + w] = ni
            return nr, ni

        if slen <= 8:
            carry = (h_re, h_im)
            for t in range(slen):
                carry = step(t, carry)
            return carry
        return lax.fori_loop(0, slen, step, (h_re, h_im), unroll=4)

    for lo in range(0, ch, lane_chunk):
        w = lane_chunk
        if segmented:
            zero = jnp.zeros((nseq, w), F32)
            e_re, e_im = run_scan(lo, zero, zero, store=False)
            p_re = a_ref[0, :, lo:lo + w]
            p_im = a_ref[0, :, ch + lo:ch + lo + w]
            for _ in range(int(math.log2(slen))):
                p_re, p_im = p_re * p_re - p_im * p_im, 2.0 * (p_re * p_im)
            cur_re = h0r_ref[0, :, lo:lo + w]
            cur_im = h0i_ref[0, :, lo:lo + w]
            init_re, init_im = [], []
            for s in range(nseq):
                init_re.append(cur_re)
                init_im.append(cur_im)
                nxt_re = p_re * cur_re - p_im * cur_im + e_re[s:s + 1, :]
                nxt_im = p_re * cur_im + p_im * cur_re + e_im[s:s + 1, :]
                cur_re, cur_im = nxt_re, nxt_im
            hfr_ref[0, :, lo:lo + w] = cur_re
            hfi_ref[0, :, lo:lo + w] = cur_im
            run_scan(lo, jnp.concatenate(init_re, axis=0), jnp.concatenate(init_im, axis=0), store=True)
        else:
            f_re, f_im = run_scan(lo, h0r_ref[0, :, lo:lo + w], h0i_ref[0, :, lo:lo + w], store=True)
            hfr_ref[0, :, lo:lo + w] = f_re
            hfi_ref[0, :, lo:lo + w] = f_im

    y = jnp.dot(bu_ref[...].astype(BF16), wc_ref[0], preferred_element_type=F32) + d_ref[0] * up
    yp_ref[...] = jax.nn.gelu(y, approximate=True)

    if slen <= 8:
        for t in range(slen):
            y_ref[pl.ds(t, nseq, stride=slen), :] = yp_ref[t * nseq:(t + 1) * nseq, :]
    else:
        def perm_out(t, c):
            y_ref[pl.ds(t, nseq, stride=slen), :] = yp_ref[slab(t), :]
            return c
        lax.fori_loop(0, slen, perm_out, 0)


def _s5_scan(z, w_b, w_c, a, d, h0_re, h0_im, *, batch, seq, segmented):
    if segmented:
        nseq, slen, nb, ns, lane_chunk = S5_SEGS, seq // S5_SEGS, batch, 1, S5_CH
        assert slen & (slen - 1) == 0
    else:
        nseq, slen, nb, ns, lane_chunk = batch, seq, 1, batch, LANES
    rows = nseq * slen
    gw = S5_GB * S5_GROUP
    kern = functools.partial(_s5_scan_kernel, nseq=nseq, slen=slen, segmented=segmented, lane_chunk=lane_chunk)
    state_spec = pl.BlockSpec((1, ns, S5_CH), lambda b, k: (b, 0, k))
    state_shape = jax.ShapeDtypeStruct((nb, ns, S5_GROUPS * S5_STATE), F32)
    return pl.pallas_call(
        kern, grid=(nb, S5_NBLK),
        in_specs=[pl.BlockSpec((rows, gw), lambda b, k: (b, COL_U // gw + k)),
                  pl.BlockSpec((1, gw, 2 * S5_CH), lambda b, k: (k, 0, 0)),
                  pl.BlockSpec((1, 2 * S5_CH, gw), lambda b, k: (k, 0, 0)),
                  pl.BlockSpec((1, 1, 2 * S5_CH), lambda b, k: (k, 0, 0)),
                  pl.BlockSpec((1, 1, gw), lambda b, k: (k, 0, 0)),
                  state_spec, state_spec],
        out_specs=[pl.BlockSpec((rows, gw), lambda b, k: (b, k)), state_spec, state_spec],
        out_shape=[jax.ShapeDtypeStruct((nb * rows, S5_WIDTH), F32), state_shape, state_shape],
        scratch_shapes=[pltpu.VMEM((rows, gw), F32), pltpu.VMEM((rows, 2 * S5_CH), F32),
                        pltpu.VMEM((rows, gw), F32)],
        compiler_params=_params("parallel", "parallel"),
        name="s5_scan",
    )(z, w_b, w_c, a, d, h0_re, h0_im)


def _gla_kernel(q_ref, k_ref, v_ref, r_ref, al_ref, wa_ref, ba_ref, gn_ref, s0_ref,
                y_ref, sf_ref, s_scr, *, ntile, groups, chained, width):
    rt = GLA_TILE
    c = rt // groups
    shift = int(math.log2(c))
    row_g = lax.broadcasted_iota(jnp.int32, (rt, rt), 0)
    col_g = lax.broadcasted_iota(jnp.int32, (rt, rt), 1)
    same = (row_g >> shift) == (col_g >> shift)
    causal = same & (row_g >= col_g)
    cum_w = jnp.concatenate([causal.astype(BF16), same.astype(BF16)], axis=0)
    sel_r = lax.broadcasted_iota(jnp.int32, (rt, groups * LANES), 0) >> shift
    sel_c = lax.broadcasted_iota(jnp.int32, (rt, groups * LANES), 1) >> int(math.log2(LANES))
    sel = (sel_r == sel_c).astype(BF16)
    tn_dims = (((0,), (0,)), ((), ()))
    nt_dims = (((1,), (1,)), ((), ()))

    def split3(x):
        hi = x.astype(BF16)
        r1 = x - hi.astype(F32)
        mid = r1.astype(BF16)
        lo = (r1 - mid.astype(F32)).astype(BF16)
        return jnp.concatenate([hi, mid, lo], axis=1)

    if chained:
        s_scr[...] = s0_ref[0, 0]

    def body(it, carry):
        tiles = [it * width + u for u in range(width)]
        rows = [pl.ds(pl.multiple_of(t * rt, rt), rt) for t in tiles]
        log_a = []
        for u in range(width):
            x = jnp.dot(al_ref[rows[u], :].astype(BF16), wa_ref[...], preferred_element_type=F32) + ba_ref[...]
            log_a.append((jnp.minimum(x, 0.0) - jnp.log1p(jnp.exp(-jnp.abs(x)))) * (1.0 / GLA_TAU))
        la3 = [split3(la) for la in log_a]
        cums, e_col = [], []
        for u in range(width):
            cs = jnp.dot(cum_w, la3[u], preferred_element_type=F32)
            cums.append(cs[:, :LANES] + cs[:, LANES:2 * LANES] + cs[:, 2 * LANES:])
            tot = lax.dot_general(la3[u], sel, tn_dims, preferred_element_type=F32)
            e_col.append(jnp.exp(tot[:LANES] + tot[LANES:2 * LANES] + tot[2 * LANES:]))
        qb, kd, v, att, upd = [], [], [], [], []
        for u in range(width):
            b = cums[u][:rt]
            b_end = cums[u][rt:]
            k = k_ref[rows[u], :]
            v.append(v_ref[rows[u], :])
            qb.append(q_ref[rows[u], :] * (GLA_DK ** -0.5) * jnp.exp(b))
            kd.append(k * jnp.exp(b_end - b))
            a = lax.dot_general(qb[u].astype(BF16), (k * jnp.exp(-b)).astype(BF16), nt_dims,
                                preferred_element_type=F32)
            att.append(jnp.where(causal, a, 0.0).astype(BF16))
            upd.append([lax.dot_general(kd[u][g * c:(g + 1) * c].astype(BF16), v[u][g * c:(g + 1) * c].astype(BF16),
                                        tn_dims, preferred_element_type=F32) for g in range(groups)])
        o = [jnp.dot(att[u], v[u].astype(BF16), preferred_element_type=F32) for u in range(width)]
        for u in range(width):
            o_state = []
            for g in range(groups):
                s = s_scr[...] if chained else s0_ref[tiles[u] * groups + g, 0]
                o_state.append(jnp.dot(qb[u][g * c:(g + 1) * c].astype(BF16), s.astype(BF16),
                                       preferred_element_type=F32))
                e = e_col[u][:, g * LANES:(g + 1) * LANES]
                s_new = jnp.concatenate([s[:, :LANES] * e, s[:, LANES:] * e], axis=1) + upd[u][g]
                if chained:
                    s_scr[...] = s_new
                else:
                    sf_ref[tiles[u] * groups + g, 0] = s_new
            o[u] = o[u] + (o_state[0] if groups == 1 else jnp.concatenate(o_state, axis=0))
        for u in range(width):
            y = o[u] * lax.rsqrt(jnp.mean(o[u] * o[u], axis=-1, keepdims=True) + RMS_EPS)
            y = y * gn_ref[...]
            r = r_ref[rows[u], :]
            y_ref[rows[u], :] = y * (r * jax.nn.sigmoid(r))
        return carry

    lax.fori_loop(0, ntile // width, body, 0)
    if chained:
        sf_ref[0, 0] = s_scr[...]


def _gla(z, alow, w_a2, b_a, g_norm, s0, *, batch, seq, chained):
    chunk = math.gcd(seq, GLA_CHUNK)
    if chained:
        assert chunk == GLA_TILE
        nb, groups, width = 1, 1, 4
    else:
        assert GLA_TILE % seq == 0 and chunk == seq
        nb, groups, width = 32, GLA_TILE // seq, 2
    rows = nb * seq
    kern = functools.partial(_gla_kernel, ntile=rows // GLA_TILE, groups=groups, chained=chained, width=width)
    state_spec = pl.BlockSpec((nb, 1, GLA_DK, GLA_DV), lambda i, h: (i, h, 0, 0))
    return pl.pallas_call(
        kern, grid=(batch // nb, GLA_HEADS),
        in_specs=[pl.BlockSpec((rows, GLA_DK), lambda i, h: (i, COL_Q // GLA_DK + h)),
                  pl.BlockSpec((rows, GLA_DK), lambda i, h: (i, COL_K // GLA_DK + h)),
                  pl.BlockSpec((rows, GLA_DV), lambda i, h: (i, COL_V // GLA_DV + h)),
                  pl.BlockSpec((rows, GLA_DV), lambda i, h: (i, COL_R // GLA_DV + h)),
                  pl.BlockSpec((rows, LANES), lambda i, h: (i, 0)),
                  pl.BlockSpec((LANES, GLA_DK), lambda i, h: (0, h)),
                  pl.BlockSpec((1, GLA_DK), lambda i, h: (0, h)),
                  pl.BlockSpec((1, GLA_DV), lambda i, h: (0, h)),
                  state_spec],
        out_specs=[pl.BlockSpec((rows, GLA_DV), lambda i, h: (i, h)), state_spec],
        out_shape=[jax.ShapeDtypeStruct((batch * seq, GLA_HEADS * GLA_DV), F32),
                   jax.ShapeDtypeStruct((batch, GLA_HEADS, GLA_DK, GLA_DV), F32)],
        scratch_shapes=[pltpu.VMEM((GLA_DK, GLA_DV), F32)],
        compiler_params=_params("parallel", "parallel"),
        name="gla",
    )(z, z, z, z, alow, w_a2, b_a, g_norm, s0)


def _attn_kernel(q_ref, k_ref, v_ref, o_ref, *, nb, seq, heads):
    nt_dims = (((1,), (1,)), ((), ()))

    def body(bi, carry):
        rows = pl.ds(pl.multiple_of(bi * seq, SUBLANES), seq)
        for h in range(heads):
            cols = slice(h * XA_HEAD_DIM, (h + 1) * XA_HEAD_DIM)
            q = q_ref[rows, cols].astype(BF16)
            k = k_ref[bi, :, cols].astype(BF16)
            s = lax.dot_general(q, k, nt_dims, preferred_element_type=F32) * (XA_HEAD_DIM ** -0.5)
            p = jnp.exp(s - jnp.max(s, axis=-1, keepdims=True))
            p = p / jnp.sum(p, axis=-1, keepdims=True)
            o_ref[rows, cols] = jnp.dot(p.astype(BF16), v_ref[bi, :, cols].astype(BF16),
                                        preferred_element_type=F32)
        return carry

    lax.fori_loop(0, nb, body, 0)


def _attn_prompt(z, mem_k, mem_v, *, batch, seq, rc):
    nrc = seq // rc
    kern = functools.partial(_attn_kernel, nb=1, seq=rc, heads=1)
    kv_spec = pl.BlockSpec((1, MEM_LEN, XA_HEAD_DIM), lambda b, h, c: (b, 0, h))
    return pl.pallas_call(
        kern, grid=(batch, XA_HEADS, nrc),
        in_specs=[pl.BlockSpec((rc, XA_HEAD_DIM), lambda b, h, c: (b * nrc + c, COL_QX // XA_HEAD_DIM + h)),
                  kv_spec, kv_spec],
        out_specs=pl.BlockSpec((rc, XA_HEAD_DIM), lambda b, h, c: (b * nrc + c, h)),
        out_shape=jax.ShapeDtypeStruct((batch * seq, XA_WIDTH), F32),
        compiler_params=_params("parallel", "parallel", "parallel"),
        name="attn_prompt",
    )(z, mem_k, mem_v)


XA_HALF = XA_HEAD_DIM // 2
XA_ROWS = 2 * XA_HEADS


def _attn_cache_kernel(q_ref, k_ref, v_ref, o_ref, *, nb, seq, width):
    nt_dims = (((1,), (1,)), ((), ()))

    def half_rows(ref, bi, h, half):
        return ref[bi, pl.ds(half * XA_HEADS + h, MEM_LEN, stride=XA_ROWS), :].astype(BF16)

    def body(it, carry):
        work = [(it * width + u, h) for u in range(width) for h in range(XA_HEADS)]
        scores = []
        for bi, h in work:
            rows = pl.ds(pl.multiple_of(bi * seq, SUBLANES), seq)
            lo = h * XA_HEAD_DIM
            s = lax.dot_general(q_ref[rows, lo:lo + XA_HALF].astype(BF16), half_rows(k_ref, bi, h, 0),
                                nt_dims, preferred_element_type=F32)
            s = s + lax.dot_general(q_ref[rows, lo + XA_HALF:lo + XA_HEAD_DIM].astype(BF16),
                                    half_rows(k_ref, bi, h, 1), nt_dims, preferred_element_type=F32)
            scores.append(s * (XA_HEAD_DIM ** -0.5))
        probs = []
        for s in scores:
            p = jnp.exp(s - jnp.max(s, axis=-1, keepdims=True))
            probs.append((p / jnp.sum(p, axis=-1, keepdims=True)).astype(BF16))
        for (bi, h), p in zip(work, probs):
            rows = pl.ds(pl.multiple_of(bi * seq, SUBLANES), seq)
            lo = h * XA_HEAD_DIM
            for half in range(2):
                o_ref[rows, lo + half * XA_HALF:lo + (half + 1) * XA_HALF] = jnp.dot(
                    p, half_rows(v_ref, bi, h, half), preferred_element_type=F32)
        return carry

    lax.fori_loop(0, nb // width, body, 0)


def _cache_rows(cache):
    bs = cache.shape[0]
    c = cache.reshape(bs, MEM_LEN, XA_HEADS, 2, XA_HALF).transpose(0, 1, 3, 2, 4)
    return c.reshape(bs, MEM_LEN * XA_ROWS, XA_HALF)


def _attn_sample(z, mem_k, mem_v, *, batch, seq, nb):
    rows = nb * seq
    kern = functools.partial(_attn_cache_kernel, nb=nb, seq=seq, width=2)
    kv_spec = pl.BlockSpec((nb, MEM_LEN * XA_ROWS, XA_HALF), lambda i: (i, 0, 0))
    return pl.pallas_call(
        kern, grid=(batch // nb,),
        in_specs=[pl.BlockSpec((rows, XA_WIDTH), lambda i: (i, COL_QX // XA_WIDTH)), kv_spec, kv_spec],
        out_specs=pl.BlockSpec((rows, XA_WIDTH), lambda i: (i, 0)),
        out_shape=jax.ShapeDtypeStruct((batch * seq, XA_WIDTH), F32),
        compiler_params=_params("parallel"),
        name="attn_sample",
    )(z, mem_k, mem_v)


def _merge_kernel(ys_ref, yg_ref, yx_ref, g0_ref, g1_ref, g2_ref, wglu_ref, bglu_ref,
                  w0_ref, w1_ref, w2_ref, o_ref, s5_scr, gla_scr, xa_scr):
    @pl.when(pl.program_id(1) == 0)
    def _():
        y = ys_ref[...]
        lin = jnp.dot(y.astype(BF16), wglu_ref[...], preferred_element_type=F32) + bglu_ref[...]
        s5_scr[...] = (y * jax.nn.sigmoid(lin)).astype(BF16)
        gla_scr[...] = yg_ref[...].astype(BF16)
        xa_scr[...] = yx_ref[...].astype(BF16)

    j = pl.program_id(1)
    m = jax.nn.sigmoid(g0_ref[...]) * jnp.dot(s5_scr[...], w0_ref[j], preferred_element_type=F32)
    m = m + jax.nn.sigmoid(g1_ref[...]) * jnp.dot(gla_scr[...], w1_ref[j], preferred_element_type=F32)
    m = m + jax.nn.sigmoid(g2_ref[...]) * jnp.dot(xa_scr[...], w2_ref[j], preferred_element_type=F32)
    o_ref[...] = m.astype(o_ref.dtype)


def _merge(y_s5, y_gla, y_x, z, w_glu, b_glu, w_br_s5, w_br_gla, w_br_x, *, tm, tn):
    t = y_s5.shape[0]
    nj = D_MODEL // tn
    once = pl.Buffered(1)
    wide = pl.BlockSpec((tm, S5_WIDTH), lambda i, j: (i, 0))
    gate = lambda b: pl.BlockSpec((tm, tn), lambda i, j: (i, (COL_GATE + b * D_MODEL) // tn + j))
    w_br = pl.BlockSpec((nj, S5_WIDTH, tn), lambda i, j: (0, 0, 0), pipeline_mode=once)
    return pl.pallas_call(
        _merge_kernel, grid=(t // tm, nj),
        in_specs=[wide, wide, wide, gate(0), gate(1), gate(2),
                  pl.BlockSpec((S5_WIDTH, S5_WIDTH), lambda i, j: (0, 0), pipeline_mode=once),
                  pl.BlockSpec((1, S5_WIDTH), lambda i, j: (0, 0)),
                  w_br, w_br, w_br],
        out_specs=pl.BlockSpec((tm, tn), lambda i, j: (i, j)),
        out_shape=jax.ShapeDtypeStruct((t, D_MODEL), BF16),
        scratch_shapes=[pltpu.VMEM((tm, S5_WIDTH), BF16)] * 3,
        compiler_params=_params("parallel", "arbitrary"),
        name="merge",
    )(y_s5, y_gla, y_x, z, z, z, w_glu, b_glu, w_br_s5, w_br_gla, w_br_x)


def _out_proj_kernel(m_ref, w_ref, x_ref, o_ref):
    o_ref[...] = x_ref[...] + jnp.dot(m_ref[...], w_ref[...], preferred_element_type=F32)


def _out_proj(merged, w_out, x, *, tm, tn):
    t = x.shape[0]
    return pl.pallas_call(
        _out_proj_kernel, grid=(t // tm, D_MODEL // tn),
        in_specs=[pl.BlockSpec((tm, D_MODEL), lambda i, j: (i, 0)),
                  pl.BlockSpec((D_MODEL, tn), lambda i, j: (0, j)),
                  pl.BlockSpec((tm, tn), lambda i, j: (i, j))],
        out_specs=pl.BlockSpec((tm, tn), lambda i, j: (i, j)),
        out_shape=jax.ShapeDtypeStruct((t, D_MODEL), F32),
        compiler_params=_params("parallel", "parallel"),
        name="out_proj",
    )(merged, w_out, x)


def _ffn_kernel(x_ref, gf_ref, wg_ref, wu_ref, wd_ref, gl_ref, o_ref, h_scr, acc_scr):
    k = pl.program_id(1)

    @pl.when(k == 0)
    def _():
        h_scr[...] = _rms(x_ref[...], gf_ref[...]).astype(BF16)
        acc_scr[...] = jnp.zeros_like(acc_scr)

    h = h_scr[...]
    gate = jnp.dot(h, wg_ref[...], preferred_element_type=F32)
    up = jnp.dot(h, wu_ref[...], preferred_element_type=F32)
    act = (gate * jax.nn.sigmoid(gate) * up).astype(BF16)
    acc_scr[...] += jnp.dot(act, wd_ref[...], preferred_element_type=F32)

    @pl.when(k == pl.num_programs(1) - 1)
    def _():
        o_ref[...] = _rms(x_ref[...] + acc_scr[...], gl_ref[...])


def _ffn(x, g_ffn, w_gate, w_up, w_down, g_final, *, tm, th):
    t = x.shape[0]
    row = pl.BlockSpec((tm, D_MODEL), lambda i, k: (i, 0))
    vec = pl.BlockSpec((1, D_MODEL), lambda i, k: (0, 0))
    w_in = pl.BlockSpec((D_MODEL, th), lambda i, k: (0, k))
    return pl.pallas_call(
        _ffn_kernel, grid=(t // tm, FFN_HIDDEN // th),
        in_specs=[row, vec, w_in, w_in, pl.BlockSpec((th, D_MODEL), lambda i, k: (k, 0)), vec],
        out_specs=row,
        out_shape=jax.ShapeDtypeStruct((t, D_MODEL), F32),
        scratch_shapes=[pltpu.VMEM((tm, D_MODEL), BF16), pltpu.VMEM((tm, D_MODEL), F32)],
        compiler_params=_params("parallel", "arbitrary"),
        name="ffn",
    )(x, g_ffn, w_gate, w_up, w_down, g_final)


def _reorder_in_proj(w_in):
    u_to_r = w_in[:, 0:4096]
    a_low = w_in[:, 4096:4112]
    q_x = w_in[:, 4112:5136]
    gates = w_in[:, 5136:11280]
    w_main = jnp.concatenate([u_to_r, gates, q_x], axis=1).astype(BF16)
    w_alow = jnp.pad(a_low, ((0, 0), (0, LANES - GLA_RANK))).astype(BF16)
    return w_main, w_alow


def _col_tiles(w, tn):
    k, n = w.shape
    return w.reshape(k, n // tn, tn).transpose(1, 0, 2)


def _layer(x, mem_k, mem_v, s5_re0, s5_im0, gla_s0, w, *, batch, seq, segmented):
    z, alow = _norm_matmul(x, w['norm_mix'], w['w_main'], w['w_alow'], tm=1024, tn=1024)
    y_s5, hf_re, hf_im = _s5_scan(z, w['s5_wb'], w['s5_wc'], w['s5_a'], w['s5_d'], s5_re0, s5_im0,
                                  batch=batch, seq=seq, segmented=segmented)
    y_gla, gla_s = _gla(z, alow, w['gla_w_a2'], w['gla_b_a'], w['gla_norm'], gla_s0,
                        batch=batch, seq=seq, chained=segmented)
    if segmented:
        y_x = _attn_prompt(z, mem_k, mem_v, batch=batch, seq=seq, rc=512)
    else:
        y_x = _attn_sample(z, mem_k, mem_v, batch=batch, seq=seq, nb=8)
    merged = _merge(y_s5, y_gla, y_x, z, w['s5_w_glu'], w['s5_b_glu'],
                    w['w_br_s5'], w['w_br_gla'], w['w_br_xattn'], tm=512, tn=512)
    x1 = _out_proj(merged, w['w_out'], x, tm=1024, tn=512)
    y = _ffn(x1, w['norm_ffn'], w['w_ffn_gate'], w['w_ffn_up'], w['w_ffn_down'], w['norm_final'],
             tm=512, th=512)
    return y, hf_re, hf_im, gla_s


def kernel(x_prompt, x_sample, mem_prompt, state_s5_re, state_s5_im, state_gla, cache_mem_k, cache_mem_v,
           norm_mix, w_in, s5_lam_re, s5_lam_im, s5_log_dt, s5_b_re, s5_b_im, s5_c_re, s5_c_im,
           s5_d, s5_w_glu, s5_b_glu, gla_w_a2, gla_b_a, gla_norm, mem_norm, w_mem_k, w_mem_v,
           w_br_s5, w_br_gla, w_br_xattn, w_out, norm_ffn, w_ffn_gate, w_ffn_up, w_ffn_down, norm_final):
    depth = w_in.shape[0]
    assert depth == 1
    bp, sp, d = x_prompt.shape
    bs, ss, _ = x_sample.shape
    n_state = S5_GROUPS * S5_STATE
    row = lambda v: v.reshape(1, -1)

    l = 0
    w_main, w_alow = _reorder_in_proj(w_in[l])
    a_re, a_im, bb_re, bb_im = _s5_discretise(s5_lam_re[l], s5_lam_im[l], s5_log_dt[l], s5_b_re[l], s5_b_im[l])
    s5_wb, s5_wc, s5_a, s5_dd = _s5_block_weights(a_re, a_im, bb_re, bb_im, s5_c_re[l], s5_c_im[l], s5_d[l])
    w = {
        'norm_mix': row(norm_mix[l]), 'w_main': w_main, 'w_alow': w_alow,
        's5_wb': s5_wb, 's5_wc': s5_wc, 's5_a': s5_a, 's5_d': s5_dd,
        's5_w_glu': s5_w_glu[l].astype(BF16), 's5_b_glu': row(s5_b_glu[l]),
        'gla_w_a2': jnp.pad(gla_w_a2[l], ((0, LANES - GLA_RANK), (0, 0))).astype(BF16),
        'gla_b_a': row(gla_b_a[l]), 'gla_norm': row(gla_norm[l]),
        'w_br_s5': _col_tiles(w_br_s5[l].astype(BF16), 512),
        'w_br_gla': _col_tiles(w_br_gla[l].astype(BF16), 512),
        'w_br_xattn': _col_tiles(w_br_xattn[l].astype(BF16), 512), 'w_out': w_out[l].astype(BF16),
        'norm_ffn': row(norm_ffn[l]), 'w_ffn_gate': w_ffn_gate[l].astype(BF16),
        'w_ffn_up': w_ffn_up[l].astype(BF16), 'w_ffn_down': w_ffn_down[l].astype(BF16),
        'norm_final': row(norm_final),
    }

    w_mem = jnp.concatenate([w_mem_k[l], w_mem_v[l]], axis=1).astype(BF16)
    mem_kv = _norm_matmul(mem_prompt.reshape(bp * MEM_LEN, d), row(mem_norm[l]), w_mem, tm=512, tn=1024)
    mk = mem_kv[:, :XA_WIDTH].reshape(bp, MEM_LEN, XA_WIDTH)
    mv = mem_kv[:, XA_WIDTH:].reshape(bp, MEM_LEN, XA_WIDTH)
    zero_s5 = jnp.zeros((bp, 1, n_state), F32)
    zero_gla = jnp.zeros((bp, GLA_HEADS, GLA_DK, GLA_DV), F32)
    yp, p_re, p_im, p_gla = _layer(x_prompt.reshape(bp * sp, d), mk, mv, zero_s5, zero_s5, zero_gla, w,
                                   batch=bp, seq=sp, segmented=True)

    ys, s_re, s_im, s_gla = _layer(x_sample.reshape(bs * ss, d),
                                   _cache_rows(cache_mem_k[l]), _cache_rows(cache_mem_v[l]),
                                   state_s5_re[l].reshape(1, bs, n_state), state_s5_im[l].reshape(1, bs, n_state),
                                   state_gla[l], w, batch=bs, seq=ss, segmented=False)

    s5_shape_p = (1, bp, S5_GROUPS, S5_STATE)
    s5_shape_s = (1, bs, S5_GROUPS, S5_STATE)
    kv_shape = (1, bp, MEM_LEN, XA_HEADS, XA_HEAD_DIM)
    return (yp.reshape(bp, sp, d), ys.reshape(bs, ss, d),
            p_re.reshape(s5_shape_p), p_im.reshape(s5_shape_p), p_gla[None],
            mk.reshape(kv_shape), mv.reshape(kv_shape),
            s_re.reshape(s5_shape_s), s_im.reshape(s5_shape_s), s_gla[None])
```

```python
import functools
import math

import jax
import jax.numpy as jnp
from jax import lax
from jax.experimental import pallas as pl
from jax.experimental.pallas import tpu as pltpu

F32 = jnp.float32
BF16 = jnp.bfloat16

D_MODEL = 2048
S5_WIDTH = 1024
S5_GROUP = 16
S5_GROUPS = 64
S5_STATE = 64
GLA_HEADS = 4
GLA_DK = 128
GLA_DV = 256
GLA_RANK = 16
GLA_TAU = 16.0
GLA_CHUNK = 64
XA_HEADS = 4
XA_HEAD_DIM = 256
XA_WIDTH = 1024
MEM_LEN = 256
FFN_HIDDEN = 5632
RMS_EPS = 1e-6

LANES = 128
SUBLANES = 8
VMEM_LIMIT = 56 * 1024 * 1024

COL_U = 0
COL_Q = 1024
COL_K = 1536
COL_V = 2048
COL_R = 3072
COL_GATE = 4096
COL_QX = 10240
Z_WIDTH = 11264

S5_GB = 8
S5_CH = S5_GB * S5_STATE
S5_NBLK = S5_GROUPS // S5_GB
S5_SEGS = 8
GLA_TILE = 64


def _params(*sem):
    return pltpu.CompilerParams(dimension_semantics=sem, vmem_limit_bytes=VMEM_LIMIT)


def _rms(x, g):
    return x * lax.rsqrt(jnp.mean(x * x, axis=-1, keepdims=True) + RMS_EPS) * g


def _norm_matmul_kernel(x_ref, g_ref, w_ref, o_ref, h_ref):
    @pl.when(pl.program_id(1) == 0)
    def _():
        h_ref[...] = _rms(x_ref[...], g_ref[...]).astype(BF16)

    o_ref[...] = jnp.dot(h_ref[...], w_ref[...], preferred_element_type=F32)


def _norm_matmul2_kernel(x_ref, g_ref, w_ref, ws_ref, o_ref, os_ref, h_ref):
    @pl.when(pl.program_id(1) == 0)
    def _():
        h = _rms(x_ref[...], g_ref[...]).astype(BF16)
        h_ref[...] = h
        os_ref[...] = jnp.dot(h, ws_ref[...], preferred_element_type=F32)

    o_ref[...] = jnp.dot(h_ref[...], w_ref[...], preferred_element_type=F32)


def _norm_matmul(x, g, w, w_small=None, *, tm, tn):
    t, d = x.shape
    n = w.shape[1]
    grid = (t // tm, n // tn)
    in_specs = [pl.BlockSpec((tm, d), lambda i, j: (i, 0)),
                pl.BlockSpec((1, d), lambda i, j: (0, 0)),
                pl.BlockSpec((d, tn), lambda i, j: (0, j))]
    out_specs = pl.BlockSpec((tm, tn), lambda i, j: (i, j))
    out_shape = jax.ShapeDtypeStruct((t, n), F32)
    scratch = [pltpu.VMEM((tm, d), BF16)]
    if w_small is None:
        return pl.pallas_call(_norm_matmul_kernel, grid=grid, in_specs=in_specs, out_specs=out_specs,
                              out_shape=out_shape, scratch_shapes=scratch,
                              compiler_params=_params("parallel", "arbitrary"),
                              name="norm_matmul")(x, g, w)
    ns = w_small.shape[1]
    in_specs.append(pl.BlockSpec((d, ns), lambda i, j: (0, 0)))
    return pl.pallas_call(_norm_matmul2_kernel, grid=grid, in_specs=in_specs,
                          out_specs=[out_specs, pl.BlockSpec((tm, ns), lambda i, j: (i, 0))],
                          out_shape=[out_shape, jax.ShapeDtypeStruct((t, ns), F32)],
                          scratch_shapes=scratch,
                          compiler_params=_params("parallel", "arbitrary"),
                          name="in_proj")(x, g, w, w_small)


def _s5_disc_kernel(lr_ref, li_ref, ldt_ref, br_ref, bi_ref, ar_ref, ai_ref, bbr_ref, bbi_ref):
    lam_re = lr_ref[...]
    lam_im = li_ref[...]
    dt = jnp.exp(ldt_ref[...])
    mag = jnp.exp(lam_re * dt)
    a_re = mag * jnp.cos(lam_im * dt)
    a_im = mag * jnp.sin(lam_im * dt)
    den = lam_re * lam_re + lam_im * lam_im
    coef_re = ((a_re - 1.0) * lam_re + a_im * lam_im) / den
    coef_im = (a_im * lam_re - (a_re - 1.0) * lam_im) / den
    b_re = br_ref[...]
    b_im = bi_ref[...]
    ar_ref[...] = a_re
    ai_ref[...] = a_im
    bbr_ref[...] = coef_re * b_re - coef_im * b_im
    bbi_ref[...] = coef_re * b_im + coef_im * b_re


def _s5_discretise(lam_re, lam_im, log_dt, b_re, b_im):
    n = S5_GROUPS * S5_STATE
    col = lambda a: a.reshape(n, 1)
    ldt = jnp.broadcast_to(log_dt[:, None], (S5_GROUPS, S5_STATE))
    outs = pl.pallas_call(
        _s5_disc_kernel,
        out_shape=[jax.ShapeDtypeStruct((n, 1), F32), jax.ShapeDtypeStruct((n, 1), F32),
                   jax.ShapeDtypeStruct((n, S5_GROUP), F32), jax.ShapeDtypeStruct((n, S5_GROUP), F32)],
        compiler_params=pltpu.CompilerParams(vmem_limit_bytes=VMEM_LIMIT),
        name="s5_discretise",
    )(col(lam_re), col(lam_im), col(ldt), b_re.reshape(n, S5_GROUP), b_im.reshape(n, S5_GROUP))
    return outs


def _s5_block_weights(a_re, a_im, bb_re, bb_im, c_re, c_im, d_skip):
    eye = jnp.eye(S5_GB, dtype=F32)

    def b_blk(bb):
        bb = bb.reshape(S5_NBLK, S5_GB, S5_STATE, S5_GROUP)
        return jnp.einsum('kgpc,gh->kgchp', bb, eye).reshape(S5_NBLK, S5_GB * S5_GROUP, S5_CH)

    def c_blk(c):
        c = c.reshape(S5_NBLK, S5_GB, S5_GROUP, S5_STATE)
        return jnp.einsum('kgcp,gh->kgphc', c, eye).reshape(S5_NBLK, S5_CH, S5_GB * S5_GROUP)

    w_b = jnp.concatenate([b_blk(bb_re), b_blk(bb_im)], axis=-1).astype(BF16)
    w_c = jnp.concatenate([c_blk(c_re), -c_blk(c_im)], axis=1).astype(BF16)
    a = jnp.concatenate([a_re.reshape(S5_NBLK, 1, S5_CH), a_im.reshape(S5_NBLK, 1, S5_CH)], axis=-1)
    d = d_skip.reshape(S5_NBLK, 1, S5_GB * S5_GROUP)
    return w_b, w_c, a, d


def _s5_scan_kernel(u_ref, wb_ref, wc_ref, a_ref, d_ref, h0r_ref, h0i_ref,
                    y_ref, hfr_ref, hfi_ref, up_ref, bu_ref, yp_ref,
                    *, nseq, slen, segmented, lane_chunk):
    ch = S5_CH
    rows = nseq * slen

    def slab(t):
        return pl.ds(pl.multiple_of(t * nseq, nseq), nseq)

    if slen <= 8:
        for t in range(slen):
            up_ref[t * nseq:(t + 1) * nseq, :] = u_ref[pl.ds(t, nseq, stride=slen), :]
    else:
        def perm_in(t, c):
            up_ref[slab(t), :] = u_ref[pl.ds(t, nseq, stride=slen), :]
            return c
        lax.fori_loop(0, slen, perm_in, 0)

    up = up_ref[...]
    bu_ref[...] = jnp.dot(up.astype(BF16), wb_ref[0], preferred_element_type=F32)

    def run_scan(lo, h_re, h_im, store):
        w = h_re.shape[1]
        a_re = jnp.broadcast_to(a_ref[0, :, lo:lo + w], (nseq, w))
        a_im = jnp.broadcast_to(a_ref[0, :, ch + lo:ch + lo + w], (nseq, w))

        def step(t, carry):
            hr, hi = carry
            br = bu_ref[slab(t), lo:lo + w]
            bi = bu_ref[slab(t), ch + lo:ch + lo + w]
            nr = a_re * hr - a_im * hi + br
            ni = a_re * hi + a_im * hr + bi
            if store:
                bu_ref[slab(t), lo:lo + w] = nr
                bu_ref[slab(t), ch + lo:ch + lo + w] = ni
            return nr, ni

        if slen <= 8:
            carry = (h_re, h_im)
            for t in range(slen):
                carry = step(t, carry)
            return carry
        return lax.fori_loop(0, slen, step, (h_re, h_im), unroll=4)

    for lo in range(0, ch, lane_chunk):
        w = lane_chunk
        if segmented:
            zero = jnp.zeros((nseq, w), F32)
            e_re, e_im = run_scan(lo, zero, zero, store=False)
            p_re = a_ref[0, :, lo:lo + w]
            p_im = a_ref[0, :, ch + lo:ch + lo + w]
            for _ in range(int(math.log2(slen))):
                p_re, p_im = p_re * p_re - p_im * p_im, 2.0 * (p_re * p_im)
            cur_re = h0r_ref[0, :, lo:lo + w]
            cur_im = h0i_ref[0, :, lo:lo + w]
            init_re, init_im = [], []
            for s in range(nseq):
                init_re.append(cur_re)
                init_im.append(cur_im)
                nxt_re = p_re * cur_re - p_im * cur_im + e_re[s:s + 1, :]
                nxt_im = p_re * cur_im + p_im * cur_re + e_im[s:s + 1, :]
                cur_re, cur_im = nxt_re, nxt_im
            hfr_ref[0, :, lo:lo + w] = cur_re
            hfi_ref[0, :, lo:lo + w] = cur_im
            run_scan(lo, jnp.concatenate(init_re, axis=0), jnp.concatenate(init_im, axis=0), store=True)
        else:
            f_re, f_im = run_scan(lo, h0r_ref[0, :, lo:lo + w], h0i_ref[0, :, lo:lo + w], store=True)
            hfr_ref[0, :, lo:lo + w] = f_re
            hfi_ref[0, :, lo:lo + w] = f_im

    y = jnp.dot(bu_ref[...].astype(BF16), wc_ref[0], preferred_element_type=F32) + d_ref[0] * up
    yp_ref[...] = jax.nn.gelu(y, approximate=True)

    if slen <= 8:
        for t in range(slen):
            y_ref[pl.ds(t, nseq, stride=slen), :] = yp_ref[t * nseq:(t + 1) * nseq, :]
    else:
        def perm_out(t, c):
            y_ref[pl.ds(t, nseq, stride=slen), :] = yp_ref[slab(t), :]
            return c
        lax.fori_loop(0, slen, perm_out, 0)


def _s5_scan(z, w_b, w_c, a, d, h0_re, h0_im, *, batch, seq, segmented):
    if segmented:
        nseq, slen, nb, ns, lane_chunk = S5_SEGS, seq // S5_SEGS, batch, 1, S5_CH
        assert slen & (slen - 1) == 0
    else:
        nseq, slen, nb, ns, lane_chunk = batch, seq, 1, batch, LANES
    rows = nseq * slen
    gw = S5_GB * S5_GROUP
    kern = functools.partial(_s5_scan_kernel, nseq=nseq, slen=slen, segmented=segmented, lane_chunk=lane_chunk)
    state_spec = pl.BlockSpec((1, ns, S5_CH), lambda b, k: (b, 0, k))
    state_shape = jax.ShapeDtypeStruct((nb, ns, S5_GROUPS * S5_STATE), F32)
    return pl.pallas_call(
        kern, grid=(nb, S5_NBLK),
        in_specs=[pl.BlockSpec((rows, gw), lambda b, k: (b, COL_U // gw + k)),
                  pl.BlockSpec((1, gw, 2 * S5_CH), lambda b, k: (k, 0, 0)),
                  pl.BlockSpec((1, 2 * S5_CH, gw), lambda b, k: (k, 0, 0)),
                  pl.BlockSpec((1, 1, 2 * S5_CH), lambda b, k: (k, 0, 0)),
                  pl.BlockSpec((1, 1, gw), lambda b, k: (k, 0, 0)),
                  state_spec, state_spec],
        out_specs=[pl.BlockSpec((rows, gw), lambda b, k: (b, k)), state_spec, state_spec],
        out_shape=[jax.ShapeDtypeStruct((nb * rows, S5_WIDTH), F32), state_shape, state_shape],
        scratch_shapes=[pltpu.VMEM((rows, gw), F32), pltpu.VMEM((rows, 2 * S5_CH), F32),
                        pltpu.VMEM((rows, gw), F32)],
        compiler_params=_params("parallel", "parallel"),
        name="s5_scan",
    )(z, w_b, w_c, a, d, h0_re, h0_im)


def _gla_kernel(q_ref, k_ref, v_ref, r_ref, al_ref, wa_ref, ba_ref, gn_ref, s0_ref,
                y_ref, sf_ref, s_scr, *, ntile, groups, chained, width):
    rt = GLA_TILE
    c = rt // groups
    shift = int(math.log2(c))
    row_g = lax.broadcasted_iota(jnp.int32, (rt, rt), 0)
    col_g = lax.broadcasted_iota(jnp.int32, (rt, rt), 1)
    same = (row_g >> shift) == (col_g >> shift)
    causal = same & (row_g >= col_g)
    cum_w = jnp.concatenate([causal.astype(BF16), same.astype(BF16)], axis=0)
    sel_r = lax.broadcasted_iota(jnp.int32, (rt, groups * LANES), 0) >> shift
    sel_c = lax.broadcasted_iota(jnp.int32, (rt, groups * LANES), 1) >> int(math.log2(LANES))
    sel = (sel_r == sel_c).astype(BF16)
    tn_dims = (((0,), (0,)), ((), ()))
    nt_dims = (((1,), (1,)), ((), ()))

    def split3(x):
        hi = x.astype(BF16)
        r1 = x - hi.astype(F32)
        mid = r1.astype(BF16)
        lo = (r1 - mid.astype(F32)).astype(BF16)
        return jnp.concatenate([hi, mid, lo], axis=1)

    if chained:
        s_scr[...] = s0_ref[0, 0]

    def body(it, carry):
        tiles = [it * width + u for u in range(width)]
        rows = [pl.ds(pl.multiple_of(t * rt, rt), rt) for t in tiles]
        log_a = []
        for u in range(width):
            x = jnp.dot(al_ref[rows[u], :].astype(BF16), wa_ref[...], preferred_element_type=F32) + ba_ref[...]
            log_a.append((jnp.minimum(x, 0.0) - jnp.log1p(jnp.exp(-jnp.abs(x)))) * (1.0 / GLA_TAU))
        la3 = [split3(la) for la in log_a]
        cums, e_col = [], []
        for u in range(width):
            cs = jnp.dot(cum_w, la3[u], preferred_element_type=F32)
            cums.append(cs[:, :LANES] + cs[:, LANES:2 * LANES] + cs[:, 2 * LANES:])
            tot = lax.dot_general(la3[u], sel, tn_dims, preferred_element_type=F32)
            e_col.append(jnp.exp(tot[:LANES] + tot[LANES:2 * LANES] + tot[2 * LANES:]))
        qb, kd, v, att, upd = [], [], [], [], []
        for u in range(width):
            b = cums[u][:rt]
            b_end = cums[u][rt:]
            k = k_ref[rows[u], :]
            v.append(v_ref[rows[u], :])
            qb.append(q_ref[rows[u], :] * (GLA_DK ** -0.5) * jnp.exp(b))
            kd.append(k * jnp.exp(b_end - b))
            a = lax.dot_general(qb[u].astype(BF16), (k * jnp.exp(-b)).astype(BF16), nt_dims,
                                preferred_element_type=F32)
            att.append(jnp.where(causal, a, 0.0).astype(BF16))
            upd.append([lax.dot_general(kd[u][g * c:(g + 1) * c].astype(BF16), v[u][g * c:(g + 1) * c].astype(BF16),
                                        tn_dims, preferred_element_type=F32) for g in range(groups)])
        o = [jnp.dot(att[u], v[u].astype(BF16), preferred_element_type=F32) for u in range(width)]
        for u in range(width):
            o_state = []
            for g in range(groups):
                s = s_scr[...] if chained else s0_ref[tiles[u] * groups + g, 0]
                o_state.append(jnp.dot(qb[u][g * c:(g + 1) * c].astype(BF16), s.astype(BF16),
                                       preferred_element_type=F32))
                e = e_col[u][:, g * LANES:(g + 1) * LANES]
                s_new = jnp.concatenate([s[:, :LANES] * e, s[:, LANES:] * e], axis=1) + upd[u][g]
                if chained:
                    s_scr[...] = s_new
                else:
                    sf_ref[tiles[u] * groups + g, 0] = s_new
            o[u] = o[u] + (o_state[0] if groups == 1 else jnp.concatenate(o_state, axis=0))
        for u in range(width):
            y = o[u] * lax.rsqrt(jnp.mean(o[u] * o[u], axis=-1, keepdims=True) + RMS_EPS)
            y = y * gn_ref[...]
            r = r_ref[rows[u], :]
            y_ref[rows[u], :] = y * (r * jax.nn.sigmoid(r))
        return carry

    lax.fori_loop(0, ntile // width, body, 0)
    if chained:
        sf_ref[0, 0] = s_scr[...]


def _gla(z, alow, w_a2, b_a, g_norm, s0, *, batch, seq, chained):
    chunk = math.gcd(seq, GLA_CHUNK)
    if chained:
        assert chunk == GLA_TILE
        nb, groups, width = 1, 1, 4
    else:
        assert GLA_TILE % seq == 0 and chunk == seq
        nb, groups, width = 32, GLA_TILE // seq, 2
    rows = nb * seq
    kern = functools.partial(_gla_kernel, ntile=rows // GLA_TILE, groups=groups, chained=chained, width=width)
    state_spec = pl.BlockSpec((nb, 1, GLA_DK, GLA_DV), lambda i, h: (i, h, 0, 0))
    return pl.pallas_call(
        kern, grid=(batch // nb, GLA_HEADS),
        in_specs=[pl.BlockSpec((rows, GLA_DK), lambda i, h: (i, COL_Q // GLA_DK + h)),
                  pl.BlockSpec((rows, GLA_DK), lambda i, h: (i, COL_K // GLA_DK + h)),
                  pl.BlockSpec((rows, GLA_DV), lambda i, h: (i, COL_V // GLA_DV + h)),
                  pl.BlockSpec((rows, GLA_DV), lambda i, h: (i, COL_R // GLA_DV + h)),
                  pl.BlockSpec((rows, LANES), lambda i, h: (i, 0)),
                  pl.BlockSpec((LANES, GLA_DK), lambda i, h: (0, h)),
                  pl.BlockSpec((1, GLA_DK), lambda i, h: (0, h)),
                  pl.BlockSpec((1, GLA_DV), lambda i, h: (0, h)),
                  state_spec],
        out_specs=[pl.BlockSpec((rows, GLA_DV), lambda i, h: (i, h)), state_spec],
        out_shape=[jax.ShapeDtypeStruct((batch * seq, GLA_HEADS * GLA_DV), F32),
                   jax.ShapeDtypeStruct((batch, GLA_HEADS, GLA_DK, GLA_DV), F32)],
        scratch_shapes=[pltpu.VMEM((GLA_DK, GLA_DV), F32)],
        compiler_params=_params("parallel", "parallel"),
        name="gla",
    )(z, z, z, z, alow, w_a2, b_a, g_norm, s0)


def _attn_kernel(q_ref, k_ref, v_ref, o_ref, *, nb, seq, heads):
    nt_dims = (((1,), (1,)), ((), ()))

    def body(bi, carry):
        rows = pl.ds(pl.multiple_of(bi * seq, SUBLANES), seq)
        for h in range(heads):
            cols = slice(h * XA_HEAD_DIM, (h + 1) * XA_HEAD_DIM)
            q = q_ref[rows, cols].astype(BF16)
            k = k_ref[bi, :, cols].astype(BF16)
            s = lax.dot_general(q, k, nt_dims, preferred_element_type=F32) * (XA_HEAD_DIM ** -0.5)
            p = jnp.exp(s - jnp.max(s, axis=-1, keepdims=True))
            p = p / jnp.sum(p, axis=-1, keepdims=True)
            o_ref[rows, cols] = jnp.dot(p.astype(BF16), v_ref[bi, :, cols].astype(BF16),
                                        preferred_element_type=F32)
        return carry

    lax.fori_loop(0, nb, body, 0)


def _attn_prompt(z, mem_k, mem_v, *, batch, seq, rc):
    nrc = seq // rc
    kern = functools.partial(_attn_kernel, nb=1, seq=rc, heads=1)
    kv_spec = pl.BlockSpec((1, MEM_LEN, XA_HEAD_DIM), lambda b, h, c: (b, 0, h))
    return pl.pallas_call(
        kern, grid=(batch, XA_HEADS, nrc),
        in_specs=[pl.BlockSpec((rc, XA_HEAD_DIM), lambda b, h, c: (b * nrc + c, COL_QX // XA_HEAD_DIM + h)),
                  kv_spec, kv_spec],
        out_specs=pl.BlockSpec((rc, XA_HEAD_DIM), lambda b, h, c: (b * nrc + c, h)),
        out_shape=jax.ShapeDtypeStruct((batch * seq, XA_WIDTH), F32),
        compiler_params=_params("parallel", "parallel", "parallel"),
        name="attn_prompt",
    )(z, mem_k, mem_v)


XA_HALF = XA_HEAD_DIM // 2
XA_ROWS = 2 * XA_HEADS


def _attn_cache_kernel(q_ref, k_ref, v_ref, o_ref, *, nb, seq, width):
    nt_dims = (((1,), (1,)), ((), ()))

    def half_rows(ref, bi, h, half):
        return ref[bi, pl.ds(half * XA_HEADS + h, MEM_LEN, stride=XA_ROWS), :].astype(BF16)

    def body(it, carry):
        work = [(it * width + u, h) for u in range(width) for h in range(XA_HEADS)]
        scores = []
        for bi, h in work:
            rows = pl.ds(pl.multiple_of(bi * seq, SUBLANES), seq)
            lo = h * XA_HEAD_DIM
            s = lax.dot_general(q_ref[rows, lo:lo + XA_HALF].astype(BF16), half_rows(k_ref, bi, h, 0),
                                nt_dims, preferred_element_type=F32)
            s = s + lax.dot_general(q_ref[rows, lo + XA_HALF:lo + XA_HEAD_DIM].astype(BF16),
                                    half_rows(k_ref, bi, h, 1), nt_dims, preferred_element_type=F32)
            scores.append(s * (XA_HEAD_DIM ** -0.5))
        probs = []
        for s in scores:
            p = jnp.exp(s - jnp.max(s, axis=-1, keepdims=True))
            probs.append((p / jnp.sum(p, axis=-1, keepdims=True)).astype(BF16))
        for (bi, h), p in zip(work, probs):
            rows = pl.ds(pl.multiple_of(bi * seq, SUBLANES), seq)
            lo = h * XA_HEAD_DIM
            for half in range(2):
                o_ref[rows, lo + half * XA_HALF:lo + (half + 1) * XA_HALF] = jnp.dot(
                    p, half_rows(v_ref, bi, h, half), preferred_element_type=F32)
        return carry

    lax.fori_loop(0, nb // width, body, 0)


def _cache_rows(cache):
    bs = cache.shape[0]
    c = cache.reshape(bs, MEM_LEN, XA_HEADS, 2, XA_HALF).transpose(0, 1, 3, 2, 4)
    return c.reshape(bs, MEM_LEN * XA_ROWS, XA_HALF)


def _attn_sample(z, mem_k, mem_v, *, batch, seq, nb):
    rows = nb * seq
    kern = functools.partial(_attn_cache_kernel, nb=nb, seq=seq, width=2)
    kv_spec = pl.BlockSpec((nb, MEM_LEN * XA_ROWS, XA_HALF), lambda i: (i, 0, 0))
    return pl.pallas_call(
        kern, grid=(batch // nb,),
        in_specs=[pl.BlockSpec((rows, XA_WIDTH), lambda i: (i, COL_QX // XA_WIDTH)), kv_spec, kv_spec],
        out_specs=pl.BlockSpec((rows, XA_WIDTH), lambda i: (i, 0)),
        out_shape=jax.ShapeDtypeStruct((batch * seq, XA_WIDTH), F32),
        compiler_params=_params("parallel"),
        name="attn_sample",
    )(z, mem_k, mem_v)


def _merge_kernel(ys_ref, yg_ref, yx_ref, g0_ref, g1_ref, g2_ref, wglu_ref, bglu_ref,
                  w0_ref, w1_ref, w2_ref, o_ref, s5_scr, gla_scr, xa_scr):
    @pl.when(pl.program_id(1) == 0)
    def _():
        y = ys_ref[...]
        lin = jnp.dot(y.astype(BF16), wglu_ref[...], preferred_element_type=F32) + bglu_ref[...]
        s5_scr[...] = (y * jax.nn.sigmoid(lin)).astype(BF16)
        gla_scr[...] = yg_ref[...].astype(BF16)
        xa_scr[...] = yx_ref[...].astype(BF16)

    j = pl.program_id(1)
    m = jax.nn.sigmoid(g0_ref[...]) * jnp.dot(s5_scr[...], w0_ref[j], preferred_element_type=F32)
    m = m + jax.nn.sigmoid(g1_ref[...]) * jnp.dot(gla_scr[...], w1_ref[j], preferred_element_type=F32)
    m = m + jax.nn.sigmoid(g2_ref[...]) * jnp.dot(xa_scr[...], w2_ref[j], preferred_element_type=F32)
    o_ref[...] = m.astype(o_ref.dtype)


def _merge(y_s5, y_gla, y_x, z, w_glu, b_glu, w_br_s5, w_br_gla, w_br_x, *, tm, tn):
    t = y_s5.shape[0]
    nj = D_MODEL // tn
    once = pl.Buffered(1)
    wide = pl.BlockSpec((tm, S5_WIDTH), lambda i, j: (i, 0))
    gate = lambda b: pl.BlockSpec((tm, tn), lambda i, j: (i, (COL_GATE + b * D_MODEL) // tn + j))
    w_br = pl.BlockSpec((nj, S5_WIDTH, tn), lambda i, j: (0, 0, 0), pipeline_mode=once)
    return pl.pallas_call(
        _merge_kernel, grid=(t // tm, nj),
        in_specs=[wide, wide, wide, gate(0), gate(1), gate(2),
                  pl.BlockSpec((S5_WIDTH, S5_WIDTH), lambda i, j: (0, 0), pipeline_mode=once),
                  pl.BlockSpec((1, S5_WIDTH), lambda i, j: (0, 0)),
                  w_br, w_br, w_br],
        out_specs=pl.BlockSpec((tm, tn), lambda i, j: (i, j)),
        out_shape=jax.ShapeDtypeStruct((t, D_MODEL), BF16),
        scratch_shapes=[pltpu.VMEM((tm, S5_WIDTH), BF16)] * 3,
        compiler_params=_params("parallel", "arbitrary"),
        name="merge",
    )(y_s5, y_gla, y_x, z, z, z, w_glu, b_glu, w_br_s5, w_br_gla, w_br_x)


def _out_proj_kernel(m_ref, w_ref, x_ref, o_ref):
    o_ref[...] = x_ref[...] + jnp.dot(m_ref[...], w_ref[...], preferred_element_type=F32)


def _out_proj(merged, w_out, x, *, tm, tn):
    t = x.shape[0]
    return pl.pallas_call(
        _out_proj_kernel, grid=(t // tm, D_MODEL // tn),
        in_specs=[pl.BlockSpec((tm, D_MODEL), lambda i, j: (i, 0)),
                  pl.BlockSpec((D_MODEL, tn), lambda i, j: (0, j)),
                  pl.BlockSpec((tm, tn), lambda i, j: (i, j))],
        out_specs=pl.BlockSpec((tm, tn), lambda i, j: (i, j)),
        out_shape=jax.ShapeDtypeStruct((t, D_MODEL), F32),
        compiler_params=_params("parallel", "parallel"),
        name="out_proj",
    )(merged, w_out, x)


def _ffn_kernel(x_ref, gf_ref, wg_ref, wu_ref, wd_ref, gl_ref, o_ref, h_scr, acc_scr):
    k = pl.program_id(1)

    @pl.when(k == 0)
    def _():
        h_scr[...] = _rms(x_ref[...], gf_ref[...]).astype(BF16)
        acc_scr[...] = jnp.zeros_like(acc_scr)

    h = h_scr[...]
    gate = jnp.dot(h, wg_ref[...], preferred_element_type=F32)
    up = jnp.dot(h, wu_ref[...], preferred_element_type=F32)
    act = (gate * jax.nn.sigmoid(gate) * up).astype(BF16)
    acc_scr[...] += jnp.dot(act, wd_ref[...], preferred_element_type=F32)

    @pl.when(k == pl.num_programs(1) - 1)
    def _():
        o_ref[...] = _rms(x_ref[...] + acc_scr[...], gl_ref[...])


def _ffn(x, g_ffn, w_gate, w_up, w_down, g_final, *, tm, th):
    t = x.shape[0]
    row = pl.BlockSpec((tm, D_MODEL), lambda i, k: (i, 0))
    vec = pl.BlockSpec((1, D_MODEL), lambda i, k: (0, 0))
    w_in = pl.BlockSpec((D_MODEL, th), lambda i, k: (0, k))
    return pl.pallas_call(
        _ffn_kernel, grid=(t // tm, FFN_HIDDEN // th),
        in_specs=[row, vec, w_in, w_in, pl.BlockSpec((th, D_MODEL), lambda i, k: (k, 0)), vec],
        out_specs=row,
        out_shape=jax.ShapeDtypeStruct((t, D_MODEL), F32),
        scratch_shapes=[pltpu.VMEM((tm, D_MODEL), BF16), pltpu.VMEM((tm, D_MODEL), F32)],
        compiler_params=_params("parallel", "arbitrary"),
        name="ffn",
    )(x, g_ffn, w_gate, w_up, w_down, g_final)


def _reorder_in_proj_kernel(w_ref, main_ref, alow_ref):
    main_ref[:, 0:COL_GATE] = w_ref[:, 0:COL_GATE].astype(BF16)
    main_ref[:, COL_GATE:COL_QX] = w_ref[:, 5136:11280].astype(BF16)
    main_ref[:, COL_QX:Z_WIDTH] = w_ref[:, 4112:5136].astype(BF16)
    alow_ref[...] = jnp.zeros_like(alow_ref)
    alow_ref[:, 0:GLA_RANK] = w_ref[:, 4096:4112].astype(BF16)


def _reorder_in_proj(w_in):
    d, n = w_in.shape
    tr = 256
    return pl.pallas_call(
        _reorder_in_proj_kernel, grid=(d // tr,),
        in_specs=[pl.BlockSpec((tr, n), lambda i: (i, 0))],
        out_specs=[pl.BlockSpec((tr, Z_WIDTH), lambda i: (i, 0)), pl.BlockSpec((tr, LANES), lambda i: (i, 0))],
        out_shape=[jax.ShapeDtypeStruct((d, Z_WIDTH), BF16), jax.ShapeDtypeStruct((d, LANES), BF16)],
        compiler_params=_params("parallel"),
        name="reorder_in_proj",
    )(w_in)


def _col_tiles(w, tn):
    k, n = w.shape
    return w.reshape(k, n // tn, tn).transpose(1, 0, 2)


def _layer(x, mem_k, mem_v, s5_re0, s5_im0, gla_s0, w, *, batch, seq, segmented):
    z, alow = _norm_matmul(x, w['norm_mix'], w['w_main'], w['w_alow'], tm=1024, tn=1024)
    y_s5, hf_re, hf_im = _s5_scan(z, w['s5_wb'], w['s5_wc'], w['s5_a'], w['s5_d'], s5_re0, s5_im0,
                                  batch=batch, seq=seq, segmented=segmented)
    y_gla, gla_s = _gla(z, alow, w['gla_w_a2'], w['gla_b_a'], w['gla_norm'], gla_s0,
                        batch=batch, seq=seq, chained=segmented)
    if segmented:
        y_x = _attn_prompt(z, mem_k, mem_v, batch=batch, seq=seq, rc=512)
    else:
        y_x = _attn_sample(z, mem_k, mem_v, batch=batch, seq=seq, nb=8)
    merged = _merge(y_s5, y_gla, y_x, z, w['s5_w_glu'], w['s5_b_glu'],
                    w['w_br_s5'], w['w_br_gla'], w['w_br_xattn'], tm=512, tn=512)
    x1 = _out_proj(merged, w['w_out'], x, tm=1024, tn=512)
    y = _ffn(x1, w['norm_ffn'], w['w_ffn_gate'], w['w_ffn_up'], w['w_ffn_down'], w['norm_final'],
             tm=512, th=512)
    return y, hf_re, hf_im, gla_s


def kernel(x_prompt, x_sample, mem_prompt, state_s5_re, state_s5_im, state_gla, cache_mem_k, cache_mem_v,
           norm_mix, w_in, s5_lam_re, s5_lam_im, s5_log_dt, s5_b_re, s5_b_im, s5_c_re, s5_c_im,
           s5_d, s5_w_glu, s5_b_glu, gla_w_a2, gla_b_a, gla_norm, mem_norm, w_mem_k, w_mem_v,
           w_br_s5, w_br_gla, w_br_xattn, w_out, norm_ffn, w_ffn_gate, w_ffn_up, w_ffn_down, norm_final):
    depth = w_in.shape[0]
    assert depth == 1
    bp, sp, d = x_prompt.shape
    bs, ss, _ = x_sample.shape
    n_state = S5_GROUPS * S5_STATE
    row = lambda v: v.reshape(1, -1)

    l = 0
    w_main, w_alow = _reorder_in_proj(w_in[l])
    a_re, a_im, bb_re, bb_im = _s5_discretise(s5_lam_re[l], s5_lam_im[l], s5_log_dt[l], s5_b_re[l], s5_b_im[l])
    s5_wb, s5_wc, s5_a, s5_dd = _s5_block_weights(a_re, a_im, bb_re, bb_im, s5_c_re[l], s5_c_im[l], s5_d[l])
    w = {
        'norm_mix': row(norm_mix[l]), 'w_main': w_main, 'w_alow': w_alow,
        's5_wb': s5_wb, 's5_wc': s5_wc, 's5_a': s5_a, 's5_d': s5_dd,
        's5_w_glu': s5_w_glu[l].astype(BF16), 's5_b_glu': row(s5_b_glu[l]),
        'gla_w_a2': jnp.pad(gla_w_a2[l], ((0, LANES - GLA_RANK), (0, 0))).astype(BF16),
        'gla_b_a': row(gla_b_a[l]), 'gla_norm': row(gla_norm[l]),
        'w_br_s5': _col_tiles(w_br_s5[l].astype(BF16), 512),
        'w_br_gla': _col_tiles(w_br_gla[l].astype(BF16), 512),
        'w_br_xattn': _col_tiles(w_br_xattn[l].astype(BF16), 512), 'w_out': w_out[l].astype(BF16),
        'norm_ffn': row(norm_ffn[l]), 'w_ffn_gate': w_ffn_gate[l].astype(BF16),
        'w_ffn_up': w_ffn_up[l].astype(BF16), 'w_ffn_down': w_ffn_down[l].astype(BF16),
        'norm_final': row(norm_final),
    }

    w_mem = jnp.concatenate([w_mem_k[l], w_mem_v[l]], axis=1).astype(BF16)
    mem_kv = _norm_matmul(mem_prompt.reshape(bp * MEM_LEN, d), row(mem_norm[l]), w_mem, tm=512, tn=1024)
    mk = mem_kv[:, :XA_WIDTH].reshape(bp, MEM_LEN, XA_WIDTH)
    mv = mem_kv[:, XA_WIDTH:].reshape(bp, MEM_LEN, XA_WIDTH)
    zero_s5 = jnp.zeros((bp, 1, n_state), F32)
    zero_gla = jnp.zeros((bp, GLA_HEADS, GLA_DK, GLA_DV), F32)
    yp, p_re, p_im, p_gla = _layer(x_prompt.reshape(bp * sp, d), mk, mv, zero_s5, zero_s5, zero_gla, w,
                                   batch=bp, seq=sp, segmented=True)

    ys, s_re, s_im, s_gla = _layer(x_sample.reshape(bs * ss, d),
                                   _cache_rows(cache_mem_k[l]), _cache_rows(cache_mem_v[l]),
                                   state_s5_re[l].reshape(1, bs, n_state), state_s5_im[l].reshape(1, bs, n_state),
                                   state_gla[l], w, batch=bs, seq=ss, segmented=False)

    s5_shape_p = (1, bp, S5_GROUPS, S5_STATE)
    s5_shape_s = (1, bs, S5_GROUPS, S5_STATE)
    kv_shape = (1, bp, MEM_LEN, XA_HEADS, XA_HEAD_DIM)
    return (yp.reshape(bp, sp, d), ys.reshape(bs, ss, d),
            p_re.reshape(s5_shape_p), p_im.reshape(s5_shape_p), p_gla[None],
            mk.reshape(kv_shape), mv.reshape(kv_shape),
            s_re.reshape(s5_shape_s), s_im.reshape(s5_shape_s), s_gla[None])
```

```python
import functools
import math

import jax
import jax.numpy as jnp
from jax import lax
from jax.experimental import pallas as pl
from jax.experimental.pallas import tpu as pltpu

F32 = jnp.float32
BF16 = jnp.bfloat16

D_MODEL = 2048
S5_WIDTH = 1024
S5_GROUP = 16
S5_GROUPS = 64
S5_STATE = 64
GLA_HEADS = 4
GLA_DK = 128
GLA_DV = 256
GLA_RANK = 16
GLA_TAU = 16.0
GLA_CHUNK = 64
XA_HEADS = 4
XA_HEAD_DIM = 256
XA_WIDTH = 1024
MEM_LEN = 256
FFN_HIDDEN = 5632
RMS_EPS = 1e-6

LANES = 128
SUBLANES = 8
VMEM_LIMIT = 56 * 1024 * 1024

COL_U = 0
COL_Q = 1024
COL_K = 1536
COL_V = 2048
COL_R = 3072
COL_GATE = 4096
COL_QX = 10240
Z_WIDTH = 11264

S5_GB = 8
S5_CH = S5_GB * S5_STATE
S5_NBLK = S5_GROUPS // S5_GB
S5_SEGS = 8
GLA_TILE = 64


def _params(*sem):
    return pltpu.CompilerParams(dimension_semantics=sem, vmem_limit_bytes=VMEM_LIMIT)


def _rms(x, g):
    return x * lax.rsqrt(jnp.mean(x * x, axis=-1, keepdims=True) + RMS_EPS) * g


def _norm_matmul_kernel(x_ref, g_ref, w_ref, o_ref, h_ref):
    @pl.when(pl.program_id(1) == 0)
    def _():
        h_ref[...] = _rms(x_ref[...], g_ref[...]).astype(BF16)

    o_ref[...] = jnp.dot(h_ref[...], w_ref[...], preferred_element_type=F32)


def _norm_matmul2_kernel(x_ref, g_ref, w_ref, ws_ref, o_ref, os_ref, h_ref):
    @pl.when(pl.program_id(1) == 0)
    def _():
        h = _rms(x_ref[...], g_ref[...]).astype(BF16)
        h_ref[...] = h
        os_ref[...] = jnp.dot(h, ws_ref[...], preferred_element_type=F32)

    o_ref[...] = jnp.dot(h_ref[...], w_ref[...], preferred_element_type=F32)


def _norm_matmul(x, g, w, w_small=None, *, tm, tn):
    t, d = x.shape
    n = w.shape[1]
    grid = (t // tm, n // tn)
    in_specs = [pl.BlockSpec((tm, d), lambda i, j: (i, 0)),
                pl.BlockSpec((1, d), lambda i, j: (0, 0)),
                pl.BlockSpec((d, tn), lambda i, j: (0, j))]
    out_specs = pl.BlockSpec((tm, tn), lambda i, j: (i, j))
    out_shape = jax.ShapeDtypeStruct((t, n), F32)
    scratch = [pltpu.VMEM((tm, d), BF16)]
    if w_small is None:
        return pl.pallas_call(_norm_matmul_kernel, grid=grid, in_specs=in_specs, out_specs=out_specs,
                              out_shape=out_shape, scratch_shapes=scratch,
                              compiler_params=_params("parallel", "arbitrary"),
                              name="norm_matmul")(x, g, w)
    ns = w_small.shape[1]
    in_specs.append(pl.BlockSpec((d, ns), lambda i, j: (0, 0)))
    return pl.pallas_call(_norm_matmul2_kernel, grid=grid, in_specs=in_specs,
                          out_specs=[out_specs, pl.BlockSpec((tm, ns), lambda i, j: (i, 0))],
                          out_shape=[out_shape, jax.ShapeDtypeStruct((t, ns), F32)],
                          scratch_shapes=scratch,
                          compiler_params=_params("parallel", "arbitrary"),
                          name="in_proj")(x, g, w, w_small)


def _s5_disc_kernel(lr_ref, li_ref, ldt_ref, br_ref, bi_ref, ar_ref, ai_ref, bbr_ref, bbi_ref):
    lam_re = lr_ref[...]
    lam_im = li_ref[...]
    dt = jnp.exp(ldt_ref[...])
    mag = jnp.exp(lam_re * dt)
    a_re = mag * jnp.cos(lam_im * dt)
    a_im = mag * jnp.sin(lam_im * dt)
    den = lam_re * lam_re + lam_im * lam_im
    coef_re = ((a_re - 1.0) * lam_re + a_im * lam_im) / den
    coef_im = (a_im * lam_re - (a_re - 1.0) * lam_im) / den
    b_re = br_ref[...]
    b_im = bi_ref[...]
    ar_ref[...] = a_re
    ai_ref[...] = a_im
    bbr_ref[...] = coef_re * b_re - coef_im * b_im
    bbi_ref[...] = coef_re * b_im + coef_im * b_re


def _s5_discretise(lam_re, lam_im, log_dt, b_re, b_im):
    n = S5_GROUPS * S5_STATE
    col = lambda a: a.reshape(n, 1)
    ldt = jnp.broadcast_to(log_dt[:, None], (S5_GROUPS, S5_STATE))
    outs = pl.pallas_call(
        _s5_disc_kernel,
        out_shape=[jax.ShapeDtypeStruct((n, 1), F32), jax.ShapeDtypeStruct((n, 1), F32),
                   jax.ShapeDtypeStruct((n, S5_GROUP), F32), jax.ShapeDtypeStruct((n, S5_GROUP), F32)],
        compiler_params=pltpu.CompilerParams(vmem_limit_bytes=VMEM_LIMIT),
        name="s5_discretise",
    )(col(lam_re), col(lam_im), col(ldt), b_re.reshape(n, S5_GROUP), b_im.reshape(n, S5_GROUP))
    return outs


def _s5_block_weights(a_re, a_im, bb_re, bb_im, c_re, c_im, d_skip):
    eye = jnp.eye(S5_GB, dtype=F32)

    def b_blk(bb):
        bb = bb.reshape(S5_NBLK, S5_GB, S5_STATE, S5_GROUP)
        return jnp.einsum('kgpc,gh->kgchp', bb, eye).reshape(S5_NBLK, S5_GB * S5_GROUP, S5_CH)

    def c_blk(c):
        c = c.reshape(S5_NBLK, S5_GB, S5_GROUP, S5_STATE)
        return jnp.einsum('kgcp,gh->kgphc', c, eye).reshape(S5_NBLK, S5_CH, S5_GB * S5_GROUP)

    w_b = jnp.concatenate([b_blk(bb_re), b_blk(bb_im)], axis=-1).astype(BF16)
    w_c = jnp.concatenate([c_blk(c_re), -c_blk(c_im)], axis=1).astype(BF16)
    a = jnp.concatenate([a_re.reshape(S5_NBLK, 1, S5_CH), a_im.reshape(S5_NBLK, 1, S5_CH)], axis=-1)
    d = d_skip.reshape(S5_NBLK, 1, S5_GB * S5_GROUP)
    return w_b, w_c, a, d


def _s5_scan_kernel(u_ref, wb_ref, wc_ref, a_ref, d_ref, h0r_ref, h0i_ref,
                    y_ref, hfr_ref, hfi_ref, up_ref, bu_ref, yp_ref,
                    *, nseq, slen, segmented, lane_chunk):
    ch = S5_CH
    rows = nseq * slen

    def slab(t):
        return pl.ds(pl.multiple_of(t * nseq, nseq), nseq)

    if slen <= 8:
        for t in range(slen):
            up_ref[t * nseq:(t + 1) * nseq, :] = u_ref[pl.ds(t, nseq, stride=slen), :]
    else:
        def perm_in(t, c):
            up_ref[slab(t), :] = u_ref[pl.ds(t, nseq, stride=slen), :]
            return c
        lax.fori_loop(0, slen, perm_in, 0)

    up = up_ref[...]
    bu_ref[...] = jnp.dot(up.astype(BF16), wb_ref[0], preferred_element_type=F32)

    def run_scan(lo, h_re, h_im, store):
        w = h_re.shape[1]
        a_re = jnp.broadcast_to(a_ref[0, :, lo:lo + w], (nseq, w))
        a_im = jnp.broadcast_to(a_ref[0, :, ch + lo:ch + lo + w], (nseq, w))

        def step(t, carry):
            hr, hi = carry
            br = bu_ref[slab(t), lo:lo + w]
            bi = bu_ref[slab(t), ch + lo:ch + lo + w]
            nr = a_re * hr - a_im * hi + br
            ni = a_re * hi + a_im * hr + bi
            if store:
                bu_ref[slab(t), lo:lo + w] = nr
                bu_ref[slab(t), ch + lo:ch + lo + w] = ni
            return nr, ni

        if slen <= 8:
            carry = (h_re, h_im)
            for t in range(slen):
                carry = step(t, carry)
            return carry
        return lax.fori_loop(0, slen, step, (h_re, h_im), unroll=4)

    for lo in range(0, ch, lane_chunk):
        w = lane_chunk
        if segmented:
            zero = jnp.zeros((nseq, w), F32)
            e_re, e_im = run_scan(lo, zero, zero, store=False)
            p_re = a_ref[0, :, lo:lo + w]
            p_im = a_ref[0, :, ch + lo:ch + lo + w]
            for _ in range(int(math.log2(slen))):
                p_re, p_im = p_re * p_re - p_im * p_im, 2.0 * (p_re * p_im)
            cur_re = h0r_ref[0, :, lo:lo + w]
            cur_im = h0i_ref[0, :, lo:lo + w]
            init_re, init_im = [], []
            for s in range(nseq):
                init_re.append(cur_re)
                init_im.append(cur_im)
                nxt_re = p_re * cur_re - p_im * cur_im + e_re[s:s + 1, :]
                nxt_im = p_re * cur_im + p_im * cur_re + e_im[s:s + 1, :]
                cur_re, cur_im = nxt_re, nxt_im
            hfr_ref[0, :, lo:lo + w] = cur_re
            hfi_ref[0, :, lo:lo + w] = cur_im
            run_scan(lo, jnp.concatenate(init_re, axis=0), jnp.concatenate(init_im, axis=0), store=True)
        else:
            f_re, f_im = run_scan(lo, h0r_ref[0, :, lo:lo + w], h0i_ref[0, :, lo:lo + w], store=True)
            hfr_ref[0, :, lo:lo + w] = f_re
            hfi_ref[0, :, lo:lo + w] = f_im

    y = jnp.dot(bu_ref[...].astype(BF16), wc_ref[0], preferred_element_type=F32) + d_ref[0] * up
    yp_ref[...] = jax.nn.gelu(y, approximate=True)

    if slen <= 8:
        for t in range(slen):
            y_ref[pl.ds(t, nseq, stride=slen), :] = yp_ref[t * nseq:(t + 1) * nseq, :]
    else:
        def perm_out(t, c):
            y_ref[pl.ds(t, nseq, stride=slen), :] = yp_ref[slab(t), :]
            return c
        lax.fori_loop(0, slen, perm_out, 0)


def _s5_scan(z, w_b, w_c, a, d, h0_re, h0_im, *, batch, seq, segmented):
    if segmented:
        nseq, slen, nb, ns, lane_chunk = S5_SEGS, seq // S5_SEGS, batch, 1, S5_CH
        assert slen & (slen - 1) == 0
    else:
        nseq, slen, nb, ns, lane_chunk = batch, seq, 1, batch, LANES
    rows = nseq * slen
    gw = S5_GB * S5_GROUP
    kern = functools.partial(_s5_scan_kernel, nseq=nseq, slen=slen, segmented=segmented, lane_chunk=lane_chunk)
    state_spec = pl.BlockSpec((1, ns, S5_CH), lambda b, k: (b, 0, k))
    state_shape = jax.ShapeDtypeStruct((nb, ns, S5_GROUPS * S5_STATE), F32)
    return pl.pallas_call(
        kern, grid=(nb, S5_NBLK),
        in_specs=[pl.BlockSpec((rows, gw), lambda b, k: (b, COL_U // gw + k)),
                  pl.BlockSpec((1, gw, 2 * S5_CH), lambda b, k: (k, 0, 0)),
                  pl.BlockSpec((1, 2 * S5_CH, gw), lambda b, k: (k, 0, 0)),
                  pl.BlockSpec((1, 1, 2 * S5_CH), lambda b, k: (k, 0, 0)),
                  pl.BlockSpec((1, 1, gw), lambda b, k: (k, 0, 0)),
                  state_spec, state_spec],
        out_specs=[pl.BlockSpec((rows, gw), lambda b, k: (b, k)), state_spec, state_spec],
        out_shape=[jax.ShapeDtypeStruct((nb * rows, S5_WIDTH), F32), state_shape, state_shape],
        scratch_shapes=[pltpu.VMEM((rows, gw), F32), pltpu.VMEM((rows, 2 * S5_CH), F32),
                        pltpu.VMEM((rows, gw), F32)],
        compiler_params=_params("parallel", "parallel"),
        name="s5_scan",
    )(z, w_b, w_c, a, d, h0_re, h0_im)


def _gla_kernel(q_ref, k_ref, v_ref, r_ref, al_ref, wa_ref, ba_ref, gn_ref, s0_ref,
                y_ref, sf_ref, s_scr, *, ntile, groups, chained, width):
    rt = GLA_TILE
    c = rt // groups
    shift = int(math.log2(c))
    row_g = lax.broadcasted_iota(jnp.int32, (rt, rt), 0)
    col_g = lax.broadcasted_iota(jnp.int32, (rt, rt), 1)
    same = (row_g >> shift) == (col_g >> shift)
    causal = same & (row_g >= col_g)
    cum_w = jnp.concatenate([causal.astype(BF16), same.astype(BF16)], axis=0)
    sel_r = lax.broadcasted_iota(jnp.int32, (rt, groups * LANES), 0) >> shift
    sel_c = lax.broadcasted_iota(jnp.int32, (rt, groups * LANES), 1) >> int(math.log2(LANES))
    sel = (sel_r == sel_c).astype(BF16)
    tn_dims = (((0,), (0,)), ((), ()))
    nt_dims = (((1,), (1,)), ((), ()))

    def split3(x):
        hi = x.astype(BF16)
        r1 = x - hi.astype(F32)
        mid = r1.astype(BF16)
        lo = (r1 - mid.astype(F32)).astype(BF16)
        return jnp.concatenate([hi, mid, lo], axis=1)

    if chained:
        s_scr[...] = s0_ref[0, 0]

    def body(it, carry):
        tiles = [it * width + u for u in range(width)]
        rows = [pl.ds(pl.multiple_of(t * rt, rt), rt) for t in tiles]
        log_a = []
        for u in range(width):
            x = jnp.dot(al_ref[rows[u], :].astype(BF16), wa_ref[...], preferred_element_type=F32) + ba_ref[...]
            log_a.append((jnp.minimum(x, 0.0) - jnp.log1p(jnp.exp(-jnp.abs(x)))) * (1.0 / GLA_TAU))
        la3 = [split3(la) for la in log_a]
        cums, e_col = [], []
        for u in range(width):
            cs = jnp.dot(cum_w, la3[u], preferred_element_type=F32)
            cums.append(cs[:, :LANES] + cs[:, LANES:2 * LANES] + cs[:, 2 * LANES:])
            tot = lax.dot_general(la3[u], sel, tn_dims, preferred_element_type=F32)
            e_col.append(jnp.exp(tot[:LANES] + tot[LANES:2 * LANES] + tot[2 * LANES:]))
        qb, kd, v, att, upd = [], [], [], [], []
        for u in range(width):
            b = cums[u][:rt]
            b_end = cums[u][rt:]
            k = k_ref[rows[u], :]
            v.append(v_ref[rows[u], :])
            qb.append(q_ref[rows[u], :] * (GLA_DK ** -0.5) * jnp.exp(b))
            kd.append(k * jnp.exp(b_end - b))
            a = lax.dot_general(qb[u].astype(BF16), (k * jnp.exp(-b)).astype(BF16), nt_dims,
                                preferred_element_type=F32)
            att.append(jnp.where(causal, a, 0.0).astype(BF16))
            upd.append([lax.dot_general(kd[u][g * c:(g + 1) * c].astype(BF16), v[u][g * c:(g + 1) * c].astype(BF16),
                                        tn_dims, preferred_element_type=F32) for g in range(groups)])
        o = [jnp.dot(att[u], v[u].astype(BF16), preferred_element_type=F32) for u in range(width)]
        for u in range(width):
            o_state = []
            for g in range(groups):
                s = s_scr[...] if chained else s0_ref[tiles[u] * groups + g, 0]
                o_state.append(jnp.dot(qb[u][g * c:(g + 1) * c].astype(BF16), s.astype(BF16),
                                       preferred_element_type=F32))
                e = e_col[u][:, g * LANES:(g + 1) * LANES]
                s_new = jnp.concatenate([s[:, :LANES] * e, s[:, LANES:] * e], axis=1) + upd[u][g]
                if chained:
                    s_scr[...] = s_new
                else:
                    sf_ref[tiles[u] * groups + g, 0] = s_new
            o[u] = o[u] + (o_state[0] if groups == 1 else jnp.concatenate(o_state, axis=0))
        for u in range(width):
            y = o[u] * lax.rsqrt(jnp.mean(o[u] * o[u], axis=-1, keepdims=True) + RMS_EPS)
            y = y * gn_ref[...]
            r = r_ref[rows[u], :]
            y_ref[rows[u], :] = y * (r * jax.nn.sigmoid(r))
        return carry

    lax.fori_loop(0, ntile // width, body, 0)
    if chained:
        sf_ref[0, 0] = s_scr[...]


def _gla(z, alow, w_a2, b_a, g_norm, s0, *, batch, seq, chained):
    chunk = math.gcd(seq, GLA_CHUNK)
    if chained:
        assert chunk == GLA_TILE
        nb, groups, width = 1, 1, 4
    else:
        assert GLA_TILE % seq == 0 and chunk == seq
        nb, groups, width = 32, GLA_TILE // seq, 2
    rows = nb * seq
    kern = functools.partial(_gla_kernel, ntile=rows // GLA_TILE, groups=groups, chained=chained, width=width)
    state_spec = pl.BlockSpec((nb, 1, GLA_DK, GLA_DV), lambda i, h: (i, h, 0, 0))
    return pl.pallas_call(
        kern, grid=(batch // nb, GLA_HEADS),
        in_specs=[pl.BlockSpec((rows, GLA_DK), lambda i, h: (i, COL_Q // GLA_DK + h)),
                  pl.BlockSpec((rows, GLA_DK), lambda i, h: (i, COL_K // GLA_DK + h)),
                  pl.BlockSpec((rows, GLA_DV), lambda i, h: (i, COL_V // GLA_DV + h)),
                  pl.BlockSpec((rows, GLA_DV), lambda i, h: (i, COL_R // GLA_DV + h)),
                  pl.BlockSpec((rows, LANES), lambda i, h: (i, 0)),
                  pl.BlockSpec((LANES, GLA_DK), lambda i, h: (0, h)),
                  pl.BlockSpec((1, GLA_DK), lambda i, h: (0, h)),
                  pl.BlockSpec((1, GLA_DV), lambda i, h: (0, h)),
                  state_spec],
        out_specs=[pl.BlockSpec((rows, GLA_DV), lambda i, h: (i, h)), state_spec],
        out_shape=[jax.ShapeDtypeStruct((batch * seq, GLA_HEADS * GLA_DV), F32),
                   jax.ShapeDtypeStruct((batch, GLA_HEADS, GLA_DK, GLA_DV), F32)],
        scratch_shapes=[pltpu.VMEM((GLA_DK, GLA_DV), F32)],
        compiler_params=_params("parallel", "parallel"),
        name="gla",
    )(z, z, z, z, alow, w_a2, b_a, g_norm, s0)


def _attn_kernel(q_ref, k_ref, v_ref, o_ref, *, nb, seq, heads):
    nt_dims = (((1,), (1,)), ((), ()))

    def body(bi, carry):
        rows = pl.ds(pl.multiple_of(bi * seq, SUBLANES), seq)
        for h in range(heads):
            cols = slice(h * XA_HEAD_DIM, (h + 1) * XA_HEAD_DIM)
            q = q_ref[rows, cols].astype(BF16)
            k = k_ref[bi, :, cols].astype(BF16)
            s = lax.dot_general(q, k, nt_dims, preferred_element_type=F32) * (XA_HEAD_DIM ** -0.5)
            p = jnp.exp(s - jnp.max(s, axis=-1, keepdims=True))
            p = p / jnp.sum(p, axis=-1, keepdims=True)
            o_ref[rows, cols] = jnp.dot(p.astype(BF16), v_ref[bi, :, cols].astype(BF16),
                                        preferred_element_type=F32)
        return carry

    lax.fori_loop(0, nb, body, 0)


def _attn_prompt(z, mem_k, mem_v, *, batch, seq, rc):
    nrc = seq // rc
    kern = functools.partial(_attn_kernel, nb=1, seq=rc, heads=1)
    kv_spec = pl.BlockSpec((1, MEM_LEN, XA_HEAD_DIM), lambda b, h, c: (b, 0, h))
    return pl.pallas_call(
        kern, grid=(batch, XA_HEADS, nrc),
        in_specs=[pl.BlockSpec((rc, XA_HEAD_DIM), lambda b, h, c: (b * nrc + c, COL_QX // XA_HEAD_DIM + h)),
                  kv_spec, kv_spec],
        out_specs=pl.BlockSpec((rc, XA_HEAD_DIM), lambda b, h, c: (b * nrc + c, h)),
        out_shape=jax.ShapeDtypeStruct((batch * seq, XA_WIDTH), F32),
        compiler_params=_params("parallel", "parallel", "parallel"),
        name="attn_prompt",
    )(z, mem_k, mem_v)


XA_HALF = XA_HEAD_DIM // 2
XA_ROWS = 2 * XA_HEADS


def _attn_cache_kernel(q_ref, k_ref, v_ref, o_ref, *, nb, seq, width):
    nt_dims = (((1,), (1,)), ((), ()))

    def half_rows(ref, bi, h, half):
        return ref[bi, pl.ds(half * XA_HEADS + h, MEM_LEN, stride=XA_ROWS), :].astype(BF16)

    def body(it, carry):
        work = [(it * width + u, h) for u in range(width) for h in range(XA_HEADS)]
        scores = []
        for bi, h in work:
            rows = pl.ds(pl.multiple_of(bi * seq, SUBLANES), seq)
            lo = h * XA_HEAD_DIM
            s = lax.dot_general(q_ref[rows, lo:lo + XA_HALF].astype(BF16), half_rows(k_ref, bi, h, 0),
                                nt_dims, preferred_element_type=F32)
            s = s + lax.dot_general(q_ref[rows, lo + XA_HALF:lo + XA_HEAD_DIM].astype(BF16),
                                    half_rows(k_ref, bi, h, 1), nt_dims, preferred_element_type=F32)
            scores.append(s * (XA_HEAD_DIM ** -0.5))
        probs = []
        for s in scores:
            p = jnp.exp(s - jnp.max(s, axis=-1, keepdims=True))
            probs.append((p / jnp.sum(p, axis=-1, keepdims=True)).astype(BF16))
        for (bi, h), p in zip(work, probs):
            rows = pl.ds(pl.multiple_of(bi * seq, SUBLANES), seq)
            lo = h * XA_HEAD_DIM
            for half in range(2):
                o_ref[rows, lo + half * XA_HALF:lo + (half + 1) * XA_HALF] = jnp.dot(
                    p, half_rows(v_ref, bi, h, half), preferred_element_type=F32)
        return carry

    lax.fori_loop(0, nb // width, body, 0)


def _cache_rows(cache):
    bs = cache.shape[0]
    c = cache.reshape(bs, MEM_LEN, XA_HEADS, 2, XA_HALF).transpose(0, 1, 3, 2, 4)
    return c.reshape(bs, MEM_LEN * XA_ROWS, XA_HALF)


def _attn_sample(z, mem_k, mem_v, *, batch, seq, nb):
    rows = nb * seq
    kern = functools.partial(_attn_cache_kernel, nb=nb, seq=seq, width=2)
    kv_spec = pl.BlockSpec((nb, MEM_LEN * XA_ROWS, XA_HALF), lambda i: (i, 0, 0))
    return pl.pallas_call(
        kern, grid=(batch // nb,),
        in_specs=[pl.BlockSpec((rows, XA_WIDTH), lambda i: (i, COL_QX // XA_WIDTH)), kv_spec, kv_spec],
        out_specs=pl.BlockSpec((rows, XA_WIDTH), lambda i: (i, 0)),
        out_shape=jax.ShapeDtypeStruct((batch * seq, XA_WIDTH), F32),
        compiler_params=_params("parallel"),
        name="attn_sample",
    )(z, mem_k, mem_v)


def _merge_kernel(ys_ref, yg_ref, yx_ref, g0_ref, g1_ref, g2_ref, wglu_ref, bglu_ref,
                  w0_ref, w1_ref, w2_ref, o_ref, s5_scr, gla_scr, xa_scr):
    @pl.when(pl.program_id(1) == 0)
    def _():
        y = ys_ref[...]
        lin = jnp.dot(y.astype(BF16), wglu_ref[...], preferred_element_type=F32) + bglu_ref[...]
        s5_scr[...] = (y * jax.nn.sigmoid(lin)).astype(BF16)
        gla_scr[...] = yg_ref[...].astype(BF16)
        xa_scr[...] = yx_ref[...].astype(BF16)

    j = pl.program_id(1)
    m = jax.nn.sigmoid(g0_ref[...]) * jnp.dot(s5_scr[...], w0_ref[j], preferred_element_type=F32)
    m = m + jax.nn.sigmoid(g1_ref[...]) * jnp.dot(gla_scr[...], w1_ref[j], preferred_element_type=F32)
    m = m + jax.nn.sigmoid(g2_ref[...]) * jnp.dot(xa_scr[...], w2_ref[j], preferred_element_type=F32)
    o_ref[...] = m.astype(o_ref.dtype)


def _merge(y_s5, y_gla, y_x, z, w_glu, b_glu, w_br_s5, w_br_gla, w_br_x, *, tm, tn):
    t = y_s5.shape[0]
    nj = D_MODEL // tn
    once = pl.Buffered(1)
    wide = pl.BlockSpec((tm, S5_WIDTH), lambda i, j: (i, 0))
    gate = lambda b: pl.BlockSpec((tm, tn), lambda i, j: (i, (COL_GATE + b * D_MODEL) // tn + j))
    w_br = pl.BlockSpec((nj, S5_WIDTH, tn), lambda i, j: (0, 0, 0), pipeline_mode=once)
    return pl.pallas_call(
        _merge_kernel, grid=(t // tm, nj),
        in_specs=[wide, wide, wide, gate(0), gate(1), gate(2),
                  pl.BlockSpec((S5_WIDTH, S5_WIDTH), lambda i, j: (0, 0), pipeline_mode=once),
                  pl.BlockSpec((1, S5_WIDTH), lambda i, j: (0, 0)),
                  w_br, w_br, w_br],
        out_specs=pl.BlockSpec((tm, tn), lambda i, j: (i, j)),
        out_shape=jax.ShapeDtypeStruct((t, D_MODEL), BF16),
        scratch_shapes=[pltpu.VMEM((tm, S5_WIDTH), BF16)] * 3,
        compiler_params=_params("parallel", "arbitrary"),
        name="merge",
    )(y_s5, y_gla, y_x, z, z, z, w_glu, b_glu, w_br_s5, w_br_gla, w_br_x)


def _out_proj_kernel(m_ref, w_ref, x_ref, o_ref):
    o_ref[...] = x_ref[...] + jnp.dot(m_ref[...], w_ref[...], preferred_element_type=F32)


def _out_proj(merged, w_out, x, *, tm, tn):
    t = x.shape[0]
    return pl.pallas_call(
        _out_proj_kernel, grid=(t // tm, D_MODEL // tn),
        in_specs=[pl.BlockSpec((tm, D_MODEL), lambda i, j: (i, 0)),
                  pl.BlockSpec((D_MODEL, tn), lambda i, j: (0, j)),
                  pl.BlockSpec((tm, tn), lambda i, j: (i, j))],
        out_specs=pl.BlockSpec((tm, tn), lambda i, j: (i, j)),
        out_shape=jax.ShapeDtypeStruct((t, D_MODEL), F32),
        compiler_params=_params("parallel", "parallel"),
        name="out_proj",
    )(merged, w_out, x)


def _ffn_kernel(x_ref, gf_ref, wg_ref, wu_ref, wd_ref, gl_ref, o_ref, h_scr, acc_scr):
    k = pl.program_id(1)

    @pl.when(k == 0)
    def _():
        h_scr[...] = _rms(x_ref[...], gf_ref[...]).astype(BF16)
        acc_scr[...] = jnp.zeros_like(acc_scr)

    h = h_scr[...]
    gate = jnp.dot(h, wg_ref[...], preferred_element_type=F32)
    up = jnp.dot(h, wu_ref[...], preferred_element_type=F32)
    act = (gate * jax.nn.sigmoid(gate) * up).astype(BF16)
    acc_scr[...] += jnp.dot(act, wd_ref[...], preferred_element_type=F32)

    @pl.when(k == pl.num_programs(1) - 1)
    def _():
        o_ref[...] = _rms(x_ref[...] + acc_scr[...], gl_ref[...])


def _ffn(x, g_ffn, w_gate, w_up, w_down, g_final, *, tm, th):
    t = x.shape[0]
    row = pl.BlockSpec((tm, D_MODEL), lambda i, k: (i, 0))
    vec = pl.BlockSpec((1, D_MODEL), lambda i, k: (0, 0))
    w_in = pl.BlockSpec((D_MODEL, th), lambda i, k: (0, k))
    return pl.pallas_call(
        _ffn_kernel, grid=(t // tm, FFN_HIDDEN // th),
        in_specs=[row, vec, w_in, w_in, pl.BlockSpec((th, D_MODEL), lambda i, k: (k, 0)), vec],
        out_specs=row,
        out_shape=jax.ShapeDtypeStruct((t, D_MODEL), F32),
        scratch_shapes=[pltpu.VMEM((tm, D_MODEL), BF16), pltpu.VMEM((tm, D_MODEL), F32)],
        compiler_params=_params("parallel", "arbitrary"),
        name="ffn",
    )(x, g_ffn, w_gate, w_up, w_down, g_final)


def _reorder_in_proj_kernel(wt_ref, wa_ref, main_ref, alow_ref):
    main_ref[...] = wt_ref[...].T.astype(BF16)

    @pl.when(pl.program_id(0) == 0)
    def _():
        lane = lax.broadcasted_iota(jnp.int32, alow_ref.shape, 1)
        alow_ref[...] = jnp.where(lane < GLA_RANK, wa_ref[...].T, 0.0).astype(BF16)


def _reorder_in_proj(w_in):
    d, n = w_in.shape
    wt = w_in.T
    tc = 512
    n_first, n_gate = COL_GATE // tc, (COL_QX - COL_GATE) // tc
    gate_start, qx_start = 4096 + GLA_RANK + XA_WIDTH, 4096 + GLA_RANK

    def src_row(j):
        t8, g8, q8 = tc // SUBLANES, gate_start // SUBLANES, qx_start // SUBLANES
        r8 = jnp.where(j < n_first, j * t8,
                       jnp.where(j < n_first + n_gate, g8 + (j - n_first) * t8,
                                 q8 + (j - n_first - n_gate) * t8))
        return r8 * SUBLANES

    return pl.pallas_call(
        _reorder_in_proj_kernel, grid=(Z_WIDTH // tc,),
        in_specs=[pl.BlockSpec((pl.Element(tc), pl.Element(d)), lambda j: (src_row(j), 0)),
                  pl.BlockSpec((pl.Element(LANES), pl.Element(d)), lambda j: (COL_GATE, 0))],
        out_specs=[pl.BlockSpec((d, tc), lambda j: (0, j)), pl.BlockSpec((d, LANES), lambda j: (0, 0))],
        out_shape=[jax.ShapeDtypeStruct((d, Z_WIDTH), BF16), jax.ShapeDtypeStruct((d, LANES), BF16)],
        compiler_params=_params("arbitrary"),
        name="reorder_in_proj",
    )(wt, wt)


def _col_tiles(w, tn):
    k, n = w.shape
    return w.reshape(k, n // tn, tn).transpose(1, 0, 2)


def _layer(x, mem_k, mem_v, s5_re0, s5_im0, gla_s0, w, *, batch, seq, segmented):
    z, alow = _norm_matmul(x, w['norm_mix'], w['w_main'], w['w_alow'], tm=1024, tn=1024)
    y_s5, hf_re, hf_im = _s5_scan(z, w['s5_wb'], w['s5_wc'], w['s5_a'], w['s5_d'], s5_re0, s5_im0,
                                  batch=batch, seq=seq, segmented=segmented)
    y_gla, gla_s = _gla(z, alow, w['gla_w_a2'], w['gla_b_a'], w['gla_norm'], gla_s0,
                        batch=batch, seq=seq, chained=segmented)
    if segmented:
        y_x = _attn_prompt(z, mem_k, mem_v, batch=batch, seq=seq, rc=512)
    else:
        y_x = _attn_sample(z, mem_k, mem_v, batch=batch, seq=seq, nb=8)
    merged = _merge(y_s5, y_gla, y_x, z, w['s5_w_glu'], w['s5_b_glu'],
                    w['w_br_s5'], w['w_br_gla'], w['w_br_xattn'], tm=512, tn=512)
    x1 = _out_proj(merged, w['w_out'], x, tm=1024, tn=512)
    y = _ffn(x1, w['norm_ffn'], w['w_ffn_gate'], w['w_ffn_up'], w['w_ffn_down'], w['norm_final'],
             tm=512, th=512)
    return y, hf_re, hf_im, gla_s


def kernel(x_prompt, x_sample, mem_prompt, state_s5_re, state_s5_im, state_gla, cache_mem_k, cache_mem_v,
           norm_mix, w_in, s5_lam_re, s5_lam_im, s5_log_dt, s5_b_re, s5_b_im, s5_c_re, s5_c_im,
           s5_d, s5_w_glu, s5_b_glu, gla_w_a2, gla_b_a, gla_norm, mem_norm, w_mem_k, w_mem_v,
           w_br_s5, w_br_gla, w_br_xattn, w_out, norm_ffn, w_ffn_gate, w_ffn_up, w_ffn_down, norm_final):
    depth = w_in.shape[0]
    assert depth == 1
    bp, sp, d = x_prompt.shape
    bs, ss, _ = x_sample.shape
    n_state = S5_GROUPS * S5_STATE
    row = lambda v: v.reshape(1, -1)

    l = 0
    w_main, w_alow = _reorder_in_proj(w_in[l])
    a_re, a_im, bb_re, bb_im = _s5_discretise(s5_lam_re[l], s5_lam_im[l], s5_log_dt[l], s5_b_re[l], s5_b_im[l])
    s5_wb, s5_wc, s5_a, s5_dd = _s5_block_weights(a_re, a_im, bb_re, bb_im, s5_c_re[l], s5_c_im[l], s5_d[l])
    w = {
        'norm_mix': row(norm_mix[l]), 'w_main': w_main, 'w_alow': w_alow,
        's5_wb': s5_wb, 's5_wc': s5_wc, 's5_a': s5_a, 's5_d': s5_dd,
        's5_w_glu': s5_w_glu[l].astype(BF16), 's5_b_glu': row(s5_b_glu[l]),
        'gla_w_a2': jnp.pad(gla_w_a2[l], ((0, LANES - GLA_RANK), (0, 0))).astype(BF16),
        'gla_b_a': row(gla_b_a[l]), 'gla_norm': row(gla_norm[l]),
        'w_br_s5': _col_tiles(w_br_s5[l].astype(BF16), 512),
        'w_br_gla': _col_tiles(w_br_gla[l].astype(BF16), 512),
        'w_br_xattn': _col_tiles(w_br_xattn[l].astype(BF16), 512), 'w_out': w_out[l].astype(BF16),
        'norm_ffn': row(norm_ffn[l]), 'w_ffn_gate': w_ffn_gate[l].astype(BF16),
        'w_ffn_up': w_ffn_up[l].astype(BF16), 'w_ffn_down': w_ffn_down[l].astype(BF16),
        'norm_final': row(norm_final),
    }

    w_mem = jnp.concatenate([w_mem_k[l], w_mem_v[l]], axis=1).astype(BF16)
    mem_kv = _norm_matmul(mem_prompt.reshape(bp * MEM_LEN, d), row(mem_norm[l]), w_mem, tm=512, tn=1024)
    mk = mem_kv[:, :XA_WIDTH].reshape(bp, MEM_LEN, XA_WIDTH)
    mv = mem_kv[:, XA_WIDTH:].reshape(bp, MEM_LEN, XA_WIDTH)
    zero_s5 = jnp.zeros((bp, 1, n_state), F32)
    zero_gla = jnp.zeros((bp, GLA_HEADS, GLA_DK, GLA_DV), F32)
    yp, p_re, p_im, p_gla = _layer(x_prompt.reshape(bp * sp, d), mk, mv, zero_s5, zero_s5, zero_gla, w,
                                   batch=bp, seq=sp, segmented=True)

    ys, s_re, s_im, s_gla = _layer(x_sample.reshape(bs * ss, d),
                                   _cache_rows(cache_mem_k[l]), _cache_rows(cache_mem_v[l]),
                                   state_s5_re[l].reshape(1, bs, n_state), state_s5_im[l].reshape(1, bs, n_state),
                                   state_gla[l], w, batch=bs, seq=ss, segmented=False)

    s5_shape_p = (1, bp, S5_GROUPS, S5_STATE)
    s5_shape_s = (1, bs, S5_GROUPS, S5_STATE)
    kv_shape = (1, bp, MEM_LEN, XA_HEADS, XA_HEAD_DIM)
    return (yp.reshape(bp, sp, d), ys.reshape(bs, ss, d),
            p_re.reshape(s5_shape_p), p_im.reshape(s5_shape_p), p_gla[None],
            mk.reshape(kv_shape), mv.reshape(kv_shape),
            s_re.reshape(s5_shape_s), s_im.reshape(s5_shape_s), s_gla[None])
```

```python
import functools
import math

import jax
import jax.numpy as jnp
from jax import lax
from jax.experimental import pallas as pl
from jax.experimental.pallas import tpu as pltpu

F32 = jnp.float32
BF16 = jnp.bfloat16

D_MODEL = 2048
S5_WIDTH = 1024
S5_GROUP = 16
S5_GROUPS = 64
S5_STATE = 64
GLA_HEADS = 4
GLA_DK = 128
GLA_DV = 256
GLA_RANK = 16
GLA_TAU = 16.0
GLA_CHUNK = 64
XA_HEADS = 4
XA_HEAD_DIM = 256
XA_WIDTH = 1024
MEM_LEN = 256
FFN_HIDDEN = 5632
RMS_EPS = 1e-6

LANES = 128
SUBLANES = 8
VMEM_LIMIT = 56 * 1024 * 1024

COL_U = 0
COL_Q = 1024
COL_K = 1536
COL_V = 2048
COL_R = 3072
COL_GATE = 4096
COL_QX = 10240
Z_WIDTH = 11264

S5_GB = 8
S5_CH = S5_GB * S5_STATE
S5_NBLK = S5_GROUPS // S5_GB
S5_TC = 16
GLA_TILE = 64


def _params(*sem):
    return pltpu.CompilerParams(dimension_semantics=sem, vmem_limit_bytes=VMEM_LIMIT)


def _rms(x, g):
    return x * lax.rsqrt(jnp.mean(x * x, axis=-1, keepdims=True) + RMS_EPS) * g


def _norm_matmul_kernel(x_ref, g_ref, w_ref, o_ref, h_ref):
    @pl.when(pl.program_id(1) == 0)
    def _():
        h_ref[...] = _rms(x_ref[...], g_ref[...]).astype(BF16)

    o_ref[...] = jnp.dot(h_ref[...], w_ref[...], preferred_element_type=F32)


def _norm_matmul2_kernel(x_ref, g_ref, w_ref, ws_ref, o_ref, os_ref, h_ref):
    @pl.when(pl.program_id(1) == 0)
    def _():
        h = _rms(x_ref[...], g_ref[...]).astype(BF16)
        h_ref[...] = h
        os_ref[...] = jnp.dot(h, ws_ref[...], preferred_element_type=F32)

    o_ref[...] = jnp.dot(h_ref[...], w_ref[...], preferred_element_type=F32)


def _norm_matmul(x, g, w, w_small=None, *, tm, tn):
    t, d = x.shape
    n = w.shape[1]
    grid = (t // tm, n // tn)
    in_specs = [pl.BlockSpec((tm, d), lambda i, j: (i, 0)),
                pl.BlockSpec((1, d), lambda i, j: (0, 0)),
                pl.BlockSpec((d, tn), lambda i, j: (0, j))]
    out_specs = pl.BlockSpec((tm, tn), lambda i, j: (i, j))
    out_shape = jax.ShapeDtypeStruct((t, n), F32)
    scratch = [pltpu.VMEM((tm, d), BF16)]
    if w_small is None:
        return pl.pallas_call(_norm_matmul_kernel, grid=grid, in_specs=in_specs, out_specs=out_specs,
                              out_shape=out_shape, scratch_shapes=scratch,
                              compiler_params=_params("parallel", "arbitrary"),
                              name="norm_matmul")(x, g, w)
    ns = w_small.shape[1]
    in_specs.append(pl.BlockSpec((d, ns), lambda i, j: (0, 0)))
    return pl.pallas_call(_norm_matmul2_kernel, grid=grid, in_specs=in_specs,
                          out_specs=[out_specs, pl.BlockSpec((tm, ns), lambda i, j: (i, 0))],
                          out_shape=[out_shape, jax.ShapeDtypeStruct((t, ns), F32)],
                          scratch_shapes=scratch,
                          compiler_params=_params("parallel", "arbitrary"),
                          name="in_proj")(x, g, w, w_small)


def _s5_param_kernel(lrr_ref, lir_ref, lrc_ref, lic_ref, ldt_ref, br_ref, bi_ref, cr_ref, ci_ref,
                     k_ref, car_ref, cai_ref, bpr_ref, bpi_ref, ap_ref):
    dt = jnp.exp(ldt_ref[0])

    def discretise(lam_re, lam_im):
        mag = jnp.exp(lam_re * dt)
        return mag * jnp.cos(lam_im * dt), mag * jnp.sin(lam_im * dt)

    a_re_r, a_im_r = discretise(lrr_ref[0], lir_ref[0])
    lam_re, lam_im = lrc_ref[0], lic_ref[0]
    a_re_c, a_im_c = discretise(lam_re, lam_im)
    den = lam_re * lam_re + lam_im * lam_im
    coef_re = ((a_re_c - 1.0) * lam_re + a_im_c * lam_im) / den
    coef_im = (a_im_c * lam_re - (a_re_c - 1.0) * lam_im) / den
    b_re, b_im = br_ref[0], bi_ref[0]
    bb_re = coef_re * b_re - coef_im * b_im
    bb_im = coef_re * b_im + coef_im * b_re
    c_re, c_im = cr_ref[0], ci_ref[0]

    p_re_r, p_im_r = jnp.ones_like(a_re_r), jnp.zeros_like(a_re_r)
    p_re_c, p_im_c = jnp.ones_like(a_re_c), jnp.zeros_like(a_re_c)
    ca_re, ca_im, bp_re, bp_im, saved = [], [], [], [], {}
    for j in range(S5_TC + 1):
        saved[j] = (p_re_r, p_im_r)
        ca_re.append(c_re * p_re_r - c_im * p_im_r)
        ca_im.append(c_re * p_im_r + c_im * p_re_r)
        if j < S5_TC:
            bp_re.append(p_re_c * bb_re - p_im_c * bb_im)
            bp_im.append(p_re_c * bb_im + p_im_c * bb_re)
        p_re_r, p_im_r = p_re_r * a_re_r - p_im_r * a_im_r, p_re_r * a_im_r + p_im_r * a_re_r
        p_re_c, p_im_c = p_re_c * a_re_c - p_im_c * a_im_c, p_re_c * a_im_c + p_im_c * a_re_c
    ca_re = jnp.concatenate(ca_re, axis=0)
    ca_im = jnp.concatenate(ca_im, axis=0)
    n_k = S5_TC * S5_GROUP
    hi = lax.Precision.HIGHEST
    k_ref[0] = (jnp.dot(ca_re[:n_k], bb_re, precision=hi, preferred_element_type=F32)
                - jnp.dot(ca_im[:n_k], bb_im, precision=hi, preferred_element_type=F32))
    car_ref[0] = ca_re
    cai_ref[0] = ca_im
    bpr_ref[0] = jnp.concatenate(bp_re, axis=0)
    bpi_ref[0] = jnp.concatenate(bp_im, axis=0)
    ap_ref[0] = jnp.concatenate([saved[S5_TC][0], saved[S5_TC][1],
                                 saved[S5_TC // 2][0], saved[S5_TC // 2][1]], axis=0)


def _s5_params(lam_re, lam_im, log_dt, b_re, b_im, c_re, c_im):
    g, p, c = S5_GROUPS, S5_STATE, S5_GROUP
    blk = lambda *shape: pl.BlockSpec((1,) + shape, lambda i: (i, 0, 0))
    out = lambda *shape: jax.ShapeDtypeStruct((g,) + shape, F32)
    return pl.pallas_call(
        _s5_param_kernel, grid=(g,),
        in_specs=[blk(1, p), blk(1, p), blk(p, 1), blk(p, 1), blk(1, 1),
                  blk(p, c), blk(p, c), blk(c, p), blk(c, p)],
        out_specs=[blk(S5_TC * c, c), blk((S5_TC + 1) * c, p), blk((S5_TC + 1) * c, p),
                   blk(S5_TC * p, c), blk(S5_TC * p, c), blk(4, p)],
        out_shape=[out(S5_TC * c, c), out((S5_TC + 1) * c, p), out((S5_TC + 1) * c, p),
                   out(S5_TC * p, c), out(S5_TC * p, c), out(4, p)],
        compiler_params=_params("parallel"),
        name="s5_params",
    )(lam_re.reshape(g, 1, p), lam_im.reshape(g, 1, p), lam_re.reshape(g, p, 1), lam_im.reshape(g, p, 1),
      log_dt.reshape(g, 1, 1), b_re, b_im, c_re, c_im)


def _s5_chunk_weights(k, ca_re, ca_im, bp_re, bp_im, apow, d_skip, tc):
    g, p, c = S5_GROUPS, S5_STATE, S5_GROUP
    w = tc * c
    pos = jnp.arange(tc)
    lag = pos[None, :] - pos[:, None]
    k4 = k.reshape(g, S5_TC, c, c)
    m = jnp.where((lag >= 0)[None, :, :, None, None], k4[:, jnp.clip(lag, 0, S5_TC - 1)], 0.0)
    m = m.transpose(0, 1, 4, 2, 3).reshape(g, w, w)

    def end_state(bp):
        return bp.reshape(g, S5_TC, p, c)[:, :tc][:, ::-1].transpose(0, 1, 3, 2).reshape(g, w, p)

    e_re, e_im = end_state(bp_re), end_state(bp_im)
    zero = jnp.zeros_like(e_re)
    odd = (jnp.arange(g) % 2 == 1)[:, None, None]
    pair_cols = jnp.where(odd, jnp.concatenate([zero, e_re, zero, e_im], axis=-1),
                          jnp.concatenate([e_re, zero, e_im, zero], axis=-1))
    w1 = jnp.concatenate([m, pair_cols], axis=-1).astype(BF16)

    def carry_in(ca):
        ca = ca.reshape(g, S5_TC + 1, c, p)[:, 1:tc + 1]
        return ca.transpose(0, 3, 1, 2).reshape(g // 2, 2, p, w)

    def pair_rows(x):
        z = jnp.zeros_like(x[:, 0])
        return jnp.concatenate([jnp.concatenate([x[:, 0], z], axis=-1),
                                jnp.concatenate([z, x[:, 1]], axis=-1)], axis=1)

    cc = jnp.concatenate([pair_rows(carry_in(ca_re)), pair_rows(-carry_in(ca_im))], axis=1).astype(BF16)
    row = 0 if tc == S5_TC else 2
    a = jnp.concatenate([apow[:, row].reshape(S5_NBLK, 1, S5_CH), apow[:, row + 1].reshape(S5_NBLK, 1, S5_CH)],
                        axis=-1)
    d = jnp.tile(d_skip, (1, tc)).reshape(g, 1, w)
    return w1, cc, a, d


def _s5_chunk_kernel(u_ref, w1_ref, cc_ref, a_ref, d_ref, h0r_ref, h0i_ref,
                     y_ref, hfr_ref, hfi_ref, e_ref, *, ns, nk, nseg, uw):
    ch = S5_CH
    pw = 2 * S5_STATE

    for pair in range(S5_GB // 2):
        ends = None
        for gi in range(2):
            g = 2 * pair + gi
            em = jnp.dot(u_ref[g].astype(BF16), w1_ref[g], preferred_element_type=F32)
            y_ref[g] = em[:, :uw]
            ends = em[:, uw:] if ends is None else ends + em[:, uw:]
        e_ref[:, pair * pw:(pair + 1) * pw] = ends[:, :pw]
        e_ref[:, ch + pair * pw:ch + (pair + 1) * pw] = ends[:, pw:]

    a_row_re, a_row_im = a_ref[0, :, 0:ch], a_ref[0, :, ch:2 * ch]
    a_re = jnp.broadcast_to(a_row_re, (ns, ch))
    a_im = jnp.broadcast_to(a_row_im, (ns, ch))

    def slab(t):
        return pl.ds(pl.multiple_of(t * ns, ns), ns)

    def run_scan(h_re, h_im, store):
        def step(t, carry):
            hr, hi = carry
            er = e_ref[slab(t), 0:ch]
            ei = e_ref[slab(t), ch:2 * ch]
            if store:
                e_ref[slab(t), 0:ch] = hr
                e_ref[slab(t), ch:2 * ch] = hi
            return a_re * hr - a_im * hi + er, a_re * hi + a_im * hr + ei

        if nk == 1:
            return step(0, (h_re, h_im))
        return lax.fori_loop(0, nk, step, (h_re, h_im), unroll=4)

    h0_re, h0_im = h0r_ref[...], h0i_ref[...]
    if nseg == 1:
        f_re, f_im = run_scan(h0_re, h0_im, store=True)
    else:
        zero = jnp.zeros((ns, ch), F32)
        e_re, e_im = run_scan(zero, zero, store=False)
        p_re, p_im = a_row_re, a_row_im
        for _ in range(int(math.log2(nk))):
            p_re, p_im = p_re * p_re - p_im * p_im, 2.0 * (p_re * p_im)
        second = (lax.broadcasted_iota(jnp.int32, (ns, ch), 0) & 1) == 1
        init_re = jnp.where(second, p_re * h0_re - p_im * h0_im + pltpu.roll(e_re, 1, 0), h0_re)
        init_im = jnp.where(second, p_re * h0_im + p_im * h0_re + pltpu.roll(e_im, 1, 0), h0_im)
        f_re = p_re * init_re - p_im * init_im + e_re
        f_im = p_re * init_im + p_im * init_re + e_im
        run_scan(init_re, init_im, store=True)
    hfr_ref[...] = f_re
    hfi_ref[...] = f_im

    for pair in range(S5_GB // 2):
        h_in = jnp.concatenate([e_ref[:, pair * pw:(pair + 1) * pw],
                                e_ref[:, ch + pair * pw:ch + (pair + 1) * pw]], axis=1).astype(BF16)
        carried = jnp.dot(h_in, cc_ref[pair], preferred_element_type=F32)
        for gi in range(2):
            g = 2 * pair + gi
            y = y_ref[g] + carried[:, gi * uw:(gi + 1) * uw] + d_ref[g] * u_ref[g]
            y_ref[g] = jax.nn.gelu(y, approximate=True)


def _s5_chunked(u, w1, cc, a, d, h0_re, h0_im, *, ns, nk, nseg):
    g, rows, uw = u.shape
    assert rows == ns * nk and nseg in (1, 2) and nk & (nk - 1) == 0
    kern = functools.partial(_s5_chunk_kernel, ns=ns, nk=nk, nseg=nseg, uw=uw)
    grp = lambda *shape: pl.BlockSpec((S5_GB,) + shape, lambda k: (k, 0, 0))
    state_spec = pl.BlockSpec((ns, S5_CH), lambda k: (0, k))
    state_shape = jax.ShapeDtypeStruct((ns, S5_GROUPS * S5_STATE), F32)
    return pl.pallas_call(
        kern, grid=(S5_NBLK,),
        in_specs=[grp(rows, uw), grp(uw, uw + 4 * S5_STATE),
                  pl.BlockSpec((S5_GB // 2, 4 * S5_STATE, 2 * uw), lambda k: (k, 0, 0)),
                  pl.BlockSpec((1, 1, 2 * S5_CH), lambda k: (k, 0, 0)),
                  grp(1, uw), state_spec, state_spec],
        out_specs=[grp(rows, uw), state_spec, state_spec],
        out_shape=[jax.ShapeDtypeStruct((g, rows, uw), F32), state_shape, state_shape],
        scratch_shapes=[pltpu.VMEM((rows, 2 * S5_CH), F32)],
        compiler_params=_params("parallel"),
        name="s5_chunked",
    )(u, w1, cc, a, d, h0_re, h0_im)


def _s5_branch(z, weights, h0_re, h0_im, *, batch, seq):
    g, c = S5_GROUPS, S5_GROUP
    u = z[:, COL_U:COL_U + S5_WIDTH]
    if seq > S5_TC:
        tc, nseg = S5_TC, 2
        nk = seq // (tc * nseg)
        ns = batch * nseg
        u = u.reshape(batch, nseg, nk, tc, g, c).transpose(4, 2, 0, 1, 3, 5).reshape(g, nk * ns, tc * c)
        h0_re, h0_im = jnp.repeat(h0_re, nseg, axis=0), jnp.repeat(h0_im, nseg, axis=0)
    else:
        tc, nseg, nk, ns = seq, 1, 1, batch
        assert tc == S5_TC // 2
        u = u.reshape(batch, tc, g, c).transpose(2, 0, 1, 3).reshape(g, batch, tc * c)
    y, hf_re, hf_im = _s5_chunked(u, *weights[tc], h0_re, h0_im, ns=ns, nk=nk, nseg=nseg)
    if nseg == 2:
        y = y.reshape(g, nk, batch, nseg, tc, c).transpose(2, 3, 1, 4, 0, 5)
        hf_re, hf_im = hf_re[1::2], hf_im[1::2]
    else:
        y = y.reshape(g, batch, tc, c).transpose(1, 2, 0, 3)
    return y.reshape(batch * seq, S5_WIDTH), hf_re, hf_im


def _gla_kernel(q_ref, k_ref, v_ref, r_ref, al_ref, wa_ref, ba_ref, gn_ref, s0_ref,
                y_ref, sf_ref, s_scr, *, ntile, groups, chained, width):
    rt = GLA_TILE
    c = rt // groups
    shift = int(math.log2(c))
    row_g = lax.broadcasted_iota(jnp.int32, (rt, rt), 0)
    col_g = lax.broadcasted_iota(jnp.int32, (rt, rt), 1)
    same = (row_g >> shift) == (col_g >> shift)
    causal = same & (row_g >= col_g)
    cum_w = jnp.concatenate([causal.astype(BF16), same.astype(BF16)], axis=0)
    sel_r = lax.broadcasted_iota(jnp.int32, (rt, groups * LANES), 0) >> shift
    sel_c = lax.broadcasted_iota(jnp.int32, (rt, groups * LANES), 1) >> int(math.log2(LANES))
    sel = (sel_r == sel_c).astype(BF16)
    tn_dims = (((0,), (0,)), ((), ()))
    nt_dims = (((1,), (1,)), ((), ()))

    def split3(x):
        hi = x.astype(BF16)
        r1 = x - hi.astype(F32)
        mid = r1.astype(BF16)
        lo = (r1 - mid.astype(F32)).astype(BF16)
        return jnp.concatenate([hi, mid, lo], axis=1)

    if chained:
        s_scr[...] = s0_ref[0, 0]

    def body(it, carry):
        tiles = [it * width + u for u in range(width)]
        rows = [pl.ds(pl.multiple_of(t * rt, rt), rt) for t in tiles]
        log_a = []
        for u in range(width):
            x = jnp.dot(al_ref[rows[u], :].astype(BF16), wa_ref[...], preferred_element_type=F32) + ba_ref[...]
            log_a.append((jnp.minimum(x, 0.0) - jnp.log1p(jnp.exp(-jnp.abs(x)))) * (1.0 / GLA_TAU))
        la3 = [split3(la) for la in log_a]
        cums, e_col = [], []
        for u in range(width):
            cs = jnp.dot(cum_w, la3[u], preferred_element_type=F32)
            cums.append(cs[:, :LANES] + cs[:, LANES:2 * LANES] + cs[:, 2 * LANES:])
            tot = lax.dot_general(la3[u], sel, tn_dims, preferred_element_type=F32)
            e_col.append(jnp.exp(tot[:LANES] + tot[LANES:2 * LANES] + tot[2 * LANES:]))
        qb, kd, v, att, upd = [], [], [], [], []
        for u in range(width):
            b = cums[u][:rt]
            b_end = cums[u][rt:]
            k = k_ref[rows[u], :]
            v.append(v_ref[rows[u], :])
            qb.append(q_ref[rows[u], :] * (GLA_DK ** -0.5) * jnp.exp(b))
            kd.append(k * jnp.exp(b_end - b))
            a = lax.dot_general(qb[u].astype(BF16), (k * jnp.exp(-b)).astype(BF16), nt_dims,
                                preferred_element_type=F32)
            att.append(jnp.where(causal, a, 0.0).astype(BF16))
            upd.append([lax.dot_general(kd[u][g * c:(g + 1) * c].astype(BF16), v[u][g * c:(g + 1) * c].astype(BF16),
                                        tn_dims, preferred_element_type=F32) for g in range(groups)])
        o = [jnp.dot(att[u], v[u].astype(BF16), preferred_element_type=F32) for u in range(width)]
        for u in range(width):
            o_state = []
            for g in range(groups):
                s = s_scr[...] if chained else s0_ref[tiles[u] * groups + g, 0]
                o_state.append(jnp.dot(qb[u][g * c:(g + 1) * c].astype(BF16), s.astype(BF16),
                                       preferred_element_type=F32))
                e = e_col[u][:, g * LANES:(g + 1) * LANES]
                s_new = jnp.concatenate([s[:, :LANES] * e, s[:, LANES:] * e], axis=1) + upd[u][g]
                if chained:
                    s_scr[...] = s_new
                else:
                    sf_ref[tiles[u] * groups + g, 0] = s_new
            o[u] = o[u] + (o_state[0] if groups == 1 else jnp.concatenate(o_state, axis=0))
        for u in range(width):
            y = o[u] * lax.rsqrt(jnp.mean(o[u] * o[u], axis=-1, keepdims=True) + RMS_EPS)
            y = y * gn_ref[...]
            r = r_ref[rows[u], :]
            y_ref[rows[u], :] = y * (r * jax.nn.sigmoid(r))
        return carry

    lax.fori_loop(0, ntile // width, body, 0)
    if chained:
        sf_ref[0, 0] = s_scr[...]


def _gla(z, alow, w_a2, b_a, g_norm, s0, *, batch, seq, chained):
    chunk = math.gcd(seq, GLA_CHUNK)
    if chained:
        assert chunk == GLA_TILE
        nb, groups, width = 1, 1, 4
    else:
        assert GLA_TILE % seq == 0 and chunk == seq
        nb, groups, width = 32, GLA_TILE // seq, 2
    rows = nb * seq
    kern = functools.partial(_gla_kernel, ntile=rows // GLA_TILE, groups=groups, chained=chained, width=width)
    state_spec = pl.BlockSpec((nb, 1, GLA_DK, GLA_DV), lambda i, h: (i, h, 0, 0))
    return pl.pallas_call(
        kern, grid=(batch // nb, GLA_HEADS),
        in_specs=[pl.BlockSpec((rows, GLA_DK), lambda i, h: (i, COL_Q // GLA_DK + h)),
                  pl.BlockSpec((rows, GLA_DK), lambda i, h: (i, COL_K // GLA_DK + h)),
                  pl.BlockSpec((rows, GLA_DV), lambda i, h: (i, COL_V // GLA_DV + h)),
                  pl.BlockSpec((rows, GLA_DV), lambda i, h: (i, COL_R // GLA_DV + h)),
                  pl.BlockSpec((rows, LANES), lambda i, h: (i, 0)),
                  pl.BlockSpec((LANES, GLA_DK), lambda i, h: (0, h)),
                  pl.BlockSpec((1, GLA_DK), lambda i, h: (0, h)),
                  pl.BlockSpec((1, GLA_DV), lambda i, h: (0, h)),
                  state_spec],
        out_specs=[pl.BlockSpec((rows, GLA_DV), lambda i, h: (i, h)), state_spec],
        out_shape=[jax.ShapeDtypeStruct((batch * seq, GLA_HEADS * GLA_DV), F32),
                   jax.ShapeDtypeStruct((batch, GLA_HEADS, GLA_DK, GLA_DV), F32)],
        scratch_shapes=[pltpu.VMEM((GLA_DK, GLA_DV), F32)],
        compiler_params=_params("parallel", "parallel"),
        name="gla",
    )(z, z, z, z, alow, w_a2, b_a, g_norm, s0)


def _attn_kernel(q_ref, k_ref, v_ref, o_ref, *, nb, seq, heads):
    nt_dims = (((1,), (1,)), ((), ()))

    def body(bi, carry):
        rows = pl.ds(pl.multiple_of(bi * seq, SUBLANES), seq)
        for h in range(heads):
            cols = slice(h * XA_HEAD_DIM, (h + 1) * XA_HEAD_DIM)
            q = q_ref[rows, cols].astype(BF16)
            k = k_ref[bi, :, cols].astype(BF16)
            s = lax.dot_general(q, k, nt_dims, preferred_element_type=F32) * (XA_HEAD_DIM ** -0.5)
            p = jnp.exp(s - jnp.max(s, axis=-1, keepdims=True))
            p = p / jnp.sum(p, axis=-1, keepdims=True)
            o_ref[rows, cols] = jnp.dot(p.astype(BF16), v_ref[bi, :, cols].astype(BF16),
                                        preferred_element_type=F32)
        return carry

    lax.fori_loop(0, nb, body, 0)


def _attn_prompt(z, mem_k, mem_v, *, batch, seq, rc):
    nrc = seq // rc
    kern = functools.partial(_attn_kernel, nb=1, seq=rc, heads=1)
    kv_spec = pl.BlockSpec((1, MEM_LEN, XA_HEAD_DIM), lambda b, h, c: (b, 0, h))
    return pl.pallas_call(
        kern, grid=(batch, XA_HEADS, nrc),
        in_specs=[pl.BlockSpec((rc, XA_HEAD_DIM), lambda b, h, c: (b * nrc + c, COL_QX // XA_HEAD_DIM + h)),
                  kv_spec, kv_spec],
        out_specs=pl.BlockSpec((rc, XA_HEAD_DIM), lambda b, h, c: (b * nrc + c, h)),
        out_shape=jax.ShapeDtypeStruct((batch * seq, XA_WIDTH), F32),
        compiler_params=_params("parallel", "parallel", "parallel"),
        name="attn_prompt",
    )(z, mem_k, mem_v)


XA_HALF = XA_HEAD_DIM // 2
XA_ROWS = 2 * XA_HEADS


def _attn_cache_kernel(q_ref, k_ref, v_ref, o_ref, *, nb, seq, width):
    nt_dims = (((1,), (1,)), ((), ()))

    def half_rows(ref, bi, h, half):
        return ref[bi, pl.ds(half * XA_HEADS + h, MEM_LEN, stride=XA_ROWS), :].astype(BF16)

    def body(it, carry):
        work = [(it * width + u, h) for u in range(width) for h in range(XA_HEADS)]
        scores = []
        for bi, h in work:
            rows = pl.ds(pl.multiple_of(bi * seq, SUBLANES), seq)
            lo = h * XA_HEAD_DIM
            s = lax.dot_general(q_ref[rows, lo:lo + XA_HALF].astype(BF16), half_rows(k_ref, bi, h, 0),
                                nt_dims, preferred_element_type=F32)
            s = s + lax.dot_general(q_ref[rows, lo + XA_HALF:lo + XA_HEAD_DIM].astype(BF16),
                                    half_rows(k_ref, bi, h, 1), nt_dims, preferred_element_type=F32)
            scores.append(s * (XA_HEAD_DIM ** -0.5))
        probs = []
        for s in scores:
            p = jnp.exp(s - jnp.max(s, axis=-1, keepdims=True))
            probs.append((p / jnp.sum(p, axis=-1, keepdims=True)).astype(BF16))
        for (bi, h), p in zip(work, probs):
            rows = pl.ds(pl.multiple_of(bi * seq, SUBLANES), seq)
            lo = h * XA_HEAD_DIM
            for half in range(2):
                o_ref[rows, lo + half * XA_HALF:lo + (half + 1) * XA_HALF] = jnp.dot(
                    p, half_rows(v_ref, bi, h, half), preferred_element_type=F32)
        return carry

    lax.fori_loop(0, nb // width, body, 0)


def _cache_rows(cache):
    bs = cache.shape[0]
    c = cache.reshape(bs, MEM_LEN, XA_HEADS, 2, XA_HALF).transpose(0, 1, 3, 2, 4)
    return c.reshape(bs, MEM_LEN * XA_ROWS, XA_HALF)


def _attn_sample(z, mem_k, mem_v, *, batch, seq, nb):
    rows = nb * seq
    kern = functools.partial(_attn_cache_kernel, nb=nb, seq=seq, width=2)
    kv_spec = pl.BlockSpec((nb, MEM_LEN * XA_ROWS, XA_HALF), lambda i: (i, 0, 0))
    return pl.pallas_call(
        kern, grid=(batch // nb,),
        in_specs=[pl.BlockSpec((rows, XA_WIDTH), lambda i: (i, COL_QX // XA_WIDTH)), kv_spec, kv_spec],
        out_specs=pl.BlockSpec((rows, XA_WIDTH), lambda i: (i, 0)),
        out_shape=jax.ShapeDtypeStruct((batch * seq, XA_WIDTH), F32),
        compiler_params=_params("parallel"),
        name="attn_sample",
    )(z, mem_k, mem_v)


def _merge_kernel(ys_ref, yg_ref, yx_ref, g0_ref, g1_ref, g2_ref, wglu_ref, bglu_ref,
                  w0_ref, w1_ref, w2_ref, o_ref, s5_scr, gla_scr, xa_scr):
    @pl.when(pl.program_id(1) == 0)
    def _():
        y = ys_ref[...]
        lin = jnp.dot(y.astype(BF16), wglu_ref[...], preferred_element_type=F32) + bglu_ref[...]
        s5_scr[...] = (y * jax.nn.sigmoid(lin)).astype(BF16)
        gla_scr[...] = yg_ref[...].astype(BF16)
        xa_scr[...] = yx_ref[...].astype(BF16)

    j = pl.program_id(1)
    m = jax.nn.sigmoid(g0_ref[...]) * jnp.dot(s5_scr[...], w0_ref[j], preferred_element_type=F32)
    m = m + jax.nn.sigmoid(g1_ref[...]) * jnp.dot(gla_scr[...], w1_ref[j], preferred_element_type=F32)
    m = m + jax.nn.sigmoid(g2_ref[...]) * jnp.dot(xa_scr[...], w2_ref[j], preferred_element_type=F32)
    o_ref[...] = m.astype(o_ref.dtype)


def _merge(y_s5, y_gla, y_x, z, w_glu, b_glu, w_br_s5, w_br_gla, w_br_x, *, tm, tn):
    t = y_s5.shape[0]
    nj = D_MODEL // tn
    once = pl.Buffered(1)
    wide = pl.BlockSpec((tm, S5_WIDTH), lambda i, j: (i, 0))
    gate = lambda b: pl.BlockSpec((tm, tn), lambda i, j: (i, (COL_GATE + b * D_MODEL) // tn + j))
    w_br = pl.BlockSpec((nj, S5_WIDTH, tn), lambda i, j: (0, 0, 0), pipeline_mode=once)
    return pl.pallas_call(
        _merge_kernel, grid=(t // tm, nj),
        in_specs=[wide, wide, wide, gate(0), gate(1), gate(2),
                  pl.BlockSpec((S5_WIDTH, S5_WIDTH), lambda i, j: (0, 0), pipeline_mode=once),
                  pl.BlockSpec((1, S5_WIDTH), lambda i, j: (0, 0)),
                  w_br, w_br, w_br],
        out_specs=pl.BlockSpec((tm, tn), lambda i, j: (i, j)),
        out_shape=jax.ShapeDtypeStruct((t, D_MODEL), BF16),
        scratch_shapes=[pltpu.VMEM((tm, S5_WIDTH), BF16)] * 3,
        compiler_params=_params("parallel", "arbitrary"),
        name="merge",
    )(y_s5, y_gla, y_x, z, z, z, w_glu, b_glu, w_br_s5, w_br_gla, w_br_x)


def _out_proj_kernel(m_ref, w_ref, x_ref, o_ref):
    o_ref[...] = x_ref[...] + jnp.dot(m_ref[...], w_ref[...], preferred_element_type=F32)


def _out_proj(merged, w_out, x, *, tm, tn):
    t = x.shape[0]
    return pl.pallas_call(
        _out_proj_kernel, grid=(t // tm, D_MODEL // tn),
        in_specs=[pl.BlockSpec((tm, D_MODEL), lambda i, j: (i, 0)),
                  pl.BlockSpec((D_MODEL, tn), lambda i, j: (0, j)),
                  pl.BlockSpec((tm, tn), lambda i, j: (i, j))],
        out_specs=pl.BlockSpec((tm, tn), lambda i, j: (i, j)),
        out_shape=jax.ShapeDtypeStruct((t, D_MODEL), F32),
        compiler_params=_params("parallel", "parallel"),
        name="out_proj",
    )(merged, w_out, x)


def _ffn_kernel(x_ref, gf_ref, wg_ref, wu_ref, wd_ref, gl_ref, o_ref, h_scr, acc_scr):
    k = pl.program_id(1)

    @pl.when(k == 0)
    def _():
        h_scr[...] = _rms(x_ref[...], gf_ref[...]).astype(BF16)
        acc_scr[...] = jnp.zeros_like(acc_scr)

    h = h_scr[...]
    gate = jnp.dot(h, wg_ref[...], preferred_element_type=F32)
    up = jnp.dot(h, wu_ref[...], preferred_element_type=F32)
    act = (gate * jax.nn.sigmoid(gate) * up).astype(BF16)
    acc_scr[...] += jnp.dot(act, wd_ref[...], preferred_element_type=F32)

    @pl.when(k == pl.num_programs(1) - 1)
    def _():
        o_ref[...] = _rms(x_ref[...] + acc_scr[...], gl_ref[...])


def _ffn(x, g_ffn, w_gate, w_up, w_down, g_final, *, tm, th):
    t = x.shape[0]
    row = pl.BlockSpec((tm, D_MODEL), lambda i, k: (i, 0))
    vec = pl.BlockSpec((1, D_MODEL), lambda i, k: (0, 0))
    w_in = pl.BlockSpec((D_MODEL, th), lambda i, k: (0, k))
    return pl.pallas_call(
        _ffn_kernel, grid=(t // tm, FFN_HIDDEN // th),
        in_specs=[row, vec, w_in, w_in, pl.BlockSpec((th, D_MODEL), lambda i, k: (k, 0)), vec],
        out_specs=row,
        out_shape=jax.ShapeDtypeStruct((t, D_MODEL), F32),
        scratch_shapes=[pltpu.VMEM((tm, D_MODEL), BF16), pltpu.VMEM((tm, D_MODEL), F32)],
        compiler_params=_params("parallel", "arbitrary"),
        name="ffn",
    )(x, g_ffn, w_gate, w_up, w_down, g_final)


def _reorder_in_proj_kernel(wt_ref, wa_ref, main_ref, alow_ref):
    main_ref[...] = wt_ref[...].T.astype(BF16)

    @pl.when(pl.program_id(0) == 0)
    def _():
        lane = lax.broadcasted_iota(jnp.int32, alow_ref.shape, 1)
        alow_ref[...] = jnp.where(lane < GLA_RANK, wa_ref[...].T, 0.0).astype(BF16)


def _reorder_in_proj(w_in):
    d, n = w_in.shape
    wt = w_in.T
    tc = 512
    n_first, n_gate = COL_GATE // tc, (COL_QX - COL_GATE) // tc
    gate_start, qx_start = 4096 + GLA_RANK + XA_WIDTH, 4096 + GLA_RANK

    def src_row(j):
        t8, g8, q8 = tc // SUBLANES, gate_start // SUBLANES, qx_start // SUBLANES
        r8 = jnp.where(j < n_first, j * t8,
                       jnp.where(j < n_first + n_gate, g8 + (j - n_first) * t8,
                                 q8 + (j - n_first - n_gate) * t8))
        return r8 * SUBLANES

    return pl.pallas_call(
        _reorder_in_proj_kernel, grid=(Z_WIDTH // tc,),
        in_specs=[pl.BlockSpec((pl.Element(tc), pl.Element(d)), lambda j: (src_row(j), 0)),
                  pl.BlockSpec((pl.Element(LANES), pl.Element(d)), lambda j: (COL_GATE, 0))],
        out_specs=[pl.BlockSpec((d, tc), lambda j: (0, j)), pl.BlockSpec((d, LANES), lambda j: (0, 0))],
        out_shape=[jax.ShapeDtypeStruct((d, Z_WIDTH), BF16), jax.ShapeDtypeStruct((d, LANES), BF16)],
        compiler_params=_params("arbitrary"),
        name="reorder_in_proj",
    )(wt, wt)


def _col_tiles(w, tn):
    k, n = w.shape
    return w.reshape(k, n // tn, tn).transpose(1, 0, 2)


def _layer(x, mem_k, mem_v, s5_re0, s5_im0, gla_s0, w, *, batch, seq, segmented):
    z, alow = _norm_matmul(x, w['norm_mix'], w['w_main'], w['w_alow'], tm=1024, tn=1024)
    y_s5, hf_re, hf_im = _s5_branch(z, w['s5'], s5_re0, s5_im0, batch=batch, seq=seq)
    y_gla, gla_s = _gla(z, alow, w['gla_w_a2'], w['gla_b_a'], w['gla_norm'], gla_s0,
                        batch=batch, seq=seq, chained=segmented)
    if segmented:
        y_x = _attn_prompt(z, mem_k, mem_v, batch=batch, seq=seq, rc=512)
    else:
        y_x = _attn_sample(z, mem_k, mem_v, batch=batch, seq=seq, nb=8)
    merged = _merge(y_s5, y_gla, y_x, z, w['s5_w_glu'], w['s5_b_glu'],
                    w['w_br_s5'], w['w_br_gla'], w['w_br_xattn'], tm=512, tn=512)
    x1 = _out_proj(merged, w['w_out'], x, tm=1024, tn=512)
    y = _ffn(x1, w['norm_ffn'], w['w_ffn_gate'], w['w_ffn_up'], w['w_ffn_down'], w['norm_final'],
             tm=512, th=512)
    return y, hf_re, hf_im, gla_s


def kernel(x_prompt, x_sample, mem_prompt, state_s5_re, state_s5_im, state_gla, cache_mem_k, cache_mem_v,
           norm_mix, w_in, s5_lam_re, s5_lam_im, s5_log_dt, s5_b_re, s5_b_im, s5_c_re, s5_c_im,
           s5_d, s5_w_glu, s5_b_glu, gla_w_a2, gla_b_a, gla_norm, mem_norm, w_mem_k, w_mem_v,
           w_br_s5, w_br_gla, w_br_xattn, w_out, norm_ffn, w_ffn_gate, w_ffn_up, w_ffn_down, norm_final):
    depth = w_in.shape[0]
    assert depth == 1
    bp, sp, d = x_prompt.shape
    bs, ss, _ = x_sample.shape
    n_state = S5_GROUPS * S5_STATE
    row = lambda v: v.reshape(1, -1)

    l = 0
    w_main, w_alow = _reorder_in_proj(w_in[l])
    s5_tables = _s5_params(s5_lam_re[l], s5_lam_im[l], s5_log_dt[l], s5_b_re[l], s5_b_im[l], s5_c_re[l], s5_c_im[l])
    s5_w = {tc: _s5_chunk_weights(*s5_tables, s5_d[l], tc) for tc in (S5_TC, S5_TC // 2)}
    w = {
        'norm_mix': row(norm_mix[l]), 'w_main': w_main, 'w_alow': w_alow,
        's5': s5_w,
        's5_w_glu': s5_w_glu[l].astype(BF16), 's5_b_glu': row(s5_b_glu[l]),
        'gla_w_a2': jnp.pad(gla_w_a2[l], ((0, LANES - GLA_RANK), (0, 0))).astype(BF16),
        'gla_b_a': row(gla_b_a[l]), 'gla_norm': row(gla_norm[l]),
        'w_br_s5': _col_tiles(w_br_s5[l].astype(BF16), 512),
        'w_br_gla': _col_tiles(w_br_gla[l].astype(BF16), 512),
        'w_br_xattn': _col_tiles(w_br_xattn[l].astype(BF16), 512), 'w_out': w_out[l].astype(BF16),
        'norm_ffn': row(norm_ffn[l]), 'w_ffn_gate': w_ffn_gate[l].astype(BF16),
        'w_ffn_up': w_ffn_up[l].astype(BF16), 'w_ffn_down': w_ffn_down[l].astype(BF16),
        'norm_final': row(norm_final),
    }

    w_mem = jnp.concatenate([w_mem_k[l], w_mem_v[l]], axis=1).astype(BF16)
    mem_kv = _norm_matmul(mem_prompt.reshape(bp * MEM_LEN, d), row(mem_norm[l]), w_mem, tm=512, tn=1024)
    mk = mem_kv[:, :XA_WIDTH].reshape(bp, MEM_LEN, XA_WIDTH)
    mv = mem_kv[:, XA_WIDTH:].reshape(bp, MEM_LEN, XA_WIDTH)
    zero_s5 = jnp.zeros((bp, n_state), F32)
    zero_gla = jnp.zeros((bp, GLA_HEADS, GLA_DK, GLA_DV), F32)
    yp, p_re, p_im, p_gla = _layer(x_prompt.reshape(bp * sp, d), mk, mv, zero_s5, zero_s5, zero_gla, w,
                                   batch=bp, seq=sp, segmented=True)

    ys, s_re, s_im, s_gla = _layer(x_sample.reshape(bs * ss, d),
                                   _cache_rows(cache_mem_k[l]), _cache_rows(cache_mem_v[l]),
                                   state_s5_re[l].reshape(bs, n_state), state_s5_im[l].reshape(bs, n_state),
                                   state_gla[l], w, batch=bs, seq=ss, segmented=False)

    s5_shape_p = (1, bp, S5_GROUPS, S5_STATE)
    s5_shape_s = (1, bs, S5_GROUPS, S5_STATE)
    kv_shape = (1, bp, MEM_LEN, XA_HEADS, XA_HEAD_DIM)
    return (yp.reshape(bp, sp, d), ys.reshape(bs, ss, d),
            p_re.reshape(s5_shape_p), p_im.reshape(s5_shape_p), p_gla[None],
            mk.reshape(kv_shape), mv.reshape(kv_shape),
            s_re.reshape(s5_shape_s), s_im.reshape(s5_shape_s), s_gla[None])
```

```python
import functools
import math

import jax
import jax.numpy as jnp
from jax import lax
from jax.experimental import pallas as pl
from jax.experimental.pallas import tpu as pltpu

F32 = jnp.float32
BF16 = jnp.bfloat16

D_MODEL = 2048
S5_WIDTH = 1024
S5_GROUP = 16
S5_GROUPS = 64
S5_STATE = 64
GLA_HEADS = 4
GLA_DK = 128
GLA_DV = 256
GLA_RANK = 16
GLA_TAU = 16.0
GLA_CHUNK = 64
XA_HEADS = 4
XA_HEAD_DIM = 256
XA_WIDTH = 1024
MEM_LEN = 256
FFN_HIDDEN = 5632
RMS_EPS = 1e-6

LANES = 128
SUBLANES = 8
VMEM_LIMIT = 56 * 1024 * 1024

COL_U = 0
COL_Q = 1024
COL_K = 1536
COL_V = 2048
COL_R = 3072
COL_GATE = 4096
COL_QX = 10240
Z_WIDTH = 11264

S5_GB = 8
S5_CH = S5_GB * S5_STATE
S5_NBLK = S5_GROUPS // S5_GB
S5_TC = 16
GLA_TILE = 64


def _params(*sem):
    return pltpu.CompilerParams(dimension_semantics=sem, vmem_limit_bytes=VMEM_LIMIT)


def _rms(x, g):
    return x * lax.rsqrt(jnp.mean(x * x, axis=-1, keepdims=True) + RMS_EPS) * g


def _norm_matmul_kernel(x_ref, g_ref, w_ref, o_ref, h_ref):
    @pl.when(pl.program_id(1) == 0)
    def _():
        h_ref[...] = _rms(x_ref[...], g_ref[...]).astype(BF16)

    o_ref[...] = jnp.dot(h_ref[...], w_ref[...], preferred_element_type=F32)


def _norm_matmul2_kernel(x_ref, g_ref, w_ref, ws_ref, o_ref, os_ref, h_ref):
    @pl.when(pl.program_id(1) == 0)
    def _():
        h = _rms(x_ref[...], g_ref[...]).astype(BF16)
        h_ref[...] = h
        os_ref[...] = jnp.dot(h, ws_ref[...], preferred_element_type=F32)

    o_ref[...] = jnp.dot(h_ref[...], w_ref[...], preferred_element_type=F32)


def _norm_matmul(x, g, w, w_small=None, *, tm, tn):
    t, d = x.shape
    n = w.shape[1]
    grid = (t // tm, n // tn)
    in_specs = [pl.BlockSpec((tm, d), lambda i, j: (i, 0)),
                pl.BlockSpec((1, d), lambda i, j: (0, 0)),
                pl.BlockSpec((d, tn), lambda i, j: (0, j))]
    out_specs = pl.BlockSpec((tm, tn), lambda i, j: (i, j))
    out_shape = jax.ShapeDtypeStruct((t, n), F32)
    scratch = [pltpu.VMEM((tm, d), BF16)]
    if w_small is None:
        return pl.pallas_call(_norm_matmul_kernel, grid=grid, in_specs=in_specs, out_specs=out_specs,
                              out_shape=out_shape, scratch_shapes=scratch,
                              compiler_params=_params("parallel", "arbitrary"),
                              name="norm_matmul")(x, g, w)
    ns = w_small.shape[1]
    in_specs.append(pl.BlockSpec((d, ns), lambda i, j: (0, 0)))
    return pl.pallas_call(_norm_matmul2_kernel, grid=grid, in_specs=in_specs,
                          out_specs=[out_specs, pl.BlockSpec((tm, ns), lambda i, j: (i, 0))],
                          out_shape=[out_shape, jax.ShapeDtypeStruct((t, ns), F32)],
                          scratch_shapes=scratch,
                          compiler_params=_params("parallel", "arbitrary"),
                          name="in_proj")(x, g, w, w_small)


def _s5_param_kernel(lr_ref, li_ref, ldt_ref, btr_ref, bti_ref, cr_ref, ci_ref,
                     w16_ref, cc16_ref, w8_ref, cc8_ref, ap_ref):
    p = S5_STATE
    half = S5_TC // 2
    lam_re, lam_im = lr_ref[0], li_ref[0]
    dt = jnp.exp(ldt_ref[0])
    mag = jnp.exp(lam_re * dt)
    a_re = mag * jnp.cos(lam_im * dt)
    a_im = mag * jnp.sin(lam_im * dt)
    den = lam_re * lam_re + lam_im * lam_im
    coef_re = ((a_re - 1.0) * lam_re + a_im * lam_im) / den
    coef_im = (a_im * lam_re - (a_re - 1.0) * lam_im) / den
    bt_re, bt_im = btr_ref[0], bti_ref[0]
    bb_re = coef_re * bt_re - coef_im * bt_im
    bb_im = coef_re * bt_im + coef_im * bt_re
    c_re, c_im = cr_ref[0], ci_ref[0]

    pw_re, pw_im = jnp.ones_like(a_re), jnp.zeros_like(a_re)
    ca_re, ca_im, ab_re, ab_im, powers = [], [], [], [], {}
    for j in range(S5_TC + 1):
        powers[j] = (pw_re, pw_im)
        ca_re.append(c_re * pw_re - c_im * pw_im)
        ca_im.append(c_re * pw_im + c_im * pw_re)
        ab_re.append(bb_re * pw_re - bb_im * pw_im)
        ab_im.append(bb_re * pw_im + bb_im * pw_re)
        pw_re, pw_im = pw_re * a_re - pw_im * a_im, pw_re * a_im + pw_im * a_re
    lag_re = jnp.concatenate(ca_re[:S5_TC], axis=0)
    lag_im = jnp.concatenate(ca_im[:S5_TC], axis=0)
    end_re = jnp.concatenate([ab_re[S5_TC - 1 - s] for s in range(S5_TC)], axis=0)
    end_im = jnp.concatenate([ab_im[S5_TC - 1 - s] for s in range(S5_TC)], axis=0)
    car_re = jnp.concatenate(ca_re[1:], axis=0).T
    car_im = -jnp.concatenate(ca_im[1:], axis=0).T

    def pair_blocks(x):
        first = lax.broadcasted_iota(jnp.int32, x.shape, 0) < p
        return jnp.concatenate([jnp.where(first, x, 0.0), jnp.where(first, 0.0, x)], axis=1)

    cc16_ref[0] = jnp.concatenate([pair_blocks(car_re), pair_blocks(car_im)], axis=0).astype(BF16)
    n8 = half * S5_GROUP
    cc8_ref[0] = jnp.concatenate([pair_blocks(car_re[:, :n8]), pair_blocks(car_im[:, :n8])], axis=0).astype(BF16)

    nt_dims = (((1,), (1,)), ((), ()))
    hi = lax.Precision.HIGHEST
    lane = lax.broadcasted_iota(jnp.int32, (S5_GROUP, LANES), 1)
    end_lane = lax.broadcasted_iota(jnp.int32, end_re.shape, 1)
    for gi in range(2):
        mine = (lane < p) if gi == 0 else (lane >= p)
        own_re, own_im = jnp.where(mine, bb_re, 0.0), jnp.where(mine, bb_im, 0.0)
        strip = (lax.dot_general(own_re, lag_re, nt_dims, precision=hi, preferred_element_type=F32)
                 - lax.dot_general(own_im, lag_im, nt_dims, precision=hi, preferred_element_type=F32))
        lo, up = strip[:, :LANES], strip[:, LANES:]
        blocks = [strip]
        for s in range(1, S5_TC):
            sh = (s % half) * S5_GROUP
            lo_r = pltpu.roll(lo, sh, 1) if sh else lo
            up_r = pltpu.roll(up, sh, 1) if sh else up
            if s < half:
                blocks.append(jnp.concatenate([jnp.where(lane >= sh, lo_r, 0.0),
                                               jnp.where(lane >= sh, up_r, lo_r)], axis=1))
            else:
                blocks.append(jnp.concatenate([jnp.zeros_like(lo), jnp.where(lane >= sh, lo_r, 0.0)], axis=1))
        toeplitz = jnp.concatenate(blocks, axis=0)
        own_end = (end_lane < p) if gi == 0 else (end_lane >= p)
        ends = jnp.concatenate([jnp.where(own_end, end_re, 0.0), jnp.where(own_end, end_im, 0.0)], axis=1)
        w16_ref[gi] = jnp.concatenate([toeplitz, ends], axis=1).astype(BF16)
        w8_ref[gi] = jnp.concatenate([toeplitz[:n8, :n8], ends[n8:, :]], axis=1).astype(BF16)
    ap_ref[0] = jnp.concatenate([powers[S5_TC][0], powers[S5_TC][1], powers[half][0], powers[half][1]], axis=0)


def _s5_params(lam_re, lam_im, log_dt, b_re, b_im, c_re, c_im, d_skip):
    g, p, c = S5_GROUPS, S5_STATE, S5_GROUP
    npair = g // 2
    w16, w8 = S5_TC * c, S5_TC // 2 * c

    def pair_lanes(x):
        return x.reshape(npair, 2, x.shape[1], p).transpose(0, 2, 1, 3).reshape(npair, x.shape[1], 2 * p)

    row = lambda x: pair_lanes(x.reshape(g, 1, p))
    blk = lambda *shape: pl.BlockSpec((1,) + shape, lambda i: (i, 0, 0))
    two = lambda *shape: pl.BlockSpec((2,) + shape, lambda i: (i, 0, 0))
    wt16, cc16, wt8, cc8, apow = pl.pallas_call(
        _s5_param_kernel, grid=(npair,),
        in_specs=[blk(1, 2 * p)] * 3 + [blk(c, 2 * p)] * 4,
        out_specs=[two(w16, w16 + 4 * p), blk(4 * p, 2 * w16), two(w8, w8 + 4 * p), blk(4 * p, 2 * w8),
                   blk(4, 2 * p)],
        out_shape=[jax.ShapeDtypeStruct((g, w16, w16 + 4 * p), BF16),
                   jax.ShapeDtypeStruct((npair, 4 * p, 2 * w16), BF16),
                   jax.ShapeDtypeStruct((g, w8, w8 + 4 * p), BF16),
                   jax.ShapeDtypeStruct((npair, 4 * p, 2 * w8), BF16),
                   jax.ShapeDtypeStruct((npair, 4, 2 * p), F32)],
        compiler_params=_params("parallel"),
        name="s5_params",
    )(row(lam_re), row(lam_im), row(jnp.broadcast_to(log_dt[:, None], (g, p))),
      pair_lanes(b_re.transpose(0, 2, 1)), pair_lanes(b_im.transpose(0, 2, 1)),
      pair_lanes(c_re), pair_lanes(c_im))
    d = d_skip.reshape(S5_NBLK, 1, S5_GB * c)

    def transition(r):
        return jnp.concatenate([apow[:, r].reshape(S5_NBLK, 1, S5_CH), apow[:, r + 1].reshape(S5_NBLK, 1, S5_CH)],
                               axis=-1)

    return {S5_TC: (wt16, cc16, transition(0), d), S5_TC // 2: (wt8, cc8, transition(2), d)}


def _unit_transpose(vs):
    unit = lax.broadcasted_iota(jnp.int32, vs[0].shape, 1) >> int(math.log2(S5_GROUP))
    for dist in (4, 2, 1):
        keep = (unit & dist) == 0
        nxt = list(vs)
        for i in range(8):
            if i & dist == 0:
                a, b = vs[i], vs[i + dist]
                nxt[i] = jnp.where(keep, a, pltpu.roll(b, dist * S5_GROUP, 1))
                nxt[i + dist] = jnp.where(keep, pltpu.roll(a, LANES - dist * S5_GROUP, 1), b)
        vs = nxt
    return vs


def _s5_chunk_kernel(z_ref, w_ref, cc_ref, a_ref, d_ref, h0r_ref, h0i_ref,
                     y_ref, hfr_ref, hfi_ref, ut_ref, ug_ref, yg_ref, e_ref, *, nsl, rps, tc, nseg):
    uw = tc * S5_GROUP
    nq = uw // LANES
    rows = nsl * rps
    nk = rps if nsl > 1 else 1
    ns = rows // nk
    npair = S5_GB // 2
    rchunk = min(rows, 64)

    for s in range(nsl):
        for t in range(tc):
            ut_ref[t, s * rps:(s + 1) * rps, :] = z_ref[pl.ds(s * rps * tc + t, rps, stride=tc), :]

    def row_chunk(rc):
        return pl.ds(pl.multiple_of(rc * rchunk, rchunk), rchunk)

    def to_groups(rc, carry):
        for q in range(nq):
            per_group = _unit_transpose([ut_ref[q * 8 + t, row_chunk(rc), :] for t in range(8)])
            for g in range(S5_GB):
                ug_ref[g, row_chunk(rc), q * LANES:(q + 1) * LANES] = per_group[g].astype(BF16)
        return carry

    lax.fori_loop(0, rows // rchunk, to_groups, 0)

    for pair in range(npair):
        ends = None
        for gi in range(2):
            g = 2 * pair + gi
            em = jnp.dot(ug_ref[g], w_ref[g], preferred_element_type=F32)
            yg_ref[g] = em[:, :uw]
            ends = em[:, uw:] if ends is None else ends + em[:, uw:]
        for part, blk in ((ends[:, :LANES], pair), (ends[:, LANES:], npair + pair)):
            if nk == 1:
                e_ref[blk] = part
            else:
                for s in range(nsl):
                    e_ref[blk, pl.ds(s, rps, stride=nsl), :] = part[s * rps:(s + 1) * rps, :]

    a_row = [(a_ref[0, :, i * LANES:(i + 1) * LANES], a_ref[0, :, S5_CH + i * LANES:S5_CH + (i + 1) * LANES])
             for i in range(npair)]
    a_full = [(jnp.broadcast_to(ar, (ns, LANES)), jnp.broadcast_to(ai, (ns, LANES))) for ar, ai in a_row]

    def slab(t):
        return pl.ds(pl.multiple_of(t * ns, ns), ns)

    def run_scan(h, store):
        def step(t, carry):
            out = []
            for i in range(npair):
                hr, hi = carry[i]
                ar, ai = a_full[i]
                er = e_ref[i, slab(t), :]
                ei = e_ref[npair + i, slab(t), :]
                if store:
                    e_ref[i, slab(t), :] = hr
                    e_ref[npair + i, slab(t), :] = hi
                out.append((ar * hr - ai * hi + er, ar * hi + ai * hr + ei))
            return tuple(out)

        if nk == 1:
            return step(0, h)
        return lax.fori_loop(0, nk, step, h, unroll=4)

    h0 = tuple((h0r_ref[:, i * LANES:(i + 1) * LANES], h0i_ref[:, i * LANES:(i + 1) * LANES]) for i in range(npair))
    if nseg == 1:
        final = run_scan(h0, store=True)
    else:
        zero = jnp.zeros((ns, LANES), F32)
        seg_end = run_scan(tuple((zero, zero) for _ in range(npair)), store=False)
        second = (lax.broadcasted_iota(jnp.int32, (ns, LANES), 0) & 1) == 1
        init, final = [], []
        for i in range(npair):
            p_re, p_im = a_row[i]
            for _ in range(int(math.log2(nk))):
                p_re, p_im = p_re * p_re - p_im * p_im, 2.0 * (p_re * p_im)
            (hr, hi), (er, ei) = h0[i], seg_end[i]
            i_re = jnp.where(second, p_re * hr - p_im * hi + pltpu.roll(er, 1, 0), hr)
            i_im = jnp.where(second, p_re * hi + p_im * hr + pltpu.roll(ei, 1, 0), hi)
            init.append((i_re, i_im))
            final.append((p_re * i_re - p_im * i_im + er, p_re * i_im + p_im * i_re + ei))
        run_scan(tuple(init), store=True)
    for i in range(npair):
        hfr_ref[:, i * LANES:(i + 1) * LANES] = final[i][0]
        hfi_ref[:, i * LANES:(i + 1) * LANES] = final[i][1]

    def chunk_rows(blk):
        if nk == 1:
            return e_ref[blk]
        return jnp.concatenate([e_ref[blk, pl.ds(s, rps, stride=nsl), :] for s in range(nsl)], axis=0)

    for pair in range(npair):
        h_in = jnp.concatenate([chunk_rows(pair), chunk_rows(npair + pair)], axis=1).astype(BF16)
        carried = jnp.dot(h_in, cc_ref[pair], preferred_element_type=F32)
        for gi in range(2):
            yg_ref[2 * pair + gi] += carried[:, gi * uw:(gi + 1) * uw]

    d_row = d_ref[0]

    def to_tokens(rc, carry):
        for q in range(nq):
            per_tau = _unit_transpose([yg_ref[g, row_chunk(rc), q * LANES:(q + 1) * LANES] for g in range(S5_GB)])
            for t in range(8):
                y = per_tau[t] + d_row * ut_ref[q * 8 + t, row_chunk(rc), :]
                ut_ref[q * 8 + t, row_chunk(rc), :] = jax.nn.gelu(y, approximate=True)
        return carry

    lax.fori_loop(0, rows // rchunk, to_tokens, 0)
    for s in range(nsl):
        for t in range(tc):
            y_ref[pl.ds(s * rps * tc + t, rps, stride=tc), :] = ut_ref[t, s * rps:(s + 1) * rps, :]


def _s5_branch(z, weights, h0_re, h0_im, *, batch, seq):
    tokens = batch * seq
    if seq > S5_TC:
        tc, nseg = S5_TC, 2
        nsl, rps = batch * nseg, seq // (nseg * tc)
        assert rps & (rps - 1) == 0
        h0_re, h0_im = jnp.repeat(h0_re, nseg, axis=0), jnp.repeat(h0_im, nseg, axis=0)
    else:
        tc, nseg, nsl, rps = seq, 1, 1, batch
    w, cc, a, d = weights[tc]
    uw = tc * S5_GROUP
    rows = nsl * rps
    ns = h0_re.shape[0]
    kern = functools.partial(_s5_chunk_kernel, nsl=nsl, rps=rps, tc=tc, nseg=nseg)
    state_spec = pl.BlockSpec((ns, S5_CH), lambda k: (0, k))
    state_shape = jax.ShapeDtypeStruct((ns, S5_GROUPS * S5_STATE), F32)
    y, hf_re, hf_im = pl.pallas_call(
        kern, grid=(S5_NBLK,),
        in_specs=[pl.BlockSpec((tokens, LANES), lambda k: (0, COL_U // LANES + k)),
                  pl.BlockSpec((S5_GB, uw, uw + 4 * S5_STATE), lambda k: (k, 0, 0)),
                  pl.BlockSpec((S5_GB // 2, 4 * S5_STATE, 2 * uw), lambda k: (k, 0, 0)),
                  pl.BlockSpec((1, 1, 2 * S5_CH), lambda k: (k, 0, 0)),
                  pl.BlockSpec((1, 1, LANES), lambda k: (k, 0, 0)),
                  state_spec, state_spec],
        out_specs=[pl.BlockSpec((tokens, LANES), lambda k: (0, k)), state_spec, state_spec],
        out_shape=[jax.ShapeDtypeStruct((tokens, S5_WIDTH), F32), state_shape, state_shape],
        scratch_shapes=[pltpu.VMEM((tc, rows, LANES), F32), pltpu.VMEM((S5_GB, rows, uw), BF16),
                        pltpu.VMEM((S5_GB, rows, uw), F32), pltpu.VMEM((S5_GB, rows, LANES), F32)],
        compiler_params=_params("parallel"),
        name="s5_chunked",
    )(z, w, cc, a, d, h0_re, h0_im)
    if nseg == 2:
        hf_re, hf_im = hf_re[1::2], hf_im[1::2]
    return y, hf_re, hf_im


def _gla_kernel(q_ref, k_ref, v_ref, r_ref, al_ref, wa_ref, ba_ref, gn_ref, s0_ref,
                y_ref, sf_ref, s_scr, *, ntile, groups, chained, width):
    rt = GLA_TILE
    c = rt // groups
    shift = int(math.log2(c))
    row_g = lax.broadcasted_iota(jnp.int32, (rt, rt), 0)
    col_g = lax.broadcasted_iota(jnp.int32, (rt, rt), 1)
    same = (row_g >> shift) == (col_g >> shift)
    causal = same & (row_g >= col_g)
    cum_w = jnp.concatenate([causal.astype(BF16), same.astype(BF16)], axis=0)
    sel_r = lax.broadcasted_iota(jnp.int32, (rt, groups * LANES), 0) >> shift
    sel_c = lax.broadcasted_iota(jnp.int32, (rt, groups * LANES), 1) >> int(math.log2(LANES))
    sel = (sel_r == sel_c).astype(BF16)
    tn_dims = (((0,), (0,)), ((), ()))
    nt_dims = (((1,), (1,)), ((), ()))

    def split3(x):
        hi = x.astype(BF16)
        r1 = x - hi.astype(F32)
        mid = r1.astype(BF16)
        lo = (r1 - mid.astype(F32)).astype(BF16)
        return jnp.concatenate([hi, mid, lo], axis=1)

    if chained:
        s_scr[...] = s0_ref[0, 0]

    def body(it, carry):
        tiles = [it * width + u for u in range(width)]
        rows = [pl.ds(pl.multiple_of(t * rt, rt), rt) for t in tiles]
        log_a = []
        for u in range(width):
            x = jnp.dot(al_ref[rows[u], :].astype(BF16), wa_ref[...], preferred_element_type=F32) + ba_ref[...]
            log_a.append((jnp.minimum(x, 0.0) - jnp.log1p(jnp.exp(-jnp.abs(x)))) * (1.0 / GLA_TAU))
        la3 = [split3(la) for la in log_a]
        cums, e_col = [], []
        for u in range(width):
            cs = jnp.dot(cum_w, la3[u], preferred_element_type=F32)
            cums.append(cs[:, :LANES] + cs[:, LANES:2 * LANES] + cs[:, 2 * LANES:])
            tot = lax.dot_general(la3[u], sel, tn_dims, preferred_element_type=F32)
            e_col.append(jnp.exp(tot[:LANES] + tot[LANES:2 * LANES] + tot[2 * LANES:]))
        qb, kd, v, att, upd = [], [], [], [], []
        for u in range(width):
            b = cums[u][:rt]
            b_end = cums[u][rt:]
            k = k_ref[rows[u], :]
            v.append(v_ref[rows[u], :])
            qb.append(q_ref[rows[u], :] * (GLA_DK ** -0.5) * jnp.exp(b))
            kd.append(k * jnp.exp(b_end - b))
            a = lax.dot_general(qb[u].astype(BF16), (k * jnp.exp(-b)).astype(BF16), nt_dims,
                                preferred_element_type=F32)
            att.append(jnp.where(causal, a, 0.0).astype(BF16))
            upd.append([lax.dot_general(kd[u][g * c:(g + 1) * c].astype(BF16), v[u][g * c:(g + 1) * c].astype(BF16),
                                        tn_dims, preferred_element_type=F32) for g in range(groups)])
        o = [jnp.dot(att[u], v[u].astype(BF16), preferred_element_type=F32) for u in range(width)]
        for u in range(width):
            o_state = []
            for g in range(groups):
                s = s_scr[...] if chained else s0_ref[tiles[u] * groups + g, 0]
                o_state.append(jnp.dot(qb[u][g * c:(g + 1) * c].astype(BF16), s.astype(BF16),
                                       preferred_element_type=F32))
                e = e_col[u][:, g * LANES:(g + 1) * LANES]
                s_new = jnp.concatenate([s[:, :LANES] * e, s[:, LANES:] * e], axis=1) + upd[u][g]
                if chained:
                    s_scr[...] = s_new
                else:
                    sf_ref[tiles[u] * groups + g, 0] = s_new
            o[u] = o[u] + (o_state[0] if groups == 1 else jnp.concatenate(o_state, axis=0))
        for u in range(width):
            y = o[u] * lax.rsqrt(jnp.mean(o[u] * o[u], axis=-1, keepdims=True) + RMS_EPS)
            y = y * gn_ref[...]
            r = r_ref[rows[u], :]
            y_ref[rows[u], :] = y * (r * jax.nn.sigmoid(r))
        return carry

    lax.fori_loop(0, ntile // width, body, 0)
    if chained:
        sf_ref[0, 0] = s_scr[...]


def _gla(z, alow, w_a2, b_a, g_norm, s0, *, batch, seq, chained):
    chunk = math.gcd(seq, GLA_CHUNK)
    if chained:
        assert chunk == GLA_TILE
        nb, groups, width = 1, 1, 4
    else:
        assert GLA_TILE % seq == 0 and chunk == seq
        nb, groups, width = 32, GLA_TILE // seq, 2
    rows = nb * seq
    kern = functools.partial(_gla_kernel, ntile=rows // GLA_TILE, groups=groups, chained=chained, width=width)
    state_spec = pl.BlockSpec((nb, 1, GLA_DK, GLA_DV), lambda i, h: (i, h, 0, 0))
    return pl.pallas_call(
        kern, grid=(batch // nb, GLA_HEADS),
        in_specs=[pl.BlockSpec((rows, GLA_DK), lambda i, h: (i, COL_Q // GLA_DK + h)),
                  pl.BlockSpec((rows, GLA_DK), lambda i, h: (i, COL_K // GLA_DK + h)),
                  pl.BlockSpec((rows, GLA_DV), lambda i, h: (i, COL_V // GLA_DV + h)),
                  pl.BlockSpec((rows, GLA_DV), lambda i, h: (i, COL_R // GLA_DV + h)),
                  pl.BlockSpec((rows, LANES), lambda i, h: (i, 0)),
                  pl.BlockSpec((LANES, GLA_DK), lambda i, h: (0, h)),
                  pl.BlockSpec((1, GLA_DK), lambda i, h: (0, h)),
                  pl.BlockSpec((1, GLA_DV), lambda i, h: (0, h)),
                  state_spec],
        out_specs=[pl.BlockSpec((rows, GLA_DV), lambda i, h: (i, h)), state_spec],
        out_shape=[jax.ShapeDtypeStruct((batch * seq, GLA_HEADS * GLA_DV), F32),
                   jax.ShapeDtypeStruct((batch, GLA_HEADS, GLA_DK, GLA_DV), F32)],
        scratch_shapes=[pltpu.VMEM((GLA_DK, GLA_DV), F32)],
        compiler_params=_params("parallel", "parallel"),
        name="gla",
    )(z, z, z, z, alow, w_a2, b_a, g_norm, s0)


def _attn_kernel(q_ref, k_ref, v_ref, o_ref, *, nb, seq, heads):
    nt_dims = (((1,), (1,)), ((), ()))

    def body(bi, carry):
        rows = pl.ds(pl.multiple_of(bi * seq, SUBLANES), seq)
        for h in range(heads):
            cols = slice(h * XA_HEAD_DIM, (h + 1) * XA_HEAD_DIM)
            q = q_ref[rows, cols].astype(BF16)
            k = k_ref[bi, :, cols].astype(BF16)
            s = lax.dot_general(q, k, nt_dims, preferred_element_type=F32) * (XA_HEAD_DIM ** -0.5)
            p = jnp.exp(s - jnp.max(s, axis=-1, keepdims=True))
            p = p / jnp.sum(p, axis=-1, keepdims=True)
            o_ref[rows, cols] = jnp.dot(p.astype(BF16), v_ref[bi, :, cols].astype(BF16),
                                        preferred_element_type=F32)
        return carry

    lax.fori_loop(0, nb, body, 0)


def _attn_prompt(z, mem_k, mem_v, *, batch, seq, rc):
    nrc = seq // rc
    kern = functools.partial(_attn_kernel, nb=1, seq=rc, heads=1)
    kv_spec = pl.BlockSpec((1, MEM_LEN, XA_HEAD_DIM), lambda b, h, c: (b, 0, h))
    return pl.pallas_call(
        kern, grid=(batch, XA_HEADS, nrc),
        in_specs=[pl.BlockSpec((rc, XA_HEAD_DIM), lambda b, h, c: (b * nrc + c, COL_QX // XA_HEAD_DIM + h)),
                  kv_spec, kv_spec],
        out_specs=pl.BlockSpec((rc, XA_HEAD_DIM), lambda b, h, c: (b * nrc + c, h)),
        out_shape=jax.ShapeDtypeStruct((batch * seq, XA_WIDTH), F32),
        compiler_params=_params("parallel", "parallel", "parallel"),
        name="attn_prompt",
    )(z, mem_k, mem_v)


XA_HALF = XA_HEAD_DIM // 2
XA_ROWS = 2 * XA_HEADS


def _attn_cache_kernel(q_ref, k_ref, v_ref, o_ref, *, nb, seq, width):
    nt_dims = (((1,), (1,)), ((), ()))

    def half_rows(ref, bi, h, half):
        return ref[bi, pl.ds(half * XA_HEADS + h, MEM_LEN, stride=XA_ROWS), :].astype(BF16)

    def body(it, carry):
        work = [(it * width + u, h) for u in range(width) for h in range(XA_HEADS)]
        scores = []
        for bi, h in work:
            rows = pl.ds(pl.multiple_of(bi * seq, SUBLANES), seq)
            lo = h * XA_HEAD_DIM
            s = lax.dot_general(q_ref[rows, lo:lo + XA_HALF].astype(BF16), half_rows(k_ref, bi, h, 0),
                                nt_dims, preferred_element_type=F32)
            s = s + lax.dot_general(q_ref[rows, lo + XA_HALF:lo + XA_HEAD_DIM].astype(BF16),
                                    half_rows(k_ref, bi, h, 1), nt_dims, preferred_element_type=F32)
            scores.append(s * (XA_HEAD_DIM ** -0.5))
        probs = []
        for s in scores:
            p = jnp.exp(s - jnp.max(s, axis=-1, keepdims=True))
            probs.append((p / jnp.sum(p, axis=-1, keepdims=True)).astype(BF16))
        for (bi, h), p in zip(work, probs):
            rows = pl.ds(pl.multiple_of(bi * seq, SUBLANES), seq)
            lo = h * XA_HEAD_DIM
            for half in range(2):
                o_ref[rows, lo + half * XA_HALF:lo + (half + 1) * XA_HALF] = jnp.dot(
                    p, half_rows(v_ref, bi, h, half), preferred_element_type=F32)
        return carry

    lax.fori_loop(0, nb // width, body, 0)


def _cache_rows(cache):
    bs = cache.shape[0]
    c = cache.reshape(bs, MEM_LEN, XA_HEADS, 2, XA_HALF).transpose(0, 1, 3, 2, 4)
    return c.reshape(bs, MEM_LEN * XA_ROWS, XA_HALF)


def _attn_sample(z, mem_k, mem_v, *, batch, seq, nb):
    rows = nb * seq
    kern = functools.partial(_attn_cache_kernel, nb=nb, seq=seq, width=2)
    kv_spec = pl.BlockSpec((nb, MEM_LEN * XA_ROWS, XA_HALF), lambda i: (i, 0, 0))
    return pl.pallas_call(
        kern, grid=(batch // nb,),
        in_specs=[pl.BlockSpec((rows, XA_WIDTH), lambda i: (i, COL_QX // XA_WIDTH)), kv_spec, kv_spec],
        out_specs=pl.BlockSpec((rows, XA_WIDTH), lambda i: (i, 0)),
        out_shape=jax.ShapeDtypeStruct((batch * seq, XA_WIDTH), F32),
        compiler_params=_params("parallel"),
        name="attn_sample",
    )(z, mem_k, mem_v)


def _merge_kernel(ys_ref, yg_ref, yx_ref, g0_ref, g1_ref, g2_ref, wglu_ref, bglu_ref,
                  w0_ref, w1_ref, w2_ref, o_ref, s5_scr, gla_scr, xa_scr):
    @pl.when(pl.program_id(1) == 0)
    def _():
        y = ys_ref[...]
        lin = jnp.dot(y.astype(BF16), wglu_ref[...], preferred_element_type=F32) + bglu_ref[...]
        s5_scr[...] = (y * jax.nn.sigmoid(lin)).astype(BF16)
        gla_scr[...] = yg_ref[...].astype(BF16)
        xa_scr[...] = yx_ref[...].astype(BF16)

    j = pl.program_id(1)
    m = jax.nn.sigmoid(g0_ref[...]) * jnp.dot(s5_scr[...], w0_ref[j], preferred_element_type=F32)
    m = m + jax.nn.sigmoid(g1_ref[...]) * jnp.dot(gla_scr[...], w1_ref[j], preferred_element_type=F32)
    m = m + jax.nn.sigmoid(g2_ref[...]) * jnp.dot(xa_scr[...], w2_ref[j], preferred_element_type=F32)
    o_ref[...] = m.astype(o_ref.dtype)


def _merge(y_s5, y_gla, y_x, z, w_glu, b_glu, w_br_s5, w_br_gla, w_br_x, *, tm, tn):
    t = y_s5.shape[0]
    nj = D_MODEL // tn
    once = pl.Buffered(1)
    wide = pl.BlockSpec((tm, S5_WIDTH), lambda i, j: (i, 0))
    gate = lambda b: pl.BlockSpec((tm, tn), lambda i, j: (i, (COL_GATE + b * D_MODEL) // tn + j))
    w_br = pl.BlockSpec((nj, S5_WIDTH, tn), lambda i, j: (0, 0, 0), pipeline_mode=once)
    return pl.pallas_call(
        _merge_kernel, grid=(t // tm, nj),
        in_specs=[wide, wide, wide, gate(0), gate(1), gate(2),
                  pl.BlockSpec((S5_WIDTH, S5_WIDTH), lambda i, j: (0, 0), pipeline_mode=once),
                  pl.BlockSpec((1, S5_WIDTH), lambda i, j: (0, 0)),
                  w_br, w_br, w_br],
        out_specs=pl.BlockSpec((tm, tn), lambda i, j: (i, j)),
        out_shape=jax.ShapeDtypeStruct((t, D_MODEL), BF16),
        scratch_shapes=[pltpu.VMEM((tm, S5_WIDTH), BF16)] * 3,
        compiler_params=_params("parallel", "arbitrary"),
        name="merge",
    )(y_s5, y_gla, y_x, z, z, z, w_glu, b_glu, w_br_s5, w_br_gla, w_br_x)


def _out_proj_kernel(m_ref, w_ref, x_ref, o_ref):
    o_ref[...] = x_ref[...] + jnp.dot(m_ref[...], w_ref[...], preferred_element_type=F32)


def _out_proj(merged, w_out, x, *, tm, tn):
    t = x.shape[0]
    return pl.pallas_call(
        _out_proj_kernel, grid=(t // tm, D_MODEL // tn),
        in_specs=[pl.BlockSpec((tm, D_MODEL), lambda i, j: (i, 0)),
                  pl.BlockSpec((D_MODEL, tn), lambda i, j: (0, j)),
                  pl.BlockSpec((tm, tn), lambda i, j: (i, j))],
        out_specs=pl.BlockSpec((tm, tn), lambda i, j: (i, j)),
        out_shape=jax.ShapeDtypeStruct((t, D_MODEL), F32),
        compiler_params=_params("parallel", "parallel"),
        name="out_proj",
    )(merged, w_out, x)


def _ffn_kernel(x_ref, gf_ref, wg_ref, wu_ref, wd_ref, gl_ref, o_ref, h_scr, acc_scr):
    k = pl.program_id(1)

    @pl.when(k == 0)
    def _():
        h_scr[...] = _rms(x_ref[...], gf_ref[...]).astype(BF16)
        acc_scr[...] = jnp.zeros_like(acc_scr)

    h = h_scr[...]
    gate = jnp.dot(h, wg_ref[...], preferred_element_type=F32)
    up = jnp.dot(h, wu_ref[...], preferred_element_type=F32)
    act = (gate * jax.nn.sigmoid(gate) * up).astype(BF16)
    acc_scr[...] += jnp.dot(act, wd_ref[...], preferred_element_type=F32)

    @pl.when(k == pl.num_programs(1) - 1)
    def _():
        o_ref[...] = _rms(x_ref[...] + acc_scr[...], gl_ref[...])


def _ffn(x, g_ffn, w_gate, w_up, w_down, g_final, *, tm, th):
    t = x.shape[0]
    row = pl.BlockSpec((tm, D_MODEL), lambda i, k: (i, 0))
    vec = pl.BlockSpec((1, D_MODEL), lambda i, k: (0, 0))
    w_in = pl.BlockSpec((D_MODEL, th), lambda i, k: (0, k))
    return pl.pallas_call(
        _ffn_kernel, grid=(t // tm, FFN_HIDDEN // th),
        in_specs=[row, vec, w_in, w_in, pl.BlockSpec((th, D_MODEL), lambda i, k: (k, 0)), vec],
        out_specs=row,
        out_shape=jax.ShapeDtypeStruct((t, D_MODEL), F32),
        scratch_shapes=[pltpu.VMEM((tm, D_MODEL), BF16), pltpu.VMEM((tm, D_MODEL), F32)],
        compiler_params=_params("parallel", "arbitrary"),
        name="ffn",
    )(x, g_ffn, w_gate, w_up, w_down, g_final)


def _reorder_in_proj_kernel(wt_ref, wa_ref, main_ref, alow_ref):
    main_ref[...] = wt_ref[...].T.astype(BF16)

    @pl.when(pl.program_id(0) == 0)
    def _():
        lane = lax.broadcasted_iota(jnp.int32, alow_ref.shape, 1)
        alow_ref[...] = jnp.where(lane < GLA_RANK, wa_ref[...].T, 0.0).astype(BF16)


def _reorder_in_proj(w_in):
    d, n = w_in.shape
    wt = w_in.T
    tc = 512
    n_first, n_gate = COL_GATE // tc, (COL_QX - COL_GATE) // tc
    gate_start, qx_start = 4096 + GLA_RANK + XA_WIDTH, 4096 + GLA_RANK

    def src_row(j):
        t8, g8, q8 = tc // SUBLANES, gate_start // SUBLANES, qx_start // SUBLANES
        r8 = jnp.where(j < n_first, j * t8,
                       jnp.where(j < n_first + n_gate, g8 + (j - n_first) * t8,
                                 q8 + (j - n_first - n_gate) * t8))
        return r8 * SUBLANES

    return pl.pallas_call(
        _reorder_in_proj_kernel, grid=(Z_WIDTH // tc,),
        in_specs=[pl.BlockSpec((pl.Element(tc), pl.Element(d)), lambda j: (src_row(j), 0)),
                  pl.BlockSpec((pl.Element(LANES), pl.Element(d)), lambda j: (COL_GATE, 0))],
        out_specs=[pl.BlockSpec((d, tc), lambda j: (0, j)), pl.BlockSpec((d, LANES), lambda j: (0, 0))],
        out_shape=[jax.ShapeDtypeStruct((d, Z_WIDTH), BF16), jax.ShapeDtypeStruct((d, LANES), BF16)],
        compiler_params=_params("arbitrary"),
        name="reorder_in_proj",
    )(wt, wt)


def _col_tiles(w, tn):
    k, n = w.shape
    return w.reshape(k, n // tn, tn).transpose(1, 0, 2)


def _layer(x, mem_k, mem_v, s5_re0, s5_im0, gla_s0, w, *, batch, seq, segmented):
    z, alow = _norm_matmul(x, w['norm_mix'], w['w_main'], w['w_alow'], tm=1024, tn=1024)
    y_s5, hf_re, hf_im = _s5_branch(z, w['s5'], s5_re0, s5_im0, batch=batch, seq=seq)
    y_gla, gla_s = _gla(z, alow, w['gla_w_a2'], w['gla_b_a'], w['gla_norm'], gla_s0,
                        batch=batch, seq=seq, chained=segmented)
    if segmented:
        y_x = _attn_prompt(z, mem_k, mem_v, batch=batch, seq=seq, rc=512)
    else:
        y_x = _attn_sample(z, mem_k, mem_v, batch=batch, seq=seq, nb=8)
    merged = _merge(y_s5, y_gla, y_x, z, w['s5_w_glu'], w['s5_b_glu'],
                    w['w_br_s5'], w['w_br_gla'], w['w_br_xattn'], tm=512, tn=512)
    x1 = _out_proj(merged, w['w_out'], x, tm=1024, tn=512)
    y = _ffn(x1, w['norm_ffn'], w['w_ffn_gate'], w['w_ffn_up'], w['w_ffn_down'], w['norm_final'],
             tm=512, th=512)
    return y, hf_re, hf_im, gla_s


def kernel(x_prompt, x_sample, mem_prompt, state_s5_re, state_s5_im, state_gla, cache_mem_k, cache_mem_v,
           norm_mix, w_in, s5_lam_re, s5_lam_im, s5_log_dt, s5_b_re, s5_b_im, s5_c_re, s5_c_im,
           s5_d, s5_w_glu, s5_b_glu, gla_w_a2, gla_b_a, gla_norm, mem_norm, w_mem_k, w_mem_v,
           w_br_s5, w_br_gla, w_br_xattn, w_out, norm_ffn, w_ffn_gate, w_ffn_up, w_ffn_down, norm_final):
    depth = w_in.shape[0]
    assert depth == 1
    bp, sp, d = x_prompt.shape
    bs, ss, _ = x_sample.shape
    n_state = S5_GROUPS * S5_STATE
    row = lambda v: v.reshape(1, -1)

    l = 0
    w_main, w_alow = _reorder_in_proj(w_in[l])
    s5_w = _s5_params(s5_lam_re[l], s5_lam_im[l], s5_log_dt[l], s5_b_re[l], s5_b_im[l],
                      s5_c_re[l], s5_c_im[l], s5_d[l])
    w = {
        'norm_mix': row(norm_mix[l]), 'w_main': w_main, 'w_alow': w_alow,
        's5': s5_w,
        's5_w_glu': s5_w_glu[l].astype(BF16), 's5_b_glu': row(s5_b_glu[l]),
        'gla_w_a2': jnp.pad(gla_w_a2[l], ((0, LANES - GLA_RANK), (0, 0))).astype(BF16),
        'gla_b_a': row(gla_b_a[l]), 'gla_norm': row(gla_norm[l]),
        'w_br_s5': _col_tiles(w_br_s5[l].astype(BF16), 512),
        'w_br_gla': _col_tiles(w_br_gla[l].astype(BF16), 512),
        'w_br_xattn': _col_tiles(w_br_xattn[l].astype(BF16), 512), 'w_out': w_out[l].astype(BF16),
        'norm_ffn': row(norm_ffn[l]), 'w_ffn_gate': w_ffn_gate[l].astype(BF16),
        'w_ffn_up': w_ffn_up[l].astype(BF16), 'w_ffn_down': w_ffn_down[l].astype(BF16),
        'norm_final': row(norm_final),
    }

    w_mem = jnp.concatenate([w_mem_k[l], w_mem_v[l]], axis=1).astype(BF16)
    mem_kv = _norm_matmul(mem_prompt.reshape(bp * MEM_LEN, d), row(mem_norm[l]), w_mem, tm=512, tn=1024)
    mk = mem_kv[:, :XA_WIDTH].reshape(bp, MEM_LEN, XA_WIDTH)
    mv = mem_kv[:, XA_WIDTH:].reshape(bp, MEM_LEN, XA_WIDTH)
    zero_s5 = jnp.zeros((bp, n_state), F32)
    zero_gla = jnp.zeros((bp, GLA_HEADS, GLA_DK, GLA_DV), F32)
    yp, p_re, p_im, p_gla = _layer(x_prompt.reshape(bp * sp, d), mk, mv, zero_s5, zero_s5, zero_gla, w,
                                   batch=bp, seq=sp, segmented=True)

    ys, s_re, s_im, s_gla = _layer(x_sample.reshape(bs * ss, d),
                                   _cache_rows(cache_mem_k[l]), _cache_rows(cache_mem_v[l]),
                                   state_s5_re[l].reshape(bs, n_state), state_s5_im[l].reshape(bs, n_state),
                                   state_gla[l], w, batch=bs, seq=ss, segmented=False)

    s5_shape_p = (1, bp, S5_GROUPS, S5_STATE)
    s5_shape_s = (1, bs, S5_GROUPS, S5_STATE)
    kv_shape = (1, bp, MEM_LEN, XA_HEADS, XA_HEAD_DIM)
    return (yp.reshape(bp, sp, d), ys.reshape(bs, ss, d),
            p_re.reshape(s5_shape_p), p_im.reshape(s5_shape_p), p_gla[None],
            mk.reshape(kv_shape), mv.reshape(kv_shape),
            s_re.reshape(s5_shape_s), s_im.reshape(s5_shape_s), s_gla[None])
```

```python
import functools
import math

import jax
import jax.numpy as jnp
from jax import lax
from jax.experimental import pallas as pl
from jax.experimental.pallas import tpu as pltpu

F32 = jnp.float32
BF16 = jnp.bfloat16

D_MODEL = 2048
S5_WIDTH = 1024
S5_GROUP = 16
S5_GROUPS = 64
S5_STATE = 64
GLA_HEADS = 4
GLA_DK = 128
GLA_DV = 256
GLA_RANK = 16
GLA_TAU = 16.0
GLA_CHUNK = 64
XA_HEADS = 4
XA_HEAD_DIM = 256
XA_WIDTH = 1024
MEM_LEN = 256
FFN_HIDDEN = 5632
RMS_EPS = 1e-6

LANES = 128
SUBLANES = 8
VMEM_LIMIT = 56 * 1024 * 1024

COL_U = 0
COL_Q = 1024
COL_K = 1536
COL_V = 2048
COL_R = 3072
COL_GATE = 4096
COL_QX = 10240
Z_WIDTH = 11264

S5_GB = 8
S5_CH = S5_GB * S5_STATE
S5_NBLK = S5_GROUPS // S5_GB
S5_TC = 16
GLA_TILE = 64
MERGE_TN = 1024


def _params(*sem):
    return pltpu.CompilerParams(dimension_semantics=sem, vmem_limit_bytes=VMEM_LIMIT)


def _rms(x, g):
    return x * lax.rsqrt(jnp.mean(x * x, axis=-1, keepdims=True) + RMS_EPS) * g


def _norm_matmul_kernel(x_ref, g_ref, w_ref, o_ref, h_ref):
    @pl.when(pl.program_id(1) == 0)
    def _():
        h_ref[...] = _rms(x_ref[...], g_ref[...]).astype(BF16)

    o_ref[...] = jnp.dot(h_ref[...], w_ref[...], preferred_element_type=F32)


def _norm_matmul2_kernel(x_ref, g_ref, w_ref, ws_ref, o_ref, os_ref, h_ref):
    @pl.when(pl.program_id(1) == 0)
    def _():
        h = _rms(x_ref[...], g_ref[...]).astype(BF16)
        h_ref[...] = h
        os_ref[...] = jnp.dot(h, ws_ref[...], preferred_element_type=F32)

    o_ref[...] = jnp.dot(h_ref[...], w_ref[...], preferred_element_type=F32)


def _norm_matmul(x, g, w, w_small=None, *, tm, tn):
    t, d = x.shape
    n = w.shape[1]
    grid = (t // tm, n // tn)
    in_specs = [pl.BlockSpec((tm, d), lambda i, j: (i, 0)),
                pl.BlockSpec((1, d), lambda i, j: (0, 0)),
                pl.BlockSpec((d, tn), lambda i, j: (0, j))]
    out_specs = pl.BlockSpec((tm, tn), lambda i, j: (i, j))
    out_shape = jax.ShapeDtypeStruct((t, n), F32)
    scratch = [pltpu.VMEM((tm, d), BF16)]
    if w_small is None:
        return pl.pallas_call(_norm_matmul_kernel, grid=grid, in_specs=in_specs, out_specs=out_specs,
                              out_shape=out_shape, scratch_shapes=scratch,
                              compiler_params=_params("parallel", "arbitrary"),
                              name="norm_matmul")(x, g, w)
    ns = w_small.shape[1]
    in_specs.append(pl.BlockSpec((d, ns), lambda i, j: (0, 0)))
    return pl.pallas_call(_norm_matmul2_kernel, grid=grid, in_specs=in_specs,
                          out_specs=[out_specs, pl.BlockSpec((tm, ns), lambda i, j: (i, 0))],
                          out_shape=[out_shape, jax.ShapeDtypeStruct((t, ns), F32)],
                          scratch_shapes=scratch,
                          compiler_params=_params("parallel", "arbitrary"),
                          name="in_proj")(x, g, w, w_small)


def _s5_param_kernel(lr_ref, li_ref, ldt_ref, btr_ref, bti_ref, cr_ref, ci_ref,
                     w16_ref, cc16_ref, w8_ref, cc8_ref, ap_ref):
    p = S5_STATE
    half = S5_TC // 2
    lam_re, lam_im = lr_ref[0], li_ref[0]
    dt = jnp.exp(ldt_ref[0])
    mag = jnp.exp(lam_re * dt)
    a_re = mag * jnp.cos(lam_im * dt)
    a_im = mag * jnp.sin(lam_im * dt)
    den = lam_re * lam_re + lam_im * lam_im
    coef_re = ((a_re - 1.0) * lam_re + a_im * lam_im) / den
    coef_im = (a_im * lam_re - (a_re - 1.0) * lam_im) / den
    bt_re, bt_im = btr_ref[0], bti_ref[0]
    bb_re = coef_re * bt_re - coef_im * bt_im
    bb_im = coef_re * bt_im + coef_im * bt_re
    c_re, c_im = cr_ref[0], ci_ref[0]

    pw_re, pw_im = jnp.ones_like(a_re), jnp.zeros_like(a_re)
    ca_re, ca_im, ab_re, ab_im, powers = [], [], [], [], {}
    for j in range(S5_TC + 1):
        powers[j] = (pw_re, pw_im)
        ca_re.append(c_re * pw_re - c_im * pw_im)
        ca_im.append(c_re * pw_im + c_im * pw_re)
        ab_re.append(bb_re * pw_re - bb_im * pw_im)
        ab_im.append(bb_re * pw_im + bb_im * pw_re)
        pw_re, pw_im = pw_re * a_re - pw_im * a_im, pw_re * a_im + pw_im * a_re
    lag_re = jnp.concatenate(ca_re[:S5_TC], axis=0)
    lag_im = jnp.concatenate(ca_im[:S5_TC], axis=0)
    end_re = jnp.concatenate([ab_re[S5_TC - 1 - s] for s in range(S5_TC)], axis=0)
    end_im = jnp.concatenate([ab_im[S5_TC - 1 - s] for s in range(S5_TC)], axis=0)
    car_re = jnp.concatenate(ca_re[1:], axis=0).T
    car_im = -jnp.concatenate(ca_im[1:], axis=0).T

    def pair_blocks(x):
        first = lax.broadcasted_iota(jnp.int32, x.shape, 0) < p
        return jnp.concatenate([jnp.where(first, x, 0.0), jnp.where(first, 0.0, x)], axis=1)

    cc16_ref[0] = jnp.concatenate([pair_blocks(car_re), pair_blocks(car_im)], axis=0).astype(BF16)
    n8 = half * S5_GROUP
    cc8_ref[0] = jnp.concatenate([pair_blocks(car_re[:, :n8]), pair_blocks(car_im[:, :n8])], axis=0).astype(BF16)

    nt_dims = (((1,), (1,)), ((), ()))
    hi = lax.Precision.HIGHEST
    lane = lax.broadcasted_iota(jnp.int32, (S5_GROUP, LANES), 1)
    end_lane = lax.broadcasted_iota(jnp.int32, end_re.shape, 1)
    for gi in range(2):
        mine = (lane < p) if gi == 0 else (lane >= p)
        own_re, own_im = jnp.where(mine, bb_re, 0.0), jnp.where(mine, bb_im, 0.0)
        strip = (lax.dot_general(own_re, lag_re, nt_dims, precision=hi, preferred_element_type=F32)
                 - lax.dot_general(own_im, lag_im, nt_dims, precision=hi, preferred_element_type=F32))
        lo, up = strip[:, :LANES], strip[:, LANES:]
        blocks = [strip]
        for s in range(1, S5_TC):
            sh = (s % half) * S5_GROUP
            lo_r = pltpu.roll(lo, sh, 1) if sh else lo
            up_r = pltpu.roll(up, sh, 1) if sh else up
            if s < half:
                blocks.append(jnp.concatenate([jnp.where(lane >= sh, lo_r, 0.0),
                                               jnp.where(lane >= sh, up_r, lo_r)], axis=1))
            else:
                blocks.append(jnp.concatenate([jnp.zeros_like(lo), jnp.where(lane >= sh, lo_r, 0.0)], axis=1))
        toeplitz = jnp.concatenate(blocks, axis=0)
        own_end = (end_lane < p) if gi == 0 else (end_lane >= p)
        ends = jnp.concatenate([jnp.where(own_end, end_re, 0.0), jnp.where(own_end, end_im, 0.0)], axis=1)
        w16_ref[gi] = jnp.concatenate([toeplitz, ends], axis=1).astype(BF16)
        w8_ref[gi] = jnp.concatenate([toeplitz[:n8, :n8], ends[n8:, :]], axis=1).astype(BF16)
    ap_ref[0] = jnp.concatenate([powers[S5_TC][0], powers[S5_TC][1], powers[half][0], powers[half][1]], axis=0)


def _s5_params(lam_re, lam_im, log_dt, b_re, b_im, c_re, c_im, d_skip):
    g, p, c = S5_GROUPS, S5_STATE, S5_GROUP
    npair = g // 2
    w16, w8 = S5_TC * c, S5_TC // 2 * c

    def pair_lanes(x):
        return x.reshape(npair, 2, x.shape[1], p).transpose(0, 2, 1, 3).reshape(npair, x.shape[1], 2 * p)

    row = lambda x: pair_lanes(x.reshape(g, 1, p))
    blk = lambda *shape: pl.BlockSpec((1,) + shape, lambda i: (i, 0, 0))
    two = lambda *shape: pl.BlockSpec((2,) + shape, lambda i: (i, 0, 0))
    wt16, cc16, wt8, cc8, apow = pl.pallas_call(
        _s5_param_kernel, grid=(npair,),
        in_specs=[blk(1, 2 * p)] * 3 + [blk(c, 2 * p)] * 4,
        out_specs=[two(w16, w16 + 4 * p), blk(4 * p, 2 * w16), two(w8, w8 + 4 * p), blk(4 * p, 2 * w8),
                   blk(4, 2 * p)],
        out_shape=[jax.ShapeDtypeStruct((g, w16, w16 + 4 * p), BF16),
                   jax.ShapeDtypeStruct((npair, 4 * p, 2 * w16), BF16),
                   jax.ShapeDtypeStruct((g, w8, w8 + 4 * p), BF16),
                   jax.ShapeDtypeStruct((npair, 4 * p, 2 * w8), BF16),
                   jax.ShapeDtypeStruct((npair, 4, 2 * p), F32)],
        compiler_params=_params("parallel"),
        name="s5_params",
    )(row(lam_re), row(lam_im), row(jnp.broadcast_to(log_dt[:, None], (g, p))),
      pair_lanes(b_re.transpose(0, 2, 1)), pair_lanes(b_im.transpose(0, 2, 1)),
      pair_lanes(c_re), pair_lanes(c_im))
    d = d_skip.reshape(S5_NBLK, 1, S5_GB * c)

    def transition(r):
        return jnp.concatenate([apow[:, r].reshape(S5_NBLK, 1, S5_CH), apow[:, r + 1].reshape(S5_NBLK, 1, S5_CH)],
                               axis=-1)

    return {S5_TC: (wt16, cc16, transition(0), d), S5_TC // 2: (wt8, cc8, transition(2), d)}


def _unit_transpose(vs):
    unit = lax.broadcasted_iota(jnp.int32, vs[0].shape, 1) >> int(math.log2(S5_GROUP))
    for dist in (4, 2, 1):
        keep = (unit & dist) == 0
        nxt = list(vs)
        for i in range(8):
            if i & dist == 0:
                a, b = vs[i], vs[i + dist]
                nxt[i] = jnp.where(keep, a, pltpu.roll(b, dist * S5_GROUP, 1))
                nxt[i + dist] = jnp.where(keep, pltpu.roll(a, LANES - dist * S5_GROUP, 1), b)
        vs = nxt
    return vs


def _s5_chunk_kernel(z_ref, w_ref, cc_ref, a_ref, d_ref, h0r_ref, h0i_ref,
                     y_ref, hfr_ref, hfi_ref, ut_ref, ug_ref, yg_ref, e_ref, *, nsl, rps, tc, nseg):
    uw = tc * S5_GROUP
    nq = uw // LANES
    rows = nsl * rps
    nk = rps if nsl > 1 else 1
    ns = rows // nk
    npair = S5_GB // 2
    rchunk = min(rows, 64)

    for s in range(nsl):
        for t in range(tc):
            ut_ref[t, s * rps:(s + 1) * rps, :] = z_ref[pl.ds(s * rps * tc + t, rps, stride=tc), :]

    def row_chunk(rc):
        return pl.ds(pl.multiple_of(rc * rchunk, rchunk), rchunk)

    def to_groups(rc, carry):
        for q in range(nq):
            per_group = _unit_transpose([ut_ref[q * 8 + t, row_chunk(rc), :] for t in range(8)])
            for g in range(S5_GB):
                ug_ref[g, row_chunk(rc), q * LANES:(q + 1) * LANES] = per_group[g].astype(BF16)
        return carry

    lax.fori_loop(0, rows // rchunk, to_groups, 0)

    for pair in range(npair):
        ends = None
        for gi in range(2):
            g = 2 * pair + gi
            em = jnp.dot(ug_ref[g], w_ref[g], preferred_element_type=F32)
            yg_ref[g] = em[:, :uw]
            ends = em[:, uw:] if ends is None else ends + em[:, uw:]
        for part, blk in ((ends[:, :LANES], pair), (ends[:, LANES:], npair + pair)):
            if nk == 1:
                e_ref[blk] = part
            else:
                for s in range(nsl):
                    e_ref[blk, pl.ds(s, rps, stride=nsl), :] = part[s * rps:(s + 1) * rps, :]

    a_row = [(a_ref[0, :, i * LANES:(i + 1) * LANES], a_ref[0, :, S5_CH + i * LANES:S5_CH + (i + 1) * LANES])
             for i in range(npair)]
    a_full = [(jnp.broadcast_to(ar, (ns, LANES)), jnp.broadcast_to(ai, (ns, LANES))) for ar, ai in a_row]

    def slab(t):
        return pl.ds(pl.multiple_of(t * ns, ns), ns)

    def run_scan(h, store):
        def step(t, carry):
            out = []
            for i in range(npair):
                hr, hi = carry[i]
                ar, ai = a_full[i]
                er = e_ref[i, slab(t), :]
                ei = e_ref[npair + i, slab(t), :]
                if store:
                    e_ref[i, slab(t), :] = hr
                    e_ref[npair + i, slab(t), :] = hi
                out.append((ar * hr - ai * hi + er, ar * hi + ai * hr + ei))
            return tuple(out)

        if nk == 1:
            return step(0, h)
        return lax.fori_loop(0, nk, step, h, unroll=4)

    h0 = tuple((h0r_ref[:, i * LANES:(i + 1) * LANES], h0i_ref[:, i * LANES:(i + 1) * LANES]) for i in range(npair))
    if nseg == 1:
        final = run_scan(h0, store=True)
    else:
        zero = jnp.zeros((ns, LANES), F32)
        seg_end = run_scan(tuple((zero, zero) for _ in range(npair)), store=False)
        second = (lax.broadcasted_iota(jnp.int32, (ns, LANES), 0) & 1) == 1
        init, final = [], []
        for i in range(npair):
            p_re, p_im = a_row[i]
            for _ in range(int(math.log2(nk))):
                p_re, p_im = p_re * p_re - p_im * p_im, 2.0 * (p_re * p_im)
            (hr, hi), (er, ei) = h0[i], seg_end[i]
            i_re = jnp.where(second, p_re * hr - p_im * hi + pltpu.roll(er, 1, 0), hr)
            i_im = jnp.where(second, p_re * hi + p_im * hr + pltpu.roll(ei, 1, 0), hi)
            init.append((i_re, i_im))
            final.append((p_re * i_re - p_im * i_im + er, p_re * i_im + p_im * i_re + ei))
        run_scan(tuple(init), store=True)
    for i in range(npair):
        hfr_ref[:, i * LANES:(i + 1) * LANES] = final[i][0]
        hfi_ref[:, i * LANES:(i + 1) * LANES] = final[i][1]

    def chunk_rows(blk):
        if nk == 1:
            return e_ref[blk]
        return jnp.concatenate([e_ref[blk, pl.ds(s, rps, stride=nsl), :] for s in range(nsl)], axis=0)

    for pair in range(npair):
        h_in = jnp.concatenate([chunk_rows(pair), chunk_rows(npair + pair)], axis=1).astype(BF16)
        carried = jnp.dot(h_in, cc_ref[pair], preferred_element_type=F32)
        for gi in range(2):
            yg_ref[2 * pair + gi] += carried[:, gi * uw:(gi + 1) * uw]

    d_row = d_ref[0]

    def to_tokens(rc, carry):
        for q in range(nq):
            per_tau = _unit_transpose([yg_ref[g, row_chunk(rc), q * LANES:(q + 1) * LANES] for g in range(S5_GB)])
            for t in range(8):
                y = per_tau[t] + d_row * ut_ref[q * 8 + t, row_chunk(rc), :]
                ut_ref[q * 8 + t, row_chunk(rc), :] = jax.nn.gelu(y, approximate=True)
        return carry

    lax.fori_loop(0, rows // rchunk, to_tokens, 0)
    for s in range(nsl):
        for t in range(tc):
            y_ref[pl.ds(s * rps * tc + t, rps, stride=tc), :] = ut_ref[t, s * rps:(s + 1) * rps, :]


def _s5_branch(z, weights, h0_re, h0_im, *, batch, seq):
    tokens = batch * seq
    if seq > S5_TC:
        tc, nseg = S5_TC, 2
        nsl, rps = batch * nseg, seq // (nseg * tc)
        assert rps & (rps - 1) == 0
        h0_re, h0_im = jnp.repeat(h0_re, nseg, axis=0), jnp.repeat(h0_im, nseg, axis=0)
    else:
        tc, nseg, nsl, rps = seq, 1, 1, batch
    w, cc, a, d = weights[tc]
    uw = tc * S5_GROUP
    rows = nsl * rps
    ns = h0_re.shape[0]
    kern = functools.partial(_s5_chunk_kernel, nsl=nsl, rps=rps, tc=tc, nseg=nseg)
    state_spec = pl.BlockSpec((ns, S5_CH), lambda k: (0, k))
    state_shape = jax.ShapeDtypeStruct((ns, S5_GROUPS * S5_STATE), F32)
    y, hf_re, hf_im = pl.pallas_call(
        kern, grid=(S5_NBLK,),
        in_specs=[pl.BlockSpec((tokens, LANES), lambda k: (0, COL_U // LANES + k)),
                  pl.BlockSpec((S5_GB, uw, uw + 4 * S5_STATE), lambda k: (k, 0, 0)),
                  pl.BlockSpec((S5_GB // 2, 4 * S5_STATE, 2 * uw), lambda k: (k, 0, 0)),
                  pl.BlockSpec((1, 1, 2 * S5_CH), lambda k: (k, 0, 0)),
                  pl.BlockSpec((1, 1, LANES), lambda k: (k, 0, 0)),
                  state_spec, state_spec],
        out_specs=[pl.BlockSpec((tokens, LANES), lambda k: (0, k)), state_spec, state_spec],
        out_shape=[jax.ShapeDtypeStruct((tokens, S5_WIDTH), F32), state_shape, state_shape],
        scratch_shapes=[pltpu.VMEM((tc, rows, LANES), F32), pltpu.VMEM((S5_GB, rows, uw), BF16),
                        pltpu.VMEM((S5_GB, rows, uw), F32), pltpu.VMEM((S5_GB, rows, LANES), F32)],
        compiler_params=_params("parallel"),
        name="s5_chunked",
    )(z, w, cc, a, d, h0_re, h0_im)
    if nseg == 2:
        hf_re, hf_im = hf_re[1::2], hf_im[1::2]
    return y, hf_re, hf_im


def _gla_kernel(q_ref, k_ref, v_ref, r_ref, al_ref, wa_ref, ba_ref, gn_ref, s0_ref,
                y_ref, sf_ref, s_scr, *, ntile, groups, chained, width):
    rt = GLA_TILE
    c = rt // groups
    shift = int(math.log2(c))
    row_g = lax.broadcasted_iota(jnp.int32, (rt, rt), 0)
    col_g = lax.broadcasted_iota(jnp.int32, (rt, rt), 1)
    same = (row_g >> shift) == (col_g >> shift)
    causal = same & (row_g >= col_g)
    cum_w = jnp.concatenate([causal.astype(BF16), same.astype(BF16)], axis=0)
    sel_r = lax.broadcasted_iota(jnp.int32, (rt, groups * LANES), 0) >> shift
    sel_c = lax.broadcasted_iota(jnp.int32, (rt, groups * LANES), 1) >> int(math.log2(LANES))
    sel = (sel_r == sel_c).astype(BF16)
    tn_dims = (((0,), (0,)), ((), ()))
    nt_dims = (((1,), (1,)), ((), ()))

    def split3(x):
        hi = x.astype(BF16)
        r1 = x - hi.astype(F32)
        mid = r1.astype(BF16)
        lo = (r1 - mid.astype(F32)).astype(BF16)
        return jnp.concatenate([hi, mid, lo], axis=1)

    if chained:
        s_scr[...] = s0_ref[0, 0]

    def body(it, carry):
        tiles = [it * width + u for u in range(width)]
        rows = [pl.ds(pl.multiple_of(t * rt, rt), rt) for t in tiles]
        log_a = []
        for u in range(width):
            x = jnp.dot(al_ref[rows[u], :].astype(BF16), wa_ref[...], preferred_element_type=F32) + ba_ref[...]
            log_a.append((jnp.minimum(x, 0.0) - jnp.log1p(jnp.exp(-jnp.abs(x)))) * (1.0 / GLA_TAU))
        la3 = [split3(la) for la in log_a]
        cums, e_col = [], []
        for u in range(width):
            cs = jnp.dot(cum_w, la3[u], preferred_element_type=F32)
            cums.append(cs[:, :LANES] + cs[:, LANES:2 * LANES] + cs[:, 2 * LANES:])
            tot = lax.dot_general(la3[u], sel, tn_dims, preferred_element_type=F32)
            e_col.append(jnp.exp(tot[:LANES] + tot[LANES:2 * LANES] + tot[2 * LANES:]))
        qb, kd, v, att, upd = [], [], [], [], []
        for u in range(width):
            b = cums[u][:rt]
            b_end = cums[u][rt:]
            k = k_ref[rows[u], :]
            v.append(v_ref[rows[u], :])
            qb.append(q_ref[rows[u], :] * (GLA_DK ** -0.5) * jnp.exp(b))
            kd.append(k * jnp.exp(b_end - b))
            a = lax.dot_general(qb[u].astype(BF16), (k * jnp.exp(-b)).astype(BF16), nt_dims,
                                preferred_element_type=F32)
            att.append(jnp.where(causal, a, 0.0).astype(BF16))
            upd.append([lax.dot_general(kd[u][g * c:(g + 1) * c].astype(BF16), v[u][g * c:(g + 1) * c].astype(BF16),
                                        tn_dims, preferred_element_type=F32) for g in range(groups)])
        o = [jnp.dot(att[u], v[u].astype(BF16), preferred_element_type=F32) for u in range(width)]
        for u in range(width):
            o_state = []
            for g in range(groups):
                s = s_scr[...] if chained else s0_ref[tiles[u] * groups + g, 0]
                o_state.append(jnp.dot(qb[u][g * c:(g + 1) * c].astype(BF16), s.astype(BF16),
                                       preferred_element_type=F32))
                e = e_col[u][:, g * LANES:(g + 1) * LANES]
                s_new = jnp.concatenate([s[:, :LANES] * e, s[:, LANES:] * e], axis=1) + upd[u][g]
                if chained:
                    s_scr[...] = s_new
                else:
                    sf_ref[tiles[u] * groups + g, 0] = s_new
            o[u] = o[u] + (o_state[0] if groups == 1 else jnp.concatenate(o_state, axis=0))
        for u in range(width):
            y = o[u] * lax.rsqrt(jnp.mean(o[u] * o[u], axis=-1, keepdims=True) + RMS_EPS)
            y = y * gn_ref[...]
            r = r_ref[rows[u], :]
            y_ref[rows[u], :] = y * (r * jax.nn.sigmoid(r))
        return carry

    lax.fori_loop(0, ntile // width, body, 0)
    if chained:
        sf_ref[0, 0] = s_scr[...]


def _gla(z, alow, w_a2, b_a, g_norm, s0, *, batch, seq, chained):
    chunk = math.gcd(seq, GLA_CHUNK)
    if chained:
        assert chunk == GLA_TILE
        nb, groups, width = 1, 1, 8
    else:
        assert GLA_TILE % seq == 0 and chunk == seq
        nb, groups, width = 32, GLA_TILE // seq, 2
    rows = nb * seq
    kern = functools.partial(_gla_kernel, ntile=rows // GLA_TILE, groups=groups, chained=chained, width=width)
    state_spec = pl.BlockSpec((nb, 1, GLA_DK, GLA_DV), lambda i, h: (i, h, 0, 0))
    return pl.pallas_call(
        kern, grid=(batch // nb, GLA_HEADS),
        in_specs=[pl.BlockSpec((rows, GLA_DK), lambda i, h: (i, COL_Q // GLA_DK + h)),
                  pl.BlockSpec((rows, GLA_DK), lambda i, h: (i, COL_K // GLA_DK + h)),
                  pl.BlockSpec((rows, GLA_DV), lambda i, h: (i, COL_V // GLA_DV + h)),
                  pl.BlockSpec((rows, GLA_DV), lambda i, h: (i, COL_R // GLA_DV + h)),
                  pl.BlockSpec((rows, LANES), lambda i, h: (i, 0)),
                  pl.BlockSpec((LANES, GLA_DK), lambda i, h: (0, h)),
                  pl.BlockSpec((1, GLA_DK), lambda i, h: (0, h)),
                  pl.BlockSpec((1, GLA_DV), lambda i, h: (0, h)),
                  state_spec],
        out_specs=[pl.BlockSpec((rows, GLA_DV), lambda i, h: (i, h)), state_spec],
        out_shape=[jax.ShapeDtypeStruct((batch * seq, GLA_HEADS * GLA_DV), F32),
                   jax.ShapeDtypeStruct((batch, GLA_HEADS, GLA_DK, GLA_DV), F32)],
        scratch_shapes=[pltpu.VMEM((GLA_DK, GLA_DV), F32)],
        compiler_params=_params("parallel", "parallel"),
        name="gla",
    )(z, z, z, z, alow, w_a2, b_a, g_norm, s0)


def _attn_kernel(q_ref, k_ref, v_ref, o_ref):
    nt_dims = (((1,), (1,)), ((), ()))
    cols = [slice(h * XA_HEAD_DIM, (h + 1) * XA_HEAD_DIM) for h in range(XA_HEADS)]
    scores = [lax.dot_general(q_ref[:, c].astype(BF16), k_ref[0, :, c].astype(BF16), nt_dims,
                              preferred_element_type=F32) * (XA_HEAD_DIM ** -0.5) for c in cols]
    probs = []
    for s in scores:
        p = jnp.exp(s - jnp.max(s, axis=-1, keepdims=True))
        probs.append((p / jnp.sum(p, axis=-1, keepdims=True)).astype(BF16))
    for c, p in zip(cols, probs):
        o_ref[:, c] = jnp.dot(p, v_ref[0, :, c].astype(BF16), preferred_element_type=F32)


def _attn_prompt(z, mem_k, mem_v, *, batch, seq, rc):
    nrc = seq // rc
    kv_spec = pl.BlockSpec((1, MEM_LEN, XA_WIDTH), lambda b, c: (b, 0, 0))
    return pl.pallas_call(
        _attn_kernel, grid=(batch, nrc),
        in_specs=[pl.BlockSpec((rc, XA_WIDTH), lambda b, c: (b * nrc + c, COL_QX // XA_WIDTH)), kv_spec, kv_spec],
        out_specs=pl.BlockSpec((rc, XA_WIDTH), lambda b, c: (b * nrc + c, 0)),
        out_shape=jax.ShapeDtypeStruct((batch * seq, XA_WIDTH), F32),
        compiler_params=_params("parallel", "parallel"),
        name="attn_prompt",
    )(z, mem_k, mem_v)


XA_HALF = XA_HEAD_DIM // 2
XA_ROWS = 2 * XA_HEADS


def _attn_cache_kernel(q_ref, k_ref, v_ref, o_ref, *, nb, seq, width):
    nt_dims = (((1,), (1,)), ((), ()))

    def half_rows(ref, bi, h, half):
        return ref[bi, pl.ds(half * XA_HEADS + h, MEM_LEN, stride=XA_ROWS), :].astype(BF16)

    def body(it, carry):
        work = [(it * width + u, h) for u in range(width) for h in range(XA_HEADS)]
        scores = []
        for bi, h in work:
            rows = pl.ds(pl.multiple_of(bi * seq, SUBLANES), seq)
            lo = h * XA_HEAD_DIM
            s = lax.dot_general(q_ref[rows, lo:lo + XA_HALF].astype(BF16), half_rows(k_ref, bi, h, 0),
                                nt_dims, preferred_element_type=F32)
            s = s + lax.dot_general(q_ref[rows, lo + XA_HALF:lo + XA_HEAD_DIM].astype(BF16),
                                    half_rows(k_ref, bi, h, 1), nt_dims, preferred_element_type=F32)
            scores.append(s * (XA_HEAD_DIM ** -0.5))
        probs = []
        for s in scores:
            p = jnp.exp(s - jnp.max(s, axis=-1, keepdims=True))
            probs.append((p / jnp.sum(p, axis=-1, keepdims=True)).astype(BF16))
        for (bi, h), p in zip(work, probs):
            rows = pl.ds(pl.multiple_of(bi * seq, SUBLANES), seq)
            lo = h * XA_HEAD_DIM
            for half in range(2):
                o_ref[rows, lo + half * XA_HALF:lo + (half + 1) * XA_HALF] = jnp.dot(
                    p, half_rows(v_ref, bi, h, half), preferred_element_type=F32)
        return carry

    lax.fori_loop(0, nb // width, body, 0)


def _cache_rows(cache):
    bs = cache.shape[0]
    c = cache.reshape(bs, MEM_LEN, XA_HEADS, 2, XA_HALF).transpose(0, 1, 3, 2, 4)
    return c.reshape(bs, MEM_LEN * XA_ROWS, XA_HALF)


def _attn_sample(z, mem_k, mem_v, *, batch, seq, nb):
    rows = nb * seq
    kern = functools.partial(_attn_cache_kernel, nb=nb, seq=seq, width=2)
    kv_spec = pl.BlockSpec((nb, MEM_LEN * XA_ROWS, XA_HALF), lambda i: (i, 0, 0))
    return pl.pallas_call(
        kern, grid=(batch // nb,),
        in_specs=[pl.BlockSpec((rows, XA_WIDTH), lambda i: (i, COL_QX // XA_WIDTH)), kv_spec, kv_spec],
        out_specs=pl.BlockSpec((rows, XA_WIDTH), lambda i: (i, 0)),
        out_shape=jax.ShapeDtypeStruct((batch * seq, XA_WIDTH), F32),
        compiler_params=_params("parallel"),
        name="attn_sample",
    )(z, mem_k, mem_v)


def _merge_kernel(ys_ref, yg_ref, yx_ref, g0_ref, g1_ref, g2_ref, wglu_ref, bglu_ref,
                  w0_ref, w1_ref, w2_ref, o_ref, s5_scr, gla_scr, xa_scr):
    @pl.when(pl.program_id(1) == 0)
    def _():
        y = ys_ref[...]
        lin = jnp.dot(y.astype(BF16), wglu_ref[...], preferred_element_type=F32) + bglu_ref[...]
        s5_scr[...] = (y * jax.nn.sigmoid(lin)).astype(BF16)
        gla_scr[...] = yg_ref[...].astype(BF16)
        xa_scr[...] = yx_ref[...].astype(BF16)

    j = pl.program_id(1)
    m = jax.nn.sigmoid(g0_ref[...]) * jnp.dot(s5_scr[...], w0_ref[j], preferred_element_type=F32)
    m = m + jax.nn.sigmoid(g1_ref[...]) * jnp.dot(gla_scr[...], w1_ref[j], preferred_element_type=F32)
    m = m + jax.nn.sigmoid(g2_ref[...]) * jnp.dot(xa_scr[...], w2_ref[j], preferred_element_type=F32)
    o_ref[...] = m.astype(o_ref.dtype)


def _merge(y_s5, y_gla, y_x, z, w_glu, b_glu, w_br_s5, w_br_gla, w_br_x, *, tm, tn):
    t = y_s5.shape[0]
    nj = D_MODEL // tn
    once = pl.Buffered(1)
    wide = pl.BlockSpec((tm, S5_WIDTH), lambda i, j: (i, 0))
    gate = lambda b: pl.BlockSpec((tm, tn), lambda i, j: (i, (COL_GATE + b * D_MODEL) // tn + j))
    w_br = pl.BlockSpec((nj, S5_WIDTH, tn), lambda i, j: (0, 0, 0), pipeline_mode=once)
    return pl.pallas_call(
        _merge_kernel, grid=(t // tm, nj),
        in_specs=[wide, wide, wide, gate(0), gate(1), gate(2),
                  pl.BlockSpec((S5_WIDTH, S5_WIDTH), lambda i, j: (0, 0), pipeline_mode=once),
                  pl.BlockSpec((1, S5_WIDTH), lambda i, j: (0, 0)),
                  w_br, w_br, w_br],
        out_specs=pl.BlockSpec((tm, tn), lambda i, j: (i, j)),
        out_shape=jax.ShapeDtypeStruct((t, D_MODEL), BF16),
        scratch_shapes=[pltpu.VMEM((tm, S5_WIDTH), BF16)] * 3,
        compiler_params=_params("parallel", "arbitrary"),
        name="merge",
    )(y_s5, y_gla, y_x, z, z, z, w_glu, b_glu, w_br_s5, w_br_gla, w_br_x)


def _out_proj_kernel(m_ref, w_ref, x_ref, o_ref):
    o_ref[...] = x_ref[...] + jnp.dot(m_ref[...], w_ref[...], preferred_element_type=F32)


def _out_proj(merged, w_out, x, *, tm, tn):
    t = x.shape[0]
    return pl.pallas_call(
        _out_proj_kernel, grid=(t // tm, D_MODEL // tn),
        in_specs=[pl.BlockSpec((tm, D_MODEL), lambda i, j: (i, 0)),
                  pl.BlockSpec((D_MODEL, tn), lambda i, j: (0, j)),
                  pl.BlockSpec((tm, tn), lambda i, j: (i, j))],
        out_specs=pl.BlockSpec((tm, tn), lambda i, j: (i, j)),
        out_shape=jax.ShapeDtypeStruct((t, D_MODEL), F32),
        compiler_params=_params("parallel", "parallel"),
        name="out_proj",
    )(merged, w_out, x)


def _ffn_kernel(x_ref, gf_ref, wg_ref, wu_ref, wd_ref, gl_ref, o_ref, h_scr, acc_scr):
    k = pl.program_id(1)

    @pl.when(k == 0)
    def _():
        h_scr[...] = _rms(x_ref[...], gf_ref[...]).astype(BF16)
        acc_scr[...] = jnp.zeros_like(acc_scr)

    h = h_scr[...]
    gate = jnp.dot(h, wg_ref[...], preferred_element_type=F32)
    up = jnp.dot(h, wu_ref[...], preferred_element_type=F32)
    act = (gate * jax.nn.sigmoid(gate) * up).astype(BF16)
    acc_scr[...] += jnp.dot(act, wd_ref[...], preferred_element_type=F32)

    @pl.when(k == pl.num_programs(1) - 1)
    def _():
        o_ref[...] = _rms(x_ref[...] + acc_scr[...], gl_ref[...])


def _ffn(x, g_ffn, w_gate, w_up, w_down, g_final, *, tm, th):
    t = x.shape[0]
    row = pl.BlockSpec((tm, D_MODEL), lambda i, k: (i, 0))
    vec = pl.BlockSpec((1, D_MODEL), lambda i, k: (0, 0))
    w_in = pl.BlockSpec((D_MODEL, th), lambda i, k: (0, k))
    return pl.pallas_call(
        _ffn_kernel, grid=(t // tm, FFN_HIDDEN // th),
        in_specs=[row, vec, w_in, w_in, pl.BlockSpec((th, D_MODEL), lambda i, k: (k, 0)), vec],
        out_specs=row,
        out_shape=jax.ShapeDtypeStruct((t, D_MODEL), F32),
        scratch_shapes=[pltpu.VMEM((tm, D_MODEL), BF16), pltpu.VMEM((tm, D_MODEL), F32)],
        compiler_params=_params("parallel", "arbitrary"),
        name="ffn",
    )(x, g_ffn, w_gate, w_up, w_down, g_final)


def _reorder_in_proj_kernel(wt_ref, wa_ref, main_ref, alow_ref):
    main_ref[...] = wt_ref[...].T.astype(BF16)

    @pl.when(pl.program_id(0) == 0)
    def _():
        lane = lax.broadcasted_iota(jnp.int32, alow_ref.shape, 1)
        alow_ref[...] = jnp.where(lane < GLA_RANK, wa_ref[...].T, 0.0).astype(BF16)


def _reorder_in_proj(w_in):
    d, n = w_in.shape
    wt = w_in.T
    tc = 512
    n_first, n_gate = COL_GATE // tc, (COL_QX - COL_GATE) // tc
    gate_start, qx_start = 4096 + GLA_RANK + XA_WIDTH, 4096 + GLA_RANK

    def src_row(j):
        t8, g8, q8 = tc // SUBLANES, gate_start // SUBLANES, qx_start // SUBLANES
        r8 = jnp.where(j < n_first, j * t8,
                       jnp.where(j < n_first + n_gate, g8 + (j - n_first) * t8,
                                 q8 + (j - n_first - n_gate) * t8))
        return r8 * SUBLANES

    return pl.pallas_call(
        _reorder_in_proj_kernel, grid=(Z_WIDTH // tc,),
        in_specs=[pl.BlockSpec((pl.Element(tc), pl.Element(d)), lambda j: (src_row(j), 0)),
                  pl.BlockSpec((pl.Element(LANES), pl.Element(d)), lambda j: (COL_GATE, 0))],
        out_specs=[pl.BlockSpec((d, tc), lambda j: (0, j)), pl.BlockSpec((d, LANES), lambda j: (0, 0))],
        out_shape=[jax.ShapeDtypeStruct((d, Z_WIDTH), BF16), jax.ShapeDtypeStruct((d, LANES), BF16)],
        compiler_params=_params("arbitrary"),
        name="reorder_in_proj",
    )(wt, wt)


def _col_tiles(w, tn):
    k, n = w.shape
    return w.reshape(k, n // tn, tn).transpose(1, 0, 2)


def _layer(x, mem_k, mem_v, s5_re0, s5_im0, gla_s0, w, *, batch, seq, segmented):
    z, alow = _norm_matmul(x, w['norm_mix'], w['w_main'], w['w_alow'], tm=1024, tn=1024)
    y_s5, hf_re, hf_im = _s5_branch(z, w['s5'], s5_re0, s5_im0, batch=batch, seq=seq)
    y_gla, gla_s = _gla(z, alow, w['gla_w_a2'], w['gla_b_a'], w['gla_norm'], gla_s0,
                        batch=batch, seq=seq, chained=segmented)
    if segmented:
        y_x = _attn_prompt(z, mem_k, mem_v, batch=batch, seq=seq, rc=512)
    else:
        y_x = _attn_sample(z, mem_k, mem_v, batch=batch, seq=seq, nb=8)
    merged = _merge(y_s5, y_gla, y_x, z, w['s5_w_glu'], w['s5_b_glu'],
                    w['w_br_s5'], w['w_br_gla'], w['w_br_xattn'], tm=512, tn=MERGE_TN)
    x1 = _out_proj(merged, w['w_out'], x, tm=1024, tn=512)
    y = _ffn(x1, w['norm_ffn'], w['w_ffn_gate'], w['w_ffn_up'], w['w_ffn_down'], w['norm_final'],
             tm=512, th=512)
    return y, hf_re, hf_im, gla_s


def kernel(x_prompt, x_sample, mem_prompt, state_s5_re, state_s5_im, state_gla, cache_mem_k, cache_mem_v,
           norm_mix, w_in, s5_lam_re, s5_lam_im, s5_log_dt, s5_b_re, s5_b_im, s5_c_re, s5_c_im,
           s5_d, s5_w_glu, s5_b_glu, gla_w_a2, gla_b_a, gla_norm, mem_norm, w_mem_k, w_mem_v,
           w_br_s5, w_br_gla, w_br_xattn, w_out, norm_ffn, w_ffn_gate, w_ffn_up, w_ffn_down, norm_final):
    depth = w_in.shape[0]
    assert depth == 1
    bp, sp, d = x_prompt.shape
    bs, ss, _ = x_sample.shape
    n_state = S5_GROUPS * S5_STATE
    row = lambda v: v.reshape(1, -1)

    l = 0
    w_main, w_alow = _reorder_in_proj(w_in[l])
    s5_w = _s5_params(s5_lam_re[l], s5_lam_im[l], s5_log_dt[l], s5_b_re[l], s5_b_im[l],
                      s5_c_re[l], s5_c_im[l], s5_d[l])
    w = {
        'norm_mix': row(norm_mix[l]), 'w_main': w_main, 'w_alow': w_alow,
        's5': s5_w,
        's5_w_glu': s5_w_glu[l].astype(BF16), 's5_b_glu': row(s5_b_glu[l]),
        'gla_w_a2': jnp.pad(gla_w_a2[l], ((0, LANES - GLA_RANK), (0, 0))).astype(BF16),
        'gla_b_a': row(gla_b_a[l]), 'gla_norm': row(gla_norm[l]),
        'w_br_s5': _col_tiles(w_br_s5[l].astype(BF16), MERGE_TN),
        'w_br_gla': _col_tiles(w_br_gla[l].astype(BF16), MERGE_TN),
        'w_br_xattn': _col_tiles(w_br_xattn[l].astype(BF16), MERGE_TN), 'w_out': w_out[l].astype(BF16),
        'norm_ffn': row(norm_ffn[l]), 'w_ffn_gate': w_ffn_gate[l].astype(BF16),
        'w_ffn_up': w_ffn_up[l].astype(BF16), 'w_ffn_down': w_ffn_down[l].astype(BF16),
        'norm_final': row(norm_final),
    }

    w_mem = jnp.concatenate([w_mem_k[l], w_mem_v[l]], axis=1).astype(BF16)
    mem_kv = _norm_matmul(mem_prompt.reshape(bp * MEM_LEN, d), row(mem_norm[l]), w_mem, tm=512, tn=1024)
    mk = mem_kv[:, :XA_WIDTH].reshape(bp, MEM_LEN, XA_WIDTH)
    mv = mem_kv[:, XA_WIDTH:].reshape(bp, MEM_LEN, XA_WIDTH)
    zero_s5 = jnp.zeros((bp, n_state), F32)
    zero_gla = jnp.zeros((bp, GLA_HEADS, GLA_DK, GLA_DV), F32)
    yp, p_re, p_im, p_gla = _layer(x_prompt.reshape(bp * sp, d), mk, mv, zero_s5, zero_s5, zero_gla, w,
                                   batch=bp, seq=sp, segmented=True)

    ys, s_re, s_im, s_gla = _layer(x_sample.reshape(bs * ss, d),
                                   _cache_rows(cache_mem_k[l]), _cache_rows(cache_mem_v[l]),
                                   state_s5_re[l].reshape(bs, n_state), state_s5_im[l].reshape(bs, n_state),
                                   state_gla[l], w, batch=bs, seq=ss, segmented=False)

    s5_shape_p = (1, bp, S5_GROUPS, S5_STATE)
    s5_shape_s = (1, bs, S5_GROUPS, S5_STATE)
    kv_shape = (1, bp, MEM_LEN, XA_HEADS, XA_HEAD_DIM)
    return (yp.reshape(bp, sp, d), ys.reshape(bs, ss, d),
            p_re.reshape(s5_shape_p), p_im.reshape(s5_shape_p), p_gla[None],
            mk.reshape(kv_shape), mv.reshape(kv_shape),
            s_re.reshape(s5_shape_s), s_im.reshape(s5_shape_s), s_gla[None])
```

```python
import functools
import math

import jax
import jax.numpy as jnp
from jax import lax
from jax.experimental import pallas as pl
from jax.experimental.pallas import tpu as pltpu

F32 = jnp.float32
BF16 = jnp.bfloat16

D_MODEL = 2048
S5_WIDTH = 1024
S5_GROUP = 16
S5_GROUPS = 64
S5_STATE = 64
GLA_HEADS = 4
GLA_DK = 128
GLA_DV = 256
GLA_RANK = 16
GLA_TAU = 16.0
GLA_CHUNK = 64
XA_HEADS = 4
XA_HEAD_DIM = 256
XA_WIDTH = 1024
MEM_LEN = 256
FFN_HIDDEN = 5632
RMS_EPS = 1e-6

LANES = 128
SUBLANES = 8
VMEM_LIMIT = 56 * 1024 * 1024

COL_U = 0
COL_Q = 1024
COL_K = 1536
COL_V = 2048
COL_R = 3072
COL_GATE = 4096
COL_QX = 10240
Z_WIDTH = 11264

S5_GB = 8
S5_CH = S5_GB * S5_STATE
S5_NBLK = S5_GROUPS // S5_GB
S5_TC = 16
GLA_TILE = 64
MERGE_TN = 1024
IN_TN = 1024
OUT_TN = 512
FFN_TH = 512


def _params(*sem):
    return pltpu.CompilerParams(dimension_semantics=sem, vmem_limit_bytes=VMEM_LIMIT)


def _rms(x, g):
    return x * lax.rsqrt(jnp.mean(x * x, axis=-1, keepdims=True) + RMS_EPS) * g


def _norm_matmul_kernel(x_ref, g_ref, w_ref, o_ref, h_ref):
    @pl.when(pl.program_id(1) == 0)
    def _():
        h_ref[...] = _rms(x_ref[...], g_ref[...]).astype(BF16)

    o_ref[...] = jnp.dot(h_ref[...], w_ref[0], preferred_element_type=F32)


def _norm_matmul2_kernel(x_ref, g_ref, w_ref, ws_ref, o_ref, os_ref, h_ref):
    @pl.when(pl.program_id(1) == 0)
    def _():
        h = _rms(x_ref[...], g_ref[...]).astype(BF16)
        h_ref[...] = h
        os_ref[...] = jnp.dot(h, ws_ref[...], preferred_element_type=F32)

    o_ref[...] = jnp.dot(h_ref[...], w_ref[0], preferred_element_type=F32)


def _norm_matmul(x, g, w, w_small=None, *, tm):
    t, d = x.shape
    nj, _, tn = w.shape
    n = nj * tn
    grid = (t // tm, n // tn)
    in_specs = [pl.BlockSpec((tm, d), lambda i, j: (i, 0)),
                pl.BlockSpec((1, d), lambda i, j: (0, 0)),
                pl.BlockSpec((1, d, tn), lambda i, j: (j, 0, 0))]
    out_specs = pl.BlockSpec((tm, tn), lambda i, j: (i, j))
    out_shape = jax.ShapeDtypeStruct((t, n), F32)
    scratch = [pltpu.VMEM((tm, d), BF16)]
    if w_small is None:
        return pl.pallas_call(_norm_matmul_kernel, grid=grid, in_specs=in_specs, out_specs=out_specs,
                              out_shape=out_shape, scratch_shapes=scratch,
                              compiler_params=_params("parallel", "arbitrary"),
                              name="norm_matmul")(x, g, w)
    ns = w_small.shape[1]
    in_specs.append(pl.BlockSpec((d, ns), lambda i, j: (0, 0)))
    return pl.pallas_call(_norm_matmul2_kernel, grid=grid, in_specs=in_specs,
                          out_specs=[out_specs, pl.BlockSpec((tm, ns), lambda i, j: (i, 0))],
                          out_shape=[out_shape, jax.ShapeDtypeStruct((t, ns), F32)],
                          scratch_shapes=scratch,
                          compiler_params=_params("parallel", "arbitrary"),
                          name="in_proj")(x, g, w, w_small)


def _s5_param_kernel(lr_ref, li_ref, ldt_ref, btr_ref, bti_ref, cr_ref, ci_ref,
                     w16_ref, cc16_ref, w8_ref, cc8_ref, ap_ref):
    p = S5_STATE
    half = S5_TC // 2
    lam_re, lam_im = lr_ref[0], li_ref[0]
    dt = jnp.exp(ldt_ref[0])
    mag = jnp.exp(lam_re * dt)
    a_re = mag * jnp.cos(lam_im * dt)
    a_im = mag * jnp.sin(lam_im * dt)
    den = lam_re * lam_re + lam_im * lam_im
    coef_re = ((a_re - 1.0) * lam_re + a_im * lam_im) / den
    coef_im = (a_im * lam_re - (a_re - 1.0) * lam_im) / den
    bt_re, bt_im = btr_ref[0], bti_ref[0]
    bb_re = coef_re * bt_re - coef_im * bt_im
    bb_im = coef_re * bt_im + coef_im * bt_re
    c_re, c_im = cr_ref[0], ci_ref[0]

    pw_re, pw_im = jnp.ones_like(a_re), jnp.zeros_like(a_re)
    ca_re, ca_im, ab_re, ab_im, powers = [], [], [], [], {}
    for j in range(S5_TC + 1):
        powers[j] = (pw_re, pw_im)
        ca_re.append(c_re * pw_re - c_im * pw_im)
        ca_im.append(c_re * pw_im + c_im * pw_re)
        ab_re.append(bb_re * pw_re - bb_im * pw_im)
        ab_im.append(bb_re * pw_im + bb_im * pw_re)
        pw_re, pw_im = pw_re * a_re - pw_im * a_im, pw_re * a_im + pw_im * a_re
    lag_re = jnp.concatenate(ca_re[:S5_TC], axis=0)
    lag_im = jnp.concatenate(ca_im[:S5_TC], axis=0)
    end_re = jnp.concatenate([ab_re[S5_TC - 1 - s] for s in range(S5_TC)], axis=0)
    end_im = jnp.concatenate([ab_im[S5_TC - 1 - s] for s in range(S5_TC)], axis=0)
    car_re = jnp.concatenate(ca_re[1:], axis=0).T
    car_im = -jnp.concatenate(ca_im[1:], axis=0).T

    def pair_blocks(x):
        first = lax.broadcasted_iota(jnp.int32, x.shape, 0) < p
        return jnp.concatenate([jnp.where(first, x, 0.0), jnp.where(first, 0.0, x)], axis=1)

    cc16_ref[0] = jnp.concatenate([pair_blocks(car_re), pair_blocks(car_im)], axis=0).astype(BF16)
    n8 = half * S5_GROUP
    cc8_ref[0] = jnp.concatenate([pair_blocks(car_re[:, :n8]), pair_blocks(car_im[:, :n8])], axis=0).astype(BF16)

    nt_dims = (((1,), (1,)), ((), ()))
    hi = lax.Precision.HIGHEST
    lane = lax.broadcasted_iota(jnp.int32, (S5_GROUP, LANES), 1)
    end_lane = lax.broadcasted_iota(jnp.int32, end_re.shape, 1)
    for gi in range(2):
        mine = (lane < p) if gi == 0 else (lane >= p)
        own_re, own_im = jnp.where(mine, bb_re, 0.0), jnp.where(mine, bb_im, 0.0)
        strip = (lax.dot_general(own_re, lag_re, nt_dims, precision=hi, preferred_element_type=F32)
                 - lax.dot_general(own_im, lag_im, nt_dims, precision=hi, preferred_element_type=F32))
        lo, up = strip[:, :LANES], strip[:, LANES:]
        blocks = [strip]
        for s in range(1, S5_TC):
            sh = (s % half) * S5_GROUP
            lo_r = pltpu.roll(lo, sh, 1) if sh else lo
            up_r = pltpu.roll(up, sh, 1) if sh else up
            if s < half:
                blocks.append(jnp.concatenate([jnp.where(lane >= sh, lo_r, 0.0),
                                               jnp.where(lane >= sh, up_r, lo_r)], axis=1))
            else:
                blocks.append(jnp.concatenate([jnp.zeros_like(lo), jnp.where(lane >= sh, lo_r, 0.0)], axis=1))
        toeplitz = jnp.concatenate(blocks, axis=0)
        own_end = (end_lane < p) if gi == 0 else (end_lane >= p)
        ends = jnp.concatenate([jnp.where(own_end, end_re, 0.0), jnp.where(own_end, end_im, 0.0)], axis=1)
        w16_ref[gi] = jnp.concatenate([toeplitz, ends], axis=1).astype(BF16)
        w8_ref[gi] = jnp.concatenate([toeplitz[:n8, :n8], ends[n8:, :]], axis=1).astype(BF16)
    ap_ref[0] = jnp.concatenate([powers[S5_TC][0], powers[S5_TC][1], powers[half][0], powers[half][1]], axis=0)


def _s5_params(lam_re, lam_im, log_dt, b_re, b_im, c_re, c_im, d_skip):
    g, p, c = S5_GROUPS, S5_STATE, S5_GROUP
    npair = g // 2
    w16, w8 = S5_TC * c, S5_TC // 2 * c

    def pair_lanes(x):
        return x.reshape(npair, 2, x.shape[1], p).transpose(0, 2, 1, 3).reshape(npair, x.shape[1], 2 * p)

    row = lambda x: pair_lanes(x.reshape(g, 1, p))
    blk = lambda *shape: pl.BlockSpec((1,) + shape, lambda i: (i, 0, 0))
    two = lambda *shape: pl.BlockSpec((2,) + shape, lambda i: (i, 0, 0))
    wt16, cc16, wt8, cc8, apow = pl.pallas_call(
        _s5_param_kernel, grid=(npair,),
        in_specs=[blk(1, 2 * p)] * 3 + [blk(c, 2 * p)] * 4,
        out_specs=[two(w16, w16 + 4 * p), blk(4 * p, 2 * w16), two(w8, w8 + 4 * p), blk(4 * p, 2 * w8),
                   blk(4, 2 * p)],
        out_shape=[jax.ShapeDtypeStruct((g, w16, w16 + 4 * p), BF16),
                   jax.ShapeDtypeStruct((npair, 4 * p, 2 * w16), BF16),
                   jax.ShapeDtypeStruct((g, w8, w8 + 4 * p), BF16),
                   jax.ShapeDtypeStruct((npair, 4 * p, 2 * w8), BF16),
                   jax.ShapeDtypeStruct((npair, 4, 2 * p), F32)],
        compiler_params=_params("parallel"),
        name="s5_params",
    )(row(lam_re), row(lam_im), row(jnp.broadcast_to(log_dt[:, None], (g, p))),
      pair_lanes(b_re.transpose(0, 2, 1)), pair_lanes(b_im.transpose(0, 2, 1)),
      pair_lanes(c_re), pair_lanes(c_im))
    d = d_skip.reshape(S5_NBLK, 1, S5_GB * c)

    def transition(r):
        return jnp.concatenate([apow[:, r].reshape(S5_NBLK, 1, S5_CH), apow[:, r + 1].reshape(S5_NBLK, 1, S5_CH)],
                               axis=-1)

    return {S5_TC: (wt16, cc16, transition(0), d), S5_TC // 2: (wt8, cc8, transition(2), d)}


def _unit_transpose(vs):
    unit = lax.broadcasted_iota(jnp.int32, vs[0].shape, 1) >> int(math.log2(S5_GROUP))
    for dist in (4, 2, 1):
        keep = (unit & dist) == 0
        nxt = list(vs)
        for i in range(8):
            if i & dist == 0:
                a, b = vs[i], vs[i + dist]
                nxt[i] = jnp.where(keep, a, pltpu.roll(b, dist * S5_GROUP, 1))
                nxt[i + dist] = jnp.where(keep, pltpu.roll(a, LANES - dist * S5_GROUP, 1), b)
        vs = nxt
    return vs


def _s5_chunk_kernel(z_ref, w_ref, cc_ref, a_ref, d_ref, h0r_ref, h0i_ref,
                     y_ref, hfr_ref, hfi_ref, ut_ref, ug_ref, yg_ref, e_ref, *, nsl, rps, tc, nseg):
    uw = tc * S5_GROUP
    nq = uw // LANES
    rows = nsl * rps
    nk = rps if nsl > 1 else 1
    ns = rows // nk
    npair = S5_GB // 2
    rchunk = min(rows, 64)

    for s in range(nsl):
        for t in range(tc):
            ut_ref[t, s * rps:(s + 1) * rps, :] = z_ref[pl.ds(s * rps * tc + t, rps, stride=tc), :]

    def row_chunk(rc):
        return pl.ds(pl.multiple_of(rc * rchunk, rchunk), rchunk)

    def to_groups(rc, carry):
        for q in range(nq):
            per_group = _unit_transpose([ut_ref[q * 8 + t, row_chunk(rc), :] for t in range(8)])
            for g in range(S5_GB):
                ug_ref[g, row_chunk(rc), q * LANES:(q + 1) * LANES] = per_group[g].astype(BF16)
        return carry

    lax.fori_loop(0, rows // rchunk, to_groups, 0)

    for pair in range(npair):
        ends = None
        for gi in range(2):
            g = 2 * pair + gi
            em = jnp.dot(ug_ref[g], w_ref[g], preferred_element_type=F32)
            yg_ref[g] = em[:, :uw]
            ends = em[:, uw:] if ends is None else ends + em[:, uw:]
        for part, blk in ((ends[:, :LANES], pair), (ends[:, LANES:], npair + pair)):
            if nk == 1:
                e_ref[blk] = part
            else:
                for s in range(nsl):
                    e_ref[blk, pl.ds(s, rps, stride=nsl), :] = part[s * rps:(s + 1) * rps, :]

    a_row = [(a_ref[0, :, i * LANES:(i + 1) * LANES], a_ref[0, :, S5_CH + i * LANES:S5_CH + (i + 1) * LANES])
             for i in range(npair)]
    a_full = [(jnp.broadcast_to(ar, (ns, LANES)), jnp.broadcast_to(ai, (ns, LANES))) for ar, ai in a_row]

    def slab(t):
        return pl.ds(pl.multiple_of(t * ns, ns), ns)

    def run_scan(h, store):
        def step(t, carry):
            out = []
            for i in range(npair):
                hr, hi = carry[i]
                ar, ai = a_full[i]
                er = e_ref[i, slab(t), :]
                ei = e_ref[npair + i, slab(t), :]
                if store:
                    e_ref[i, slab(t), :] = hr
                    e_ref[npair + i, slab(t), :] = hi
                out.append((ar * hr - ai * hi + er, ar * hi + ai * hr + ei))
            return tuple(out)

        if nk == 1:
            return step(0, h)
        return lax.fori_loop(0, nk, step, h, unroll=4)

    h0 = tuple((h0r_ref[:, i * LANES:(i + 1) * LANES], h0i_ref[:, i * LANES:(i + 1) * LANES]) for i in range(npair))
    if nseg == 1:
        final = run_scan(h0, store=True)
    else:
        zero = jnp.zeros((ns, LANES), F32)
        seg_end = run_scan(tuple((zero, zero) for _ in range(npair)), store=False)
        second = (lax.broadcasted_iota(jnp.int32, (ns, LANES), 0) & 1) == 1
        init, final = [], []
        for i in range(npair):
            p_re, p_im = a_row[i]
            for _ in range(int(math.log2(nk))):
                p_re, p_im = p_re * p_re - p_im * p_im, 2.0 * (p_re * p_im)
            (hr, hi), (er, ei) = h0[i], seg_end[i]
            i_re = jnp.where(second, p_re * hr - p_im * hi + pltpu.roll(er, 1, 0), hr)
            i_im = jnp.where(second, p_re * hi + p_im * hr + pltpu.roll(ei, 1, 0), hi)
            init.append((i_re, i_im))
            final.append((p_re * i_re - p_im * i_im + er, p_re * i_im + p_im * i_re + ei))
        run_scan(tuple(init), store=True)
    for i in range(npair):
        hfr_ref[:, i * LANES:(i + 1) * LANES] = final[i][0]
        hfi_ref[:, i * LANES:(i + 1) * LANES] = final[i][1]

    def chunk_rows(blk):
        if nk == 1:
            return e_ref[blk]
        return jnp.concatenate([e_ref[blk, pl.ds(s, rps, stride=nsl), :] for s in range(nsl)], axis=0)

    for pair in range(npair):
        h_in = jnp.concatenate([chunk_rows(pair), chunk_rows(npair + pair)], axis=1).astype(BF16)
        carried = jnp.dot(h_in, cc_ref[pair], preferred_element_type=F32)
        for gi in range(2):
            yg_ref[2 * pair + gi] += carried[:, gi * uw:(gi + 1) * uw]

    d_row = d_ref[0]

    def to_tokens(rc, carry):
        for q in range(nq):
            per_tau = _unit_transpose([yg_ref[g, row_chunk(rc), q * LANES:(q + 1) * LANES] for g in range(S5_GB)])
            for t in range(8):
                y = per_tau[t] + d_row * ut_ref[q * 8 + t, row_chunk(rc), :]
                ut_ref[q * 8 + t, row_chunk(rc), :] = jax.nn.gelu(y, approximate=True)
        return carry

    lax.fori_loop(0, rows // rchunk, to_tokens, 0)
    for s in range(nsl):
        for t in range(tc):
            y_ref[pl.ds(s * rps * tc + t, rps, stride=tc), :] = ut_ref[t, s * rps:(s + 1) * rps, :]


def _s5_branch(z, weights, h0_re, h0_im, *, batch, seq):
    tokens = batch * seq
    if seq > S5_TC:
        tc, nseg = S5_TC, 2
        nsl, rps = batch * nseg, seq // (nseg * tc)
        assert rps & (rps - 1) == 0
        h0_re, h0_im = jnp.repeat(h0_re, nseg, axis=0), jnp.repeat(h0_im, nseg, axis=0)
    else:
        tc, nseg, nsl, rps = seq, 1, 1, batch
    w, cc, a, d = weights[tc]
    uw = tc * S5_GROUP
    rows = nsl * rps
    ns = h0_re.shape[0]
    kern = functools.partial(_s5_chunk_kernel, nsl=nsl, rps=rps, tc=tc, nseg=nseg)
    state_spec = pl.BlockSpec((ns, S5_CH), lambda k: (0, k))
    state_shape = jax.ShapeDtypeStruct((ns, S5_GROUPS * S5_STATE), F32)
    y, hf_re, hf_im = pl.pallas_call(
        kern, grid=(S5_NBLK,),
        in_specs=[pl.BlockSpec((tokens, LANES), lambda k: (0, COL_U // LANES + k)),
                  pl.BlockSpec((S5_GB, uw, uw + 4 * S5_STATE), lambda k: (k, 0, 0)),
                  pl.BlockSpec((S5_GB // 2, 4 * S5_STATE, 2 * uw), lambda k: (k, 0, 0)),
                  pl.BlockSpec((1, 1, 2 * S5_CH), lambda k: (k, 0, 0)),
                  pl.BlockSpec((1, 1, LANES), lambda k: (k, 0, 0)),
                  state_spec, state_spec],
        out_specs=[pl.BlockSpec((tokens, LANES), lambda k: (0, k)), state_spec, state_spec],
        out_shape=[jax.ShapeDtypeStruct((tokens, S5_WIDTH), F32), state_shape, state_shape],
        scratch_shapes=[pltpu.VMEM((tc, rows, LANES), F32), pltpu.VMEM((S5_GB, rows, uw), BF16),
                        pltpu.VMEM((S5_GB, rows, uw), F32), pltpu.VMEM((S5_GB, rows, LANES), F32)],
        compiler_params=_params("parallel"),
        name="s5_chunked",
    )(z, w, cc, a, d, h0_re, h0_im)
    if nseg == 2:
        hf_re, hf_im = hf_re[1::2], hf_im[1::2]
    return y, hf_re, hf_im


def _gla_kernel(q_ref, k_ref, v_ref, r_ref, al_ref, wa_ref, ba_ref, gn_ref, s0_ref,
                y_ref, sf_ref, s_scr, *, ntile, groups, chained, width):
    rt = GLA_TILE
    c = rt // groups
    shift = int(math.log2(c))
    row_g = lax.broadcasted_iota(jnp.int32, (rt, rt), 0)
    col_g = lax.broadcasted_iota(jnp.int32, (rt, rt), 1)
    same = (row_g >> shift) == (col_g >> shift)
    causal = same & (row_g >= col_g)
    cum_w = jnp.concatenate([causal.astype(BF16), same.astype(BF16)], axis=0)
    sel_r = lax.broadcasted_iota(jnp.int32, (rt, groups * LANES), 0) >> shift
    sel_c = lax.broadcasted_iota(jnp.int32, (rt, groups * LANES), 1) >> int(math.log2(LANES))
    sel = (sel_r == sel_c).astype(BF16)
    tn_dims = (((0,), (0,)), ((), ()))
    nt_dims = (((1,), (1,)), ((), ()))

    def split3(x):
        hi = x.astype(BF16)
        r1 = x - hi.astype(F32)
        mid = r1.astype(BF16)
        lo = (r1 - mid.astype(F32)).astype(BF16)
        return jnp.concatenate([hi, mid, lo], axis=1)

    if chained:
        s_scr[...] = s0_ref[0, 0]

    def body(it, carry):
        tiles = [it * width + u for u in range(width)]
        rows = [pl.ds(pl.multiple_of(t * rt, rt), rt) for t in tiles]
        log_a = []
        for u in range(width):
            x = jnp.dot(al_ref[rows[u], :].astype(BF16), wa_ref[...], preferred_element_type=F32) + ba_ref[...]
            log_a.append((jnp.minimum(x, 0.0) - jnp.log1p(jnp.exp(-jnp.abs(x)))) * (1.0 / GLA_TAU))
        la3 = [split3(la) for la in log_a]
        cums, e_col = [], []
        for u in range(width):
            cs = jnp.dot(cum_w, la3[u], preferred_element_type=F32)
            cums.append(cs[:, :LANES] + cs[:, LANES:2 * LANES] + cs[:, 2 * LANES:])
            tot = lax.dot_general(la3[u], sel, tn_dims, preferred_element_type=F32)
            e_col.append(jnp.exp(tot[:LANES] + tot[LANES:2 * LANES] + tot[2 * LANES:]))
        qb, kd, v, att, upd = [], [], [], [], []
        for u in range(width):
            b = cums[u][:rt]
            b_end = cums[u][rt:]
            k = k_ref[rows[u], :]
            v.append(v_ref[rows[u], :])
            q = q_ref[rows[u], :] * (GLA_DK ** -0.5)
            qb.append(q * jnp.exp(b))
            kd.append(k * jnp.exp(b_end - b))
            mid = 0.5 * b_end
            a = lax.dot_general((q * jnp.exp(b - mid)).astype(BF16), (k * jnp.exp(mid - b)).astype(BF16), nt_dims,
                                preferred_element_type=F32)
            att.append(jnp.where(causal, a, 0.0).astype(BF16))
            upd.append([lax.dot_general(kd[u][g * c:(g + 1) * c].astype(BF16), v[u][g * c:(g + 1) * c].astype(BF16),
                                        tn_dims, preferred_element_type=F32) for g in range(groups)])
        o = [jnp.dot(att[u], v[u].astype(BF16), preferred_element_type=F32) for u in range(width)]
        for u in range(width):
            o_state = []
            for g in range(groups):
                s = s_scr[...] if chained else s0_ref[tiles[u] * groups + g, 0]
                o_state.append(jnp.dot(qb[u][g * c:(g + 1) * c].astype(BF16), s.astype(BF16),
                                       preferred_element_type=F32))
                e = e_col[u][:, g * LANES:(g + 1) * LANES]
                s_new = jnp.concatenate([s[:, :LANES] * e, s[:, LANES:] * e], axis=1) + upd[u][g]
                if chained:
                    s_scr[...] = s_new
                else:
                    sf_ref[tiles[u] * groups + g, 0] = s_new
            o[u] = o[u] + (o_state[0] if groups == 1 else jnp.concatenate(o_state, axis=0))
        for u in range(width):
            y = o[u] * lax.rsqrt(jnp.mean(o[u] * o[u], axis=-1, keepdims=True) + RMS_EPS)
            y = y * gn_ref[...]
            r = r_ref[rows[u], :]
            y_ref[rows[u], :] = y * (r * jax.nn.sigmoid(r))
        return carry

    lax.fori_loop(0, ntile // width, body, 0)
    if chained:
        sf_ref[0, 0] = s_scr[...]


def _gla(z, alow, w_a2, b_a, g_norm, s0, *, batch, seq, chained):
    chunk = math.gcd(seq, GLA_CHUNK)
    if chained:
        assert chunk == GLA_TILE
        nb, groups, width = 1, 1, 8
    else:
        assert GLA_TILE % seq == 0 and chunk == seq
        nb, groups, width = 32, GLA_TILE // seq, 2
    rows = nb * seq
    kern = functools.partial(_gla_kernel, ntile=rows // GLA_TILE, groups=groups, chained=chained, width=width)
    state_spec = pl.BlockSpec((nb, 1, GLA_DK, GLA_DV), lambda i, h: (i, h, 0, 0))
    return pl.pallas_call(
        kern, grid=(batch // nb, GLA_HEADS),
        in_specs=[pl.BlockSpec((rows, GLA_DK), lambda i, h: (i, COL_Q // GLA_DK + h)),
                  pl.BlockSpec((rows, GLA_DK), lambda i, h: (i, COL_K // GLA_DK + h)),
                  pl.BlockSpec((rows, GLA_DV), lambda i, h: (i, COL_V // GLA_DV + h)),
                  pl.BlockSpec((rows, GLA_DV), lambda i, h: (i, COL_R // GLA_DV + h)),
                  pl.BlockSpec((rows, LANES), lambda i, h: (i, 0)),
                  pl.BlockSpec((LANES, GLA_DK), lambda i, h: (0, h)),
                  pl.BlockSpec((1, GLA_DK), lambda i, h: (0, h)),
                  pl.BlockSpec((1, GLA_DV), lambda i, h: (0, h)),
                  state_spec],
        out_specs=[pl.BlockSpec((rows, GLA_DV), lambda i, h: (i, h)), state_spec],
        out_shape=[jax.ShapeDtypeStruct((batch * seq, GLA_HEADS * GLA_DV), F32),
                   jax.ShapeDtypeStruct((batch, GLA_HEADS, GLA_DK, GLA_DV), F32)],
        scratch_shapes=[pltpu.VMEM((GLA_DK, GLA_DV), F32)],
        compiler_params=_params("parallel", "parallel"),
        name="gla",
    )(z, z, z, z, alow, w_a2, b_a, g_norm, s0)


def _attn_kernel(q_ref, k_ref, v_ref, o_ref):
    nt_dims = (((1,), (1,)), ((), ()))
    cols = [slice(h * XA_HEAD_DIM, (h + 1) * XA_HEAD_DIM) for h in range(XA_HEADS)]
    scores = [lax.dot_general(q_ref[:, c].astype(BF16), k_ref[0, :, c].astype(BF16), nt_dims,
                              preferred_element_type=F32) * (XA_HEAD_DIM ** -0.5) for c in cols]
    probs = []
    for s in scores:
        p = jnp.exp(s - jnp.max(s, axis=-1, keepdims=True))
        probs.append((p / jnp.sum(p, axis=-1, keepdims=True)).astype(BF16))
    for c, p in zip(cols, probs):
        o_ref[:, c] = jnp.dot(p, v_ref[0, :, c].astype(BF16), preferred_element_type=F32)


def _attn_prompt(z, mem_k, mem_v, *, batch, seq, rc):
    nrc = seq // rc
    kv_spec = pl.BlockSpec((1, MEM_LEN, XA_WIDTH), lambda b, c: (b, 0, 0))
    return pl.pallas_call(
        _attn_kernel, grid=(batch, nrc),
        in_specs=[pl.BlockSpec((rc, XA_WIDTH), lambda b, c: (b * nrc + c, COL_QX // XA_WIDTH)), kv_spec, kv_spec],
        out_specs=pl.BlockSpec((rc, XA_WIDTH), lambda b, c: (b * nrc + c, 0)),
        out_shape=jax.ShapeDtypeStruct((batch * seq, XA_WIDTH), F32),
        compiler_params=_params("parallel", "parallel"),
        name="attn_prompt",
    )(z, mem_k, mem_v)


XA_HALF = XA_HEAD_DIM // 2
XA_ROWS = 2 * XA_HEADS


def _attn_cache_kernel(q_ref, k_ref, v_ref, o_ref, *, nb, seq, width):
    nt_dims = (((1,), (1,)), ((), ()))

    def half_rows(ref, bi, h, half):
        return ref[bi, pl.ds(half * XA_HEADS + h, MEM_LEN, stride=XA_ROWS), :].astype(BF16)

    def body(it, carry):
        work = [(it * width + u, h) for u in range(width) for h in range(XA_HEADS)]
        scores = []
        for bi, h in work:
            rows = pl.ds(pl.multiple_of(bi * seq, SUBLANES), seq)
            lo = h * XA_HEAD_DIM
            s = lax.dot_general(q_ref[rows, lo:lo + XA_HALF].astype(BF16), half_rows(k_ref, bi, h, 0),
                                nt_dims, preferred_element_type=F32)
            s = s + lax.dot_general(q_ref[rows, lo + XA_HALF:lo + XA_HEAD_DIM].astype(BF16),
                                    half_rows(k_ref, bi, h, 1), nt_dims, preferred_element_type=F32)
            scores.append(s * (XA_HEAD_DIM ** -0.5))
        probs = []
        for s in scores:
            p = jnp.exp(s - jnp.max(s, axis=-1, keepdims=True))
            probs.append((p / jnp.sum(p, axis=-1, keepdims=True)).astype(BF16))
        for (bi, h), p in zip(work, probs):
            rows = pl.ds(pl.multiple_of(bi * seq, SUBLANES), seq)
            lo = h * XA_HEAD_DIM
            for half in range(2):
                o_ref[rows, lo + half * XA_HALF:lo + (half + 1) * XA_HALF] = jnp.dot(
                    p, half_rows(v_ref, bi, h, half), preferred_element_type=F32)
        return carry

    lax.fori_loop(0, nb // width, body, 0)


def _cache_rows(cache):
    bs = cache.shape[0]
    c = cache.reshape(bs, MEM_LEN, XA_HEADS, 2, XA_HALF).transpose(0, 1, 3, 2, 4)
    return c.reshape(bs, MEM_LEN * XA_ROWS, XA_HALF)


def _attn_sample(z, mem_k, mem_v, *, batch, seq, nb):
    rows = nb * seq
    kern = functools.partial(_attn_cache_kernel, nb=nb, seq=seq, width=2)
    kv_spec = pl.BlockSpec((nb, MEM_LEN * XA_ROWS, XA_HALF), lambda i: (i, 0, 0))
    return pl.pallas_call(
        kern, grid=(batch // nb,),
        in_specs=[pl.BlockSpec((rows, XA_WIDTH), lambda i: (i, COL_QX // XA_WIDTH)), kv_spec, kv_spec],
        out_specs=pl.BlockSpec((rows, XA_WIDTH), lambda i: (i, 0)),
        out_shape=jax.ShapeDtypeStruct((batch * seq, XA_WIDTH), F32),
        compiler_params=_params("parallel"),
        name="attn_sample",
    )(z, mem_k, mem_v)


def _merge_kernel(ys_ref, yg_ref, yx_ref, g0_ref, g1_ref, g2_ref, wglu_ref, bglu_ref,
                  w0_ref, w1_ref, w2_ref, o_ref, s5_scr, gla_scr, xa_scr):
    @pl.when(pl.program_id(1) == 0)
    def _():
        y = ys_ref[...]
        lin = jnp.dot(y.astype(BF16), wglu_ref[...], preferred_element_type=F32) + bglu_ref[...]
        s5_scr[...] = (y * jax.nn.sigmoid(lin)).astype(BF16)
        gla_scr[...] = yg_ref[...].astype(BF16)
        xa_scr[...] = yx_ref[...].astype(BF16)

    j = pl.program_id(1)
    m = jax.nn.sigmoid(g0_ref[...]) * jnp.dot(s5_scr[...], w0_ref[j], preferred_element_type=F32)
    m = m + jax.nn.sigmoid(g1_ref[...]) * jnp.dot(gla_scr[...], w1_ref[j], preferred_element_type=F32)
    m = m + jax.nn.sigmoid(g2_ref[...]) * jnp.dot(xa_scr[...], w2_ref[j], preferred_element_type=F32)
    o_ref[...] = m.astype(o_ref.dtype)


def _merge(y_s5, y_gla, y_x, z, w_glu, b_glu, w_br_s5, w_br_gla, w_br_x, *, tm, tn):
    t = y_s5.shape[0]
    nj = D_MODEL // tn
    once = pl.Buffered(1)
    wide = pl.BlockSpec((tm, S5_WIDTH), lambda i, j: (i, 0))
    gate = lambda b: pl.BlockSpec((tm, tn), lambda i, j: (i, (COL_GATE + b * D_MODEL) // tn + j))
    w_br = pl.BlockSpec((nj, S5_WIDTH, tn), lambda i, j: (0, 0, 0), pipeline_mode=once)
    return pl.pallas_call(
        _merge_kernel, grid=(t // tm, nj),
        in_specs=[wide, wide, wide, gate(0), gate(1), gate(2),
                  pl.BlockSpec((S5_WIDTH, S5_WIDTH), lambda i, j: (0, 0), pipeline_mode=once),
                  pl.BlockSpec((1, S5_WIDTH), lambda i, j: (0, 0)),
                  w_br, w_br, w_br],
        out_specs=pl.BlockSpec((tm, tn), lambda i, j: (i, j)),
        out_shape=jax.ShapeDtypeStruct((t, D_MODEL), BF16),
        scratch_shapes=[pltpu.VMEM((tm, S5_WIDTH), BF16)] * 3,
        compiler_params=_params("parallel", "arbitrary"),
        name="merge",
    )(y_s5, y_gla, y_x, z, z, z, w_glu, b_glu, w_br_s5, w_br_gla, w_br_x)


def _out_proj_kernel(m_ref, w_ref, x_ref, o_ref):
    o_ref[...] = x_ref[...] + jnp.dot(m_ref[...], w_ref[0], preferred_element_type=F32)


def _out_proj(merged, w_out, x, *, tm):
    t = x.shape[0]
    tn = w_out.shape[2]
    return pl.pallas_call(
        _out_proj_kernel, grid=(t // tm, D_MODEL // tn),
        in_specs=[pl.BlockSpec((tm, D_MODEL), lambda i, j: (i, 0)),
                  pl.BlockSpec((1, D_MODEL, tn), lambda i, j: (j, 0, 0)),
                  pl.BlockSpec((tm, tn), lambda i, j: (i, j))],
        out_specs=pl.BlockSpec((tm, tn), lambda i, j: (i, j)),
        out_shape=jax.ShapeDtypeStruct((t, D_MODEL), F32),
        compiler_params=_params("parallel", "parallel"),
        name="out_proj",
    )(merged, w_out, x)


def _ffn_kernel(x_ref, gf_ref, wg_ref, wu_ref, wd_ref, gl_ref, o_ref, h_scr, acc_scr):
    k = pl.program_id(1)

    @pl.when(k == 0)
    def _():
        h_scr[...] = _rms(x_ref[...], gf_ref[...]).astype(BF16)
        acc_scr[...] = jnp.zeros_like(acc_scr)

    h = h_scr[...]
    gate = jnp.dot(h, wg_ref[0], preferred_element_type=F32)
    up = jnp.dot(h, wu_ref[0], preferred_element_type=F32)
    act = (gate * jax.nn.sigmoid(gate) * up).astype(BF16)
    acc_scr[...] += jnp.dot(act, wd_ref[...], preferred_element_type=F32)

    @pl.when(k == pl.num_programs(1) - 1)
    def _():
        o_ref[...] = _rms(x_ref[...] + acc_scr[...], gl_ref[...])


def _ffn(x, g_ffn, w_gate, w_up, w_down, g_final, *, tm):
    t = x.shape[0]
    th = w_gate.shape[2]
    row = pl.BlockSpec((tm, D_MODEL), lambda i, k: (i, 0))
    vec = pl.BlockSpec((1, D_MODEL), lambda i, k: (0, 0))
    w_in = pl.BlockSpec((1, D_MODEL, th), lambda i, k: (k, 0, 0))
    return pl.pallas_call(
        _ffn_kernel, grid=(t // tm, FFN_HIDDEN // th),
        in_specs=[row, vec, w_in, w_in, pl.BlockSpec((th, D_MODEL), lambda i, k: (k, 0)), vec],
        out_specs=row,
        out_shape=jax.ShapeDtypeStruct((t, D_MODEL), F32),
        scratch_shapes=[pltpu.VMEM((tm, D_MODEL), BF16), pltpu.VMEM((tm, D_MODEL), F32)],
        compiler_params=_params("parallel", "arbitrary"),
        name="ffn",
    )(x, g_ffn, w_gate, w_up, w_down, g_final)


def _reorder_in_proj_kernel(wt_ref, wa_ref, main_ref, alow_ref):
    main_ref[...] = wt_ref[...].T.astype(BF16)

    @pl.when(pl.program_id(0) == 0)
    def _():
        lane = lax.broadcasted_iota(jnp.int32, alow_ref.shape, 1)
        alow_ref[...] = jnp.where(lane < GLA_RANK, wa_ref[...].T, 0.0).astype(BF16)


def _reorder_in_proj(w_in):
    d, n = w_in.shape
    wt = w_in.T
    tc = 512
    per_tile = IN_TN // tc
    n_first, n_gate = COL_GATE // tc, (COL_QX - COL_GATE) // tc
    gate_start, qx_start = 4096 + GLA_RANK + XA_WIDTH, 4096 + GLA_RANK

    def src_row(j):
        t8, g8, q8 = tc // SUBLANES, gate_start // SUBLANES, qx_start // SUBLANES
        r8 = jnp.where(j < n_first, j * t8,
                       jnp.where(j < n_first + n_gate, g8 + (j - n_first) * t8,
                                 q8 + (j - n_first - n_gate) * t8))
        return r8 * SUBLANES

    return pl.pallas_call(
        _reorder_in_proj_kernel, grid=(Z_WIDTH // tc,),
        in_specs=[pl.BlockSpec((pl.Element(tc), pl.Element(d)), lambda j: (src_row(j), 0)),
                  pl.BlockSpec((pl.Element(LANES), pl.Element(d)), lambda j: (COL_GATE, 0))],
        out_specs=[pl.BlockSpec((None, d, tc), lambda j: (j // per_tile, 0, j % per_tile)),
                   pl.BlockSpec((d, LANES), lambda j: (0, 0))],
        out_shape=[jax.ShapeDtypeStruct((Z_WIDTH // IN_TN, d, IN_TN), BF16),
                   jax.ShapeDtypeStruct((d, LANES), BF16)],
        compiler_params=_params("arbitrary"),
        name="reorder_in_proj",
    )(wt, wt)


def _col_tiles(w, tn):
    k, n = w.shape
    return w.reshape(k, n // tn, tn).transpose(1, 0, 2)


def _layer(x, mem_k, mem_v, s5_re0, s5_im0, gla_s0, w, *, batch, seq, segmented):
    z, alow = _norm_matmul(x, w['norm_mix'], w['w_main'], w['w_alow'], tm=1024)
    y_s5, hf_re, hf_im = _s5_branch(z, w['s5'], s5_re0, s5_im0, batch=batch, seq=seq)
    y_gla, gla_s = _gla(z, alow, w['gla_w_a2'], w['gla_b_a'], w['gla_norm'], gla_s0,
                        batch=batch, seq=seq, chained=segmented)
    if segmented:
        y_x = _attn_prompt(z, mem_k, mem_v, batch=batch, seq=seq, rc=512)
    else:
        y_x = _attn_sample(z, mem_k, mem_v, batch=batch, seq=seq, nb=8)
    merged = _merge(y_s5, y_gla, y_x, z, w['s5_w_glu'], w['s5_b_glu'],
                    w['w_br_s5'], w['w_br_gla'], w['w_br_xattn'], tm=512, tn=MERGE_TN)
    x1 = _out_proj(merged, w['w_out'], x, tm=1024)
    y = _ffn(x1, w['norm_ffn'], w['w_ffn_gate'], w['w_ffn_up'], w['w_ffn_down'], w['norm_final'], tm=512)
    return y, hf_re, hf_im, gla_s


def kernel(x_prompt, x_sample, mem_prompt, state_s5_re, state_s5_im, state_gla, cache_mem_k, cache_mem_v,
           norm_mix, w_in, s5_lam_re, s5_lam_im, s5_log_dt, s5_b_re, s5_b_im, s5_c_re, s5_c_im,
           s5_d, s5_w_glu, s5_b_glu, gla_w_a2, gla_b_a, gla_norm, mem_norm, w_mem_k, w_mem_v,
           w_br_s5, w_br_gla, w_br_xattn, w_out, norm_ffn, w_ffn_gate, w_ffn_up, w_ffn_down, norm_final):
    depth = w_in.shape[0]
    assert depth == 1
    bp, sp, d = x_prompt.shape
    bs, ss, _ = x_sample.shape
    n_state = S5_GROUPS * S5_STATE
    row = lambda v: v.reshape(1, -1)

    l = 0
    w_main, w_alow = _reorder_in_proj(w_in[l])
    s5_w = _s5_params(s5_lam_re[l], s5_lam_im[l], s5_log_dt[l], s5_b_re[l], s5_b_im[l],
                      s5_c_re[l], s5_c_im[l], s5_d[l])
    w = {
        'norm_mix': row(norm_mix[l]), 'w_main': w_main, 'w_alow': w_alow,
        's5': s5_w,
        's5_w_glu': s5_w_glu[l].astype(BF16), 's5_b_glu': row(s5_b_glu[l]),
        'gla_w_a2': jnp.pad(gla_w_a2[l], ((0, LANES - GLA_RANK), (0, 0))).astype(BF16),
        'gla_b_a': row(gla_b_a[l]), 'gla_norm': row(gla_norm[l]),
        'w_br_s5': _col_tiles(w_br_s5[l].astype(BF16), MERGE_TN),
        'w_br_gla': _col_tiles(w_br_gla[l].astype(BF16), MERGE_TN),
        'w_br_xattn': _col_tiles(w_br_xattn[l].astype(BF16), MERGE_TN),
        'w_out': _col_tiles(w_out[l].astype(BF16), OUT_TN),
        'norm_ffn': row(norm_ffn[l]), 'w_ffn_gate': _col_tiles(w_ffn_gate[l].astype(BF16), FFN_TH),
        'w_ffn_up': _col_tiles(w_ffn_up[l].astype(BF16), FFN_TH), 'w_ffn_down': w_ffn_down[l].astype(BF16),
        'norm_final': row(norm_final),
    }

    w_mem = jnp.stack([w_mem_k[l], w_mem_v[l]]).astype(BF16)
    mem_kv = _norm_matmul(mem_prompt.reshape(bp * MEM_LEN, d), row(mem_norm[l]), w_mem, tm=512)
    mk = mem_kv[:, :XA_WIDTH].reshape(bp, MEM_LEN, XA_WIDTH)
    mv = mem_kv[:, XA_WIDTH:].reshape(bp, MEM_LEN, XA_WIDTH)
    zero_s5 = jnp.zeros((bp, n_state), F32)
    zero_gla = jnp.zeros((bp, GLA_HEADS, GLA_DK, GLA_DV), F32)
    yp, p_re, p_im, p_gla = _layer(x_prompt.reshape(bp * sp, d), mk, mv, zero_s5, zero_s5, zero_gla, w,
                                   batch=bp, seq=sp, segmented=True)

    ys, s_re, s_im, s_gla = _layer(x_sample.reshape(bs * ss, d),
                                   _cache_rows(cache_mem_k[l]), _cache_rows(cache_mem_v[l]),
                                   state_s5_re[l].reshape(bs, n_state), state_s5_im[l].reshape(bs, n_state),
                                   state_gla[l], w, batch=bs, seq=ss, segmented=False)

    s5_shape_p = (1, bp, S5_GROUPS, S5_STATE)
    s5_shape_s = (1, bs, S5_GROUPS, S5_STATE)
    kv_shape = (1, bp, MEM_LEN, XA_HEADS, XA_HEAD_DIM)
    return (yp.reshape(bp, sp, d), ys.reshape(bs, ss, d),
            p_re.reshape(s5_shape_p), p_im.reshape(s5_shape_p), p_gla[None],
            mk.reshape(kv_shape), mv.reshape(kv_shape),
            s_re.reshape(s5_shape_s), s_im.reshape(s5_shape_s), s_gla[None])
```

```python
import functools
import math

import jax
import jax.numpy as jnp
from jax import lax
from jax.experimental import pallas as pl
from jax.experimental.pallas import tpu as pltpu

F32 = jnp.float32
BF16 = jnp.bfloat16

D_MODEL = 2048
S5_WIDTH = 1024
S5_GROUP = 16
S5_GROUPS = 64
S5_STATE = 64
GLA_HEADS = 4
GLA_DK = 128
GLA_DV = 256
GLA_RANK = 16
GLA_TAU = 16.0
GLA_CHUNK = 64
XA_HEADS = 4
XA_HEAD_DIM = 256
XA_WIDTH = 1024
MEM_LEN = 256
FFN_HIDDEN = 5632
RMS_EPS = 1e-6

LANES = 128
SUBLANES = 8
VMEM_LIMIT = 56 * 1024 * 1024
OUT_FFN_VMEM = 52 * 1024 * 1024

COL_U = 0
COL_Q = 1024
COL_K = 1536
COL_V = 2048
COL_R = 3072
COL_GATE = 4096
COL_QX = 10240
Z_WIDTH = 11264

S5_GB = 8
S5_CH = S5_GB * S5_STATE
S5_NBLK = S5_GROUPS // S5_GB
S5_TC = 16
GLA_TILE = 64
MERGE_TN = 1024
IN_TN = 1024
FFN_TH = 512


def _params(*sem):
    return pltpu.CompilerParams(dimension_semantics=sem, vmem_limit_bytes=VMEM_LIMIT)


def _rms(x, g):
    return x * lax.rsqrt(jnp.mean(x * x, axis=-1, keepdims=True) + RMS_EPS) * g


def _norm_matmul_kernel(x_ref, g_ref, w_ref, o_ref, h_ref):
    @pl.when(pl.program_id(1) == 0)
    def _():
        h_ref[...] = _rms(x_ref[...], g_ref[...]).astype(BF16)

    o_ref[...] = jnp.dot(h_ref[...], w_ref[0], preferred_element_type=F32)


def _norm_matmul2_kernel(x_ref, g_ref, w_ref, ws_ref, o_ref, os_ref, h_ref):
    @pl.when(pl.program_id(1) == 0)
    def _():
        h = _rms(x_ref[...], g_ref[...]).astype(BF16)
        h_ref[...] = h
        os_ref[...] = jnp.dot(h, ws_ref[...], preferred_element_type=F32)

    o_ref[...] = jnp.dot(h_ref[...], w_ref[0], preferred_element_type=F32)


def _norm_matmul(x, g, w, w_small=None, *, tm):
    t, d = x.shape
    nj, _, tn = w.shape
    n = nj * tn
    grid = (t // tm, n // tn)
    in_specs = [pl.BlockSpec((tm, d), lambda i, j: (i, 0)),
                pl.BlockSpec((1, d), lambda i, j: (0, 0)),
                pl.BlockSpec((1, d, tn), lambda i, j: (j, 0, 0))]
    out_specs = pl.BlockSpec((tm, tn), lambda i, j: (i, j))
    out_shape = jax.ShapeDtypeStruct((t, n), F32)
    scratch = [pltpu.VMEM((tm, d), BF16)]
    if w_small is None:
        return pl.pallas_call(_norm_matmul_kernel, grid=grid, in_specs=in_specs, out_specs=out_specs,
                              out_shape=out_shape, scratch_shapes=scratch,
                              compiler_params=_params("parallel", "arbitrary"),
                              name="norm_matmul")(x, g, w)
    ns = w_small.shape[1]
    in_specs.append(pl.BlockSpec((d, ns), lambda i, j: (0, 0)))
    return pl.pallas_call(_norm_matmul2_kernel, grid=grid, in_specs=in_specs,
                          out_specs=[out_specs, pl.BlockSpec((tm, ns), lambda i, j: (i, 0))],
                          out_shape=[out_shape, jax.ShapeDtypeStruct((t, ns), F32)],
                          scratch_shapes=scratch,
                          compiler_params=_params("parallel", "arbitrary"),
                          name="in_proj")(x, g, w, w_small)


def _s5_param_kernel(lr_ref, li_ref, ldt_ref, btr_ref, bti_ref, cr_ref, ci_ref,
                     w16_ref, cc16_ref, w8_ref, cc8_ref, ap_ref):
    p = S5_STATE
    half = S5_TC // 2
    lam_re, lam_im = lr_ref[0], li_ref[0]
    dt = jnp.exp(ldt_ref[0])
    mag = jnp.exp(lam_re * dt)
    a_re = mag * jnp.cos(lam_im * dt)
    a_im = mag * jnp.sin(lam_im * dt)
    den = lam_re * lam_re + lam_im * lam_im
    coef_re = ((a_re - 1.0) * lam_re + a_im * lam_im) / den
    coef_im = (a_im * lam_re - (a_re - 1.0) * lam_im) / den
    bt_re, bt_im = btr_ref[0], bti_ref[0]
    bb_re = coef_re * bt_re - coef_im * bt_im
    bb_im = coef_re * bt_im + coef_im * bt_re
    c_re, c_im = cr_ref[0], ci_ref[0]

    pw_re, pw_im = jnp.ones_like(a_re), jnp.zeros_like(a_re)
    ca_re, ca_im, ab_re, ab_im, powers = [], [], [], [], {}
    for j in range(S5_TC + 1):
        powers[j] = (pw_re, pw_im)
        ca_re.append(c_re * pw_re - c_im * pw_im)
        ca_im.append(c_re * pw_im + c_im * pw_re)
        ab_re.append(bb_re * pw_re - bb_im * pw_im)
        ab_im.append(bb_re * pw_im + bb_im * pw_re)
        pw_re, pw_im = pw_re * a_re - pw_im * a_im, pw_re * a_im + pw_im * a_re
    lag_re = jnp.concatenate(ca_re[:S5_TC], axis=0)
    lag_im = jnp.concatenate(ca_im[:S5_TC], axis=0)
    end_re = jnp.concatenate([ab_re[S5_TC - 1 - s] for s in range(S5_TC)], axis=0)
    end_im = jnp.concatenate([ab_im[S5_TC - 1 - s] for s in range(S5_TC)], axis=0)
    car_re = jnp.concatenate(ca_re[1:], axis=0).T
    car_im = -jnp.concatenate(ca_im[1:], axis=0).T

    def pair_blocks(x):
        first = lax.broadcasted_iota(jnp.int32, x.shape, 0) < p
        return jnp.concatenate([jnp.where(first, x, 0.0), jnp.where(first, 0.0, x)], axis=1)

    cc16_ref[0] = jnp.concatenate([pair_blocks(car_re), pair_blocks(car_im)], axis=0).astype(BF16)
    n8 = half * S5_GROUP
    cc8_ref[0] = jnp.concatenate([pair_blocks(car_re[:, :n8]), pair_blocks(car_im[:, :n8])], axis=0).astype(BF16)

    nt_dims = (((1,), (1,)), ((), ()))
    hi = lax.Precision.HIGHEST
    lane = lax.broadcasted_iota(jnp.int32, (S5_GROUP, LANES), 1)
    end_lane = lax.broadcasted_iota(jnp.int32, end_re.shape, 1)
    for gi in range(2):
        mine = (lane < p) if gi == 0 else (lane >= p)
        own_re, own_im = jnp.where(mine, bb_re, 0.0), jnp.where(mine, bb_im, 0.0)
        strip = (lax.dot_general(own_re, lag_re, nt_dims, precision=hi, preferred_element_type=F32)
                 - lax.dot_general(own_im, lag_im, nt_dims, precision=hi, preferred_element_type=F32))
        lo, up = strip[:, :LANES], strip[:, LANES:]
        blocks = [strip]
        for s in range(1, S5_TC):
            sh = (s % half) * S5_GROUP
            lo_r = pltpu.roll(lo, sh, 1) if sh else lo
            up_r = pltpu.roll(up, sh, 1) if sh else up
            if s < half:
                blocks.append(jnp.concatenate([jnp.where(lane >= sh, lo_r, 0.0),
                                               jnp.where(lane >= sh, up_r, lo_r)], axis=1))
            else:
                blocks.append(jnp.concatenate([jnp.zeros_like(lo), jnp.where(lane >= sh, lo_r, 0.0)], axis=1))
        toeplitz = jnp.concatenate(blocks, axis=0)
        own_end = (end_lane < p) if gi == 0 else (end_lane >= p)
        ends = jnp.concatenate([jnp.where(own_end, end_re, 0.0), jnp.where(own_end, end_im, 0.0)], axis=1)
        w16_ref[gi] = jnp.concatenate([toeplitz, ends], axis=1).astype(BF16)
        w8_ref[gi] = jnp.concatenate([toeplitz[:n8, :n8], ends[n8:, :]], axis=1).astype(BF16)
    ap_ref[0] = jnp.concatenate([powers[S5_TC][0], powers[S5_TC][1], powers[half][0], powers[half][1]], axis=0)


def _s5_params(lam_re, lam_im, log_dt, b_re, b_im, c_re, c_im, d_skip):
    g, p, c = S5_GROUPS, S5_STATE, S5_GROUP
    npair = g // 2
    w16, w8 = S5_TC * c, S5_TC // 2 * c

    def pair_lanes(x):
        return x.reshape(npair, 2, x.shape[1], p).transpose(0, 2, 1, 3).reshape(npair, x.shape[1], 2 * p)

    row = lambda x: pair_lanes(x.reshape(g, 1, p))
    blk = lambda *shape: pl.BlockSpec((1,) + shape, lambda i: (i, 0, 0))
    two = lambda *shape: pl.BlockSpec((2,) + shape, lambda i: (i, 0, 0))
    wt16, cc16, wt8, cc8, apow = pl.pallas_call(
        _s5_param_kernel, grid=(npair,),
        in_specs=[blk(1, 2 * p)] * 3 + [blk(c, 2 * p)] * 4,
        out_specs=[two(w16, w16 + 4 * p), blk(4 * p, 2 * w16), two(w8, w8 + 4 * p), blk(4 * p, 2 * w8),
                   blk(4, 2 * p)],
        out_shape=[jax.ShapeDtypeStruct((g, w16, w16 + 4 * p), BF16),
                   jax.ShapeDtypeStruct((npair, 4 * p, 2 * w16), BF16),
                   jax.ShapeDtypeStruct((g, w8, w8 + 4 * p), BF16),
                   jax.ShapeDtypeStruct((npair, 4 * p, 2 * w8), BF16),
                   jax.ShapeDtypeStruct((npair, 4, 2 * p), F32)],
        compiler_params=_params("parallel"),
        name="s5_params",
    )(row(lam_re), row(lam_im), row(jnp.broadcast_to(log_dt[:, None], (g, p))),
      pair_lanes(b_re.transpose(0, 2, 1)), pair_lanes(b_im.transpose(0, 2, 1)),
      pair_lanes(c_re), pair_lanes(c_im))
    d = d_skip.reshape(S5_NBLK, 1, S5_GB * c)

    def transition(r):
        return jnp.concatenate([apow[:, r].reshape(S5_NBLK, 1, S5_CH), apow[:, r + 1].reshape(S5_NBLK, 1, S5_CH)],
                               axis=-1)

    return {S5_TC: (wt16, cc16, transition(0), d), S5_TC // 2: (wt8, cc8, transition(2), d)}


def _unit_transpose(vs):
    unit = lax.broadcasted_iota(jnp.int32, vs[0].shape, 1) >> int(math.log2(S5_GROUP))
    for dist in (4, 2, 1):
        keep = (unit & dist) == 0
        nxt = list(vs)
        for i in range(8):
            if i & dist == 0:
                a, b = vs[i], vs[i + dist]
                nxt[i] = jnp.where(keep, a, pltpu.roll(b, dist * S5_GROUP, 1))
                nxt[i + dist] = jnp.where(keep, pltpu.roll(a, LANES - dist * S5_GROUP, 1), b)
        vs = nxt
    return vs


def _s5_chunk_kernel(z_ref, w_ref, cc_ref, a_ref, d_ref, h0r_ref, h0i_ref,
                     y_ref, hfr_ref, hfi_ref, ut_ref, ug_ref, yg_ref, e_ref, *, nsl, rps, tc, nseg):
    uw = tc * S5_GROUP
    nq = uw // LANES
    rows = nsl * rps
    nk = rps if nsl > 1 else 1
    ns = rows // nk
    npair = S5_GB // 2
    rchunk = min(rows, 64)

    for s in range(nsl):
        for t in range(tc):
            ut_ref[t, s * rps:(s + 1) * rps, :] = z_ref[pl.ds(s * rps * tc + t, rps, stride=tc), :]

    def row_chunk(rc):
        return pl.ds(pl.multiple_of(rc * rchunk, rchunk), rchunk)

    def to_groups(rc, carry):
        for q in range(nq):
            per_group = _unit_transpose([ut_ref[q * 8 + t, row_chunk(rc), :] for t in range(8)])
            for g in range(S5_GB):
                ug_ref[g, row_chunk(rc), q * LANES:(q + 1) * LANES] = per_group[g].astype(BF16)
        return carry

    lax.fori_loop(0, rows // rchunk, to_groups, 0)

    for pair in range(npair):
        ends = None
        for gi in range(2):
            g = 2 * pair + gi
            em = jnp.dot(ug_ref[g], w_ref[g], preferred_element_type=F32)
            yg_ref[g] = em[:, :uw]
            ends = em[:, uw:] if ends is None else ends + em[:, uw:]
        for part, blk in ((ends[:, :LANES], pair), (ends[:, LANES:], npair + pair)):
            if nk == 1:
                e_ref[blk] = part
            else:
                for s in range(nsl):
                    e_ref[blk, pl.ds(s, rps, stride=nsl), :] = part[s * rps:(s + 1) * rps, :]

    a_row = [(a_ref[0, :, i * LANES:(i + 1) * LANES], a_ref[0, :, S5_CH + i * LANES:S5_CH + (i + 1) * LANES])
             for i in range(npair)]
    a_full = [(jnp.broadcast_to(ar, (ns, LANES)), jnp.broadcast_to(ai, (ns, LANES))) for ar, ai in a_row]

    def slab(t):
        return pl.ds(pl.multiple_of(t * ns, ns), ns)

    def run_scan(h, store):
        def step(t, carry):
            out = []
            for i in range(npair):
                hr, hi = carry[i]
                ar, ai = a_full[i]
                er = e_ref[i, slab(t), :]
                ei = e_ref[npair + i, slab(t), :]
                if store:
                    e_ref[i, slab(t), :] = hr
                    e_ref[npair + i, slab(t), :] = hi
                out.append((ar * hr - ai * hi + er, ar * hi + ai * hr + ei))
            return tuple(out)

        if nk == 1:
            return step(0, h)
        return lax.fori_loop(0, nk, step, h, unroll=4)

    h0 = tuple((h0r_ref[:, i * LANES:(i + 1) * LANES], h0i_ref[:, i * LANES:(i + 1) * LANES]) for i in range(npair))
    if nseg == 1:
        final = run_scan(h0, store=True)
    else:
        zero = jnp.zeros((ns, LANES), F32)
        seg_end = run_scan(tuple((zero, zero) for _ in range(npair)), store=False)
        second = (lax.broadcasted_iota(jnp.int32, (ns, LANES), 0) & 1) == 1
        init, final = [], []
        for i in range(npair):
            p_re, p_im = a_row[i]
            for _ in range(int(math.log2(nk))):
                p_re, p_im = p_re * p_re - p_im * p_im, 2.0 * (p_re * p_im)
            (hr, hi), (er, ei) = h0[i], seg_end[i]
            i_re = jnp.where(second, p_re * hr - p_im * hi + pltpu.roll(er, 1, 0), hr)
            i_im = jnp.where(second, p_re * hi + p_im * hr + pltpu.roll(ei, 1, 0), hi)
            init.append((i_re, i_im))
            final.append((p_re * i_re - p_im * i_im + er, p_re * i_im + p_im * i_re + ei))
        run_scan(tuple(init), store=True)
    for i in range(npair):
        hfr_ref[:, i * LANES:(i + 1) * LANES] = final[i][0]
        hfi_ref[:, i * LANES:(i + 1) * LANES] = final[i][1]

    def chunk_rows(blk):
        if nk == 1:
            return e_ref[blk]
        return jnp.concatenate([e_ref[blk, pl.ds(s, rps, stride=nsl), :] for s in range(nsl)], axis=0)

    for pair in range(npair):
        h_in = jnp.concatenate([chunk_rows(pair), chunk_rows(npair + pair)], axis=1).astype(BF16)
        carried = jnp.dot(h_in, cc_ref[pair], preferred_element_type=F32)
        for gi in range(2):
            yg_ref[2 * pair + gi] += carried[:, gi * uw:(gi + 1) * uw]

    d_row = d_ref[0]

    def to_tokens(rc, carry):
        for q in range(nq):
            per_tau = _unit_transpose([yg_ref[g, row_chunk(rc), q * LANES:(q + 1) * LANES] for g in range(S5_GB)])
            for t in range(8):
                y = per_tau[t] + d_row * ut_ref[q * 8 + t, row_chunk(rc), :]
                ut_ref[q * 8 + t, row_chunk(rc), :] = jax.nn.gelu(y, approximate=True)
        return carry

    lax.fori_loop(0, rows // rchunk, to_tokens, 0)
    for s in range(nsl):
        for t in range(tc):
            y_ref[pl.ds(s * rps * tc + t, rps, stride=tc), :] = ut_ref[t, s * rps:(s + 1) * rps, :]


def _s5_branch(z, weights, h0_re, h0_im, *, batch, seq):
    tokens = batch * seq
    if seq > S5_TC:
        tc, nseg = S5_TC, 2
        nsl, rps = batch * nseg, seq // (nseg * tc)
        assert rps & (rps - 1) == 0
        h0_re, h0_im = jnp.repeat(h0_re, nseg, axis=0), jnp.repeat(h0_im, nseg, axis=0)
    else:
        tc, nseg, nsl, rps = seq, 1, 1, batch
    w, cc, a, d = weights[tc]
    uw = tc * S5_GROUP
    rows = nsl * rps
    ns = h0_re.shape[0]
    kern = functools.partial(_s5_chunk_kernel, nsl=nsl, rps=rps, tc=tc, nseg=nseg)
    state_spec = pl.BlockSpec((ns, S5_CH), lambda k: (0, k))
    state_shape = jax.ShapeDtypeStruct((ns, S5_GROUPS * S5_STATE), F32)
    y, hf_re, hf_im = pl.pallas_call(
        kern, grid=(S5_NBLK,),
        in_specs=[pl.BlockSpec((tokens, LANES), lambda k: (0, COL_U // LANES + k)),
                  pl.BlockSpec((S5_GB, uw, uw + 4 * S5_STATE), lambda k: (k, 0, 0)),
                  pl.BlockSpec((S5_GB // 2, 4 * S5_STATE, 2 * uw), lambda k: (k, 0, 0)),
                  pl.BlockSpec((1, 1, 2 * S5_CH), lambda k: (k, 0, 0)),
                  pl.BlockSpec((1, 1, LANES), lambda k: (k, 0, 0)),
                  state_spec, state_spec],
        out_specs=[pl.BlockSpec((tokens, LANES), lambda k: (0, k)), state_spec, state_spec],
        out_shape=[jax.ShapeDtypeStruct((tokens, S5_WIDTH), F32), state_shape, state_shape],
        scratch_shapes=[pltpu.VMEM((tc, rows, LANES), F32), pltpu.VMEM((S5_GB, rows, uw), BF16),
                        pltpu.VMEM((S5_GB, rows, uw), F32), pltpu.VMEM((S5_GB, rows, LANES), F32)],
        compiler_params=_params("parallel"),
        name="s5_chunked",
    )(z, w, cc, a, d, h0_re, h0_im)
    if nseg == 2:
        hf_re, hf_im = hf_re[1::2], hf_im[1::2]
    return y, hf_re, hf_im


def _gla_kernel(q_ref, k_ref, v_ref, r_ref, al_ref, wa_ref, ba_ref, gn_ref, s0_ref,
                y_ref, sf_ref, s_scr, *, ntile, groups, chained, width):
    rt = GLA_TILE
    c = rt // groups
    shift = int(math.log2(c))
    row_g = lax.broadcasted_iota(jnp.int32, (rt, rt), 0)
    col_g = lax.broadcasted_iota(jnp.int32, (rt, rt), 1)
    same = (row_g >> shift) == (col_g >> shift)
    causal = same & (row_g >= col_g)
    cum_w = jnp.concatenate([causal.astype(BF16), same.astype(BF16)], axis=0)
    sel_r = lax.broadcasted_iota(jnp.int32, (rt, groups * LANES), 0) >> shift
    sel_c = lax.broadcasted_iota(jnp.int32, (rt, groups * LANES), 1) >> int(math.log2(LANES))
    sel = (sel_r == sel_c).astype(BF16)
    tn_dims = (((0,), (0,)), ((), ()))
    nt_dims = (((1,), (1,)), ((), ()))

    def split3(x):
        hi = x.astype(BF16)
        r1 = x - hi.astype(F32)
        mid = r1.astype(BF16)
        lo = (r1 - mid.astype(F32)).astype(BF16)
        return jnp.concatenate([hi, mid, lo], axis=1)

    if chained:
        s_scr[...] = s0_ref[0, 0]

    def body(it, carry):
        tiles = [it * width + u for u in range(width)]
        rows = [pl.ds(pl.multiple_of(t * rt, rt), rt) for t in tiles]
        log_a = []
        for u in range(width):
            x = jnp.dot(al_ref[rows[u], :].astype(BF16), wa_ref[...], preferred_element_type=F32) + ba_ref[...]
            log_a.append((jnp.minimum(x, 0.0) - jnp.log1p(jnp.exp(-jnp.abs(x)))) * (1.0 / GLA_TAU))
        la3 = [split3(la) for la in log_a]
        cums, e_col = [], []
        for u in range(width):
            cs = jnp.dot(cum_w, la3[u], preferred_element_type=F32)
            cums.append(cs[:, :LANES] + cs[:, LANES:2 * LANES] + cs[:, 2 * LANES:])
            tot = lax.dot_general(la3[u], sel, tn_dims, preferred_element_type=F32)
            e_col.append(jnp.exp(tot[:LANES] + tot[LANES:2 * LANES] + tot[2 * LANES:]))
        qb, kd, v, att, upd = [], [], [], [], []
        for u in range(width):
            b = cums[u][:rt]
            b_end = cums[u][rt:]
            k = k_ref[rows[u], :]
            v.append(v_ref[rows[u], :])
            q = q_ref[rows[u], :] * (GLA_DK ** -0.5)
            qb.append(q * jnp.exp(b))
            kd.append(k * jnp.exp(b_end - b))
            mid = 0.5 * b_end
            a = lax.dot_general((q * jnp.exp(b - mid)).astype(BF16), (k * jnp.exp(mid - b)).astype(BF16), nt_dims,
                                preferred_element_type=F32)
            att.append(jnp.where(causal, a, 0.0).astype(BF16))
            upd.append([lax.dot_general(kd[u][g * c:(g + 1) * c].astype(BF16), v[u][g * c:(g + 1) * c].astype(BF16),
                                        tn_dims, preferred_element_type=F32) for g in range(groups)])
        o = [jnp.dot(att[u], v[u].astype(BF16), preferred_element_type=F32) for u in range(width)]
        for u in range(width):
            o_state = []
            for g in range(groups):
                s = s_scr[...] if chained else s0_ref[tiles[u] * groups + g, 0]
                o_state.append(jnp.dot(qb[u][g * c:(g + 1) * c].astype(BF16), s.astype(BF16),
                                       preferred_element_type=F32))
                e = e_col[u][:, g * LANES:(g + 1) * LANES]
                s_new = jnp.concatenate([s[:, :LANES] * e, s[:, LANES:] * e], axis=1) + upd[u][g]
                if chained:
                    s_scr[...] = s_new
                else:
                    sf_ref[tiles[u] * groups + g, 0] = s_new
            o[u] = o[u] + (o_state[0] if groups == 1 else jnp.concatenate(o_state, axis=0))
        for u in range(width):
            y = o[u] * lax.rsqrt(jnp.mean(o[u] * o[u], axis=-1, keepdims=True) + RMS_EPS)
            y = y * gn_ref[...]
            r = r_ref[rows[u], :]
            y_ref[rows[u], :] = (y * (r * jax.nn.sigmoid(r))).astype(y_ref.dtype)
        return carry

    lax.fori_loop(0, ntile // width, body, 0)
    if chained:
        sf_ref[0, 0] = s_scr[...]


def _gla(z, alow, w_a2, b_a, g_norm, s0, *, batch, seq, chained):
    chunk = math.gcd(seq, GLA_CHUNK)
    if chained:
        assert chunk == GLA_TILE
        nb, groups, width = 1, 1, 8
    else:
        assert GLA_TILE % seq == 0 and chunk == seq
        nb, groups, width = 32, GLA_TILE // seq, 2
    rows = nb * seq
    kern = functools.partial(_gla_kernel, ntile=rows // GLA_TILE, groups=groups, chained=chained, width=width)
    state_spec = pl.BlockSpec((nb, 1, GLA_DK, GLA_DV), lambda i, h: (i, h, 0, 0))
    return pl.pallas_call(
        kern, grid=(batch // nb, GLA_HEADS),
        in_specs=[pl.BlockSpec((rows, GLA_DK), lambda i, h: (i, COL_Q // GLA_DK + h)),
                  pl.BlockSpec((rows, GLA_DK), lambda i, h: (i, COL_K // GLA_DK + h)),
                  pl.BlockSpec((rows, GLA_DV), lambda i, h: (i, COL_V // GLA_DV + h)),
                  pl.BlockSpec((rows, GLA_DV), lambda i, h: (i, COL_R // GLA_DV + h)),
                  pl.BlockSpec((rows, LANES), lambda i, h: (i, 0)),
                  pl.BlockSpec((LANES, GLA_DK), lambda i, h: (0, h)),
                  pl.BlockSpec((1, GLA_DK), lambda i, h: (0, h)),
                  pl.BlockSpec((1, GLA_DV), lambda i, h: (0, h)),
                  state_spec],
        out_specs=[pl.BlockSpec((rows, GLA_DV), lambda i, h: (i, h)), state_spec],
        out_shape=[jax.ShapeDtypeStruct((batch * seq, GLA_HEADS * GLA_DV), BF16),
                   jax.ShapeDtypeStruct((batch, GLA_HEADS, GLA_DK, GLA_DV), F32)],
        scratch_shapes=[pltpu.VMEM((GLA_DK, GLA_DV), F32)],
        compiler_params=_params("parallel", "parallel"),
        name="gla",
    )(z, z, z, z, alow, w_a2, b_a, g_norm, s0)


def _attn_kernel(q_ref, k_ref, v_ref, o_ref):
    nt_dims = (((1,), (1,)), ((), ()))
    cols = [slice(h * XA_HEAD_DIM, (h + 1) * XA_HEAD_DIM) for h in range(XA_HEADS)]
    scores = [lax.dot_general(q_ref[:, c].astype(BF16), k_ref[0, :, c].astype(BF16), nt_dims,
                              preferred_element_type=F32) * (XA_HEAD_DIM ** -0.5) for c in cols]
    probs = []
    for s in scores:
        p = jnp.exp(s - jnp.max(s, axis=-1, keepdims=True))
        probs.append((p / jnp.sum(p, axis=-1, keepdims=True)).astype(BF16))
    for c, p in zip(cols, probs):
        o_ref[:, c] = jnp.dot(p, v_ref[0, :, c].astype(BF16), preferred_element_type=F32).astype(o_ref.dtype)


def _attn_prompt(z, mem_k, mem_v, *, batch, seq, rc):
    nrc = seq // rc
    kv_spec = pl.BlockSpec((1, MEM_LEN, XA_WIDTH), lambda b, c: (b, 0, 0))
    return pl.pallas_call(
        _attn_kernel, grid=(batch, nrc),
        in_specs=[pl.BlockSpec((rc, XA_WIDTH), lambda b, c: (b * nrc + c, COL_QX // XA_WIDTH)), kv_spec, kv_spec],
        out_specs=pl.BlockSpec((rc, XA_WIDTH), lambda b, c: (b * nrc + c, 0)),
        out_shape=jax.ShapeDtypeStruct((batch * seq, XA_WIDTH), BF16),
        compiler_params=_params("parallel", "parallel"),
        name="attn_prompt",
    )(z, mem_k, mem_v)


XA_HALF = XA_HEAD_DIM // 2
XA_ROWS = 2 * XA_HEADS


def _attn_cache_kernel(q_ref, k_ref, v_ref, o_ref, *, nb, seq, width):
    nt_dims = (((1,), (1,)), ((), ()))

    def half_rows(ref, bi, h, half):
        return ref[bi, pl.ds(half * XA_HEADS + h, MEM_LEN, stride=XA_ROWS), :].astype(BF16)

    def body(it, carry):
        work = [(it * width + u, h) for u in range(width) for h in range(XA_HEADS)]
        scores = []
        for bi, h in work:
            rows = pl.ds(pl.multiple_of(bi * seq, SUBLANES), seq)
            lo = h * XA_HEAD_DIM
            s = lax.dot_general(q_ref[rows, lo:lo + XA_HALF].astype(BF16), half_rows(k_ref, bi, h, 0),
                                nt_dims, preferred_element_type=F32)
            s = s + lax.dot_general(q_ref[rows, lo + XA_HALF:lo + XA_HEAD_DIM].astype(BF16),
                                    half_rows(k_ref, bi, h, 1), nt_dims, preferred_element_type=F32)
            scores.append(s * (XA_HEAD_DIM ** -0.5))
        probs = []
        for s in scores:
            p = jnp.exp(s - jnp.max(s, axis=-1, keepdims=True))
            probs.append((p / jnp.sum(p, axis=-1, keepdims=True)).astype(BF16))
        rows = pl.ds(pl.multiple_of(it * width * seq, width * seq), width * seq)
        for h in range(XA_HEADS):
            for half in range(2):
                lo = h * XA_HEAD_DIM + half * XA_HALF
                out = [jnp.dot(probs[u * XA_HEADS + h], half_rows(v_ref, it * width + u, h, half),
                               preferred_element_type=F32) for u in range(width)]
                o_ref[rows, lo:lo + XA_HALF] = jnp.concatenate(out, axis=0).astype(o_ref.dtype)
        return carry

    lax.fori_loop(0, nb // width, body, 0)


def _cache_rows(cache):
    bs = cache.shape[0]
    c = cache.reshape(bs, MEM_LEN, XA_HEADS, 2, XA_HALF).transpose(0, 1, 3, 2, 4)
    return c.reshape(bs, MEM_LEN * XA_ROWS, XA_HALF)


def _attn_sample(z, mem_k, mem_v, *, batch, seq, nb):
    rows = nb * seq
    kern = functools.partial(_attn_cache_kernel, nb=nb, seq=seq, width=2)
    kv_spec = pl.BlockSpec((nb, MEM_LEN * XA_ROWS, XA_HALF), lambda i: (i, 0, 0))
    return pl.pallas_call(
        kern, grid=(batch // nb,),
        in_specs=[pl.BlockSpec((rows, XA_WIDTH), lambda i: (i, COL_QX // XA_WIDTH)), kv_spec, kv_spec],
        out_specs=pl.BlockSpec((rows, XA_WIDTH), lambda i: (i, 0)),
        out_shape=jax.ShapeDtypeStruct((batch * seq, XA_WIDTH), BF16),
        compiler_params=_params("parallel"),
        name="attn_sample",
    )(z, mem_k, mem_v)


def _merge_kernel(ys_ref, yg_ref, yx_ref, g0_ref, g1_ref, g2_ref, wglu_ref, bglu_ref,
                  w0_ref, w1_ref, w2_ref, o_ref, s5_scr):
    @pl.when(pl.program_id(1) == 0)
    def _():
        y = ys_ref[...]
        lin = jnp.dot(y.astype(BF16), wglu_ref[...], preferred_element_type=F32) + bglu_ref[...]
        s5_scr[...] = (y * jax.nn.sigmoid(lin)).astype(BF16)

    j = pl.program_id(1)
    m = jax.nn.sigmoid(g0_ref[...]) * jnp.dot(s5_scr[...], w0_ref[j], preferred_element_type=F32)
    m = m + jax.nn.sigmoid(g1_ref[...]) * jnp.dot(yg_ref[...], w1_ref[j], preferred_element_type=F32)
    m = m + jax.nn.sigmoid(g2_ref[...]) * jnp.dot(yx_ref[...], w2_ref[j], preferred_element_type=F32)
    o_ref[...] = m.astype(o_ref.dtype)


def _merge(y_s5, y_gla, y_x, z, w_glu, b_glu, w_br_s5, w_br_gla, w_br_x, *, tm, tn):
    t = y_s5.shape[0]
    nj = D_MODEL // tn
    once = pl.Buffered(1)
    wide = pl.BlockSpec((tm, S5_WIDTH), lambda i, j: (i, 0))
    gate = lambda b: pl.BlockSpec((tm, tn), lambda i, j: (i, (COL_GATE + b * D_MODEL) // tn + j))
    w_br = pl.BlockSpec((nj, S5_WIDTH, tn), lambda i, j: (0, 0, 0), pipeline_mode=once)
    return pl.pallas_call(
        _merge_kernel, grid=(t // tm, nj),
        in_specs=[wide, wide, wide, gate(0), gate(1), gate(2),
                  pl.BlockSpec((S5_WIDTH, S5_WIDTH), lambda i, j: (0, 0), pipeline_mode=once),
                  pl.BlockSpec((1, S5_WIDTH), lambda i, j: (0, 0)),
                  w_br, w_br, w_br],
        out_specs=pl.BlockSpec((tm, tn), lambda i, j: (i, j)),
        out_shape=jax.ShapeDtypeStruct((t, D_MODEL), BF16),
        scratch_shapes=[pltpu.VMEM((tm, S5_WIDTH), BF16)],
        compiler_params=_params("parallel", "arbitrary"),
        name="merge",
    )(y_s5, y_gla, y_x, z, z, z, w_glu, b_glu, w_br_s5, w_br_gla, w_br_x)


def _out_ffn_kernel(x_ref, m_ref, wo_ref, gf_ref, wg_ref, wu_ref, wd_ref, gl_ref, o_ref, h_scr, acc_scr):
    k = pl.program_id(1)

    @pl.when(k == 0)
    def _():
        acc_scr[...] = x_ref[...] + jnp.dot(m_ref[...], wo_ref[...], preferred_element_type=F32)
        h_scr[...] = _rms(acc_scr[...], gf_ref[...]).astype(BF16)

    h = h_scr[...]
    gate = jnp.dot(h, wg_ref[...], preferred_element_type=F32)
    up = jnp.dot(h, wu_ref[...], preferred_element_type=F32)
    act = (gate * jax.nn.sigmoid(gate) * up).astype(BF16)
    acc_scr[...] += jnp.dot(act, wd_ref[...], preferred_element_type=F32)

    @pl.when(k == pl.num_programs(1) - 1)
    def _():
        o_ref[...] = _rms(acc_scr[...], gl_ref[...])


def _out_ffn(x, merged, w_out, g_ffn, w_gate, w_up, w_down, g_final, *, tm, th):
    t = x.shape[0]
    row = pl.BlockSpec((tm, D_MODEL), lambda i, k: (i, 0))
    vec = pl.BlockSpec((1, D_MODEL), lambda i, k: (0, 0))
    w_in = pl.BlockSpec((D_MODEL, th), lambda i, k: (0, k))
    return pl.pallas_call(
        _out_ffn_kernel, grid=(t // tm, FFN_HIDDEN // th),
        in_specs=[row, row,
                  pl.BlockSpec((D_MODEL, D_MODEL), lambda i, k: (0, 0), pipeline_mode=pl.Buffered(1)),
                  vec, w_in, w_in, pl.BlockSpec((th, D_MODEL), lambda i, k: (k, 0)), vec],
        out_specs=row,
        out_shape=jax.ShapeDtypeStruct((t, D_MODEL), F32),
        scratch_shapes=[pltpu.VMEM((tm, D_MODEL), BF16), pltpu.VMEM((tm, D_MODEL), F32)],
        compiler_params=pltpu.CompilerParams(dimension_semantics=("parallel", "arbitrary"),
                                             vmem_limit_bytes=OUT_FFN_VMEM),
        name="out_ffn",
    )(x, merged, w_out, g_ffn, w_gate, w_up, w_down, g_final)


def _reorder_in_proj_kernel(wt_ref, wa_ref, main_ref, alow_ref):
    main_ref[...] = wt_ref[...].T.astype(BF16)

    @pl.when(pl.program_id(0) == 0)
    def _():
        lane = lax.broadcasted_iota(jnp.int32, alow_ref.shape, 1)
        alow_ref[...] = jnp.where(lane < GLA_RANK, wa_ref[...].T, 0.0).astype(BF16)


def _reorder_in_proj(w_in):
    d, n = w_in.shape
    wt = w_in.T
    tc = 512
    per_tile = IN_TN // tc
    n_first, n_gate = COL_GATE // tc, (COL_QX - COL_GATE) // tc
    gate_start, qx_start = 4096 + GLA_RANK + XA_WIDTH, 4096 + GLA_RANK

    def src_row(j):
        t8, g8, q8 = tc // SUBLANES, gate_start // SUBLANES, qx_start // SUBLANES
        r8 = jnp.where(j < n_first, j * t8,
                       jnp.where(j < n_first + n_gate, g8 + (j - n_first) * t8,
                                 q8 + (j - n_first - n_gate) * t8))
        return r8 * SUBLANES

    return pl.pallas_call(
        _reorder_in_proj_kernel, grid=(Z_WIDTH // tc,),
        in_specs=[pl.BlockSpec((pl.Element(tc), pl.Element(d)), lambda j: (src_row(j), 0)),
                  pl.BlockSpec((pl.Element(LANES), pl.Element(d)), lambda j: (COL_GATE, 0))],
        out_specs=[pl.BlockSpec((None, d, tc), lambda j: (j // per_tile, 0, j % per_tile)),
                   pl.BlockSpec((d, LANES), lambda j: (0, 0))],
        out_shape=[jax.ShapeDtypeStruct((Z_WIDTH // IN_TN, d, IN_TN), BF16),
                   jax.ShapeDtypeStruct((d, LANES), BF16)],
        compiler_params=_params("arbitrary"),
        name="reorder_in_proj",
    )(wt, wt)


def _col_tiles(w, tn):
    k, n = w.shape
    return w.reshape(k, n // tn, tn).transpose(1, 0, 2)


def _layer(x, mem_k, mem_v, s5_re0, s5_im0, gla_s0, w, *, batch, seq, segmented):
    z, alow = _norm_matmul(x, w['norm_mix'], w['w_main'], w['w_alow'], tm=1024)
    y_s5, hf_re, hf_im = _s5_branch(z, w['s5'], s5_re0, s5_im0, batch=batch, seq=seq)
    y_gla, gla_s = _gla(z, alow, w['gla_w_a2'], w['gla_b_a'], w['gla_norm'], gla_s0,
                        batch=batch, seq=seq, chained=segmented)
    if segmented:
        y_x = _attn_prompt(z, mem_k, mem_v, batch=batch, seq=seq, rc=512)
    else:
        y_x = _attn_sample(z, mem_k, mem_v, batch=batch, seq=seq, nb=8)
    merged = _merge(y_s5, y_gla, y_x, z, w['s5_w_glu'], w['s5_b_glu'],
                    w['w_br_s5'], w['w_br_gla'], w['w_br_xattn'], tm=512, tn=MERGE_TN)
    y = _out_ffn(x, merged, w['w_out'], w['norm_ffn'], w['w_ffn_gate'], w['w_ffn_up'], w['w_ffn_down'],
                 w['norm_final'], tm=512, th=FFN_TH)
    return y, hf_re, hf_im, gla_s


def kernel(x_prompt, x_sample, mem_prompt, state_s5_re, state_s5_im, state_gla, cache_mem_k, cache_mem_v,
           norm_mix, w_in, s5_lam_re, s5_lam_im, s5_log_dt, s5_b_re, s5_b_im, s5_c_re, s5_c_im,
           s5_d, s5_w_glu, s5_b_glu, gla_w_a2, gla_b_a, gla_norm, mem_norm, w_mem_k, w_mem_v,
           w_br_s5, w_br_gla, w_br_xattn, w_out, norm_ffn, w_ffn_gate, w_ffn_up, w_ffn_down, norm_final):
    depth = w_in.shape[0]
    assert depth == 1
    bp, sp, d = x_prompt.shape
    bs, ss, _ = x_sample.shape
    n_state = S5_GROUPS * S5_STATE
    row = lambda v: v.reshape(1, -1)

    l = 0
    w_main, w_alow = _reorder_in_proj(w_in[l])
    s5_w = _s5_params(s5_lam_re[l], s5_lam_im[l], s5_log_dt[l], s5_b_re[l], s5_b_im[l],
                      s5_c_re[l], s5_c_im[l], s5_d[l])
    w = {
        'norm_mix': row(norm_mix[l]), 'w_main': w_main, 'w_alow': w_alow,
        's5': s5_w,
        's5_w_glu': s5_w_glu[l].astype(BF16), 's5_b_glu': row(s5_b_glu[l]),
        'gla_w_a2': jnp.pad(gla_w_a2[l], ((0, LANES - GLA_RANK), (0, 0))).astype(BF16),
        'gla_b_a': row(gla_b_a[l]), 'gla_norm': row(gla_norm[l]),
        'w_br_s5': _col_tiles(w_br_s5[l].astype(BF16), MERGE_TN),
        'w_br_gla': _col_tiles(w_br_gla[l].astype(BF16), MERGE_TN),
        'w_br_xattn': _col_tiles(w_br_xattn[l].astype(BF16), MERGE_TN),
        'w_out': w_out[l].astype(BF16),
        'norm_ffn': row(norm_ffn[l]), 'w_ffn_gate': w_ffn_gate[l].astype(BF16),
        'w_ffn_up': w_ffn_up[l].astype(BF16), 'w_ffn_down': w_ffn_down[l].astype(BF16),
        'norm_final': row(norm_final),
    }

    w_mem = jnp.stack([w_mem_k[l], w_mem_v[l]]).astype(BF16)
    mem_kv = _norm_matmul(mem_prompt.reshape(bp * MEM_LEN, d), row(mem_norm[l]), w_mem, tm=512)
    mk = mem_kv[:, :XA_WIDTH].reshape(bp, MEM_LEN, XA_WIDTH)
    mv = mem_kv[:, XA_WIDTH:].reshape(bp, MEM_LEN, XA_WIDTH)
    zero_s5 = jnp.zeros((bp, n_state), F32)
    zero_gla = jnp.zeros((bp, GLA_HEADS, GLA_DK, GLA_DV), F32)
    yp, p_re, p_im, p_gla = _layer(x_prompt.reshape(bp * sp, d), mk, mv, zero_s5, zero_s5, zero_gla, w,
                                   batch=bp, seq=sp, segmented=True)

    ys, s_re, s_im, s_gla = _layer(x_sample.reshape(bs * ss, d),
                                   _cache_rows(cache_mem_k[l]), _cache_rows(cache_mem_v[l]),
                                   state_s5_re[l].reshape(bs, n_state), state_s5_im[l].reshape(bs, n_state),
                                   state_gla[l], w, batch=bs, seq=ss, segmented=False)

    s5_shape_p = (1, bp, S5_GROUPS, S5_STATE)
    s5_shape_s = (1, bs, S5_GROUPS, S5_STATE)
    kv_shape = (1, bp, MEM_LEN, XA_HEADS, XA_HEAD_DIM)
    return (yp.reshape(bp, sp, d), ys.reshape(bs, ss, d),
            p_re.reshape(s5_shape_p), p_im.reshape(s5_shape_p), p_gla[None],
            mk.reshape(kv_shape), mv.reshape(kv_shape),
            s_re.reshape(s5_shape_s), s_im.reshape(s5_shape_s), s_gla[None])
```

```python
import functools
import math

import jax
import jax.numpy as jnp
from jax import lax
from jax.experimental import pallas as pl
from jax.experimental.pallas import tpu as pltpu

F32 = jnp.float32
BF16 = jnp.bfloat16

D_MODEL = 2048
S5_WIDTH = 1024
S5_GROUP = 16
S5_GROUPS = 64
S5_STATE = 64
GLA_HEADS = 4
GLA_DK = 128
GLA_DV = 256
GLA_RANK = 16
GLA_TAU = 16.0
GLA_CHUNK = 64
XA_HEADS = 4
XA_HEAD_DIM = 256
XA_WIDTH = 1024
MEM_LEN = 256
FFN_HIDDEN = 5632
RMS_EPS = 1e-6

LANES = 128
SUBLANES = 8
VMEM_LIMIT = 56 * 1024 * 1024
OUT_FFN_VMEM = 52 * 1024 * 1024

COL_U = 0
COL_Q = 1024
COL_K = 1536
COL_V = 2048
COL_R = 3072
COL_GATE = 4096
COL_QX = 10240
Z_WIDTH = 11264

S5_GB = 8
S5_CH = S5_GB * S5_STATE
S5_NBLK = S5_GROUPS // S5_GB
S5_TC = 16
GLA_TILE = 64
MERGE_TN = 1024
IN_TN = 1024
FFN_TH = 512


def _params(*sem):
    return pltpu.CompilerParams(dimension_semantics=sem, vmem_limit_bytes=VMEM_LIMIT)


def _rms(x, g):
    return x * lax.rsqrt(jnp.mean(x * x, axis=-1, keepdims=True) + RMS_EPS) * g


def _norm_matmul_kernel(x_ref, g_ref, w_ref, o_ref, h_ref):
    @pl.when(pl.program_id(1) == 0)
    def _():
        h_ref[...] = _rms(x_ref[...], g_ref[...]).astype(BF16)

    o_ref[...] = jnp.dot(h_ref[...], w_ref[0], preferred_element_type=F32)


def _norm_matmul2_kernel(x_ref, g_ref, w_ref, ws_ref, o_ref, os_ref, h_ref):
    @pl.when(pl.program_id(1) == 0)
    def _():
        h = _rms(x_ref[...], g_ref[...]).astype(BF16)
        h_ref[...] = h
        os_ref[...] = jnp.dot(h, ws_ref[...], preferred_element_type=F32)

    o_ref[...] = jnp.dot(h_ref[...], w_ref[0], preferred_element_type=F32)


def _norm_matmul(x, g, w, w_small=None, *, tm):
    t, d = x.shape
    nj, _, tn = w.shape
    n = nj * tn
    grid = (t // tm, n // tn)
    in_specs = [pl.BlockSpec((tm, d), lambda i, j: (i, 0)),
                pl.BlockSpec((1, d), lambda i, j: (0, 0)),
                pl.BlockSpec((1, d, tn), lambda i, j: (j, 0, 0))]
    out_specs = pl.BlockSpec((tm, tn), lambda i, j: (i, j))
    out_shape = jax.ShapeDtypeStruct((t, n), F32)
    scratch = [pltpu.VMEM((tm, d), BF16)]
    if w_small is None:
        return pl.pallas_call(_norm_matmul_kernel, grid=grid, in_specs=in_specs, out_specs=out_specs,
                              out_shape=out_shape, scratch_shapes=scratch,
                              compiler_params=_params("parallel", "arbitrary"),
                              name="norm_matmul")(x, g, w)
    ns = w_small.shape[1]
    in_specs.append(pl.BlockSpec((d, ns), lambda i, j: (0, 0)))
    return pl.pallas_call(_norm_matmul2_kernel, grid=grid, in_specs=in_specs,
                          out_specs=[out_specs, pl.BlockSpec((tm, ns), lambda i, j: (i, 0))],
                          out_shape=[out_shape, jax.ShapeDtypeStruct((t, ns), F32)],
                          scratch_shapes=scratch,
                          compiler_params=_params("parallel", "arbitrary"),
                          name="in_proj")(x, g, w, w_small)


def _s5_param_kernel(lr_ref, li_ref, ldt_ref, btr_ref, bti_ref, cr_ref, ci_ref,
                     w16_ref, cc16_ref, w8_ref, cc8_ref, ap_ref):
    p = S5_STATE
    half = S5_TC // 2
    lam_re, lam_im = lr_ref[0], li_ref[0]
    dt = jnp.exp(ldt_ref[0])
    mag = jnp.exp(lam_re * dt)
    a_re = mag * jnp.cos(lam_im * dt)
    a_im = mag * jnp.sin(lam_im * dt)
    den = lam_re * lam_re + lam_im * lam_im
    coef_re = ((a_re - 1.0) * lam_re + a_im * lam_im) / den
    coef_im = (a_im * lam_re - (a_re - 1.0) * lam_im) / den
    bt_re, bt_im = btr_ref[0], bti_ref[0]
    bb_re = coef_re * bt_re - coef_im * bt_im
    bb_im = coef_re * bt_im + coef_im * bt_re
    c_re, c_im = cr_ref[0], ci_ref[0]

    pw_re, pw_im = jnp.ones_like(a_re), jnp.zeros_like(a_re)
    ca_re, ca_im, ab_re, ab_im, powers = [], [], [], [], {}
    for j in range(S5_TC + 1):
        powers[j] = (pw_re, pw_im)
        ca_re.append(c_re * pw_re - c_im * pw_im)
        ca_im.append(c_re * pw_im + c_im * pw_re)
        ab_re.append(bb_re * pw_re - bb_im * pw_im)
        ab_im.append(bb_re * pw_im + bb_im * pw_re)
        pw_re, pw_im = pw_re * a_re - pw_im * a_im, pw_re * a_im + pw_im * a_re
    lag_re = jnp.concatenate(ca_re[:S5_TC], axis=0)
    lag_im = jnp.concatenate(ca_im[:S5_TC], axis=0)
    end_re = jnp.concatenate([ab_re[S5_TC - 1 - s] for s in range(S5_TC)], axis=0)
    end_im = jnp.concatenate([ab_im[S5_TC - 1 - s] for s in range(S5_TC)], axis=0)
    car_re = jnp.concatenate(ca_re[1:], axis=0).T
    car_im = -jnp.concatenate(ca_im[1:], axis=0).T

    def pair_blocks(x):
        first = lax.broadcasted_iota(jnp.int32, x.shape, 0) < p
        return jnp.concatenate([jnp.where(first, x, 0.0), jnp.where(first, 0.0, x)], axis=1)

    cc16_ref[0] = jnp.concatenate([pair_blocks(car_re), pair_blocks(car_im)], axis=0).astype(BF16)
    n8 = half * S5_GROUP
    cc8_ref[0] = jnp.concatenate([pair_blocks(car_re[:, :n8]), pair_blocks(car_im[:, :n8])], axis=0).astype(BF16)

    nt_dims = (((1,), (1,)), ((), ()))
    hi = lax.Precision.HIGHEST
    lane = lax.broadcasted_iota(jnp.int32, (S5_GROUP, LANES), 1)
    end_lane = lax.broadcasted_iota(jnp.int32, end_re.shape, 1)
    for gi in range(2):
        mine = (lane < p) if gi == 0 else (lane >= p)
        own_re, own_im = jnp.where(mine, bb_re, 0.0), jnp.where(mine, bb_im, 0.0)
        strip = (lax.dot_general(own_re, lag_re, nt_dims, precision=hi, preferred_element_type=F32)
                 - lax.dot_general(own_im, lag_im, nt_dims, precision=hi, preferred_element_type=F32))
        lo, up = strip[:, :LANES], strip[:, LANES:]
        blocks = [strip]
        for s in range(1, S5_TC):
            sh = (s % half) * S5_GROUP
            lo_r = pltpu.roll(lo, sh, 1) if sh else lo
            up_r = pltpu.roll(up, sh, 1) if sh else up
            if s < half:
                blocks.append(jnp.concatenate([jnp.where(lane >= sh, lo_r, 0.0),
                                               jnp.where(lane >= sh, up_r, lo_r)], axis=1))
            else:
                blocks.append(jnp.concatenate([jnp.zeros_like(lo), jnp.where(lane >= sh, lo_r, 0.0)], axis=1))
        toeplitz = jnp.concatenate(blocks, axis=0)
        own_end = (end_lane < p) if gi == 0 else (end_lane >= p)
        ends = jnp.concatenate([jnp.where(own_end, end_re, 0.0), jnp.where(own_end, end_im, 0.0)], axis=1)
        w16_ref[gi] = jnp.concatenate([toeplitz, ends], axis=1).astype(BF16)
        w8_ref[gi] = jnp.concatenate([toeplitz[:n8, :n8], ends[n8:, :]], axis=1).astype(BF16)
    ap_ref[0] = jnp.concatenate([powers[S5_TC][0], powers[S5_TC][1], powers[half][0], powers[half][1]], axis=0)


def _s5_params(lam_re, lam_im, log_dt, b_re, b_im, c_re, c_im, d_skip):
    g, p, c = S5_GROUPS, S5_STATE, S5_GROUP
    npair = g // 2
    w16, w8 = S5_TC * c, S5_TC // 2 * c

    def pair_lanes(x):
        return x.reshape(npair, 2, x.shape[1], p).transpose(0, 2, 1, 3).reshape(npair, x.shape[1], 2 * p)

    row = lambda x: pair_lanes(x.reshape(g, 1, p))
    blk = lambda *shape: pl.BlockSpec((1,) + shape, lambda i: (i, 0, 0))
    two = lambda *shape: pl.BlockSpec((2,) + shape, lambda i: (i, 0, 0))
    wt16, cc16, wt8, cc8, apow = pl.pallas_call(
        _s5_param_kernel, grid=(npair,),
        in_specs=[blk(1, 2 * p)] * 3 + [blk(c, 2 * p)] * 4,
        out_specs=[two(w16, w16 + 4 * p), blk(4 * p, 2 * w16), two(w8, w8 + 4 * p), blk(4 * p, 2 * w8),
                   blk(4, 2 * p)],
        out_shape=[jax.ShapeDtypeStruct((g, w16, w16 + 4 * p), BF16),
                   jax.ShapeDtypeStruct((npair, 4 * p, 2 * w16), BF16),
                   jax.ShapeDtypeStruct((g, w8, w8 + 4 * p), BF16),
                   jax.ShapeDtypeStruct((npair, 4 * p, 2 * w8), BF16),
                   jax.ShapeDtypeStruct((npair, 4, 2 * p), F32)],
        compiler_params=_params("parallel"),
        name="s5_params",
    )(row(lam_re), row(lam_im), row(jnp.broadcast_to(log_dt[:, None], (g, p))),
      pair_lanes(b_re.transpose(0, 2, 1)), pair_lanes(b_im.transpose(0, 2, 1)),
      pair_lanes(c_re), pair_lanes(c_im))
    d = d_skip.reshape(S5_NBLK, 1, S5_GB * c)

    def transition(r):
        return jnp.concatenate([apow[:, r].reshape(S5_NBLK, 1, S5_CH), apow[:, r + 1].reshape(S5_NBLK, 1, S5_CH)],
                               axis=-1)

    return {S5_TC: (wt16, cc16, transition(0), d), S5_TC // 2: (wt8, cc8, transition(2), d)}


def _unit_transpose(vs):
    unit = lax.broadcasted_iota(jnp.int32, vs[0].shape, 1) >> int(math.log2(S5_GROUP))
    for dist in (4, 2, 1):
        keep = (unit & dist) == 0
        nxt = list(vs)
        for i in range(8):
            if i & dist == 0:
                a, b = vs[i], vs[i + dist]
                nxt[i] = jnp.where(keep, a, pltpu.roll(b, dist * S5_GROUP, 1))
                nxt[i + dist] = jnp.where(keep, pltpu.roll(a, LANES - dist * S5_GROUP, 1), b)
        vs = nxt
    return vs


def _s5_chunk_kernel(z_ref, w_ref, cc_ref, a_ref, d_ref, h0r_ref, h0i_ref,
                     y_ref, hfr_ref, hfi_ref, ut_ref, ug_ref, yg_ref, e_ref, *, nsl, rps, tc, nseg):
    uw = tc * S5_GROUP
    nq = uw // LANES
    rows = nsl * rps
    nk = rps if nsl > 1 else 1
    ns = rows // nk
    npair = S5_GB // 2
    rchunk = min(rows, 64)

    for s in range(nsl):
        for t in range(tc):
            ut_ref[t, s * rps:(s + 1) * rps, :] = z_ref[pl.ds(s * rps * tc + t, rps, stride=tc), :]

    def row_chunk(rc):
        return pl.ds(pl.multiple_of(rc * rchunk, rchunk), rchunk)

    def to_groups(rc, carry):
        for q in range(nq):
            per_group = _unit_transpose([ut_ref[q * 8 + t, row_chunk(rc), :] for t in range(8)])
            for g in range(S5_GB):
                ug_ref[g, row_chunk(rc), q * LANES:(q + 1) * LANES] = per_group[g].astype(BF16)
        return carry

    lax.fori_loop(0, rows // rchunk, to_groups, 0)

    for pair in range(npair):
        ends = None
        for gi in range(2):
            g = 2 * pair + gi
            em = jnp.dot(ug_ref[g], w_ref[g], preferred_element_type=F32)
            yg_ref[g] = em[:, :uw]
            ends = em[:, uw:] if ends is None else ends + em[:, uw:]
        for part, blk in ((ends[:, :LANES], pair), (ends[:, LANES:], npair + pair)):
            if nk == 1:
                e_ref[blk] = part
            else:
                for s in range(nsl):
                    e_ref[blk, pl.ds(s, rps, stride=nsl), :] = part[s * rps:(s + 1) * rps, :]

    a_row = [(a_ref[0, :, i * LANES:(i + 1) * LANES], a_ref[0, :, S5_CH + i * LANES:S5_CH + (i + 1) * LANES])
             for i in range(npair)]
    a_full = [(jnp.broadcast_to(ar, (ns, LANES)), jnp.broadcast_to(ai, (ns, LANES))) for ar, ai in a_row]

    def slab(t):
        return pl.ds(pl.multiple_of(t * ns, ns), ns)

    def run_scan(h, store):
        def step(t, carry):
            out = []
            for i in range(npair):
                hr, hi = carry[i]
                ar, ai = a_full[i]
                er = e_ref[i, slab(t), :]
                ei = e_ref[npair + i, slab(t), :]
                if store:
                    e_ref[i, slab(t), :] = hr
                    e_ref[npair + i, slab(t), :] = hi
                out.append((ar * hr - ai * hi + er, ar * hi + ai * hr + ei))
            return tuple(out)

        if nk == 1:
            return step(0, h)
        return lax.fori_loop(0, nk, step, h, unroll=4)

    h0 = tuple((h0r_ref[:, i * LANES:(i + 1) * LANES], h0i_ref[:, i * LANES:(i + 1) * LANES]) for i in range(npair))
    if nseg == 1:
        final = run_scan(h0, store=True)
    else:
        zero = jnp.zeros((ns, LANES), F32)
        seg_end = run_scan(tuple((zero, zero) for _ in range(npair)), store=False)
        second = (lax.broadcasted_iota(jnp.int32, (ns, LANES), 0) & 1) == 1
        init, final = [], []
        for i in range(npair):
            p_re, p_im = a_row[i]
            for _ in range(int(math.log2(nk))):
                p_re, p_im = p_re * p_re - p_im * p_im, 2.0 * (p_re * p_im)
            (hr, hi), (er, ei) = h0[i], seg_end[i]
            i_re = jnp.where(second, p_re * hr - p_im * hi + pltpu.roll(er, 1, 0), hr)
            i_im = jnp.where(second, p_re * hi + p_im * hr + pltpu.roll(ei, 1, 0), hi)
            init.append((i_re, i_im))
            final.append((p_re * i_re - p_im * i_im + er, p_re * i_im + p_im * i_re + ei))
        run_scan(tuple(init), store=True)
    for i in range(npair):
        hfr_ref[:, i * LANES:(i + 1) * LANES] = final[i][0]
        hfi_ref[:, i * LANES:(i + 1) * LANES] = final[i][1]

    def chunk_rows(blk):
        if nk == 1:
            return e_ref[blk]
        return jnp.concatenate([e_ref[blk, pl.ds(s, rps, stride=nsl), :] for s in range(nsl)], axis=0)

    for pair in range(npair):
        h_in = jnp.concatenate([chunk_rows(pair), chunk_rows(npair + pair)], axis=1).astype(BF16)
        carried = jnp.dot(h_in, cc_ref[pair], preferred_element_type=F32)
        for gi in range(2):
            yg_ref[2 * pair + gi] += carried[:, gi * uw:(gi + 1) * uw]

    d_row = d_ref[0]

    def to_tokens(rc, carry):
        for q in range(nq):
            per_tau = _unit_transpose([yg_ref[g, row_chunk(rc), q * LANES:(q + 1) * LANES] for g in range(S5_GB)])
            for t in range(8):
                y = per_tau[t] + d_row * ut_ref[q * 8 + t, row_chunk(rc), :]
                ut_ref[q * 8 + t, row_chunk(rc), :] = jax.nn.gelu(y, approximate=True)
        return carry

    lax.fori_loop(0, rows // rchunk, to_tokens, 0)
    for s in range(nsl):
        for t in range(tc):
            y_ref[pl.ds(s * rps * tc + t, rps, stride=tc), :] = ut_ref[t, s * rps:(s + 1) * rps, :]


def _s5_branch(z, weights, h0_re, h0_im, *, batch, seq):
    tokens = batch * seq
    if seq > S5_TC:
        tc, nseg = S5_TC, 2
        nsl, rps = batch * nseg, seq // (nseg * tc)
        assert rps & (rps - 1) == 0
        h0_re, h0_im = jnp.repeat(h0_re, nseg, axis=0), jnp.repeat(h0_im, nseg, axis=0)
    else:
        tc, nseg, nsl, rps = seq, 1, 1, batch
    w, cc, a, d = weights[tc]
    uw = tc * S5_GROUP
    rows = nsl * rps
    ns = h0_re.shape[0]
    kern = functools.partial(_s5_chunk_kernel, nsl=nsl, rps=rps, tc=tc, nseg=nseg)
    state_spec = pl.BlockSpec((ns, S5_CH), lambda k: (0, k))
    state_shape = jax.ShapeDtypeStruct((ns, S5_GROUPS * S5_STATE), F32)
    y, hf_re, hf_im = pl.pallas_call(
        kern, grid=(S5_NBLK,),
        in_specs=[pl.BlockSpec((tokens, LANES), lambda k: (0, COL_U // LANES + k)),
                  pl.BlockSpec((S5_GB, uw, uw + 4 * S5_STATE), lambda k: (k, 0, 0)),
                  pl.BlockSpec((S5_GB // 2, 4 * S5_STATE, 2 * uw), lambda k: (k, 0, 0)),
                  pl.BlockSpec((1, 1, 2 * S5_CH), lambda k: (k, 0, 0)),
                  pl.BlockSpec((1, 1, LANES), lambda k: (k, 0, 0)),
                  state_spec, state_spec],
        out_specs=[pl.BlockSpec((tokens, LANES), lambda k: (0, k)), state_spec, state_spec],
        out_shape=[jax.ShapeDtypeStruct((tokens, S5_WIDTH), F32), state_shape, state_shape],
        scratch_shapes=[pltpu.VMEM((tc, rows, LANES), F32), pltpu.VMEM((S5_GB, rows, uw), BF16),
                        pltpu.VMEM((S5_GB, rows, uw), F32), pltpu.VMEM((S5_GB, rows, LANES), F32)],
        compiler_params=_params("parallel"),
        name="s5_chunked",
    )(z, w, cc, a, d, h0_re, h0_im)
    if nseg == 2:
        hf_re, hf_im = hf_re[1::2], hf_im[1::2]
    return y, hf_re, hf_im


def _gla_kernel(q_ref, k_ref, v_ref, r_ref, al_ref, wa_ref, ba_ref, gn_ref, s0_ref,
                y_ref, sf_ref, s_scr, *, ntile, groups, chained, width):
    rt = GLA_TILE
    c = rt // groups
    shift = int(math.log2(c))
    row_g = lax.broadcasted_iota(jnp.int32, (rt, rt), 0)
    col_g = lax.broadcasted_iota(jnp.int32, (rt, rt), 1)
    same = (row_g >> shift) == (col_g >> shift)
    causal = same & (row_g >= col_g)
    cum_w = jnp.concatenate([causal.astype(BF16), same.astype(BF16)], axis=0)
    tn_dims = (((0,), (0,)), ((), ()))
    nt_dims = (((1,), (1,)), ((), ()))

    def split3(x):
        hi = x.astype(BF16)
        r1 = x - hi.astype(F32)
        mid = r1.astype(BF16)
        lo = (r1 - mid.astype(F32)).astype(BF16)
        return jnp.concatenate([hi, mid, lo], axis=1)

    if chained:
        s_scr[...] = s0_ref[0, 0]

    def body(it, carry):
        tiles = [it * width + u for u in range(width)]
        rows = [pl.ds(pl.multiple_of(t * rt, rt), rt) for t in tiles]
        log_a = []
        for u in range(width):
            x = jnp.dot(al_ref[rows[u], :].astype(BF16), wa_ref[...], preferred_element_type=F32) + ba_ref[...]
            log_a.append((jnp.minimum(x, 0.0) - jnp.log1p(jnp.exp(-jnp.abs(x)))) * (1.0 / GLA_TAU))
        la3 = [split3(la) for la in log_a]
        cums, e_col = [], []
        for u in range(width):
            cs = jnp.dot(cum_w, la3[u], preferred_element_type=F32)
            cums.append(cs[:, :LANES] + cs[:, LANES:2 * LANES] + cs[:, 2 * LANES:])
            e_col.append(jnp.exp(cums[u].T))
        qb, kd, v, att, upd = [], [], [], [], []
        for u in range(width):
            b = cums[u][:rt]
            b_end = cums[u][rt:]
            k = k_ref[rows[u], :]
            v.append(v_ref[rows[u], :])
            mid = 0.5 * b_end
            q_mid = q_ref[rows[u], :] * (GLA_DK ** -0.5) * jnp.exp(b - mid)
            k_mid = k * jnp.exp(mid - b)
            e_mid = jnp.exp(mid)
            qb.append(q_mid * e_mid)
            kd.append(k_mid * e_mid)
            a = lax.dot_general(q_mid.astype(BF16), k_mid.astype(BF16), nt_dims, preferred_element_type=F32)
            att.append(jnp.where(causal, a, 0.0).astype(BF16))
            upd.append([lax.dot_general(kd[u][g * c:(g + 1) * c].astype(BF16), v[u][g * c:(g + 1) * c].astype(BF16),
                                        tn_dims, preferred_element_type=F32) for g in range(groups)])
        o = [jnp.dot(att[u], v[u].astype(BF16), preferred_element_type=F32) for u in range(width)]
        for u in range(width):
            o_state = []
            for g in range(groups):
                s = s_scr[...] if chained else s0_ref[tiles[u] * groups + g, 0]
                o_state.append(jnp.dot(qb[u][g * c:(g + 1) * c].astype(BF16), s.astype(BF16),
                                       preferred_element_type=F32))
                s_new = s * e_col[u][:, rt + g * c:rt + g * c + 1] + upd[u][g]
                if chained:
                    s_scr[...] = s_new
                else:
                    sf_ref[tiles[u] * groups + g, 0] = s_new
            o[u] = o[u] + (o_state[0] if groups == 1 else jnp.concatenate(o_state, axis=0))
        for u in range(width):
            y = o[u] * lax.rsqrt(jnp.mean(o[u] * o[u], axis=-1, keepdims=True) + RMS_EPS)
            y = y * gn_ref[...]
            r = r_ref[rows[u], :]
            y_ref[rows[u], :] = (y * (r * jax.nn.sigmoid(r))).astype(y_ref.dtype)
        return carry

    lax.fori_loop(0, ntile // width, body, 0)
    if chained:
        sf_ref[0, 0] = s_scr[...]


def _gla(z, alow, w_a2, b_a, g_norm, s0, *, batch, seq, chained):
    chunk = math.gcd(seq, GLA_CHUNK)
    if chained:
        assert chunk == GLA_TILE
        nb, groups, width = 1, 1, 8
    else:
        assert GLA_TILE % seq == 0 and chunk == seq
        nb, groups, width = 32, GLA_TILE // seq, 2
    rows = nb * seq
    kern = functools.partial(_gla_kernel, ntile=rows // GLA_TILE, groups=groups, chained=chained, width=width)
    state_spec = pl.BlockSpec((nb, 1, GLA_DK, GLA_DV), lambda i, h: (i, h, 0, 0))
    return pl.pallas_call(
        kern, grid=(batch // nb, GLA_HEADS),
        in_specs=[pl.BlockSpec((rows, GLA_DK), lambda i, h: (i, COL_Q // GLA_DK + h)),
                  pl.BlockSpec((rows, GLA_DK), lambda i, h: (i, COL_K // GLA_DK + h)),
                  pl.BlockSpec((rows, GLA_DV), lambda i, h: (i, COL_V // GLA_DV + h)),
                  pl.BlockSpec((rows, GLA_DV), lambda i, h: (i, COL_R // GLA_DV + h)),
                  pl.BlockSpec((rows, LANES), lambda i, h: (i, 0)),
                  pl.BlockSpec((LANES, GLA_DK), lambda i, h: (0, h)),
                  pl.BlockSpec((1, GLA_DK), lambda i, h: (0, h)),
                  pl.BlockSpec((1, GLA_DV), lambda i, h: (0, h)),
                  state_spec],
        out_specs=[pl.BlockSpec((rows, GLA_DV), lambda i, h: (i, h)), state_spec],
        out_shape=[jax.ShapeDtypeStruct((batch * seq, GLA_HEADS * GLA_DV), BF16),
                   jax.ShapeDtypeStruct((batch, GLA_HEADS, GLA_DK, GLA_DV), F32)],
        scratch_shapes=[pltpu.VMEM((GLA_DK, GLA_DV), F32)],
        compiler_params=_params("parallel", "parallel"),
        name="gla",
    )(z, z, z, z, alow, w_a2, b_a, g_norm, s0)


def _attn_kernel(q_ref, k_ref, v_ref, o_ref):
    nt_dims = (((1,), (1,)), ((), ()))
    cols = [slice(h * XA_HEAD_DIM, (h + 1) * XA_HEAD_DIM) for h in range(XA_HEADS)]
    scores = [lax.dot_general(q_ref[:, c].astype(BF16), k_ref[0, :, c].astype(BF16), nt_dims,
                              preferred_element_type=F32) * (XA_HEAD_DIM ** -0.5) for c in cols]
    probs = []
    for s in scores:
        p = jnp.exp(s - jnp.max(s, axis=-1, keepdims=True))
        probs.append((p / jnp.sum(p, axis=-1, keepdims=True)).astype(BF16))
    for c, p in zip(cols, probs):
        o_ref[:, c] = jnp.dot(p, v_ref[0, :, c].astype(BF16), preferred_element_type=F32).astype(o_ref.dtype)


def _attn_prompt(z, mem_k, mem_v, *, batch, seq, rc):
    nrc = seq // rc
    kv_spec = pl.BlockSpec((1, MEM_LEN, XA_WIDTH), lambda b, c: (b, 0, 0))
    return pl.pallas_call(
        _attn_kernel, grid=(batch, nrc),
        in_specs=[pl.BlockSpec((rc, XA_WIDTH), lambda b, c: (b * nrc + c, COL_QX // XA_WIDTH)), kv_spec, kv_spec],
        out_specs=pl.BlockSpec((rc, XA_WIDTH), lambda b, c: (b * nrc + c, 0)),
        out_shape=jax.ShapeDtypeStruct((batch * seq, XA_WIDTH), BF16),
        compiler_params=_params("parallel", "parallel"),
        name="attn_prompt",
    )(z, mem_k, mem_v)


XA_HALF = XA_HEAD_DIM // 2
XA_ROWS = 2 * XA_HEADS


def _attn_cache_kernel(q_ref, k_ref, v_ref, o_ref, *, nb, seq, width):
    nt_dims = (((1,), (1,)), ((), ()))

    def half_rows(ref, bi, h, half):
        return ref[bi, pl.ds(half * XA_HEADS + h, MEM_LEN, stride=XA_ROWS), :].astype(BF16)

    def body(it, carry):
        work = [(it * width + u, h) for u in range(width) for h in range(XA_HEADS)]
        scores = []
        for bi, h in work:
            rows = pl.ds(pl.multiple_of(bi * seq, SUBLANES), seq)
            lo = h * XA_HEAD_DIM
            s = lax.dot_general(q_ref[rows, lo:lo + XA_HALF].astype(BF16), half_rows(k_ref, bi, h, 0),
                                nt_dims, preferred_element_type=F32)
            s = s + lax.dot_general(q_ref[rows, lo + XA_HALF:lo + XA_HEAD_DIM].astype(BF16),
                                    half_rows(k_ref, bi, h, 1), nt_dims, preferred_element_type=F32)
            scores.append(s * (XA_HEAD_DIM ** -0.5))
        probs = []
        for s in scores:
            p = jnp.exp(s - jnp.max(s, axis=-1, keepdims=True))
            probs.append((p / jnp.sum(p, axis=-1, keepdims=True)).astype(BF16))
        rows = pl.ds(pl.multiple_of(it * width * seq, width * seq), width * seq)
        for h in range(XA_HEADS):
            for half in range(2):
                lo = h * XA_HEAD_DIM + half * XA_HALF
                out = [jnp.dot(probs[u * XA_HEADS + h], half_rows(v_ref, it * width + u, h, half),
                               preferred_element_type=F32) for u in range(width)]
                o_ref[rows, lo:lo + XA_HALF] = jnp.concatenate(out, axis=0).astype(o_ref.dtype)
        return carry

    lax.fori_loop(0, nb // width, body, 0)


def _cache_rows(cache):
    bs = cache.shape[0]
    c = cache.reshape(bs, MEM_LEN, XA_HEADS, 2, XA_HALF).transpose(0, 1, 3, 2, 4)
    return c.reshape(bs, MEM_LEN * XA_ROWS, XA_HALF)


def _attn_sample(z, mem_k, mem_v, *, batch, seq, nb):
    rows = nb * seq
    kern = functools.partial(_attn_cache_kernel, nb=nb, seq=seq, width=2)
    kv_spec = pl.BlockSpec((nb, MEM_LEN * XA_ROWS, XA_HALF), lambda i: (i, 0, 0))
    return pl.pallas_call(
        kern, grid=(batch // nb,),
        in_specs=[pl.BlockSpec((rows, XA_WIDTH), lambda i: (i, COL_QX // XA_WIDTH)), kv_spec, kv_spec],
        out_specs=pl.BlockSpec((rows, XA_WIDTH), lambda i: (i, 0)),
        out_shape=jax.ShapeDtypeStruct((batch * seq, XA_WIDTH), BF16),
        compiler_params=_params("parallel"),
        name="attn_sample",
    )(z, mem_k, mem_v)


def _merge_kernel(ys_ref, yg_ref, yx_ref, g0_ref, g1_ref, g2_ref, wglu_ref, bglu_ref,
                  w0_ref, w1_ref, w2_ref, o_ref, s5_scr):
    @pl.when(pl.program_id(1) == 0)
    def _():
        y = ys_ref[...]
        lin = jnp.dot(y.astype(BF16), wglu_ref[...], preferred_element_type=F32) + bglu_ref[...]
        s5_scr[...] = (y * jax.nn.sigmoid(lin)).astype(BF16)

    j = pl.program_id(1)
    m = jax.nn.sigmoid(g0_ref[...]) * jnp.dot(s5_scr[...], w0_ref[j], preferred_element_type=F32)
    m = m + jax.nn.sigmoid(g1_ref[...]) * jnp.dot(yg_ref[...], w1_ref[j], preferred_element_type=F32)
    m = m + jax.nn.sigmoid(g2_ref[...]) * jnp.dot(yx_ref[...], w2_ref[j], preferred_element_type=F32)
    o_ref[...] = m.astype(o_ref.dtype)


def _merge(y_s5, y_gla, y_x, z, w_glu, b_glu, w_br_s5, w_br_gla, w_br_x, *, tm, tn):
    t = y_s5.shape[0]
    nj = D_MODEL // tn
    once = pl.Buffered(1)
    wide = pl.BlockSpec((tm, S5_WIDTH), lambda i, j: (i, 0))
    gate = lambda b: pl.BlockSpec((tm, tn), lambda i, j: (i, (COL_GATE + b * D_MODEL) // tn + j))
    w_br = pl.BlockSpec((nj, S5_WIDTH, tn), lambda i, j: (0, 0, 0), pipeline_mode=once)
    return pl.pallas_call(
        _merge_kernel, grid=(t // tm, nj),
        in_specs=[wide, wide, wide, gate(0), gate(1), gate(2),
                  pl.BlockSpec((S5_WIDTH, S5_WIDTH), lambda i, j: (0, 0), pipeline_mode=once),
                  pl.BlockSpec((1, S5_WIDTH), lambda i, j: (0, 0)),
                  w_br, w_br, w_br],
        out_specs=pl.BlockSpec((tm, tn), lambda i, j: (i, j)),
        out_shape=jax.ShapeDtypeStruct((t, D_MODEL), BF16),
        scratch_shapes=[pltpu.VMEM((tm, S5_WIDTH), BF16)],
        compiler_params=_params("parallel", "arbitrary"),
        name="merge",
    )(y_s5, y_gla, y_x, z, z, z, w_glu, b_glu, w_br_s5, w_br_gla, w_br_x)


def _out_ffn_kernel(x_ref, m_ref, wo_ref, gf_ref, wg_ref, wu_ref, wd_ref, gl_ref, o_ref, h_scr, acc_scr):
    k = pl.program_id(1)

    @pl.when(k == 0)
    def _():
        acc_scr[...] = x_ref[...] + jnp.dot(m_ref[...], wo_ref[...], preferred_element_type=F32)
        h_scr[...] = _rms(acc_scr[...], gf_ref[...]).astype(BF16)

    h = h_scr[...]
    gate = jnp.dot(h, wg_ref[...], preferred_element_type=F32)
    up = jnp.dot(h, wu_ref[...], preferred_element_type=F32)
    act = (gate * jax.nn.sigmoid(gate) * up).astype(BF16)
    acc_scr[...] += jnp.dot(act, wd_ref[...], preferred_element_type=F32)

    @pl.when(k == pl.num_programs(1) - 1)
    def _():
        o_ref[...] = _rms(acc_scr[...], gl_ref[...])


def _out_ffn(x, merged, w_out, g_ffn, w_gate, w_up, w_down, g_final, *, tm, th):
    t = x.shape[0]
    row = pl.BlockSpec((tm, D_MODEL), lambda i, k: (i, 0))
    vec = pl.BlockSpec((1, D_MODEL), lambda i, k: (0, 0))
    w_in = pl.BlockSpec((D_MODEL, th), lambda i, k: (0, k))
    return pl.pallas_call(
        _out_ffn_kernel, grid=(t // tm, FFN_HIDDEN // th),
        in_specs=[row, row,
                  pl.BlockSpec((D_MODEL, D_MODEL), lambda i, k: (0, 0), pipeline_mode=pl.Buffered(1)),
                  vec, w_in, w_in, pl.BlockSpec((th, D_MODEL), lambda i, k: (k, 0)), vec],
        out_specs=row,
        out_shape=jax.ShapeDtypeStruct((t, D_MODEL), F32),
        scratch_shapes=[pltpu.VMEM((tm, D_MODEL), BF16), pltpu.VMEM((tm, D_MODEL), F32)],
        compiler_params=pltpu.CompilerParams(dimension_semantics=("parallel", "arbitrary"),
                                             vmem_limit_bytes=OUT_FFN_VMEM),
        name="out_ffn",
    )(x, merged, w_out, g_ffn, w_gate, w_up, w_down, g_final)


def _reorder_in_proj_kernel(wt_ref, wa_ref, main_ref, alow_ref):
    main_ref[...] = wt_ref[...].T.astype(BF16)

    @pl.when(pl.program_id(0) == 0)
    def _():
        lane = lax.broadcasted_iota(jnp.int32, alow_ref.shape, 1)
        alow_ref[...] = jnp.where(lane < GLA_RANK, wa_ref[...].T, 0.0).astype(BF16)


def _reorder_in_proj(w_in):
    d, n = w_in.shape
    wt = w_in.T
    tc = 512
    per_tile = IN_TN // tc
    n_first, n_gate = COL_GATE // tc, (COL_QX - COL_GATE) // tc
    gate_start, qx_start = 4096 + GLA_RANK + XA_WIDTH, 4096 + GLA_RANK

    def src_row(j):
        t8, g8, q8 = tc // SUBLANES, gate_start // SUBLANES, qx_start // SUBLANES
        r8 = jnp.where(j < n_first, j * t8,
                       jnp.where(j < n_first + n_gate, g8 + (j - n_first) * t8,
                                 q8 + (j - n_first - n_gate) * t8))
        return r8 * SUBLANES

    return pl.pallas_call(
        _reorder_in_proj_kernel, grid=(Z_WIDTH // tc,),
        in_specs=[pl.BlockSpec((pl.Element(tc), pl.Element(d)), lambda j: (src_row(j), 0)),
                  pl.BlockSpec((pl.Element(LANES), pl.Element(d)), lambda j: (COL_GATE, 0))],
        out_specs=[pl.BlockSpec((None, d, tc), lambda j: (j // per_tile, 0, j % per_tile)),
                   pl.BlockSpec((d, LANES), lambda j: (0, 0))],
        out_shape=[jax.ShapeDtypeStruct((Z_WIDTH // IN_TN, d, IN_TN), BF16),
                   jax.ShapeDtypeStruct((d, LANES), BF16)],
        compiler_params=_params("arbitrary"),
        name="reorder_in_proj",
    )(wt, wt)


def _col_tiles(w, tn):
    k, n = w.shape
    return w.reshape(k, n // tn, tn).transpose(1, 0, 2)


def _layer(x, mem_k, mem_v, s5_re0, s5_im0, gla_s0, w, *, batch, seq, segmented):
    z, alow = _norm_matmul(x, w['norm_mix'], w['w_main'], w['w_alow'], tm=1024)
    y_s5, hf_re, hf_im = _s5_branch(z, w['s5'], s5_re0, s5_im0, batch=batch, seq=seq)
    y_gla, gla_s = _gla(z, alow, w['gla_w_a2'], w['gla_b_a'], w['gla_norm'], gla_s0,
                        batch=batch, seq=seq, chained=segmented)
    if segmented:
        y_x = _attn_prompt(z, mem_k, mem_v, batch=batch, seq=seq, rc=512)
    else:
        y_x = _attn_sample(z, mem_k, mem_v, batch=batch, seq=seq, nb=8)
    merged = _merge(y_s5, y_gla, y_x, z, w['s5_w_glu'], w['s5_b_glu'],
                    w['w_br_s5'], w['w_br_gla'], w['w_br_xattn'], tm=512, tn=MERGE_TN)
    y = _out_ffn(x, merged, w['w_out'], w['norm_ffn'], w['w_ffn_gate'], w['w_ffn_up'], w['w_ffn_down'],
                 w['norm_final'], tm=512, th=FFN_TH)
    return y, hf_re, hf_im, gla_s


def kernel(x_prompt, x_sample, mem_prompt, state_s5_re, state_s5_im, state_gla, cache_mem_k, cache_mem_v,
           norm_mix, w_in, s5_lam_re, s5_lam_im, s5_log_dt, s5_b_re, s5_b_im, s5_c_re, s5_c_im,
           s5_d, s5_w_glu, s5_b_glu, gla_w_a2, gla_b_a, gla_norm, mem_norm, w_mem_k, w_mem_v,
           w_br_s5, w_br_gla, w_br_xattn, w_out, norm_ffn, w_ffn_gate, w_ffn_up, w_ffn_down, norm_final):
    depth = w_in.shape[0]
    assert depth == 1
    bp, sp, d = x_prompt.shape
    bs, ss, _ = x_sample.shape
    n_state = S5_GROUPS * S5_STATE
    row = lambda v: v.reshape(1, -1)

    l = 0
    w_main, w_alow = _reorder_in_proj(w_in[l])
    s5_w = _s5_params(s5_lam_re[l], s5_lam_im[l], s5_log_dt[l], s5_b_re[l], s5_b_im[l],
                      s5_c_re[l], s5_c_im[l], s5_d[l])
    w = {
        'norm_mix': row(norm_mix[l]), 'w_main': w_main, 'w_alow': w_alow,
        's5': s5_w,
        's5_w_glu': s5_w_glu[l].astype(BF16), 's5_b_glu': row(s5_b_glu[l]),
        'gla_w_a2': jnp.pad(gla_w_a2[l], ((0, LANES - GLA_RANK), (0, 0))).astype(BF16),
        'gla_b_a': row(gla_b_a[l]), 'gla_norm': row(gla_norm[l]),
        'w_br_s5': _col_tiles(w_br_s5[l].astype(BF16), MERGE_TN),
        'w_br_gla': _col_tiles(w_br_gla[l].astype(BF16), MERGE_TN),
        'w_br_xattn': _col_tiles(w_br_xattn[l].astype(BF16), MERGE_TN),
        'w_out': w_out[l].astype(BF16),
        'norm_ffn': row(norm_ffn[l]), 'w_ffn_gate': w_ffn_gate[l].astype(BF16),
        'w_ffn_up': w_ffn_up[l].astype(BF16), 'w_ffn_down': w_ffn_down[l].astype(BF16),
        'norm_final': row(norm_final),
    }

    w_mem = jnp.stack([w_mem_k[l], w_mem_v[l]]).astype(BF16)
    mem_kv = _norm_matmul(mem_prompt.reshape(bp * MEM_LEN, d), row(mem_norm[l]), w_mem, tm=512)
    mk = mem_kv[:, :XA_WIDTH].reshape(bp, MEM_LEN, XA_WIDTH)
    mv = mem_kv[:, XA_WIDTH:].reshape(bp, MEM_LEN, XA_WIDTH)
    zero_s5 = jnp.zeros((bp, n_state), F32)
    zero_gla = jnp.zeros((bp, GLA_HEADS, GLA_DK, GLA_DV), F32)
    yp, p_re, p_im, p_gla = _layer(x_prompt.reshape(bp * sp, d), mk, mv, zero_s5, zero_s5, zero_gla, w,
                                   batch=bp, seq=sp, segmented=True)

    ys, s_re, s_im, s_gla = _layer(x_sample.reshape(bs * ss, d),
                                   _cache_rows(cache_mem_k[l]), _cache_rows(cache_mem_v[l]),
                                   state_s5_re[l].reshape(bs, n_state), state_s5_im[l].reshape(bs, n_state),
                                   state_gla[l], w, batch=bs, seq=ss, segmented=False)

    s5_shape_p = (1, bp, S5_GROUPS, S5_STATE)
    s5_shape_s = (1, bs, S5_GROUPS, S5_STATE)
    kv_shape = (1, bp, MEM_LEN, XA_HEADS, XA_HEAD_DIM)
    return (yp.reshape(bp, sp, d), ys.reshape(bs, ss, d),
            p_re.reshape(s5_shape_p), p_im.reshape(s5_shape_p), p_gla[None],
            mk.reshape(kv_shape), mv.reshape(kv_shape),
            s_re.reshape(s5_shape_s), s_im.reshape(s5_shape_s), s_gla[None])
```

```python
import functools
import math

import jax
import jax.numpy as jnp
from jax import lax
from jax.experimental import pallas as pl
from jax.experimental.pallas import tpu as pltpu

F32 = jnp.float32
BF16 = jnp.bfloat16

D_MODEL = 2048
S5_WIDTH = 1024
S5_GROUP = 16
S5_GROUPS = 64
S5_STATE = 64
GLA_HEADS = 4
GLA_DK = 128
GLA_DV = 256
GLA_RANK = 16
GLA_TAU = 16.0
GLA_CHUNK = 64
XA_HEADS = 4
XA_HEAD_DIM = 256
XA_WIDTH = 1024
MEM_LEN = 256
FFN_HIDDEN = 5632
RMS_EPS = 1e-6

LANES = 128
SUBLANES = 8
VMEM_LIMIT = 56 * 1024 * 1024
OUT_FFN_VMEM = 52 * 1024 * 1024

COL_U = 0
COL_Q = 1024
COL_K = 1536
COL_V = 2048
COL_R = 3072
COL_GATE = 4096
COL_QX = 10240
Z_WIDTH = 11264

S5_GB = 8
S5_CH = S5_GB * S5_STATE
S5_NBLK = S5_GROUPS // S5_GB
S5_TC = 16
GLA_TILE = 64
MERGE_TN = 1024
IN_TN = 1024
FFN_TH = 512


def _params(*sem):
    return pltpu.CompilerParams(dimension_semantics=sem, vmem_limit_bytes=VMEM_LIMIT)


def _rms(x, g):
    return x * lax.rsqrt(jnp.mean(x * x, axis=-1, keepdims=True) + RMS_EPS) * g


def _norm_matmul_kernel(x_ref, g_ref, w_ref, o_ref, h_ref):
    @pl.when(pl.program_id(1) == 0)
    def _():
        h_ref[...] = _rms(x_ref[...], g_ref[...]).astype(BF16)

    o_ref[...] = jnp.dot(h_ref[...], w_ref[0], preferred_element_type=F32)


def _norm_matmul2_kernel(x_ref, g_ref, w_ref, ws_ref, o_ref, os_ref, h_ref):
    @pl.when(pl.program_id(1) == 0)
    def _():
        h = _rms(x_ref[...], g_ref[...]).astype(BF16)
        h_ref[...] = h
        os_ref[...] = jnp.dot(h, ws_ref[...], preferred_element_type=F32)

    o_ref[...] = jnp.dot(h_ref[...], w_ref[0], preferred_element_type=F32)


def _norm_matmul(x, g, w, w_small=None, *, tm):
    t, d = x.shape
    nj, _, tn = w.shape
    n = nj * tn
    grid = (t // tm, n // tn)
    in_specs = [pl.BlockSpec((tm, d), lambda i, j: (i, 0)),
                pl.BlockSpec((1, d), lambda i, j: (0, 0)),
                pl.BlockSpec((1, d, tn), lambda i, j: (j, 0, 0))]
    out_specs = pl.BlockSpec((tm, tn), lambda i, j: (i, j))
    out_shape = jax.ShapeDtypeStruct((t, n), F32)
    scratch = [pltpu.VMEM((tm, d), BF16)]
    if w_small is None:
        return pl.pallas_call(_norm_matmul_kernel, grid=grid, in_specs=in_specs, out_specs=out_specs,
                              out_shape=out_shape, scratch_shapes=scratch,
                              compiler_params=_params("parallel", "arbitrary"),
                              name="norm_matmul")(x, g, w)
    ns = w_small.shape[1]
    in_specs.append(pl.BlockSpec((d, ns), lambda i, j: (0, 0)))
    return pl.pallas_call(_norm_matmul2_kernel, grid=grid, in_specs=in_specs,
                          out_specs=[out_specs, pl.BlockSpec((tm, ns), lambda i, j: (i, 0))],
                          out_shape=[out_shape, jax.ShapeDtypeStruct((t, ns), F32)],
                          scratch_shapes=scratch,
                          compiler_params=_params("parallel", "arbitrary"),
                          name="in_proj")(x, g, w, w_small)


def _s5_param_kernel(lr_ref, li_ref, ldt_ref, btr_ref, bti_ref, cr_ref, ci_ref,
                     w16_ref, cc16_ref, w8_ref, cc8_ref, ap_ref):
    p = S5_STATE
    half = S5_TC // 2
    lam_re, lam_im = lr_ref[0], li_ref[0]
    dt = jnp.exp(ldt_ref[0])
    mag = jnp.exp(lam_re * dt)
    a_re = mag * jnp.cos(lam_im * dt)
    a_im = mag * jnp.sin(lam_im * dt)
    den = lam_re * lam_re + lam_im * lam_im
    coef_re = ((a_re - 1.0) * lam_re + a_im * lam_im) / den
    coef_im = (a_im * lam_re - (a_re - 1.0) * lam_im) / den
    bt_re, bt_im = btr_ref[0], bti_ref[0]
    bb_re = coef_re * bt_re - coef_im * bt_im
    bb_im = coef_re * bt_im + coef_im * bt_re
    c_re, c_im = cr_ref[0], ci_ref[0]

    pw_re, pw_im = jnp.ones_like(a_re), jnp.zeros_like(a_re)
    ca_re, ca_im, ab_re, ab_im, powers = [], [], [], [], {}
    for j in range(S5_TC + 1):
        powers[j] = (pw_re, pw_im)
        ca_re.append(c_re * pw_re - c_im * pw_im)
        ca_im.append(c_re * pw_im + c_im * pw_re)
        ab_re.append(bb_re * pw_re - bb_im * pw_im)
        ab_im.append(bb_re * pw_im + bb_im * pw_re)
        pw_re, pw_im = pw_re * a_re - pw_im * a_im, pw_re * a_im + pw_im * a_re
    lag_re = jnp.concatenate(ca_re[:S5_TC], axis=0)
    lag_im = jnp.concatenate(ca_im[:S5_TC], axis=0)
    end_re = jnp.concatenate([ab_re[S5_TC - 1 - s] for s in range(S5_TC)], axis=0)
    end_im = jnp.concatenate([ab_im[S5_TC - 1 - s] for s in range(S5_TC)], axis=0)
    car_re = jnp.concatenate(ca_re[1:], axis=0).T
    car_im = -jnp.concatenate(ca_im[1:], axis=0).T

    def pair_blocks(x):
        first = lax.broadcasted_iota(jnp.int32, x.shape, 0) < p
        return jnp.concatenate([jnp.where(first, x, 0.0), jnp.where(first, 0.0, x)], axis=1)

    cc16_ref[0] = jnp.concatenate([pair_blocks(car_re), pair_blocks(car_im)], axis=0).astype(BF16)
    n8 = half * S5_GROUP
    cc8_ref[0] = jnp.concatenate([pair_blocks(car_re[:, :n8]), pair_blocks(car_im[:, :n8])], axis=0).astype(BF16)

    nt_dims = (((1,), (1,)), ((), ()))
    hi = lax.Precision.HIGHEST
    lane = lax.broadcasted_iota(jnp.int32, (S5_GROUP, LANES), 1)
    end_lane = lax.broadcasted_iota(jnp.int32, end_re.shape, 1)
    for gi in range(2):
        mine = (lane < p) if gi == 0 else (lane >= p)
        own_re, own_im = jnp.where(mine, bb_re, 0.0), jnp.where(mine, bb_im, 0.0)
        strip = (lax.dot_general(own_re, lag_re, nt_dims, precision=hi, preferred_element_type=F32)
                 - lax.dot_general(own_im, lag_im, nt_dims, precision=hi, preferred_element_type=F32))
        lo, up = strip[:, :LANES], strip[:, LANES:]
        blocks = [strip]
        for s in range(1, S5_TC):
            sh = (s % half) * S5_GROUP
            lo_r = pltpu.roll(lo, sh, 1) if sh else lo
            up_r = pltpu.roll(up, sh, 1) if sh else up
            if s < half:
                blocks.append(jnp.concatenate([jnp.where(lane >= sh, lo_r, 0.0),
                                               jnp.where(lane >= sh, up_r, lo_r)], axis=1))
            else:
                blocks.append(jnp.concatenate([jnp.zeros_like(lo), jnp.where(lane >= sh, lo_r, 0.0)], axis=1))
        toeplitz = jnp.concatenate(blocks, axis=0)
        own_end = (end_lane < p) if gi == 0 else (end_lane >= p)
        ends = jnp.concatenate([jnp.where(own_end, end_re, 0.0), jnp.where(own_end, end_im, 0.0)], axis=1)
        w16_ref[gi] = jnp.concatenate([toeplitz, ends], axis=1).astype(BF16)
        w8_ref[gi] = jnp.concatenate([toeplitz[:n8, :n8], ends[n8:, :]], axis=1).astype(BF16)
    ap_ref[0] = jnp.concatenate([powers[S5_TC][0], powers[S5_TC][1], powers[half][0], powers[half][1]], axis=0)


def _s5_params(lam_re, lam_im, log_dt, b_re, b_im, c_re, c_im, d_skip):
    g, p, c = S5_GROUPS, S5_STATE, S5_GROUP
    npair = g // 2
    w16, w8 = S5_TC * c, S5_TC // 2 * c

    def pair_lanes(x):
        return x.reshape(npair, 2, x.shape[1], p).transpose(0, 2, 1, 3).reshape(npair, x.shape[1], 2 * p)

    row = lambda x: pair_lanes(x.reshape(g, 1, p))
    blk = lambda *shape: pl.BlockSpec((1,) + shape, lambda i: (i, 0, 0))
    two = lambda *shape: pl.BlockSpec((2,) + shape, lambda i: (i, 0, 0))
    wt16, cc16, wt8, cc8, apow = pl.pallas_call(
        _s5_param_kernel, grid=(npair,),
        in_specs=[blk(1, 2 * p)] * 3 + [blk(c, 2 * p)] * 4,
        out_specs=[two(w16, w16 + 4 * p), blk(4 * p, 2 * w16), two(w8, w8 + 4 * p), blk(4 * p, 2 * w8),
                   blk(4, 2 * p)],
        out_shape=[jax.ShapeDtypeStruct((g, w16, w16 + 4 * p), BF16),
                   jax.ShapeDtypeStruct((npair, 4 * p, 2 * w16), BF16),
                   jax.ShapeDtypeStruct((g, w8, w8 + 4 * p), BF16),
                   jax.ShapeDtypeStruct((npair, 4 * p, 2 * w8), BF16),
                   jax.ShapeDtypeStruct((npair, 4, 2 * p), F32)],
        compiler_params=_params("parallel"),
        name="s5_params",
    )(row(lam_re), row(lam_im), row(jnp.broadcast_to(log_dt[:, None], (g, p))),
      pair_lanes(b_re.transpose(0, 2, 1)), pair_lanes(b_im.transpose(0, 2, 1)),
      pair_lanes(c_re), pair_lanes(c_im))
    d = d_skip.reshape(S5_NBLK, 1, S5_GB * c)

    def transition(r):
        return jnp.concatenate([apow[:, r].reshape(S5_NBLK, 1, S5_CH), apow[:, r + 1].reshape(S5_NBLK, 1, S5_CH)],
                               axis=-1)

    return {S5_TC: (wt16, cc16, transition(0), d), S5_TC // 2: (wt8, cc8, transition(2), d)}


def _unit_transpose(vs):
    unit = lax.broadcasted_iota(jnp.int32, vs[0].shape, 1) >> int(math.log2(S5_GROUP))
    for dist in (4, 2, 1):
        keep = (unit & dist) == 0
        nxt = list(vs)
        for i in range(8):
            if i & dist == 0:
                a, b = vs[i], vs[i + dist]
                nxt[i] = jnp.where(keep, a, pltpu.roll(b, dist * S5_GROUP, 1))
                nxt[i + dist] = jnp.where(keep, pltpu.roll(a, LANES - dist * S5_GROUP, 1), b)
        vs = nxt
    return vs


def _s5_chunk_kernel(z_ref, w_ref, cc_ref, a_ref, d_ref, h0r_ref, h0i_ref,
                     y_ref, hfr_ref, hfi_ref, ut_ref, ug_ref, yg_ref, e_ref, *, nsl, rps, tc, nseg):
    uw = tc * S5_GROUP
    nq = uw // LANES
    rows = nsl * rps
    nk = rps if nsl > 1 else 1
    ns = rows // nk
    npair = S5_GB // 2
    rchunk = min(rows, 64)

    for s in range(nsl):
        for t in range(tc):
            ut_ref[t, s * rps:(s + 1) * rps, :] = z_ref[pl.ds(s * rps * tc + t, rps, stride=tc), :]

    def row_chunk(rc):
        return pl.ds(pl.multiple_of(rc * rchunk, rchunk), rchunk)

    def to_groups(rc, carry):
        for q in range(nq):
            per_group = _unit_transpose([ut_ref[q * 8 + t, row_chunk(rc), :] for t in range(8)])
            for g in range(S5_GB):
                ug_ref[g, row_chunk(rc), q * LANES:(q + 1) * LANES] = per_group[g].astype(BF16)
        return carry

    lax.fori_loop(0, rows // rchunk, to_groups, 0)

    for pair in range(npair):
        ends = None
        for gi in range(2):
            g = 2 * pair + gi
            em = jnp.dot(ug_ref[g], w_ref[g], preferred_element_type=F32)
            yg_ref[g] = em[:, :uw]
            ends = em[:, uw:] if ends is None else ends + em[:, uw:]
        for part, blk in ((ends[:, :LANES], pair), (ends[:, LANES:], npair + pair)):
            if nk == 1:
                e_ref[blk] = part
            else:
                for s in range(nsl):
                    e_ref[blk, pl.ds(s, rps, stride=nsl), :] = part[s * rps:(s + 1) * rps, :]

    a_row = [(a_ref[0, :, i * LANES:(i + 1) * LANES], a_ref[0, :, S5_CH + i * LANES:S5_CH + (i + 1) * LANES])
             for i in range(npair)]
    a_full = [(jnp.broadcast_to(ar, (ns, LANES)), jnp.broadcast_to(ai, (ns, LANES))) for ar, ai in a_row]

    def slab(t):
        return pl.ds(pl.multiple_of(t * ns, ns), ns)

    def run_scan(h, store):
        def step(t, carry):
            out = []
            for i in range(npair):
                hr, hi = carry[i]
                ar, ai = a_full[i]
                er = e_ref[i, slab(t), :]
                ei = e_ref[npair + i, slab(t), :]
                if store:
                    e_ref[i, slab(t), :] = hr
                    e_ref[npair + i, slab(t), :] = hi
                out.append((ar * hr - ai * hi + er, ar * hi + ai * hr + ei))
            return tuple(out)

        if nk == 1:
            return step(0, h)
        return lax.fori_loop(0, nk, step, h, unroll=4)

    h0 = tuple((h0r_ref[:, i * LANES:(i + 1) * LANES], h0i_ref[:, i * LANES:(i + 1) * LANES]) for i in range(npair))
    if nseg == 1:
        final = run_scan(h0, store=True)
    else:
        zero = jnp.zeros((ns, LANES), F32)
        seg_end = run_scan(tuple((zero, zero) for _ in range(npair)), store=False)
        second = (lax.broadcasted_iota(jnp.int32, (ns, LANES), 0) & 1) == 1
        init, final = [], []
        for i in range(npair):
            p_re, p_im = a_row[i]
            for _ in range(int(math.log2(nk))):
                p_re, p_im = p_re * p_re - p_im * p_im, 2.0 * (p_re * p_im)
            (hr, hi), (er, ei) = h0[i], seg_end[i]
            i_re = jnp.where(second, p_re * hr - p_im * hi + pltpu.roll(er, 1, 0), hr)
            i_im = jnp.where(second, p_re * hi + p_im * hr + pltpu.roll(ei, 1, 0), hi)
            init.append((i_re, i_im))
            final.append((p_re * i_re - p_im * i_im + er, p_re * i_im + p_im * i_re + ei))
        run_scan(tuple(init), store=True)
    for i in range(npair):
        hfr_ref[:, i * LANES:(i + 1) * LANES] = final[i][0]
        hfi_ref[:, i * LANES:(i + 1) * LANES] = final[i][1]

    def chunk_rows(blk):
        if nk == 1:
            return e_ref[blk]
        return jnp.concatenate([e_ref[blk, pl.ds(s, rps, stride=nsl), :] for s in range(nsl)], axis=0)

    for pair in range(npair):
        h_in = jnp.concatenate([chunk_rows(pair), chunk_rows(npair + pair)], axis=1).astype(BF16)
        carried = jnp.dot(h_in, cc_ref[pair], preferred_element_type=F32)
        for gi in range(2):
            yg_ref[2 * pair + gi] += carried[:, gi * uw:(gi + 1) * uw]

    d_row = d_ref[0]

    def to_tokens(rc, carry):
        for q in range(nq):
            per_tau = _unit_transpose([yg_ref[g, row_chunk(rc), q * LANES:(q + 1) * LANES] for g in range(S5_GB)])
            for t in range(8):
                y = per_tau[t] + d_row * ut_ref[q * 8 + t, row_chunk(rc), :]
                ut_ref[q * 8 + t, row_chunk(rc), :] = jax.nn.gelu(y, approximate=True)
        return carry

    lax.fori_loop(0, rows // rchunk, to_tokens, 0)
    for s in range(nsl):
        for t in range(tc):
            y_ref[pl.ds(s * rps * tc + t, rps, stride=tc), :] = ut_ref[t, s * rps:(s + 1) * rps, :]


def _s5_branch(z, weights, h0_re, h0_im, *, batch, seq):
    tokens = batch * seq
    if seq > S5_TC:
        tc, nseg = S5_TC, 2
        nsl, rps = batch * nseg, seq // (nseg * tc)
        assert rps & (rps - 1) == 0
        h0_re, h0_im = jnp.repeat(h0_re, nseg, axis=0), jnp.repeat(h0_im, nseg, axis=0)
    else:
        tc, nseg, nsl, rps = seq, 1, 1, batch
    w, cc, a, d = weights[tc]
    uw = tc * S5_GROUP
    rows = nsl * rps
    ns = h0_re.shape[0]
    kern = functools.partial(_s5_chunk_kernel, nsl=nsl, rps=rps, tc=tc, nseg=nseg)
    state_spec = pl.BlockSpec((ns, S5_CH), lambda k: (0, k))
    state_shape = jax.ShapeDtypeStruct((ns, S5_GROUPS * S5_STATE), F32)
    y, hf_re, hf_im = pl.pallas_call(
        kern, grid=(S5_NBLK,),
        in_specs=[pl.BlockSpec((tokens, LANES), lambda k: (0, COL_U // LANES + k)),
                  pl.BlockSpec((S5_GB, uw, uw + 4 * S5_STATE), lambda k: (k, 0, 0)),
                  pl.BlockSpec((S5_GB // 2, 4 * S5_STATE, 2 * uw), lambda k: (k, 0, 0)),
                  pl.BlockSpec((1, 1, 2 * S5_CH), lambda k: (k, 0, 0)),
                  pl.BlockSpec((1, 1, LANES), lambda k: (k, 0, 0)),
                  state_spec, state_spec],
        out_specs=[pl.BlockSpec((tokens, LANES), lambda k: (0, k)), state_spec, state_spec],
        out_shape=[jax.ShapeDtypeStruct((tokens, S5_WIDTH), F32), state_shape, state_shape],
        scratch_shapes=[pltpu.VMEM((tc, rows, LANES), F32), pltpu.VMEM((S5_GB, rows, uw), BF16),
                        pltpu.VMEM((S5_GB, rows, uw), F32), pltpu.VMEM((S5_GB, rows, LANES), F32)],
        compiler_params=_params("parallel"),
        name="s5_chunked",
    )(z, w, cc, a, d, h0_re, h0_im)
    if nseg == 2:
        hf_re, hf_im = hf_re[1::2], hf_im[1::2]
    return y, hf_re, hf_im


def _gla_kernel(q_ref, k_ref, v_ref, r_ref, al_ref, wa_ref, ba_ref, gn_ref, s0_ref,
                y_ref, sf_ref, s_scr, *, ntile, groups, chained, width):
    rt = GLA_TILE
    c = rt // groups
    shift = int(math.log2(c))
    row_g = lax.broadcasted_iota(jnp.int32, (rt, rt), 0)
    col_g = lax.broadcasted_iota(jnp.int32, (rt, rt), 1)
    same = (row_g >> shift) == (col_g >> shift)
    causal = same & (row_g >= col_g)
    cum_w = jnp.concatenate([causal.astype(BF16), same.astype(BF16)], axis=0)
    tn_dims = (((0,), (0,)), ((), ()))
    nt_dims = (((1,), (1,)), ((), ()))

    def split3(x):
        hi = x.astype(BF16)
        r1 = x - hi.astype(F32)
        mid = r1.astype(BF16)
        lo = (r1 - mid.astype(F32)).astype(BF16)
        return jnp.concatenate([hi, mid, lo], axis=1)

    if chained:
        s_scr[...] = s0_ref[0, 0]

    def body(it, carry):
        tiles = [it * width + u for u in range(width)]
        rows = [pl.ds(pl.multiple_of(t * rt, rt), rt) for t in tiles]
        log_a = []
        for u in range(width):
            x = jnp.dot(al_ref[rows[u], :].astype(BF16), wa_ref[...], preferred_element_type=F32) + ba_ref[...]
            log_a.append((jnp.minimum(x, 0.0) - jnp.log1p(jnp.exp(-jnp.abs(x)))) * (1.0 / GLA_TAU))
        la3 = [split3(la) for la in log_a]
        cums, e_col = [], []
        for u in range(width):
            cs = jnp.dot(cum_w, la3[u], preferred_element_type=F32)
            cums.append(cs[:, :LANES] + cs[:, LANES:2 * LANES] + cs[:, 2 * LANES:])
            e_col.append(jnp.exp(cums[u].T))
        qb, kd, v, att, upd = [], [], [], [], []
        for u in range(width):
            b = cums[u][:rt]
            b_end = cums[u][rt:]
            k = k_ref[rows[u], :]
            v.append(v_ref[rows[u], :])
            mid = 0.5 * b_end
            q_mid = q_ref[rows[u], :] * (GLA_DK ** -0.5) * jnp.exp(b - mid)
            k_mid = k * jnp.exp(mid - b)
            e_mid = jnp.exp(mid)
            qb.append(q_mid * e_mid)
            kd.append(k_mid * e_mid)
            a = lax.dot_general(q_mid.astype(BF16), k_mid.astype(BF16), nt_dims, preferred_element_type=F32)
            att.append(jnp.where(causal, a, 0.0).astype(BF16))
            upd.append([lax.dot_general(kd[u][g * c:(g + 1) * c].astype(BF16), v[u][g * c:(g + 1) * c].astype(BF16),
                                        tn_dims, preferred_element_type=F32) for g in range(groups)])
        o = [jnp.dot(att[u], v[u].astype(BF16), preferred_element_type=F32) for u in range(width)]
        for u in range(width):
            o_state = []
            for g in range(groups):
                s = s_scr[...] if chained else s0_ref[tiles[u] * groups + g, 0]
                o_state.append(jnp.dot(qb[u][g * c:(g + 1) * c].astype(BF16), s.astype(BF16),
                                       preferred_element_type=F32))
                s_new = s * e_col[u][:, rt + g * c:rt + g * c + 1] + upd[u][g]
                if chained:
                    s_scr[...] = s_new
                else:
                    sf_ref[tiles[u] * groups + g, 0] = s_new
            o[u] = o[u] + (o_state[0] if groups == 1 else jnp.concatenate(o_state, axis=0))
        for u in range(width):
            y = o[u] * lax.rsqrt(jnp.mean(o[u] * o[u], axis=-1, keepdims=True) + RMS_EPS)
            y = y * gn_ref[...]
            r = r_ref[rows[u], :]
            y_ref[rows[u], :] = (y * (r * jax.nn.sigmoid(r))).astype(y_ref.dtype)
        return carry

    lax.fori_loop(0, ntile // width, body, 0)
    if chained:
        sf_ref[0, 0] = s_scr[...]


def _gla(z, alow, w_a2, b_a, g_norm, s0, *, batch, seq, chained):
    chunk = math.gcd(seq, GLA_CHUNK)
    if chained:
        assert chunk == GLA_TILE
        nb, groups, width = 1, 1, 8
    else:
        assert GLA_TILE % seq == 0 and chunk == seq
        nb, groups, width = 32, GLA_TILE // seq, 4
    rows = nb * seq
    kern = functools.partial(_gla_kernel, ntile=rows // GLA_TILE, groups=groups, chained=chained, width=width)
    state_spec = pl.BlockSpec((nb, 1, GLA_DK, GLA_DV), lambda i, h: (i, h, 0, 0))
    return pl.pallas_call(
        kern, grid=(batch // nb, GLA_HEADS),
        in_specs=[pl.BlockSpec((rows, GLA_DK), lambda i, h: (i, COL_Q // GLA_DK + h)),
                  pl.BlockSpec((rows, GLA_DK), lambda i, h: (i, COL_K // GLA_DK + h)),
                  pl.BlockSpec((rows, GLA_DV), lambda i, h: (i, COL_V // GLA_DV + h)),
                  pl.BlockSpec((rows, GLA_DV), lambda i, h: (i, COL_R // GLA_DV + h)),
                  pl.BlockSpec((rows, LANES), lambda i, h: (i, 0)),
                  pl.BlockSpec((LANES, GLA_DK), lambda i, h: (0, h)),
                  pl.BlockSpec((1, GLA_DK), lambda i, h: (0, h)),
                  pl.BlockSpec((1, GLA_DV), lambda i, h: (0, h)),
                  state_spec],
        out_specs=[pl.BlockSpec((rows, GLA_DV), lambda i, h: (i, h)), state_spec],
        out_shape=[jax.ShapeDtypeStruct((batch * seq, GLA_HEADS * GLA_DV), BF16),
                   jax.ShapeDtypeStruct((batch, GLA_HEADS, GLA_DK, GLA_DV), F32)],
        scratch_shapes=[pltpu.VMEM((GLA_DK, GLA_DV), F32)],
        compiler_params=_params("parallel", "parallel"),
        name="gla",
    )(z, z, z, z, alow, w_a2, b_a, g_norm, s0)


def _attn_kernel(q_ref, k_ref, v_ref, o_ref):
    nt_dims = (((1,), (1,)), ((), ()))
    cols = [slice(h * XA_HEAD_DIM, (h + 1) * XA_HEAD_DIM) for h in range(XA_HEADS)]
    scores = [lax.dot_general(q_ref[:, c].astype(BF16), k_ref[0, :, c].astype(BF16), nt_dims,
                              preferred_element_type=F32) * (XA_HEAD_DIM ** -0.5) for c in cols]
    probs = []
    for s in scores:
        p = jnp.exp(s - jnp.max(s, axis=-1, keepdims=True))
        probs.append((p / jnp.sum(p, axis=-1, keepdims=True)).astype(BF16))
    for c, p in zip(cols, probs):
        o_ref[:, c] = jnp.dot(p, v_ref[0, :, c].astype(BF16), preferred_element_type=F32).astype(o_ref.dtype)


def _attn_prompt(z, mem_k, mem_v, *, batch, seq, rc):
    nrc = seq // rc
    kv_spec = pl.BlockSpec((1, MEM_LEN, XA_WIDTH), lambda b, c: (b, 0, 0))
    return pl.pallas_call(
        _attn_kernel, grid=(batch, nrc),
        in_specs=[pl.BlockSpec((rc, XA_WIDTH), lambda b, c: (b * nrc + c, COL_QX // XA_WIDTH)), kv_spec, kv_spec],
        out_specs=pl.BlockSpec((rc, XA_WIDTH), lambda b, c: (b * nrc + c, 0)),
        out_shape=jax.ShapeDtypeStruct((batch * seq, XA_WIDTH), BF16),
        compiler_params=_params("parallel", "parallel"),
        name="attn_prompt",
    )(z, mem_k, mem_v)


XA_HALF = XA_HEAD_DIM // 2
XA_ROWS = 2 * XA_HEADS


def _attn_cache_kernel(q_ref, k_ref, v_ref, o_ref, *, nb, seq, width):
    nt_dims = (((1,), (1,)), ((), ()))

    def half_rows(ref, bi, h, half):
        return ref[bi, pl.ds(half * XA_HEADS + h, MEM_LEN, stride=XA_ROWS), :].astype(BF16)

    def body(it, carry):
        work = [(it * width + u, h) for u in range(width) for h in range(XA_HEADS)]
        scores = []
        for bi, h in work:
            rows = pl.ds(pl.multiple_of(bi * seq, SUBLANES), seq)
            lo = h * XA_HEAD_DIM
            s = lax.dot_general(q_ref[rows, lo:lo + XA_HALF].astype(BF16), half_rows(k_ref, bi, h, 0),
                                nt_dims, preferred_element_type=F32)
            s = s + lax.dot_general(q_ref[rows, lo + XA_HALF:lo + XA_HEAD_DIM].astype(BF16),
                                    half_rows(k_ref, bi, h, 1), nt_dims, preferred_element_type=F32)
            scores.append(s * (XA_HEAD_DIM ** -0.5))
        probs = []
        for s in scores:
            p = jnp.exp(s - jnp.max(s, axis=-1, keepdims=True))
            probs.append((p / jnp.sum(p, axis=-1, keepdims=True)).astype(BF16))
        rows = pl.ds(pl.multiple_of(it * width * seq, width * seq), width * seq)
        for h in range(XA_HEADS):
            for half in range(2):
                lo = h * XA_HEAD_DIM + half * XA_HALF
                out = [jnp.dot(probs[u * XA_HEADS + h], half_rows(v_ref, it * width + u, h, half),
                               preferred_element_type=F32) for u in range(width)]
                o_ref[rows, lo:lo + XA_HALF] = jnp.concatenate(out, axis=0).astype(o_ref.dtype)
        return carry

    lax.fori_loop(0, nb // width, body, 0)


def _cache_rows(cache):
    bs = cache.shape[0]
    c = cache.reshape(bs, MEM_LEN, XA_HEADS, 2, XA_HALF).transpose(0, 1, 3, 2, 4)
    return c.reshape(bs, MEM_LEN * XA_ROWS, XA_HALF)


def _attn_sample(z, mem_k, mem_v, *, batch, seq, nb):
    rows = nb * seq
    kern = functools.partial(_attn_cache_kernel, nb=nb, seq=seq, width=2)
    kv_spec = pl.BlockSpec((nb, MEM_LEN * XA_ROWS, XA_HALF), lambda i: (i, 0, 0))
    return pl.pallas_call(
        kern, grid=(batch // nb,),
        in_specs=[pl.BlockSpec((rows, XA_WIDTH), lambda i: (i, COL_QX // XA_WIDTH)), kv_spec, kv_spec],
        out_specs=pl.BlockSpec((rows, XA_WIDTH), lambda i: (i, 0)),
        out_shape=jax.ShapeDtypeStruct((batch * seq, XA_WIDTH), BF16),
        compiler_params=_params("parallel"),
        name="attn_sample",
    )(z, mem_k, mem_v)


def _merge_kernel(ys_ref, yg_ref, yx_ref, g0_ref, g1_ref, g2_ref, wglu_ref, bglu_ref,
                  w0_ref, w1_ref, w2_ref, o_ref, s5_scr):
    @pl.when(pl.program_id(1) == 0)
    def _():
        y = ys_ref[...]
        lin = jnp.dot(y.astype(BF16), wglu_ref[...], preferred_element_type=F32) + bglu_ref[...]
        s5_scr[...] = (y * jax.nn.sigmoid(lin)).astype(BF16)

    j = pl.program_id(1)
    m = jax.nn.sigmoid(g0_ref[...]) * jnp.dot(s5_scr[...], w0_ref[j], preferred_element_type=F32)
    m = m + jax.nn.sigmoid(g1_ref[...]) * jnp.dot(yg_ref[...], w1_ref[j], preferred_element_type=F32)
    m = m + jax.nn.sigmoid(g2_ref[...]) * jnp.dot(yx_ref[...], w2_ref[j], preferred_element_type=F32)
    o_ref[...] = m.astype(o_ref.dtype)


def _merge(y_s5, y_gla, y_x, z, w_glu, b_glu, w_br_s5, w_br_gla, w_br_x, *, tm, tn):
    t = y_s5.shape[0]
    nj = D_MODEL // tn
    once = pl.Buffered(1)
    wide = pl.BlockSpec((tm, S5_WIDTH), lambda i, j: (i, 0))
    gate = lambda b: pl.BlockSpec((tm, tn), lambda i, j: (i, (COL_GATE + b * D_MODEL) // tn + j))
    w_br = pl.BlockSpec((nj, S5_WIDTH, tn), lambda i, j: (0, 0, 0), pipeline_mode=once)
    return pl.pallas_call(
        _merge_kernel, grid=(t // tm, nj),
        in_specs=[wide, wide, wide, gate(0), gate(1), gate(2),
                  pl.BlockSpec((S5_WIDTH, S5_WIDTH), lambda i, j: (0, 0), pipeline_mode=once),
                  pl.BlockSpec((1, S5_WIDTH), lambda i, j: (0, 0)),
                  w_br, w_br, w_br],
        out_specs=pl.BlockSpec((tm, tn), lambda i, j: (i, j)),
        out_shape=jax.ShapeDtypeStruct((t, D_MODEL), BF16),
        scratch_shapes=[pltpu.VMEM((tm, S5_WIDTH), BF16)],
        compiler_params=_params("parallel", "arbitrary"),
        name="merge",
    )(y_s5, y_gla, y_x, z, z, z, w_glu, b_glu, w_br_s5, w_br_gla, w_br_x)


def _out_ffn_kernel(x_ref, m_ref, wo_ref, gf_ref, wg_ref, wu_ref, wd_ref, gl_ref, o_ref, h_scr, acc_scr):
    k = pl.program_id(1)

    @pl.when(k == 0)
    def _():
        acc_scr[...] = x_ref[...] + jnp.dot(m_ref[...], wo_ref[...], preferred_element_type=F32)
        h_scr[...] = _rms(acc_scr[...], gf_ref[...]).astype(BF16)

    h = h_scr[...]
    gate = jnp.dot(h, wg_ref[...], preferred_element_type=F32)
    up = jnp.dot(h, wu_ref[...], preferred_element_type=F32)
    act = (gate * jax.nn.sigmoid(gate) * up).astype(BF16)
    acc_scr[...] += jnp.dot(act, wd_ref[...], preferred_element_type=F32)

    @pl.when(k == pl.num_programs(1) - 1)
    def _():
        o_ref[...] = _rms(acc_scr[...], gl_ref[...])


def _out_ffn(x, merged, w_out, g_ffn, w_gate, w_up, w_down, g_final, *, tm, th):
    t = x.shape[0]
    row = pl.BlockSpec((tm, D_MODEL), lambda i, k: (i, 0))
    vec = pl.BlockSpec((1, D_MODEL), lambda i, k: (0, 0))
    w_in = pl.BlockSpec((D_MODEL, th), lambda i, k: (0, k))
    return pl.pallas_call(
        _out_ffn_kernel, grid=(t // tm, FFN_HIDDEN // th),
        in_specs=[row, row,
                  pl.BlockSpec((D_MODEL, D_MODEL), lambda i, k: (0, 0), pipeline_mode=pl.Buffered(1)),
                  vec, w_in, w_in, pl.BlockSpec((th, D_MODEL), lambda i, k: (k, 0)), vec],
        out_specs=row,
        out_shape=jax.ShapeDtypeStruct((t, D_MODEL), F32),
        scratch_shapes=[pltpu.VMEM((tm, D_MODEL), BF16), pltpu.VMEM((tm, D_MODEL), F32)],
        compiler_params=pltpu.CompilerParams(dimension_semantics=("parallel", "arbitrary"),
                                             vmem_limit_bytes=OUT_FFN_VMEM),
        name="out_ffn",
    )(x, merged, w_out, g_ffn, w_gate, w_up, w_down, g_final)


def _reorder_in_proj_kernel(wt_ref, wa_ref, main_ref, alow_ref):
    main_ref[...] = wt_ref[...].T.astype(BF16)

    @pl.when(pl.program_id(0) == 0)
    def _():
        lane = lax.broadcasted_iota(jnp.int32, alow_ref.shape, 1)
        alow_ref[...] = jnp.where(lane < GLA_RANK, wa_ref[...].T, 0.0).astype(BF16)


def _reorder_in_proj(w_in):
    d, n = w_in.shape
    wt = w_in.T
    tc = 512
    per_tile = IN_TN // tc
    n_first, n_gate = COL_GATE // tc, (COL_QX - COL_GATE) // tc
    gate_start, qx_start = 4096 + GLA_RANK + XA_WIDTH, 4096 + GLA_RANK

    def src_row(j):
        t8, g8, q8 = tc // SUBLANES, gate_start // SUBLANES, qx_start // SUBLANES
        r8 = jnp.where(j < n_first, j * t8,
                       jnp.where(j < n_first + n_gate, g8 + (j - n_first) * t8,
                                 q8 + (j - n_first - n_gate) * t8))
        return r8 * SUBLANES

    return pl.pallas_call(
        _reorder_in_proj_kernel, grid=(Z_WIDTH // tc,),
        in_specs=[pl.BlockSpec((pl.Element(tc), pl.Element(d)), lambda j: (src_row(j), 0)),
                  pl.BlockSpec((pl.Element(LANES), pl.Element(d)), lambda j: (COL_GATE, 0))],
        out_specs=[pl.BlockSpec((None, d, tc), lambda j: (j // per_tile, 0, j % per_tile)),
                   pl.BlockSpec((d, LANES), lambda j: (0, 0))],
        out_shape=[jax.ShapeDtypeStruct((Z_WIDTH // IN_TN, d, IN_TN), BF16),
                   jax.ShapeDtypeStruct((d, LANES), BF16)],
        compiler_params=_params("arbitrary"),
        name="reorder_in_proj",
    )(wt, wt)


def _col_tiles(w, tn):
    k, n = w.shape
    return w.reshape(k, n // tn, tn).transpose(1, 0, 2)


def _layer(x, mem_k, mem_v, s5_re0, s5_im0, gla_s0, w, *, batch, seq, segmented):
    z, alow = _norm_matmul(x, w['norm_mix'], w['w_main'], w['w_alow'], tm=1024)
    y_s5, hf_re, hf_im = _s5_branch(z, w['s5'], s5_re0, s5_im0, batch=batch, seq=seq)
    y_gla, gla_s = _gla(z, alow, w['gla_w_a2'], w['gla_b_a'], w['gla_norm'], gla_s0,
                        batch=batch, seq=seq, chained=segmented)
    if segmented:
        y_x = _attn_prompt(z, mem_k, mem_v, batch=batch, seq=seq, rc=1024)
    else:
        y_x = _attn_sample(z, mem_k, mem_v, batch=batch, seq=seq, nb=8)
    merged = _merge(y_s5, y_gla, y_x, z, w['s5_w_glu'], w['s5_b_glu'],
                    w['w_br_s5'], w['w_br_gla'], w['w_br_xattn'], tm=512, tn=MERGE_TN)
    y = _out_ffn(x, merged, w['w_out'], w['norm_ffn'], w['w_ffn_gate'], w['w_ffn_up'], w['w_ffn_down'],
                 w['norm_final'], tm=512, th=FFN_TH)
    return y, hf_re, hf_im, gla_s


def kernel(x_prompt, x_sample, mem_prompt, state_s5_re, state_s5_im, state_gla, cache_mem_k, cache_mem_v,
           norm_mix, w_in, s5_lam_re, s5_lam_im, s5_log_dt, s5_b_re, s5_b_im, s5_c_re, s5_c_im,
           s5_d, s5_w_glu, s5_b_glu, gla_w_a2, gla_b_a, gla_norm, mem_norm, w_mem_k, w_mem_v,
           w_br_s5, w_br_gla, w_br_xattn, w_out, norm_ffn, w_ffn_gate, w_ffn_up, w_ffn_down, norm_final):
    depth = w_in.shape[0]
    assert depth == 1
    bp, sp, d = x_prompt.shape
    bs, ss, _ = x_sample.shape
    n_state = S5_GROUPS * S5_STATE
    row = lambda v: v.reshape(1, -1)

    l = 0
    w_main, w_alow = _reorder_in_proj(w_in[l])
    s5_w = _s5_params(s5_lam_re[l], s5_lam_im[l], s5_log_dt[l], s5_b_re[l], s5_b_im[l],
                      s5_c_re[l], s5_c_im[l], s5_d[l])
    w = {
        'norm_mix': row(norm_mix[l]), 'w_main': w_main, 'w_alow': w_alow,
        's5': s5_w,
        's5_w_glu': s5_w_glu[l].astype(BF16), 's5_b_glu': row(s5_b_glu[l]),
        'gla_w_a2': jnp.pad(gla_w_a2[l], ((0, LANES - GLA_RANK), (0, 0))).astype(BF16),
        'gla_b_a': row(gla_b_a[l]), 'gla_norm': row(gla_norm[l]),
        'w_br_s5': _col_tiles(w_br_s5[l].astype(BF16), MERGE_TN),
        'w_br_gla': _col_tiles(w_br_gla[l].astype(BF16), MERGE_TN),
        'w_br_xattn': _col_tiles(w_br_xattn[l].astype(BF16), MERGE_TN),
        'w_out': w_out[l].astype(BF16),
        'norm_ffn': row(norm_ffn[l]), 'w_ffn_gate': w_ffn_gate[l].astype(BF16),
        'w_ffn_up': w_ffn_up[l].astype(BF16), 'w_ffn_down': w_ffn_down[l].astype(BF16),
        'norm_final': row(norm_final),
    }

    w_mem = jnp.stack([w_mem_k[l], w_mem_v[l]]).astype(BF16)
    mem_kv = _norm_matmul(mem_prompt.reshape(bp * MEM_LEN, d), row(mem_norm[l]), w_mem, tm=512)
    mk = mem_kv[:, :XA_WIDTH].reshape(bp, MEM_LEN, XA_WIDTH)
    mv = mem_kv[:, XA_WIDTH:].reshape(bp, MEM_LEN, XA_WIDTH)
    zero_s5 = jnp.zeros((bp, n_state), F32)
    zero_gla = jnp.zeros((bp, GLA_HEADS, GLA_DK, GLA_DV), F32)
    yp, p_re, p_im, p_gla = _layer(x_prompt.reshape(bp * sp, d), mk, mv, zero_s5, zero_s5, zero_gla, w,
                                   batch=bp, seq=sp, segmented=True)

    ys, s_re, s_im, s_gla = _layer(x_sample.reshape(bs * ss, d),
                                   _cache_rows(cache_mem_k[l]), _cache_rows(cache_mem_v[l]),
                                   state_s5_re[l].reshape(bs, n_state), state_s5_im[l].reshape(bs, n_state),
                                   state_gla[l], w, batch=bs, seq=ss, segmented=False)

    s5_shape_p = (1, bp, S5_GROUPS, S5_STATE)
    s5_shape_s = (1, bs, S5_GROUPS, S5_STATE)
    kv_shape = (1, bp, MEM_LEN, XA_HEADS, XA_HEAD_DIM)
    return (yp.reshape(bp, sp, d), ys.reshape(bs, ss, d),
            p_re.reshape(s5_shape_p), p_im.reshape(s5_shape_p), p_gla[None],
            mk.reshape(kv_shape), mv.reshape(kv_shape),
            s_re.reshape(s5_shape_s), s_im.reshape(s5_shape_s), s_gla[None])
```

```python
import functools
import math
from typing import NamedTuple, Optional

import jax
import jax.numpy as jnp
from jax import lax
from jax.experimental import pallas as pl
from jax.experimental.pallas import tpu as pltpu

F32 = jnp.float32
BF16 = jnp.bfloat16

D_MODEL = 2048
S5_WIDTH = 1024
S5_GROUP = 16
S5_GROUPS = 64
S5_STATE = 64
GLA_HEADS = 4
GLA_DK = 128
GLA_DV = 256
GLA_RANK = 16
GLA_TAU = 16.0
GLA_CHUNK = 64
XA_HEADS = 4
XA_HEAD_DIM = 256
XA_WIDTH = 1024
MEM_LEN = 256
FFN_HIDDEN = 5632
RMS_EPS = 1e-6

LANES = 128
SUBLANES = 8
VMEM_LIMIT = 56 * 1024 * 1024
OUT_FFN_VMEM = 52 * 1024 * 1024

COL_U = 0
COL_Q = 1024
COL_K = 1536
COL_V = 2048
COL_R = 3072
COL_GATE = 4096
COL_QX = 10240
Z_WIDTH = 11264

S5_GB = 8
S5_CH = S5_GB * S5_STATE
S5_NBLK = S5_GROUPS // S5_GB
S5_TC = 16
GLA_TILE = 64
MERGE_TN = 1024
IN_TN = 1024
FFN_TH = 512


def _params(*sem):
    return pltpu.CompilerParams(dimension_semantics=sem, vmem_limit_bytes=VMEM_LIMIT)


def _rms(x, g):
    return x * lax.rsqrt(jnp.mean(x * x, axis=-1, keepdims=True) + RMS_EPS) * g


class _CastJob(NamedTuple):
    src: jax.Array
    row_axis: int
    col_tile: Optional[int] = None

    def specs(self, grid):
        rows, cols = self.src.shape
        ra, ca = self.row_axis, 1 - self.row_axis
        assert rows % grid[ra] == 0 and cols % grid[ca] == 0
        br, bc = rows // grid[ra], cols // grid[ca]
        in_spec = pl.BlockSpec((br, bc), lambda *g: (g[ra], g[ca]))
        if self.col_tile is None:
            return in_spec, in_spec, jax.ShapeDtypeStruct((rows, cols), BF16)
        assert self.col_tile % bc == 0 and cols % self.col_tile == 0
        per = self.col_tile // bc
        out_spec = pl.BlockSpec((None, br, bc), lambda *g: (g[ca] // per, g[ra], g[ca] % per))
        return in_spec, out_spec, jax.ShapeDtypeStruct((cols // self.col_tile, rows, self.col_tile), BF16)


def _with_casts(kernel, n_in, n_out, n_cast):
    def wrapped(*refs):
        ins, rest = refs[:n_in], refs[n_in:]
        srcs, rest = rest[:n_cast], rest[n_cast:]
        outs, rest = rest[:n_out], rest[n_out:]
        dsts, scratch = rest[:n_cast], rest[n_cast:]
        for src, dst in zip(srcs, dsts):
            dst[...] = src[...].astype(dst.dtype)
        kernel(*ins, *outs, *scratch)
    return wrapped


def _call_with_casts(kernel, casts, *, grid, in_specs, out_specs, out_shape, operands, **kwargs):
    casts = casts or {}
    specs = [job.specs(grid) for job in casts.values()]
    results = pl.pallas_call(
        _with_casts(kernel, len(in_specs), len(out_specs), len(specs)), grid=grid,
        in_specs=list(in_specs) + [s[0] for s in specs],
        out_specs=list(out_specs) + [s[1] for s in specs],
        out_shape=list(out_shape) + [s[2] for s in specs],
        **kwargs,
    )(*operands, *[job.src for job in casts.values()])
    n = len(out_specs)
    return results[:n], dict(zip(casts.keys(), results[n:]))


def _norm_matmul_kernel(x_ref, g_ref, w_ref, o_ref, h_ref):
    @pl.when(pl.program_id(1) == 0)
    def _():
        h_ref[...] = _rms(x_ref[...], g_ref[...]).astype(BF16)

    o_ref[...] = jnp.dot(h_ref[...], w_ref[0], preferred_element_type=F32)


def _norm_matmul2_kernel(x_ref, g_ref, w_ref, ws_ref, o_ref, os_ref, h_ref):
    @pl.when(pl.program_id(1) == 0)
    def _():
        h = _rms(x_ref[...], g_ref[...]).astype(BF16)
        h_ref[...] = h
        os_ref[...] = jnp.dot(h, ws_ref[...], preferred_element_type=F32)

    o_ref[...] = jnp.dot(h_ref[...], w_ref[0], preferred_element_type=F32)


def _norm_matmul(x, g, w, w_small=None, casts=None, *, tm):
    t, d = x.shape
    nj, _, tn = w.shape
    n = nj * tn
    grid = (t // tm, n // tn)
    in_specs = [pl.BlockSpec((tm, d), lambda i, j: (i, 0)),
                pl.BlockSpec((1, d), lambda i, j: (0, 0)),
                pl.BlockSpec((1, d, tn), lambda i, j: (j, 0, 0))]
    out_specs = pl.BlockSpec((tm, tn), lambda i, j: (i, j))
    out_shape = jax.ShapeDtypeStruct((t, n), F32)
    scratch = [pltpu.VMEM((tm, d), BF16)]
    if w_small is None:
        return pl.pallas_call(_norm_matmul_kernel, grid=grid, in_specs=in_specs, out_specs=out_specs,
                              out_shape=out_shape, scratch_shapes=scratch,
                              compiler_params=_params("parallel", "arbitrary"),
                              name="norm_matmul")(x, g, w)
    ns = w_small.shape[1]
    in_specs.append(pl.BlockSpec((d, ns), lambda i, j: (0, 0)))
    (z, narrow), made = _call_with_casts(
        _norm_matmul2_kernel, casts, grid=grid, in_specs=in_specs,
        out_specs=[out_specs, pl.BlockSpec((tm, ns), lambda i, j: (i, 0))],
        out_shape=[out_shape, jax.ShapeDtypeStruct((t, ns), F32)],
        operands=(x, g, w, w_small), scratch_shapes=scratch,
        compiler_params=_params("parallel", "arbitrary"), name="in_proj")
    return z, narrow, made


def _s5_param_kernel(lr_ref, li_ref, ldt_ref, btr_ref, bti_ref, cr_ref, ci_ref,
                     w16_ref, cc16_ref, w8_ref, cc8_ref, ap_ref):
    p = S5_STATE
    half = S5_TC // 2
    lam_re, lam_im = lr_ref[0], li_ref[0]
    dt = jnp.exp(ldt_ref[0])
    mag = jnp.exp(lam_re * dt)
    a_re = mag * jnp.cos(lam_im * dt)
    a_im = mag * jnp.sin(lam_im * dt)
    den = lam_re * lam_re + lam_im * lam_im
    coef_re = ((a_re - 1.0) * lam_re + a_im * lam_im) / den
    coef_im = (a_im * lam_re - (a_re - 1.0) * lam_im) / den
    bt_re, bt_im = btr_ref[0], bti_ref[0]
    bb_re = coef_re * bt_re - coef_im * bt_im
    bb_im = coef_re * bt_im + coef_im * bt_re
    c_re, c_im = cr_ref[0], ci_ref[0]

    pw_re, pw_im = jnp.ones_like(a_re), jnp.zeros_like(a_re)
    ca_re, ca_im, ab_re, ab_im, powers = [], [], [], [], {}
    for j in range(S5_TC + 1):
        powers[j] = (pw_re, pw_im)
        ca_re.append(c_re * pw_re - c_im * pw_im)
        ca_im.append(c_re * pw_im + c_im * pw_re)
        ab_re.append(bb_re * pw_re - bb_im * pw_im)
        ab_im.append(bb_re * pw_im + bb_im * pw_re)
        pw_re, pw_im = pw_re * a_re - pw_im * a_im, pw_re * a_im + pw_im * a_re
    lag_re = jnp.concatenate(ca_re[:S5_TC], axis=0)
    lag_im = jnp.concatenate(ca_im[:S5_TC], axis=0)
    end_re = jnp.concatenate([ab_re[S5_TC - 1 - s] for s in range(S5_TC)], axis=0)
    end_im = jnp.concatenate([ab_im[S5_TC - 1 - s] for s in range(S5_TC)], axis=0)
    car_re = jnp.concatenate(ca_re[1:], axis=0).T
    car_im = -jnp.concatenate(ca_im[1:], axis=0).T

    def pair_blocks(x):
        first = lax.broadcasted_iota(jnp.int32, x.shape, 0) < p
        return jnp.concatenate([jnp.where(first, x, 0.0), jnp.where(first, 0.0, x)], axis=1)

    cc16_ref[0] = jnp.concatenate([pair_blocks(car_re), pair_blocks(car_im)], axis=0).astype(BF16)
    n8 = half * S5_GROUP
    cc8_ref[0] = jnp.concatenate([pair_blocks(car_re[:, :n8]), pair_blocks(car_im[:, :n8])], axis=0).astype(BF16)

    nt_dims = (((1,), (1,)), ((), ()))
    hi = lax.Precision.HIGHEST
    lane = lax.broadcasted_iota(jnp.int32, (S5_GROUP, LANES), 1)
    end_lane = lax.broadcasted_iota(jnp.int32, end_re.shape, 1)
    for gi in range(2):
        mine = (lane < p) if gi == 0 else (lane >= p)
        own_re, own_im = jnp.where(mine, bb_re, 0.0), jnp.where(mine, bb_im, 0.0)
        strip = (lax.dot_general(own_re, lag_re, nt_dims, precision=hi, preferred_element_type=F32)
                 - lax.dot_general(own_im, lag_im, nt_dims, precision=hi, preferred_element_type=F32))
        lo, up = strip[:, :LANES], strip[:, LANES:]
        blocks = [strip]
        for s in range(1, S5_TC):
            sh = (s % half) * S5_GROUP
            lo_r = pltpu.roll(lo, sh, 1) if sh else lo
            up_r = pltpu.roll(up, sh, 1) if sh else up
            if s < half:
                blocks.append(jnp.concatenate([jnp.where(lane >= sh, lo_r, 0.0),
                                               jnp.where(lane >= sh, up_r, lo_r)], axis=1))
            else:
                blocks.append(jnp.concatenate([jnp.zeros_like(lo), jnp.where(lane >= sh, lo_r, 0.0)], axis=1))
        toeplitz = jnp.concatenate(blocks, axis=0)
        own_end = (end_lane < p) if gi == 0 else (end_lane >= p)
        ends = jnp.concatenate([jnp.where(own_end, end_re, 0.0), jnp.where(own_end, end_im, 0.0)], axis=1)
        w16_ref[gi] = jnp.concatenate([toeplitz, ends], axis=1).astype(BF16)
        w8_ref[gi] = jnp.concatenate([toeplitz[:n8, :n8], ends[n8:, :]], axis=1).astype(BF16)
    ap_ref[0] = jnp.concatenate([powers[S5_TC][0], powers[S5_TC][1], powers[half][0], powers[half][1]], axis=0)


def _s5_params(lam_re, lam_im, log_dt, b_re, b_im, c_re, c_im, d_skip):
    g, p, c = S5_GROUPS, S5_STATE, S5_GROUP
    npair = g // 2
    w16, w8 = S5_TC * c, S5_TC // 2 * c

    def pair_lanes(x):
        return x.reshape(npair, 2, x.shape[1], p).transpose(0, 2, 1, 3).reshape(npair, x.shape[1], 2 * p)

    row = lambda x: pair_lanes(x.reshape(g, 1, p))
    blk = lambda *shape: pl.BlockSpec((1,) + shape, lambda i: (i, 0, 0))
    two = lambda *shape: pl.BlockSpec((2,) + shape, lambda i: (i, 0, 0))
    wt16, cc16, wt8, cc8, apow = pl.pallas_call(
        _s5_param_kernel, grid=(npair,),
        in_specs=[blk(1, 2 * p)] * 3 + [blk(c, 2 * p)] * 4,
        out_specs=[two(w16, w16 + 4 * p), blk(4 * p, 2 * w16), two(w8, w8 + 4 * p), blk(4 * p, 2 * w8),
                   blk(4, 2 * p)],
        out_shape=[jax.ShapeDtypeStruct((g, w16, w16 + 4 * p), BF16),
                   jax.ShapeDtypeStruct((npair, 4 * p, 2 * w16), BF16),
                   jax.ShapeDtypeStruct((g, w8, w8 + 4 * p), BF16),
                   jax.ShapeDtypeStruct((npair, 4 * p, 2 * w8), BF16),
                   jax.ShapeDtypeStruct((npair, 4, 2 * p), F32)],
        compiler_params=_params("parallel"),
        name="s5_params",
    )(row(lam_re), row(lam_im), row(jnp.broadcast_to(log_dt[:, None], (g, p))),
      pair_lanes(b_re.transpose(0, 2, 1)), pair_lanes(b_im.transpose(0, 2, 1)),
      pair_lanes(c_re), pair_lanes(c_im))
    d = d_skip.reshape(S5_NBLK, 1, S5_GB * c)

    def transition(r):
        return jnp.concatenate([apow[:, r].reshape(S5_NBLK, 1, S5_CH), apow[:, r + 1].reshape(S5_NBLK, 1, S5_CH)],
                               axis=-1)

    return {S5_TC: (wt16, cc16, transition(0), d), S5_TC // 2: (wt8, cc8, transition(2), d)}


def _unit_transpose(vs):
    unit = lax.broadcasted_iota(jnp.int32, vs[0].shape, 1) >> int(math.log2(S5_GROUP))
    for dist in (4, 2, 1):
        keep = (unit & dist) == 0
        nxt = list(vs)
        for i in range(8):
            if i & dist == 0:
                a, b = vs[i], vs[i + dist]
                nxt[i] = jnp.where(keep, a, pltpu.roll(b, dist * S5_GROUP, 1))
                nxt[i + dist] = jnp.where(keep, pltpu.roll(a, LANES - dist * S5_GROUP, 1), b)
        vs = nxt
    return vs


def _s5_chunk_kernel(z_ref, w_ref, cc_ref, a_ref, d_ref, h0r_ref, h0i_ref,
                     y_ref, hfr_ref, hfi_ref, ut_ref, ug_ref, yg_ref, e_ref, *, nsl, rps, tc, nseg):
    uw = tc * S5_GROUP
    nq = uw // LANES
    rows = nsl * rps
    nk = rps if nsl > 1 else 1
    ns = rows // nk
    npair = S5_GB // 2
    rchunk = min(rows, 64)

    for s in range(nsl):
        for t in range(tc):
            ut_ref[t, s * rps:(s + 1) * rps, :] = z_ref[pl.ds(s * rps * tc + t, rps, stride=tc), :]

    def row_chunk(rc):
        return pl.ds(pl.multiple_of(rc * rchunk, rchunk), rchunk)

    def to_groups(rc, carry):
        for q in range(nq):
            per_group = _unit_transpose([ut_ref[q * 8 + t, row_chunk(rc), :] for t in range(8)])
            for g in range(S5_GB):
                ug_ref[g, row_chunk(rc), q * LANES:(q + 1) * LANES] = per_group[g].astype(BF16)
        return carry

    lax.fori_loop(0, rows // rchunk, to_groups, 0)

    for pair in range(npair):
        ends = None
        for gi in range(2):
            g = 2 * pair + gi
            em = jnp.dot(ug_ref[g], w_ref[g], preferred_element_type=F32)
            yg_ref[g] = em[:, :uw]
            ends = em[:, uw:] if ends is None else ends + em[:, uw:]
        for part, blk in ((ends[:, :LANES], pair), (ends[:, LANES:], npair + pair)):
            if nk == 1:
                e_ref[blk] = part
            else:
                for s in range(nsl):
                    e_ref[blk, pl.ds(s, rps, stride=nsl), :] = part[s * rps:(s + 1) * rps, :]

    a_row = [(a_ref[0, :, i * LANES:(i + 1) * LANES], a_ref[0, :, S5_CH + i * LANES:S5_CH + (i + 1) * LANES])
             for i in range(npair)]
    a_full = [(jnp.broadcast_to(ar, (ns, LANES)), jnp.broadcast_to(ai, (ns, LANES))) for ar, ai in a_row]

    def slab(t):
        return pl.ds(pl.multiple_of(t * ns, ns), ns)

    def run_scan(h, store):
        def step(t, carry):
            out = []
            for i in range(npair):
                hr, hi = carry[i]
                ar, ai = a_full[i]
                er = e_ref[i, slab(t), :]
                ei = e_ref[npair + i, slab(t), :]
                if store:
                    e_ref[i, slab(t), :] = hr
                    e_ref[npair + i, slab(t), :] = hi
                out.append((ar * hr - ai * hi + er, ar * hi + ai * hr + ei))
            return tuple(out)

        if nk == 1:
            return step(0, h)
        return lax.fori_loop(0, nk, step, h, unroll=4)

    h0 = tuple((h0r_ref[:, i * LANES:(i + 1) * LANES], h0i_ref[:, i * LANES:(i + 1) * LANES]) for i in range(npair))
    if nseg == 1:
        final = run_scan(h0, store=True)
    else:
        zero = jnp.zeros((ns, LANES), F32)
        seg_end = run_scan(tuple((zero, zero) for _ in range(npair)), store=False)
        second = (lax.broadcasted_iota(jnp.int32, (ns, LANES), 0) & 1) == 1
        init, final = [], []
        for i in range(npair):
            p_re, p_im = a_row[i]
            for _ in range(int(math.log2(nk))):
                p_re, p_im = p_re * p_re - p_im * p_im, 2.0 * (p_re * p_im)
            (hr, hi), (er, ei) = h0[i], seg_end[i]
            i_re = jnp.where(second, p_re * hr - p_im * hi + pltpu.roll(er, 1, 0), hr)
            i_im = jnp.where(second, p_re * hi + p_im * hr + pltpu.roll(ei, 1, 0), hi)
            init.append((i_re, i_im))
            final.append((p_re * i_re - p_im * i_im + er, p_re * i_im + p_im * i_re + ei))
        run_scan(tuple(init), store=True)
    for i in range(npair):
        hfr_ref[:, i * LANES:(i + 1) * LANES] = final[i][0]
        hfi_ref[:, i * LANES:(i + 1) * LANES] = final[i][1]

    def chunk_rows(blk):
        if nk == 1:
            return e_ref[blk]
        return jnp.concatenate([e_ref[blk, pl.ds(s, rps, stride=nsl), :] for s in range(nsl)], axis=0)

    for pair in range(npair):
        h_in = jnp.concatenate([chunk_rows(pair), chunk_rows(npair + pair)], axis=1).astype(BF16)
        carried = jnp.dot(h_in, cc_ref[pair], preferred_element_type=F32)
        for gi in range(2):
            yg_ref[2 * pair + gi] += carried[:, gi * uw:(gi + 1) * uw]

    d_row = d_ref[0]

    def to_tokens(rc, carry):
        for q in range(nq):
            per_tau = _unit_transpose([yg_ref[g, row_chunk(rc), q * LANES:(q + 1) * LANES] for g in range(S5_GB)])
            for t in range(8):
                y = per_tau[t] + d_row * ut_ref[q * 8 + t, row_chunk(rc), :]
                ut_ref[q * 8 + t, row_chunk(rc), :] = jax.nn.gelu(y, approximate=True)
        return carry

    lax.fori_loop(0, rows // rchunk, to_tokens, 0)
    for s in range(nsl):
        for t in range(tc):
            y_ref[pl.ds(s * rps * tc + t, rps, stride=tc), :] = ut_ref[t, s * rps:(s + 1) * rps, :]


def _s5_branch(z, weights, h0_re, h0_im, *, batch, seq):
    tokens = batch * seq
    if seq > S5_TC:
        tc, nseg = S5_TC, 2
        nsl, rps = batch * nseg, seq // (nseg * tc)
        assert rps & (rps - 1) == 0
        h0_re, h0_im = jnp.repeat(h0_re, nseg, axis=0), jnp.repeat(h0_im, nseg, axis=0)
    else:
        tc, nseg, nsl, rps = seq, 1, 1, batch
    w, cc, a, d = weights[tc]
    uw = tc * S5_GROUP
    rows = nsl * rps
    ns = h0_re.shape[0]
    kern = functools.partial(_s5_chunk_kernel, nsl=nsl, rps=rps, tc=tc, nseg=nseg)
    state_spec = pl.BlockSpec((ns, S5_CH), lambda k: (0, k))
    state_shape = jax.ShapeDtypeStruct((ns, S5_GROUPS * S5_STATE), F32)
    y, hf_re, hf_im = pl.pallas_call(
        kern, grid=(S5_NBLK,),
        in_specs=[pl.BlockSpec((tokens, LANES), lambda k: (0, COL_U // LANES + k)),
                  pl.BlockSpec((S5_GB, uw, uw + 4 * S5_STATE), lambda k: (k, 0, 0)),
                  pl.BlockSpec((S5_GB // 2, 4 * S5_STATE, 2 * uw), lambda k: (k, 0, 0)),
                  pl.BlockSpec((1, 1, 2 * S5_CH), lambda k: (k, 0, 0)),
                  pl.BlockSpec((1, 1, LANES), lambda k: (k, 0, 0)),
                  state_spec, state_spec],
        out_specs=[pl.BlockSpec((tokens, LANES), lambda k: (0, k)), state_spec, state_spec],
        out_shape=[jax.ShapeDtypeStruct((tokens, S5_WIDTH), F32), state_shape, state_shape],
        scratch_shapes=[pltpu.VMEM((tc, rows, LANES), F32), pltpu.VMEM((S5_GB, rows, uw), BF16),
                        pltpu.VMEM((S5_GB, rows, uw), F32), pltpu.VMEM((S5_GB, rows, LANES), F32)],
        compiler_params=_params("parallel"),
        name="s5_chunked",
    )(z, w, cc, a, d, h0_re, h0_im)
    if nseg == 2:
        hf_re, hf_im = hf_re[1::2], hf_im[1::2]
    return y, hf_re, hf_im


def _gla_kernel(q_ref, k_ref, v_ref, r_ref, al_ref, wa_ref, ba_ref, gn_ref, s0_ref,
                y_ref, sf_ref, s_scr, *, ntile, groups, chained, width):
    rt = GLA_TILE
    c = rt // groups
    shift = int(math.log2(c))
    row_g = lax.broadcasted_iota(jnp.int32, (rt, rt), 0)
    col_g = lax.broadcasted_iota(jnp.int32, (rt, rt), 1)
    same = (row_g >> shift) == (col_g >> shift)
    causal = same & (row_g >= col_g)
    cum_w = jnp.concatenate([causal.astype(BF16), same.astype(BF16)], axis=0)
    tn_dims = (((0,), (0,)), ((), ()))
    nt_dims = (((1,), (1,)), ((), ()))

    def split3(x):
        hi = x.astype(BF16)
        r1 = x - hi.astype(F32)
        mid = r1.astype(BF16)
        lo = (r1 - mid.astype(F32)).astype(BF16)
        return jnp.concatenate([hi, mid, lo], axis=1)

    if chained:
        s_scr[...] = s0_ref[0, 0]

    def body(it, carry):
        tiles = [it * width + u for u in range(width)]
        rows = [pl.ds(pl.multiple_of(t * rt, rt), rt) for t in tiles]
        log_a = []
        for u in range(width):
            x = jnp.dot(al_ref[rows[u], :].astype(BF16), wa_ref[...], preferred_element_type=F32) + ba_ref[...]
            log_a.append((jnp.minimum(x, 0.0) - jnp.log1p(jnp.exp(-jnp.abs(x)))) * (1.0 / GLA_TAU))
        la3 = [split3(la) for la in log_a]
        cums, e_col = [], []
        for u in range(width):
            cs = jnp.dot(cum_w, la3[u], preferred_element_type=F32)
            cums.append(cs[:, :LANES] + cs[:, LANES:2 * LANES] + cs[:, 2 * LANES:])
            e_col.append(jnp.exp(cums[u].T))
        qb, kd, v, att, upd = [], [], [], [], []
        for u in range(width):
            b = cums[u][:rt]
            b_end = cums[u][rt:]
            k = k_ref[rows[u], :]
            v.append(v_ref[rows[u], :])
            mid = 0.5 * b_end
            q_mid = q_ref[rows[u], :] * (GLA_DK ** -0.5) * jnp.exp(b - mid)
            k_mid = k * jnp.exp(mid - b)
            e_mid = jnp.exp(mid)
            qb.append(q_mid * e_mid)
            kd.append(k_mid * e_mid)
            a = lax.dot_general(q_mid.astype(BF16), k_mid.astype(BF16), nt_dims, preferred_element_type=F32)
            att.append(jnp.where(causal, a, 0.0).astype(BF16))
            upd.append([lax.dot_general(kd[u][g * c:(g + 1) * c].astype(BF16), v[u][g * c:(g + 1) * c].astype(BF16),
                                        tn_dims, preferred_element_type=F32) for g in range(groups)])
        o = [jnp.dot(att[u], v[u].astype(BF16), preferred_element_type=F32) for u in range(width)]
        for u in range(width):
            o_state = []
            for g in range(groups):
                s = s_scr[...] if chained else s0_ref[tiles[u] * groups + g, 0]
                o_state.append(jnp.dot(qb[u][g * c:(g + 1) * c].astype(BF16), s.astype(BF16),
                                       preferred_element_type=F32))
                s_new = s * e_col[u][:, rt + g * c:rt + g * c + 1] + upd[u][g]
                if chained:
                    s_scr[...] = s_new
                else:
                    sf_ref[tiles[u] * groups + g, 0] = s_new
            o[u] = o[u] + (o_state[0] if groups == 1 else jnp.concatenate(o_state, axis=0))
        for u in range(width):
            y = o[u] * lax.rsqrt(jnp.mean(o[u] * o[u], axis=-1, keepdims=True) + RMS_EPS)
            y = y * gn_ref[...]
            r = r_ref[rows[u], :]
            y_ref[rows[u], :] = (y * (r * jax.nn.sigmoid(r))).astype(y_ref.dtype)
        return carry

    lax.fori_loop(0, ntile // width, body, 0)
    if chained:
        sf_ref[0, 0] = s_scr[...]


def _gla(z, alow, w_a2, b_a, g_norm, s0, casts=None, *, batch, seq, chained):
    chunk = math.gcd(seq, GLA_CHUNK)
    if chained:
        assert chunk == GLA_TILE
        nb, groups, width = 1, 1, 8
    else:
        assert GLA_TILE % seq == 0 and chunk == seq
        nb, groups, width = 32, GLA_TILE // seq, 4
    rows = nb * seq
    kern = functools.partial(_gla_kernel, ntile=rows // GLA_TILE, groups=groups, chained=chained, width=width)
    state_spec = pl.BlockSpec((nb, 1, GLA_DK, GLA_DV), lambda i, h: (i, h, 0, 0))
    (y, s_fin), made = _call_with_casts(
        kern, casts, grid=(batch // nb, GLA_HEADS),
        in_specs=[pl.BlockSpec((rows, GLA_DK), lambda i, h: (i, COL_Q // GLA_DK + h)),
                  pl.BlockSpec((rows, GLA_DK), lambda i, h: (i, COL_K // GLA_DK + h)),
                  pl.BlockSpec((rows, GLA_DV), lambda i, h: (i, COL_V // GLA_DV + h)),
                  pl.BlockSpec((rows, GLA_DV), lambda i, h: (i, COL_R // GLA_DV + h)),
                  pl.BlockSpec((rows, LANES), lambda i, h: (i, 0)),
                  pl.BlockSpec((LANES, GLA_DK), lambda i, h: (0, h)),
                  pl.BlockSpec((1, GLA_DK), lambda i, h: (0, h)),
                  pl.BlockSpec((1, GLA_DV), lambda i, h: (0, h)),
                  state_spec],
        out_specs=[pl.BlockSpec((rows, GLA_DV), lambda i, h: (i, h)), state_spec],
        out_shape=[jax.ShapeDtypeStruct((batch * seq, GLA_HEADS * GLA_DV), BF16),
                   jax.ShapeDtypeStruct((batch, GLA_HEADS, GLA_DK, GLA_DV), F32)],
        operands=(z, z, z, z, alow, w_a2, b_a, g_norm, s0),
        scratch_shapes=[pltpu.VMEM((GLA_DK, GLA_DV), F32)],
        compiler_params=_params("parallel", "parallel"),
        name="gla")
    return y, s_fin, made


def _attn_kernel(q_ref, k_ref, v_ref, o_ref):
    nt_dims = (((1,), (1,)), ((), ()))
    cols = [slice(h * XA_HEAD_DIM, (h + 1) * XA_HEAD_DIM) for h in range(XA_HEADS)]
    scores = [lax.dot_general(q_ref[:, c].astype(BF16), k_ref[0, :, c].astype(BF16), nt_dims,
                              preferred_element_type=F32) * (XA_HEAD_DIM ** -0.5) for c in cols]
    probs = []
    for s in scores:
        p = jnp.exp(s - jnp.max(s, axis=-1, keepdims=True))
        probs.append((p / jnp.sum(p, axis=-1, keepdims=True)).astype(BF16))
    for c, p in zip(cols, probs):
        o_ref[:, c] = jnp.dot(p, v_ref[0, :, c].astype(BF16), preferred_element_type=F32).astype(o_ref.dtype)


def _attn_prompt(z, mem_k, mem_v, *, batch, seq, rc):
    nrc = seq // rc
    kv_spec = pl.BlockSpec((1, MEM_LEN, XA_WIDTH), lambda b, c: (b, 0, 0))
    return pl.pallas_call(
        _attn_kernel, grid=(batch, nrc),
        in_specs=[pl.BlockSpec((rc, XA_WIDTH), lambda b, c: (b * nrc + c, COL_QX // XA_WIDTH)), kv_spec, kv_spec],
        out_specs=pl.BlockSpec((rc, XA_WIDTH), lambda b, c: (b * nrc + c, 0)),
        out_shape=jax.ShapeDtypeStruct((batch * seq, XA_WIDTH), BF16),
        compiler_params=_params("parallel", "parallel"),
        name="attn_prompt",
    )(z, mem_k, mem_v)


XA_HALF = XA_HEAD_DIM // 2
XA_ROWS = 2 * XA_HEADS


def _attn_cache_kernel(q_ref, k_ref, v_ref, o_ref, *, nb, seq, width):
    nt_dims = (((1,), (1,)), ((), ()))

    def half_rows(ref, bi, h, half):
        return ref[bi, pl.ds(half * XA_HEADS + h, MEM_LEN, stride=XA_ROWS), :].astype(BF16)

    def body(it, carry):
        work = [(it * width + u, h) for u in range(width) for h in range(XA_HEADS)]
        scores = []
        for bi, h in work:
            rows = pl.ds(pl.multiple_of(bi * seq, SUBLANES), seq)
            lo = h * XA_HEAD_DIM
            s = lax.dot_general(q_ref[rows, lo:lo + XA_HALF].astype(BF16), half_rows(k_ref, bi, h, 0),
                                nt_dims, preferred_element_type=F32)
            s = s + lax.dot_general(q_ref[rows, lo + XA_HALF:lo + XA_HEAD_DIM].astype(BF16),
                                    half_rows(k_ref, bi, h, 1), nt_dims, preferred_element_type=F32)
            scores.append(s * (XA_HEAD_DIM ** -0.5))
        probs = []
        for s in scores:
            p = jnp.exp(s - jnp.max(s, axis=-1, keepdims=True))
            probs.append((p / jnp.sum(p, axis=-1, keepdims=True)).astype(BF16))
        rows = pl.ds(pl.multiple_of(it * width * seq, width * seq), width * seq)
        for h in range(XA_HEADS):
            for half in range(2):
                lo = h * XA_HEAD_DIM + half * XA_HALF
                out = [jnp.dot(probs[u * XA_HEADS + h], half_rows(v_ref, it * width + u, h, half),
                               preferred_element_type=F32) for u in range(width)]
                o_ref[rows, lo:lo + XA_HALF] = jnp.concatenate(out, axis=0).astype(o_ref.dtype)
        return carry

    lax.fori_loop(0, nb // width, body, 0)


def _cache_rows(cache):
    bs = cache.shape[0]
    c = cache.reshape(bs, MEM_LEN, XA_HEADS, 2, XA_HALF).transpose(0, 1, 3, 2, 4)
    return c.reshape(bs, MEM_LEN * XA_ROWS, XA_HALF)


def _attn_sample(z, mem_k, mem_v, *, batch, seq, nb):
    rows = nb * seq
    kern = functools.partial(_attn_cache_kernel, nb=nb, seq=seq, width=2)
    kv_spec = pl.BlockSpec((nb, MEM_LEN * XA_ROWS, XA_HALF), lambda i: (i, 0, 0))
    return pl.pallas_call(
        kern, grid=(batch // nb,),
        in_specs=[pl.BlockSpec((rows, XA_WIDTH), lambda i: (i, COL_QX // XA_WIDTH)), kv_spec, kv_spec],
        out_specs=pl.BlockSpec((rows, XA_WIDTH), lambda i: (i, 0)),
        out_shape=jax.ShapeDtypeStruct((batch * seq, XA_WIDTH), BF16),
        compiler_params=_params("parallel"),
        name="attn_sample",
    )(z, mem_k, mem_v)


def _merge_kernel(ys_ref, yg_ref, yx_ref, g0_ref, g1_ref, g2_ref, wglu_ref, bglu_ref,
                  w0_ref, w1_ref, w2_ref, o_ref, s5_scr):
    @pl.when(pl.program_id(1) == 0)
    def _():
        y = ys_ref[...]
        lin = jnp.dot(y.astype(BF16), wglu_ref[...], preferred_element_type=F32) + bglu_ref[...]
        s5_scr[...] = (y * jax.nn.sigmoid(lin)).astype(BF16)

    j = pl.program_id(1)
    m = jax.nn.sigmoid(g0_ref[...]) * jnp.dot(s5_scr[...], w0_ref[j], preferred_element_type=F32)
    m = m + jax.nn.sigmoid(g1_ref[...]) * jnp.dot(yg_ref[...], w1_ref[j], preferred_element_type=F32)
    m = m + jax.nn.sigmoid(g2_ref[...]) * jnp.dot(yx_ref[...], w2_ref[j], preferred_element_type=F32)
    o_ref[...] = m.astype(o_ref.dtype)


def _merge(y_s5, y_gla, y_x, z, w_glu, b_glu, w_br_s5, w_br_gla, w_br_x, *, tm, tn):
    t = y_s5.shape[0]
    nj = D_MODEL // tn
    once = pl.Buffered(1)
    wide = pl.BlockSpec((tm, S5_WIDTH), lambda i, j: (i, 0))
    gate = lambda b: pl.BlockSpec((tm, tn), lambda i, j: (i, (COL_GATE + b * D_MODEL) // tn + j))
    w_br = pl.BlockSpec((nj, S5_WIDTH, tn), lambda i, j: (0, 0, 0), pipeline_mode=once)
    return pl.pallas_call(
        _merge_kernel, grid=(t // tm, nj),
        in_specs=[wide, wide, wide, gate(0), gate(1), gate(2),
                  pl.BlockSpec((S5_WIDTH, S5_WIDTH), lambda i, j: (0, 0), pipeline_mode=once),
                  pl.BlockSpec((1, S5_WIDTH), lambda i, j: (0, 0)),
                  w_br, w_br, w_br],
        out_specs=pl.BlockSpec((tm, tn), lambda i, j: (i, j)),
        out_shape=jax.ShapeDtypeStruct((t, D_MODEL), BF16),
        scratch_shapes=[pltpu.VMEM((tm, S5_WIDTH), BF16)],
        compiler_params=_params("parallel", "arbitrary"),
        name="merge",
    )(y_s5, y_gla, y_x, z, z, z, w_glu, b_glu, w_br_s5, w_br_gla, w_br_x)


def _out_ffn_kernel(x_ref, m_ref, wo_ref, gf_ref, wg_ref, wu_ref, wd_ref, gl_ref, o_ref, h_scr, acc_scr):
    k = pl.program_id(1)

    @pl.when(k == 0)
    def _():
        acc_scr[...] = x_ref[...] + jnp.dot(m_ref[...], wo_ref[...], preferred_element_type=F32)
        h_scr[...] = _rms(acc_scr[...], gf_ref[...]).astype(BF16)

    h = h_scr[...]
    gate = jnp.dot(h, wg_ref[...], preferred_element_type=F32)
    up = jnp.dot(h, wu_ref[...], preferred_element_type=F32)
    act = (gate * jax.nn.sigmoid(gate) * up).astype(BF16)
    acc_scr[...] += jnp.dot(act, wd_ref[...], preferred_element_type=F32)

    @pl.when(k == pl.num_programs(1) - 1)
    def _():
        o_ref[...] = _rms(acc_scr[...], gl_ref[...])


def _out_ffn(x, merged, w_out, g_ffn, w_gate, w_up, w_down, g_final, *, tm, th):
    t = x.shape[0]
    row = pl.BlockSpec((tm, D_MODEL), lambda i, k: (i, 0))
    vec = pl.BlockSpec((1, D_MODEL), lambda i, k: (0, 0))
    w_in = pl.BlockSpec((D_MODEL, th), lambda i, k: (0, k))
    return pl.pallas_call(
        _out_ffn_kernel, grid=(t // tm, FFN_HIDDEN // th),
        in_specs=[row, row,
                  pl.BlockSpec((D_MODEL, D_MODEL), lambda i, k: (0, 0), pipeline_mode=pl.Buffered(1)),
                  vec, w_in, w_in, pl.BlockSpec((th, D_MODEL), lambda i, k: (k, 0)), vec],
        out_specs=row,
        out_shape=jax.ShapeDtypeStruct((t, D_MODEL), F32),
        scratch_shapes=[pltpu.VMEM((tm, D_MODEL), BF16), pltpu.VMEM((tm, D_MODEL), F32)],
        compiler_params=pltpu.CompilerParams(dimension_semantics=("parallel", "arbitrary"),
                                             vmem_limit_bytes=OUT_FFN_VMEM),
        name="out_ffn",
    )(x, merged, w_out, g_ffn, w_gate, w_up, w_down, g_final)


def _reorder_in_proj_kernel(wt_ref, wa_ref, main_ref, alow_ref):
    main_ref[...] = wt_ref[...].T.astype(BF16)

    @pl.when(pl.program_id(0) == 0)
    def _():
        lane = lax.broadcasted_iota(jnp.int32, alow_ref.shape, 1)
        alow_ref[...] = jnp.where(lane < GLA_RANK, wa_ref[...].T, 0.0).astype(BF16)


def _reorder_in_proj(w_in):
    d, n = w_in.shape
    wt = w_in.T
    tc = 512
    per_tile = IN_TN // tc
    n_first, n_gate = COL_GATE // tc, (COL_QX - COL_GATE) // tc
    gate_start, qx_start = 4096 + GLA_RANK + XA_WIDTH, 4096 + GLA_RANK

    def src_row(j):
        t8, g8, q8 = tc // SUBLANES, gate_start // SUBLANES, qx_start // SUBLANES
        r8 = jnp.where(j < n_first, j * t8,
                       jnp.where(j < n_first + n_gate, g8 + (j - n_first) * t8,
                                 q8 + (j - n_first - n_gate) * t8))
        return r8 * SUBLANES

    return pl.pallas_call(
        _reorder_in_proj_kernel, grid=(Z_WIDTH // tc,),
        in_specs=[pl.BlockSpec((pl.Element(tc), pl.Element(d)), lambda j: (src_row(j), 0)),
                  pl.BlockSpec((pl.Element(LANES), pl.Element(d)), lambda j: (COL_GATE, 0))],
        out_specs=[pl.BlockSpec((None, d, tc), lambda j: (j // per_tile, 0, j % per_tile)),
                   pl.BlockSpec((d, LANES), lambda j: (0, 0))],
        out_shape=[jax.ShapeDtypeStruct((Z_WIDTH // IN_TN, d, IN_TN), BF16),
                   jax.ShapeDtypeStruct((d, LANES), BF16)],
        compiler_params=_params("arbitrary"),
        name="reorder_in_proj",
    )(wt, wt)


def _layer(x, mem_k, mem_v, s5_re0, s5_im0, gla_s0, w, *, batch, seq, segmented,
           in_proj_casts=None, gla_casts=None):
    w = dict(w)
    z, alow, made = _norm_matmul(x, w['norm_mix'], w['w_main'], w['w_alow'], in_proj_casts, tm=1024)
    w.update(made)
    y_s5, hf_re, hf_im = _s5_branch(z, w['s5'], s5_re0, s5_im0, batch=batch, seq=seq)
    y_gla, gla_s, made = _gla(z, alow, w['gla_w_a2'], w['gla_b_a'], w['gla_norm'], gla_s0, gla_casts,
                              batch=batch, seq=seq, chained=segmented)
    w.update(made)
    if segmented:
        y_x = _attn_prompt(z, mem_k, mem_v, batch=batch, seq=seq, rc=1024)
    else:
        y_x = _attn_sample(z, mem_k, mem_v, batch=batch, seq=seq, nb=8)
    merged = _merge(y_s5, y_gla, y_x, z, w['s5_w_glu'], w['s5_b_glu'],
                    w['w_br_s5'], w['w_br_gla'], w['w_br_xattn'], tm=512, tn=MERGE_TN)
    y = _out_ffn(x, merged, w['w_out'], w['norm_ffn'], w['w_ffn_gate'], w['w_ffn_up'], w['w_ffn_down'],
                 w['norm_final'], tm=512, th=FFN_TH)
    return y, hf_re, hf_im, gla_s, w


def kernel(x_prompt, x_sample, mem_prompt, state_s5_re, state_s5_im, state_gla, cache_mem_k, cache_mem_v,
           norm_mix, w_in, s5_lam_re, s5_lam_im, s5_log_dt, s5_b_re, s5_b_im, s5_c_re, s5_c_im,
           s5_d, s5_w_glu, s5_b_glu, gla_w_a2, gla_b_a, gla_norm, mem_norm, w_mem_k, w_mem_v,
           w_br_s5, w_br_gla, w_br_xattn, w_out, norm_ffn, w_ffn_gate, w_ffn_up, w_ffn_down, norm_final):
    depth = w_in.shape[0]
    assert depth == 1
    bp, sp, d = x_prompt.shape
    bs, ss, _ = x_sample.shape
    n_state = S5_GROUPS * S5_STATE
    row = lambda v: v.reshape(1, -1)

    l = 0
    w_main, w_alow = _reorder_in_proj(w_in[l])
    s5_w = _s5_params(s5_lam_re[l], s5_lam_im[l], s5_log_dt[l], s5_b_re[l], s5_b_im[l],
                      s5_c_re[l], s5_c_im[l], s5_d[l])
    w = {
        'norm_mix': row(norm_mix[l]), 'w_main': w_main, 'w_alow': w_alow,
        's5': s5_w,
        's5_b_glu': row(s5_b_glu[l]),
        'gla_w_a2': jnp.pad(gla_w_a2[l], ((0, LANES - GLA_RANK), (0, 0))).astype(BF16),
        'gla_b_a': row(gla_b_a[l]), 'gla_norm': row(gla_norm[l]),
        'norm_ffn': row(norm_ffn[l]), 'norm_final': row(norm_final),
    }
    ffn_casts = {'w_ffn_gate': _CastJob(w_ffn_gate[l], row_axis=0), 'w_ffn_up': _CastJob(w_ffn_up[l], row_axis=0),
                 'w_ffn_down': _CastJob(w_ffn_down[l], row_axis=1)}
    mix_casts = {'w_out': _CastJob(w_out[l], row_axis=0), 's5_w_glu': _CastJob(s5_w_glu[l], row_axis=0),
                 'w_br_s5': _CastJob(w_br_s5[l], row_axis=0, col_tile=MERGE_TN),
                 'w_br_gla': _CastJob(w_br_gla[l], row_axis=0, col_tile=MERGE_TN),
                 'w_br_xattn': _CastJob(w_br_xattn[l], row_axis=0, col_tile=MERGE_TN)}

    w_mem = jnp.stack([w_mem_k[l], w_mem_v[l]]).astype(BF16)
    mem_kv = _norm_matmul(mem_prompt.reshape(bp * MEM_LEN, d), row(mem_norm[l]), w_mem, tm=512)
    mk = mem_kv[:, :XA_WIDTH].reshape(bp, MEM_LEN, XA_WIDTH)
    mv = mem_kv[:, XA_WIDTH:].reshape(bp, MEM_LEN, XA_WIDTH)
    zero_s5 = jnp.zeros((bp, n_state), F32)
    zero_gla = jnp.zeros((bp, GLA_HEADS, GLA_DK, GLA_DV), F32)
    yp, p_re, p_im, p_gla, w = _layer(x_prompt.reshape(bp * sp, d), mk, mv, zero_s5, zero_s5, zero_gla, w,
                                      batch=bp, seq=sp, segmented=True,
                                      in_proj_casts=ffn_casts, gla_casts=mix_casts)

    ys, s_re, s_im, s_gla, _ = _layer(x_sample.reshape(bs * ss, d),
                                   _cache_rows(cache_mem_k[l]), _cache_rows(cache_mem_v[l]),
                                   state_s5_re[l].reshape(bs, n_state), state_s5_im[l].reshape(bs, n_state),
                                   state_gla[l], w, batch=bs, seq=ss, segmented=False)

    s5_shape_p = (1, bp, S5_GROUPS, S5_STATE)
    s5_shape_s = (1, bs, S5_GROUPS, S5_STATE)
    kv_shape = (1, bp, MEM_LEN, XA_HEADS, XA_HEAD_DIM)
    return (yp.reshape(bp, sp, d), ys.reshape(bs, ss, d),
            p_re.reshape(s5_shape_p), p_im.reshape(s5_shape_p), p_gla[None],
            mk.reshape(kv_shape), mv.reshape(kv_shape),
            s_re.reshape(s5_shape_s), s_im.reshape(s5_shape_s), s_gla[None])
```

```python
import functools
import math
from typing import Callable, NamedTuple, Optional

import jax
import jax.numpy as jnp
from jax import lax
from jax.experimental import pallas as pl
from jax.experimental.pallas import tpu as pltpu

F32 = jnp.float32
BF16 = jnp.bfloat16

D_MODEL = 2048
S5_WIDTH = 1024
S5_GROUP = 16
S5_GROUPS = 64
S5_STATE = 64
GLA_HEADS = 4
GLA_DK = 128
GLA_DV = 256
GLA_RANK = 16
GLA_TAU = 16.0
GLA_CHUNK = 64
XA_HEADS = 4
XA_HEAD_DIM = 256
XA_WIDTH = 1024
MEM_LEN = 256
FFN_HIDDEN = 5632
RMS_EPS = 1e-6

LANES = 128
SUBLANES = 8
VMEM_LIMIT = 56 * 1024 * 1024
OUT_FFN_VMEM = 52 * 1024 * 1024

COL_U = 0
COL_Q = 1024
COL_K = 1536
COL_V = 2048
COL_R = 3072
COL_GATE = 4096
COL_QX = 10240
Z_WIDTH = 11264

S5_GB = 8
S5_CH = S5_GB * S5_STATE
S5_NBLK = S5_GROUPS // S5_GB
S5_TC = 16
GLA_TILE = 64
MERGE_TN = 1024
IN_TN = 1024
FFN_TH = 512


def _params(*sem):
    return pltpu.CompilerParams(dimension_semantics=sem, vmem_limit_bytes=VMEM_LIMIT)


def _rms(x, g):
    return x * lax.rsqrt(jnp.mean(x * x, axis=-1, keepdims=True) + RMS_EPS) * g


class _CastJob(NamedTuple):
    src: jax.Array
    row_axis: int
    col_tile: Optional[int] = None

    def specs(self, grid):
        rows, cols = self.src.shape
        ra, ca = self.row_axis, 1 - self.row_axis
        assert rows % grid[ra] == 0 and cols % grid[ca] == 0
        br, bc = rows // grid[ra], cols // grid[ca]
        in_spec = pl.BlockSpec((br, bc), lambda *g: (g[ra], g[ca]))
        if self.col_tile is None:
            return in_spec, in_spec, jax.ShapeDtypeStruct((rows, cols), BF16)
        assert self.col_tile % bc == 0 and cols % self.col_tile == 0
        per = self.col_tile // bc
        out_spec = pl.BlockSpec((None, br, bc), lambda *g: (g[ca] // per, g[ra], g[ca] % per))
        return in_spec, out_spec, jax.ShapeDtypeStruct((cols // self.col_tile, rows, self.col_tile), BF16)


class _SideJob(NamedTuple):
    in_specs: list
    out_specs: list
    out_shape: list
    operands: tuple
    body: Callable


def _with_side_work(kernel, n_in, n_out, n_cast, side):
    n_side_in = len(side.in_specs) if side else 0
    n_side_out = len(side.out_specs) if side else 0

    def wrapped(*refs):
        ins, rest = refs[:n_in], refs[n_in:]
        srcs, rest = rest[:n_cast], rest[n_cast:]
        side_ins, rest = rest[:n_side_in], rest[n_side_in:]
        outs, rest = rest[:n_out], rest[n_out:]
        dsts, rest = rest[:n_cast], rest[n_cast:]
        side_outs, scratch = rest[:n_side_out], rest[n_side_out:]
        for src, dst in zip(srcs, dsts):
            dst[...] = src[...].astype(dst.dtype)
        kernel(*ins, *outs, *scratch)
        if side:
            side.body(*side_ins, *side_outs)
    return wrapped


def _call_with_casts(kernel, casts, *, grid, in_specs, out_specs, out_shape, operands, side=None, **kwargs):
    casts = casts or {}
    specs = [job.specs(grid) for job in casts.values()]
    side_in = list(side.in_specs) if side else []
    side_out = list(side.out_specs) if side else []
    results = pl.pallas_call(
        _with_side_work(kernel, len(in_specs), len(out_specs), len(specs), side), grid=grid,
        in_specs=list(in_specs) + [s[0] for s in specs] + side_in,
        out_specs=list(out_specs) + [s[1] for s in specs] + side_out,
        out_shape=list(out_shape) + [s[2] for s in specs] + (list(side.out_shape) if side else []),
        **kwargs,
    )(*operands, *[job.src for job in casts.values()], *(side.operands if side else ()))
    n, c = len(out_specs), len(specs)
    return results[:n], dict(zip(casts.keys(), results[n:n + c])), results[n + c:]


def _norm_matmul_kernel(x_ref, g_ref, w_ref, o_ref, h_ref):
    @pl.when(pl.program_id(1) == 0)
    def _():
        h_ref[...] = _rms(x_ref[...], g_ref[...]).astype(BF16)

    o_ref[...] = jnp.dot(h_ref[...], w_ref[0], preferred_element_type=F32)


def _norm_matmul2_kernel(x_ref, g_ref, w_ref, ws_ref, o_ref, os_ref, h_ref):
    @pl.when(pl.program_id(1) == 0)
    def _():
        h = _rms(x_ref[...], g_ref[...]).astype(BF16)
        h_ref[...] = h
        os_ref[...] = jnp.dot(h, ws_ref[...], preferred_element_type=F32)

    o_ref[...] = jnp.dot(h_ref[...], w_ref[0], preferred_element_type=F32)


def _norm_matmul(x, g, w, w_small=None, casts=None, make_side=None, *, tm):
    t, d = x.shape
    nj, _, tn = w.shape
    n = nj * tn
    grid = (t // tm, n // tn)
    in_specs = [pl.BlockSpec((tm, d), lambda i, j: (i, 0)),
                pl.BlockSpec((1, d), lambda i, j: (0, 0)),
                pl.BlockSpec((1, d, tn), lambda i, j: (j, 0, 0))]
    out_specs = pl.BlockSpec((tm, tn), lambda i, j: (i, j))
    out_shape = jax.ShapeDtypeStruct((t, n), F32)
    scratch = [pltpu.VMEM((tm, d), BF16)]
    if w_small is None:
        return pl.pallas_call(_norm_matmul_kernel, grid=grid, in_specs=in_specs, out_specs=out_specs,
                              out_shape=out_shape, scratch_shapes=scratch,
                              compiler_params=_params("parallel", "arbitrary"),
                              name="norm_matmul")(x, g, w)
    ns = w_small.shape[1]
    in_specs.append(pl.BlockSpec((d, ns), lambda i, j: (0, 0)))
    (z, narrow), made, side_out = _call_with_casts(
        _norm_matmul2_kernel, casts, grid=grid, in_specs=in_specs,
        side=make_side(grid) if make_side else None,
        out_specs=[out_specs, pl.BlockSpec((tm, ns), lambda i, j: (i, 0))],
        out_shape=[out_shape, jax.ShapeDtypeStruct((t, ns), F32)],
        operands=(x, g, w, w_small), scratch_shapes=scratch,
        compiler_params=_params("arbitrary" if make_side else "parallel", "arbitrary"), name="in_proj")
    return z, narrow, made, side_out


def _s5_param_kernel(lr_ref, li_ref, ldt_ref, btr_ref, bti_ref, cr_ref, ci_ref,
                     w16_ref, cc16_ref, w8_ref, cc8_ref, ap_ref):
    p = S5_STATE
    half = S5_TC // 2
    lam_re, lam_im = lr_ref[0], li_ref[0]
    dt = jnp.exp(ldt_ref[0])
    mag = jnp.exp(lam_re * dt)
    a_re = mag * jnp.cos(lam_im * dt)
    a_im = mag * jnp.sin(lam_im * dt)
    den = lam_re * lam_re + lam_im * lam_im
    coef_re = ((a_re - 1.0) * lam_re + a_im * lam_im) / den
    coef_im = (a_im * lam_re - (a_re - 1.0) * lam_im) / den
    bt_re, bt_im = btr_ref[0], bti_ref[0]
    bb_re = coef_re * bt_re - coef_im * bt_im
    bb_im = coef_re * bt_im + coef_im * bt_re
    c_re, c_im = cr_ref[0], ci_ref[0]

    pw_re, pw_im = jnp.ones_like(a_re), jnp.zeros_like(a_re)
    ca_re, ca_im, ab_re, ab_im, powers = [], [], [], [], {}
    for j in range(S5_TC + 1):
        powers[j] = (pw_re, pw_im)
        ca_re.append(c_re * pw_re - c_im * pw_im)
        ca_im.append(c_re * pw_im + c_im * pw_re)
        ab_re.append(bb_re * pw_re - bb_im * pw_im)
        ab_im.append(bb_re * pw_im + bb_im * pw_re)
        pw_re, pw_im = pw_re * a_re - pw_im * a_im, pw_re * a_im + pw_im * a_re
    lag_re = jnp.concatenate(ca_re[:S5_TC], axis=0)
    lag_im = jnp.concatenate(ca_im[:S5_TC], axis=0)
    end_re = jnp.concatenate([ab_re[S5_TC - 1 - s] for s in range(S5_TC)], axis=0)
    end_im = jnp.concatenate([ab_im[S5_TC - 1 - s] for s in range(S5_TC)], axis=0)
    car_re = jnp.concatenate(ca_re[1:], axis=0).T
    car_im = -jnp.concatenate(ca_im[1:], axis=0).T

    def pair_blocks(x):
        first = lax.broadcasted_iota(jnp.int32, x.shape, 0) < p
        return jnp.concatenate([jnp.where(first, x, 0.0), jnp.where(first, 0.0, x)], axis=1)

    cc16_ref[0] = jnp.concatenate([pair_blocks(car_re), pair_blocks(car_im)], axis=0).astype(BF16)
    n8 = half * S5_GROUP
    cc8_ref[0] = jnp.concatenate([pair_blocks(car_re[:, :n8]), pair_blocks(car_im[:, :n8])], axis=0).astype(BF16)

    nt_dims = (((1,), (1,)), ((), ()))
    hi = lax.Precision.HIGHEST
    lane = lax.broadcasted_iota(jnp.int32, (S5_GROUP, LANES), 1)
    end_lane = lax.broadcasted_iota(jnp.int32, end_re.shape, 1)
    for gi in range(2):
        mine = (lane < p) if gi == 0 else (lane >= p)
        own_re, own_im = jnp.where(mine, bb_re, 0.0), jnp.where(mine, bb_im, 0.0)
        strip = (lax.dot_general(own_re, lag_re, nt_dims, precision=hi, preferred_element_type=F32)
                 - lax.dot_general(own_im, lag_im, nt_dims, precision=hi, preferred_element_type=F32))
        lo, up = strip[:, :LANES], strip[:, LANES:]
        blocks = [strip]
        for s in range(1, S5_TC):
            sh = (s % half) * S5_GROUP
            lo_r = pltpu.roll(lo, sh, 1) if sh else lo
            up_r = pltpu.roll(up, sh, 1) if sh else up
            if s < half:
                blocks.append(jnp.concatenate([jnp.where(lane >= sh, lo_r, 0.0),
                                               jnp.where(lane >= sh, up_r, lo_r)], axis=1))
            else:
                blocks.append(jnp.concatenate([jnp.zeros_like(lo), jnp.where(lane >= sh, lo_r, 0.0)], axis=1))
        toeplitz = jnp.concatenate(blocks, axis=0)
        own_end = (end_lane < p) if gi == 0 else (end_lane >= p)
        ends = jnp.concatenate([jnp.where(own_end, end_re, 0.0), jnp.where(own_end, end_im, 0.0)], axis=1)
        w16_ref[gi] = jnp.concatenate([toeplitz, ends], axis=1).astype(BF16)
        w8_ref[gi] = jnp.concatenate([toeplitz[:n8, :n8], ends[n8:, :]], axis=1).astype(BF16)
    ap_ref[0] = jnp.concatenate([powers[S5_TC][0], powers[S5_TC][1], powers[half][0], powers[half][1]], axis=0)


def _s5_params(lam_re, lam_im, log_dt, b_re, b_im, c_re, c_im, d_skip):
    g, p, c = S5_GROUPS, S5_STATE, S5_GROUP
    npair = g // 2
    w16, w8 = S5_TC * c, S5_TC // 2 * c

    def pair_lanes(x):
        return x.reshape(npair, 2, x.shape[1], p).transpose(0, 2, 1, 3).reshape(npair, x.shape[1], 2 * p)

    row = lambda x: pair_lanes(x.reshape(g, 1, p))
    blk = lambda *shape: pl.BlockSpec((1,) + shape, lambda i: (i, 0, 0))
    two = lambda *shape: pl.BlockSpec((2,) + shape, lambda i: (i, 0, 0))
    wt16, cc16, wt8, cc8, apow = pl.pallas_call(
        _s5_param_kernel, grid=(npair,),
        in_specs=[blk(1, 2 * p)] * 3 + [blk(c, 2 * p)] * 4,
        out_specs=[two(w16, w16 + 4 * p), blk(4 * p, 2 * w16), two(w8, w8 + 4 * p), blk(4 * p, 2 * w8),
                   blk(4, 2 * p)],
        out_shape=[jax.ShapeDtypeStruct((g, w16, w16 + 4 * p), BF16),
                   jax.ShapeDtypeStruct((npair, 4 * p, 2 * w16), BF16),
                   jax.ShapeDtypeStruct((g, w8, w8 + 4 * p), BF16),
                   jax.ShapeDtypeStruct((npair, 4 * p, 2 * w8), BF16),
                   jax.ShapeDtypeStruct((npair, 4, 2 * p), F32)],
        compiler_params=_params("parallel"),
        name="s5_params",
    )(row(lam_re), row(lam_im), row(jnp.broadcast_to(log_dt[:, None], (g, p))),
      pair_lanes(b_re.transpose(0, 2, 1)), pair_lanes(b_im.transpose(0, 2, 1)),
      pair_lanes(c_re), pair_lanes(c_im))
    d = d_skip.reshape(S5_NBLK, 1, S5_GB * c)

    def transition(r):
        return jnp.concatenate([apow[:, r].reshape(S5_NBLK, 1, S5_CH), apow[:, r + 1].reshape(S5_NBLK, 1, S5_CH)],
                               axis=-1)

    return {S5_TC: (wt16, cc16, transition(0), d), S5_TC // 2: (wt8, cc8, transition(2), d)}


def _unit_transpose(vs):
    unit = lax.broadcasted_iota(jnp.int32, vs[0].shape, 1) >> int(math.log2(S5_GROUP))
    for dist in (4, 2, 1):
        keep = (unit & dist) == 0
        nxt = list(vs)
        for i in range(8):
            if i & dist == 0:
                a, b = vs[i], vs[i + dist]
                nxt[i] = jnp.where(keep, a, pltpu.roll(b, dist * S5_GROUP, 1))
                nxt[i + dist] = jnp.where(keep, pltpu.roll(a, LANES - dist * S5_GROUP, 1), b)
        vs = nxt
    return vs


def _s5_chunk_kernel(z_ref, w_ref, cc_ref, a_ref, d_ref, h0r_ref, h0i_ref,
                     y_ref, hfr_ref, hfi_ref, ut_ref, ug_ref, yg_ref, e_ref, *, nsl, rps, tc, nseg):
    uw = tc * S5_GROUP
    nq = uw // LANES
    rows = nsl * rps
    nk = rps if nsl > 1 else 1
    ns = rows // nk
    npair = S5_GB // 2
    rchunk = min(rows, 64)

    for s in range(nsl):
        for t in range(tc):
            ut_ref[t, s * rps:(s + 1) * rps, :] = z_ref[pl.ds(s * rps * tc + t, rps, stride=tc), :]

    def row_chunk(rc):
        return pl.ds(pl.multiple_of(rc * rchunk, rchunk), rchunk)

    def to_groups(rc, carry):
        for q in range(nq):
            per_group = _unit_transpose([ut_ref[q * 8 + t, row_chunk(rc), :] for t in range(8)])
            for g in range(S5_GB):
                ug_ref[g, row_chunk(rc), q * LANES:(q + 1) * LANES] = per_group[g].astype(BF16)
        return carry

    lax.fori_loop(0, rows // rchunk, to_groups, 0)

    for pair in range(npair):
        ends = None
        for gi in range(2):
            g = 2 * pair + gi
            em = jnp.dot(ug_ref[g], w_ref[g], preferred_element_type=F32)
            yg_ref[g] = em[:, :uw]
            ends = em[:, uw:] if ends is None else ends + em[:, uw:]
        for part, blk in ((ends[:, :LANES], pair), (ends[:, LANES:], npair + pair)):
            if nk == 1:
                e_ref[blk] = part
            else:
                for s in range(nsl):
                    e_ref[blk, pl.ds(s, rps, stride=nsl), :] = part[s * rps:(s + 1) * rps, :]

    a_row = [(a_ref[0, :, i * LANES:(i + 1) * LANES], a_ref[0, :, S5_CH + i * LANES:S5_CH + (i + 1) * LANES])
             for i in range(npair)]
    a_full = [(jnp.broadcast_to(ar, (ns, LANES)), jnp.broadcast_to(ai, (ns, LANES))) for ar, ai in a_row]

    def slab(t):
        return pl.ds(pl.multiple_of(t * ns, ns), ns)

    def run_scan(h, store):
        def step(t, carry):
            out = []
            for i in range(npair):
                hr, hi = carry[i]
                ar, ai = a_full[i]
                er = e_ref[i, slab(t), :]
                ei = e_ref[npair + i, slab(t), :]
                if store:
                    e_ref[i, slab(t), :] = hr
                    e_ref[npair + i, slab(t), :] = hi
                out.append((ar * hr - ai * hi + er, ar * hi + ai * hr + ei))
            return tuple(out)

        if nk == 1:
            return step(0, h)
        return lax.fori_loop(0, nk, step, h, unroll=4)

    h0 = tuple((h0r_ref[:, i * LANES:(i + 1) * LANES], h0i_ref[:, i * LANES:(i + 1) * LANES]) for i in range(npair))
    if nseg == 1:
        final = run_scan(h0, store=True)
    else:
        zero = jnp.zeros((ns, LANES), F32)
        seg_end = run_scan(tuple((zero, zero) for _ in range(npair)), store=False)
        second = (lax.broadcasted_iota(jnp.int32, (ns, LANES), 0) & 1) == 1
        init, final = [], []
        for i in range(npair):
            p_re, p_im = a_row[i]
            for _ in range(int(math.log2(nk))):
                p_re, p_im = p_re * p_re - p_im * p_im, 2.0 * (p_re * p_im)
            (hr, hi), (er, ei) = h0[i], seg_end[i]
            i_re = jnp.where(second, p_re * hr - p_im * hi + pltpu.roll(er, 1, 0), hr)
            i_im = jnp.where(second, p_re * hi + p_im * hr + pltpu.roll(ei, 1, 0), hi)
            init.append((i_re, i_im))
            final.append((p_re * i_re - p_im * i_im + er, p_re * i_im + p_im * i_re + ei))
        run_scan(tuple(init), store=True)
    for i in range(npair):
        hfr_ref[:, i * LANES:(i + 1) * LANES] = final[i][0]
        hfi_ref[:, i * LANES:(i + 1) * LANES] = final[i][1]

    def chunk_rows(blk):
        if nk == 1:
            return e_ref[blk]
        return jnp.concatenate([e_ref[blk, pl.ds(s, rps, stride=nsl), :] for s in range(nsl)], axis=0)

    for pair in range(npair):
        h_in = jnp.concatenate([chunk_rows(pair), chunk_rows(npair + pair)], axis=1).astype(BF16)
        carried = jnp.dot(h_in, cc_ref[pair], preferred_element_type=F32)
        for gi in range(2):
            yg_ref[2 * pair + gi] += carried[:, gi * uw:(gi + 1) * uw]

    d_row = d_ref[0]

    def to_tokens(rc, carry):
        for q in range(nq):
            per_tau = _unit_transpose([yg_ref[g, row_chunk(rc), q * LANES:(q + 1) * LANES] for g in range(S5_GB)])
            for t in range(8):
                y = per_tau[t] + d_row * ut_ref[q * 8 + t, row_chunk(rc), :]
                ut_ref[q * 8 + t, row_chunk(rc), :] = jax.nn.gelu(y, approximate=True)
        return carry

    lax.fori_loop(0, rows // rchunk, to_tokens, 0)
    for s in range(nsl):
        for t in range(tc):
            y_ref[pl.ds(s * rps * tc + t, rps, stride=tc), :] = ut_ref[t, s * rps:(s + 1) * rps, :]


def _s5_branch(z, weights, h0_re, h0_im, *, batch, seq):
    tokens = batch * seq
    if seq > S5_TC:
        tc, nseg = S5_TC, 2
        nsl, rps = batch * nseg, seq // (nseg * tc)
        assert rps & (rps - 1) == 0
        h0_re, h0_im = jnp.repeat(h0_re, nseg, axis=0), jnp.repeat(h0_im, nseg, axis=0)
    else:
        tc, nseg, nsl, rps = seq, 1, 1, batch
    w, cc, a, d = weights[tc]
    uw = tc * S5_GROUP
    rows = nsl * rps
    ns = h0_re.shape[0]
    kern = functools.partial(_s5_chunk_kernel, nsl=nsl, rps=rps, tc=tc, nseg=nseg)
    state_spec = pl.BlockSpec((ns, S5_CH), lambda k: (0, k))
    state_shape = jax.ShapeDtypeStruct((ns, S5_GROUPS * S5_STATE), F32)
    y, hf_re, hf_im = pl.pallas_call(
        kern, grid=(S5_NBLK,),
        in_specs=[pl.BlockSpec((tokens, LANES), lambda k: (0, COL_U // LANES + k)),
                  pl.BlockSpec((S5_GB, uw, uw + 4 * S5_STATE), lambda k: (k, 0, 0)),
                  pl.BlockSpec((S5_GB // 2, 4 * S5_STATE, 2 * uw), lambda k: (k, 0, 0)),
                  pl.BlockSpec((1, 1, 2 * S5_CH), lambda k: (k, 0, 0)),
                  pl.BlockSpec((1, 1, LANES), lambda k: (k, 0, 0)),
                  state_spec, state_spec],
        out_specs=[pl.BlockSpec((tokens, LANES), lambda k: (0, k)), state_spec, state_spec],
        out_shape=[jax.ShapeDtypeStruct((tokens, S5_WIDTH), F32), state_shape, state_shape],
        scratch_shapes=[pltpu.VMEM((tc, rows, LANES), F32), pltpu.VMEM((S5_GB, rows, uw), BF16),
                        pltpu.VMEM((S5_GB, rows, uw), F32), pltpu.VMEM((S5_GB, rows, LANES), F32)],
        compiler_params=_params("parallel"),
        name="s5_chunked",
    )(z, w, cc, a, d, h0_re, h0_im)
    if nseg == 2:
        hf_re, hf_im = hf_re[1::2], hf_im[1::2]
    return y, hf_re, hf_im


def _gla_kernel(q_ref, k_ref, v_ref, r_ref, al_ref, wa_ref, ba_ref, gn_ref, s0_ref,
                y_ref, sf_ref, s_scr, *, ntile, groups, chained, width):
    rt = GLA_TILE
    c = rt // groups
    shift = int(math.log2(c))
    row_g = lax.broadcasted_iota(jnp.int32, (rt, rt), 0)
    col_g = lax.broadcasted_iota(jnp.int32, (rt, rt), 1)
    same = (row_g >> shift) == (col_g >> shift)
    causal = same & (row_g >= col_g)
    cum_w = jnp.concatenate([causal.astype(BF16), same.astype(BF16)], axis=0)
    tn_dims = (((0,), (0,)), ((), ()))
    nt_dims = (((1,), (1,)), ((), ()))

    def split3(x):
        hi = x.astype(BF16)
        r1 = x - hi.astype(F32)
        mid = r1.astype(BF16)
        lo = (r1 - mid.astype(F32)).astype(BF16)
        return jnp.concatenate([hi, mid, lo], axis=1)

    if chained:
        s_scr[...] = s0_ref[0, 0]

    def body(it, carry):
        tiles = [it * width + u for u in range(width)]
        rows = [pl.ds(pl.multiple_of(t * rt, rt), rt) for t in tiles]
        log_a = []
        for u in range(width):
            x = jnp.dot(al_ref[rows[u], :].astype(BF16), wa_ref[...], preferred_element_type=F32) + ba_ref[...]
            log_a.append((jnp.minimum(x, 0.0) - jnp.log1p(jnp.exp(-jnp.abs(x)))) * (1.0 / GLA_TAU))
        la3 = [split3(la) for la in log_a]
        cums, e_col = [], []
        for u in range(width):
            cs = jnp.dot(cum_w, la3[u], preferred_element_type=F32)
            cums.append(cs[:, :LANES] + cs[:, LANES:2 * LANES] + cs[:, 2 * LANES:])
            e_col.append(jnp.exp(cums[u].T))
        qb, kd, v, att, upd = [], [], [], [], []
        for u in range(width):
            b = cums[u][:rt]
            b_end = cums[u][rt:]
            k = k_ref[rows[u], :]
            v.append(v_ref[rows[u], :])
            mid = 0.5 * b_end
            q_mid = q_ref[rows[u], :] * (GLA_DK ** -0.5) * jnp.exp(b - mid)
            k_mid = k * jnp.exp(mid - b)
            e_mid = jnp.exp(mid)
            qb.append(q_mid * e_mid)
            kd.append(k_mid * e_mid)
            a = lax.dot_general(q_mid.astype(BF16), k_mid.astype(BF16), nt_dims, preferred_element_type=F32)
            att.append(jnp.where(causal, a, 0.0).astype(BF16))
            upd.append([lax.dot_general(kd[u][g * c:(g + 1) * c].astype(BF16), v[u][g * c:(g + 1) * c].astype(BF16),
                                        tn_dims, preferred_element_type=F32) for g in range(groups)])
        o = [jnp.dot(att[u], v[u].astype(BF16), preferred_element_type=F32) for u in range(width)]
        for u in range(width):
            o_state = []
            for g in range(groups):
                s = s_scr[...] if chained else s0_ref[tiles[u] * groups + g, 0]
                o_state.append(jnp.dot(qb[u][g * c:(g + 1) * c].astype(BF16), s.astype(BF16),
                                       preferred_element_type=F32))
                s_new = s * e_col[u][:, rt + g * c:rt + g * c + 1] + upd[u][g]
                if chained:
                    s_scr[...] = s_new
                else:
                    sf_ref[tiles[u] * groups + g, 0] = s_new
            o[u] = o[u] + (o_state[0] if groups == 1 else jnp.concatenate(o_state, axis=0))
        for u in range(width):
            y = o[u] * lax.rsqrt(jnp.mean(o[u] * o[u], axis=-1, keepdims=True) + RMS_EPS)
            y = y * gn_ref[...]
            r = r_ref[rows[u], :]
            y_ref[rows[u], :] = (y * (r * jax.nn.sigmoid(r))).astype(y_ref.dtype)
        return carry

    lax.fori_loop(0, ntile // width, body, 0)
    if chained:
        sf_ref[0, 0] = s_scr[...]


def _gla(z, alow, w_a2, b_a, g_norm, s0, casts=None, *, batch, seq, chained):
    chunk = math.gcd(seq, GLA_CHUNK)
    if chained:
        assert chunk == GLA_TILE
        nb, groups, width = 1, 1, 8
    else:
        assert GLA_TILE % seq == 0 and chunk == seq
        nb, groups, width = 32, GLA_TILE // seq, 4
    rows = nb * seq
    kern = functools.partial(_gla_kernel, ntile=rows // GLA_TILE, groups=groups, chained=chained, width=width)
    state_spec = pl.BlockSpec((nb, 1, GLA_DK, GLA_DV), lambda i, h: (i, h, 0, 0))
    (y, s_fin), made, _ = _call_with_casts(
        kern, casts, grid=(batch // nb, GLA_HEADS),
        in_specs=[pl.BlockSpec((rows, GLA_DK), lambda i, h: (i, COL_Q // GLA_DK + h)),
                  pl.BlockSpec((rows, GLA_DK), lambda i, h: (i, COL_K // GLA_DK + h)),
                  pl.BlockSpec((rows, GLA_DV), lambda i, h: (i, COL_V // GLA_DV + h)),
                  pl.BlockSpec((rows, GLA_DV), lambda i, h: (i, COL_R // GLA_DV + h)),
                  pl.BlockSpec((rows, LANES), lambda i, h: (i, 0)),
                  pl.BlockSpec((LANES, GLA_DK), lambda i, h: (0, h)),
                  pl.BlockSpec((1, GLA_DK), lambda i, h: (0, h)),
                  pl.BlockSpec((1, GLA_DV), lambda i, h: (0, h)),
                  state_spec],
        out_specs=[pl.BlockSpec((rows, GLA_DV), lambda i, h: (i, h)), state_spec],
        out_shape=[jax.ShapeDtypeStruct((batch * seq, GLA_HEADS * GLA_DV), BF16),
                   jax.ShapeDtypeStruct((batch, GLA_HEADS, GLA_DK, GLA_DV), F32)],
        operands=(z, z, z, z, alow, w_a2, b_a, g_norm, s0),
        scratch_shapes=[pltpu.VMEM((GLA_DK, GLA_DV), F32)],
        compiler_params=_params("parallel", "parallel"),
        name="gla")
    return y, s_fin, made


def _attn_kernel(q_ref, k_ref, v_ref, o_ref):
    nt_dims = (((1,), (1,)), ((), ()))
    cols = [slice(h * XA_HEAD_DIM, (h + 1) * XA_HEAD_DIM) for h in range(XA_HEADS)]
    scores = [lax.dot_general(q_ref[:, c].astype(BF16), k_ref[0, :, c].astype(BF16), nt_dims,
                              preferred_element_type=F32) * (XA_HEAD_DIM ** -0.5) for c in cols]
    probs = []
    for s in scores:
        p = jnp.exp(s - jnp.max(s, axis=-1, keepdims=True))
        probs.append((p / jnp.sum(p, axis=-1, keepdims=True)).astype(BF16))
    for c, p in zip(cols, probs):
        o_ref[:, c] = jnp.dot(p, v_ref[0, :, c].astype(BF16), preferred_element_type=F32).astype(o_ref.dtype)


def _attn_prompt(z, mem_k, mem_v, *, batch, seq, rc):
    nrc = seq // rc
    kv_spec = pl.BlockSpec((1, MEM_LEN, XA_WIDTH), lambda b, c: (b, 0, 0))
    return pl.pallas_call(
        _attn_kernel, grid=(batch, nrc),
        in_specs=[pl.BlockSpec((rc, XA_WIDTH), lambda b, c: (b * nrc + c, COL_QX // XA_WIDTH)), kv_spec, kv_spec],
        out_specs=pl.BlockSpec((rc, XA_WIDTH), lambda b, c: (b * nrc + c, 0)),
        out_shape=jax.ShapeDtypeStruct((batch * seq, XA_WIDTH), BF16),
        compiler_params=_params("parallel", "parallel"),
        name="attn_prompt",
    )(z, mem_k, mem_v)


XA_HALF = XA_HEAD_DIM // 2
XA_ROWS = 2 * XA_HEADS


def _attn_cache_body(q_ref, k_ref, v_ref, o_ref, *, nseq, seq, nblk, grid):
    nt_dims = (((1,), (1,)), ((), ()))

    def half_rows(ref, bi, h, half):
        return ref[bi, pl.ds(half * XA_HEADS + h, MEM_LEN, stride=XA_ROWS), :].astype(BF16)

    @pl.when(pl.program_id(0) * grid[1] + pl.program_id(1) < nblk)
    def _():
        scores = []
        for bi in range(nseq):
            rows = slice(bi * seq, (bi + 1) * seq)
            for h in range(XA_HEADS):
                lo = h * XA_HEAD_DIM
                s = lax.dot_general(q_ref[rows, lo:lo + XA_HALF].astype(BF16), half_rows(k_ref, bi, h, 0),
                                    nt_dims, preferred_element_type=F32)
                s = s + lax.dot_general(q_ref[rows, lo + XA_HALF:lo + XA_HEAD_DIM].astype(BF16),
                                        half_rows(k_ref, bi, h, 1), nt_dims, preferred_element_type=F32)
                scores.append(s * (XA_HEAD_DIM ** -0.5))
        probs = []
        for s in scores:
            p = jnp.exp(s - jnp.max(s, axis=-1, keepdims=True))
            probs.append((p / jnp.sum(p, axis=-1, keepdims=True)).astype(BF16))
        for h in range(XA_HEADS):
            for half in range(2):
                lo = h * XA_HEAD_DIM + half * XA_HALF
                out = [jnp.dot(probs[bi * XA_HEADS + h], half_rows(v_ref, bi, h, half),
                               preferred_element_type=F32) for bi in range(nseq)]
                o_ref[:, lo:lo + XA_HALF] = jnp.concatenate(out, axis=0).astype(o_ref.dtype)


def _cache_rows(cache):
    bs = cache.shape[0]
    c = cache.reshape(bs, MEM_LEN, XA_HEADS, 2, XA_HALF).transpose(0, 1, 3, 2, 4)
    return c.reshape(bs, MEM_LEN * XA_ROWS, XA_HALF)


def _attn_sample_side(z, mem_k, mem_v, *, batch, seq, grid, per_step=2):
    nblk = batch // per_step
    assert nblk <= grid[0] * grid[1] and (per_step * seq) % 16 == 0
    rows = per_step * seq

    def blk(*g):
        return jnp.minimum(g[0] * grid[1] + g[1], nblk - 1)

    kv_spec = pl.BlockSpec((per_step, MEM_LEN * XA_ROWS, XA_HALF), lambda *g: (blk(*g), 0, 0))
    return _SideJob(
        in_specs=[pl.BlockSpec((rows, XA_WIDTH), lambda *g: (blk(*g), COL_QX // XA_WIDTH)), kv_spec, kv_spec],
        out_specs=[pl.BlockSpec((rows, XA_WIDTH), lambda *g: (blk(*g), 0))],
        out_shape=[jax.ShapeDtypeStruct((batch * seq, XA_WIDTH), BF16)],
        operands=(z, mem_k, mem_v),
        body=functools.partial(_attn_cache_body, nseq=per_step, seq=seq, nblk=nblk, grid=grid))


def _merge_kernel(ys_ref, yg_ref, yx_ref, g0_ref, g1_ref, g2_ref, wglu_ref, bglu_ref,
                  w0_ref, w1_ref, w2_ref, o_ref, s5_scr):
    @pl.when(pl.program_id(1) == 0)
    def _():
        y = ys_ref[...]
        lin = jnp.dot(y.astype(BF16), wglu_ref[...], preferred_element_type=F32) + bglu_ref[...]
        s5_scr[...] = (y * jax.nn.sigmoid(lin)).astype(BF16)

    j = pl.program_id(1)
    m = jax.nn.sigmoid(g0_ref[...]) * jnp.dot(s5_scr[...], w0_ref[j], preferred_element_type=F32)
    m = m + jax.nn.sigmoid(g1_ref[...]) * jnp.dot(yg_ref[...], w1_ref[j], preferred_element_type=F32)
    m = m + jax.nn.sigmoid(g2_ref[...]) * jnp.dot(yx_ref[...], w2_ref[j], preferred_element_type=F32)
    o_ref[...] = m.astype(o_ref.dtype)


def _merge(y_s5, y_gla, y_x, z, w_glu, b_glu, w_br_s5, w_br_gla, w_br_x, *, tm, tn):
    t = y_s5.shape[0]
    nj = D_MODEL // tn
    once = pl.Buffered(1)
    wide = pl.BlockSpec((tm, S5_WIDTH), lambda i, j: (i, 0))
    gate = lambda b: pl.BlockSpec((tm, tn), lambda i, j: (i, (COL_GATE + b * D_MODEL) // tn + j))
    w_br = pl.BlockSpec((nj, S5_WIDTH, tn), lambda i, j: (0, 0, 0), pipeline_mode=once)
    return pl.pallas_call(
        _merge_kernel, grid=(t // tm, nj),
        in_specs=[wide, wide, wide, gate(0), gate(1), gate(2),
                  pl.BlockSpec((S5_WIDTH, S5_WIDTH), lambda i, j: (0, 0), pipeline_mode=once),
                  pl.BlockSpec((1, S5_WIDTH), lambda i, j: (0, 0)),
                  w_br, w_br, w_br],
        out_specs=pl.BlockSpec((tm, tn), lambda i, j: (i, j)),
        out_shape=jax.ShapeDtypeStruct((t, D_MODEL), BF16),
        scratch_shapes=[pltpu.VMEM((tm, S5_WIDTH), BF16)],
        compiler_params=_params("parallel", "arbitrary"),
        name="merge",
    )(y_s5, y_gla, y_x, z, z, z, w_glu, b_glu, w_br_s5, w_br_gla, w_br_x)


def _out_ffn_kernel(x_ref, m_ref, wo_ref, gf_ref, wg_ref, wu_ref, wd_ref, gl_ref, o_ref, h_scr, acc_scr):
    k = pl.program_id(1)

    @pl.when(k == 0)
    def _():
        acc_scr[...] = x_ref[...] + jnp.dot(m_ref[...], wo_ref[...], preferred_element_type=F32)
        h_scr[...] = _rms(acc_scr[...], gf_ref[...]).astype(BF16)

    h = h_scr[...]
    gate = jnp.dot(h, wg_ref[...], preferred_element_type=F32)
    up = jnp.dot(h, wu_ref[...], preferred_element_type=F32)
    act = (gate * jax.nn.sigmoid(gate) * up).astype(BF16)
    acc_scr[...] += jnp.dot(act, wd_ref[...], preferred_element_type=F32)

    @pl.when(k == pl.num_programs(1) - 1)
    def _():
        o_ref[...] = _rms(acc_scr[...], gl_ref[...])


def _out_ffn(x, merged, w_out, g_ffn, w_gate, w_up, w_down, g_final, *, tm, th):
    t = x.shape[0]
    row = pl.BlockSpec((tm, D_MODEL), lambda i, k: (i, 0))
    vec = pl.BlockSpec((1, D_MODEL), lambda i, k: (0, 0))
    w_in = pl.BlockSpec((D_MODEL, th), lambda i, k: (0, k))
    return pl.pallas_call(
        _out_ffn_kernel, grid=(t // tm, FFN_HIDDEN // th),
        in_specs=[row, row,
                  pl.BlockSpec((D_MODEL, D_MODEL), lambda i, k: (0, 0), pipeline_mode=pl.Buffered(1)),
                  vec, w_in, w_in, pl.BlockSpec((th, D_MODEL), lambda i, k: (k, 0)), vec],
        out_specs=row,
        out_shape=jax.ShapeDtypeStruct((t, D_MODEL), F32),
        scratch_shapes=[pltpu.VMEM((tm, D_MODEL), BF16), pltpu.VMEM((tm, D_MODEL), F32)],
        compiler_params=pltpu.CompilerParams(dimension_semantics=("parallel", "arbitrary"),
                                             vmem_limit_bytes=OUT_FFN_VMEM),
        name="out_ffn",
    )(x, merged, w_out, g_ffn, w_gate, w_up, w_down, g_final)


def _reorder_in_proj_kernel(wt_ref, wa_ref, main_ref, alow_ref):
    main_ref[...] = wt_ref[...].T.astype(BF16)

    @pl.when(pl.program_id(0) == 0)
    def _():
        lane = lax.broadcasted_iota(jnp.int32, alow_ref.shape, 1)
        alow_ref[...] = jnp.where(lane < GLA_RANK, wa_ref[...].T, 0.0).astype(BF16)


def _reorder_in_proj(w_in):
    d, n = w_in.shape
    wt = w_in.T
    tc = 512
    per_tile = IN_TN // tc
    n_first, n_gate = COL_GATE // tc, (COL_QX - COL_GATE) // tc
    gate_start, qx_start = 4096 + GLA_RANK + XA_WIDTH, 4096 + GLA_RANK

    def src_row(j):
        t8, g8, q8 = tc // SUBLANES, gate_start // SUBLANES, qx_start // SUBLANES
        r8 = jnp.where(j < n_first, j * t8,
                       jnp.where(j < n_first + n_gate, g8 + (j - n_first) * t8,
                                 q8 + (j - n_first - n_gate) * t8))
        return r8 * SUBLANES

    return pl.pallas_call(
        _reorder_in_proj_kernel, grid=(Z_WIDTH // tc,),
        in_specs=[pl.BlockSpec((pl.Element(tc), pl.Element(d)), lambda j: (src_row(j), 0)),
                  pl.BlockSpec((pl.Element(LANES), pl.Element(d)), lambda j: (COL_GATE, 0))],
        out_specs=[pl.BlockSpec((None, d, tc), lambda j: (j // per_tile, 0, j % per_tile)),
                   pl.BlockSpec((d, LANES), lambda j: (0, 0))],
        out_shape=[jax.ShapeDtypeStruct((Z_WIDTH // IN_TN, d, IN_TN), BF16),
                   jax.ShapeDtypeStruct((d, LANES), BF16)],
        compiler_params=_params("arbitrary"),
        name="reorder_in_proj",
    )(wt, wt)


def _layer(x, z, alow, y_x, s5_re0, s5_im0, gla_s0, w, *, batch, seq, chained, gla_casts=None):
    w = dict(w)
    y_s5, hf_re, hf_im = _s5_branch(z, w['s5'], s5_re0, s5_im0, batch=batch, seq=seq)
    y_gla, gla_s, made = _gla(z, alow, w['gla_w_a2'], w['gla_b_a'], w['gla_norm'], gla_s0, gla_casts,
                              batch=batch, seq=seq, chained=chained)
    w.update(made)
    merged = _merge(y_s5, y_gla, y_x, z, w['s5_w_glu'], w['s5_b_glu'],
                    w['w_br_s5'], w['w_br_gla'], w['w_br_xattn'], tm=512, tn=MERGE_TN)
    y = _out_ffn(x, merged, w['w_out'], w['norm_ffn'], w['w_ffn_gate'], w['w_ffn_up'], w['w_ffn_down'],
                 w['norm_final'], tm=512, th=FFN_TH)
    return y, hf_re, hf_im, gla_s, w


def kernel(x_prompt, x_sample, mem_prompt, state_s5_re, state_s5_im, state_gla, cache_mem_k, cache_mem_v,
           norm_mix, w_in, s5_lam_re, s5_lam_im, s5_log_dt, s5_b_re, s5_b_im, s5_c_re, s5_c_im,
           s5_d, s5_w_glu, s5_b_glu, gla_w_a2, gla_b_a, gla_norm, mem_norm, w_mem_k, w_mem_v,
           w_br_s5, w_br_gla, w_br_xattn, w_out, norm_ffn, w_ffn_gate, w_ffn_up, w_ffn_down, norm_final):
    depth = w_in.shape[0]
    assert depth == 1
    bp, sp, d = x_prompt.shape
    bs, ss, _ = x_sample.shape
    n_state = S5_GROUPS * S5_STATE
    row = lambda v: v.reshape(1, -1)

    l = 0
    w_main, w_alow = _reorder_in_proj(w_in[l])
    s5_w = _s5_params(s5_lam_re[l], s5_lam_im[l], s5_log_dt[l], s5_b_re[l], s5_b_im[l],
                      s5_c_re[l], s5_c_im[l], s5_d[l])
    w = {
        'norm_mix': row(norm_mix[l]), 'w_main': w_main, 'w_alow': w_alow,
        's5': s5_w,
        's5_b_glu': row(s5_b_glu[l]),
        'gla_w_a2': jnp.pad(gla_w_a2[l], ((0, LANES - GLA_RANK), (0, 0))).astype(BF16),
        'gla_b_a': row(gla_b_a[l]), 'gla_norm': row(gla_norm[l]),
        'norm_ffn': row(norm_ffn[l]), 'norm_final': row(norm_final),
    }
    ffn_casts = {'w_ffn_gate': _CastJob(w_ffn_gate[l], row_axis=0), 'w_ffn_up': _CastJob(w_ffn_up[l], row_axis=0),
                 'w_ffn_down': _CastJob(w_ffn_down[l], row_axis=1)}
    mix_casts = {'w_out': _CastJob(w_out[l], row_axis=0), 's5_w_glu': _CastJob(s5_w_glu[l], row_axis=0),
                 'w_br_s5': _CastJob(w_br_s5[l], row_axis=0, col_tile=MERGE_TN),
                 'w_br_gla': _CastJob(w_br_gla[l], row_axis=0, col_tile=MERGE_TN),
                 'w_br_xattn': _CastJob(w_br_xattn[l], row_axis=0, col_tile=MERGE_TN)}

    w_mem = jnp.stack([w_mem_k[l], w_mem_v[l]]).astype(BF16)
    mem_kv = _norm_matmul(mem_prompt.reshape(bp * MEM_LEN, d), row(mem_norm[l]), w_mem, tm=512)
    mk = mem_kv[:, :XA_WIDTH].reshape(bp, MEM_LEN, XA_WIDTH)
    mv = mem_kv[:, XA_WIDTH:].reshape(bp, MEM_LEN, XA_WIDTH)
    zero_s5 = jnp.zeros((bp, n_state), F32)
    zero_gla = jnp.zeros((bp, GLA_HEADS, GLA_DK, GLA_DV), F32)
    xp, xs = x_prompt.reshape(bp * sp, d), x_sample.reshape(bs * ss, d)

    z_s, alow_s, _, _ = _norm_matmul(xs, w['norm_mix'], w['w_main'], w['w_alow'], tm=1024)
    cache_k, cache_v = _cache_rows(cache_mem_k[l]), _cache_rows(cache_mem_v[l])
    z_p, alow_p, made, (yx_s,) = _norm_matmul(
        xp, w['norm_mix'], w['w_main'], w['w_alow'], ffn_casts,
        lambda grid: _attn_sample_side(z_s, cache_k, cache_v, batch=bs, seq=ss, grid=grid), tm=1024)
    w.update(made)
    yx_p = _attn_prompt(z_p, mk, mv, batch=bp, seq=sp, rc=1024)
    yp, p_re, p_im, p_gla, w = _layer(xp, z_p, alow_p, yx_p, zero_s5, zero_s5, zero_gla, w,
                                      batch=bp, seq=sp, chained=True, gla_casts=mix_casts)

    ys, s_re, s_im, s_gla, _ = _layer(xs, z_s, alow_s, yx_s,
                                      state_s5_re[l].reshape(bs, n_state), state_s5_im[l].reshape(bs, n_state),
                                      state_gla[l], w, batch=bs, seq=ss, chained=False)

    s5_shape_p = (1, bp, S5_GROUPS, S5_STATE)
    s5_shape_s = (1, bs, S5_GROUPS, S5_STATE)
    kv_shape = (1, bp, MEM_LEN, XA_HEADS, XA_HEAD_DIM)
    return (yp.reshape(bp, sp, d), ys.reshape(bs, ss, d),
            p_re.reshape(s5_shape_p), p_im.reshape(s5_shape_p), p_gla[None],
            mk.reshape(kv_shape), mv.reshape(kv_shape),
            s_re.reshape(s5_shape_s), s_im.reshape(s5_shape_s), s_gla[None])
```

```python
import functools
import math
from typing import Callable, NamedTuple, Optional

import jax
import jax.numpy as jnp
from jax import lax
from jax.experimental import pallas as pl
from jax.experimental.pallas import tpu as pltpu

F32 = jnp.float32
BF16 = jnp.bfloat16

D_MODEL = 2048
S5_WIDTH = 1024
S5_GROUP = 16
S5_GROUPS = 64
S5_STATE = 64
GLA_HEADS = 4
GLA_DK = 128
GLA_DV = 256
GLA_RANK = 16
GLA_TAU = 16.0
GLA_CHUNK = 64
XA_HEADS = 4
XA_HEAD_DIM = 256
XA_WIDTH = 1024
MEM_LEN = 256
FFN_HIDDEN = 5632
RMS_EPS = 1e-6

LANES = 128
SUBLANES = 8
VMEM_LIMIT = 56 * 1024 * 1024
OUT_FFN_VMEM = 52 * 1024 * 1024

COL_U = 0
COL_Q = 1024
COL_K = 1536
COL_V = 2048
COL_R = 3072
COL_GATE = 4096
COL_QX = 10240
Z_WIDTH = 11264

S5_GB = 8
S5_CH = S5_GB * S5_STATE
S5_NBLK = S5_GROUPS // S5_GB
S5_TC = 16
GLA_TILE = 64
MERGE_TN = 1024
IN_TN = 1024
FFN_TH = 512


def _params(*sem):
    return pltpu.CompilerParams(dimension_semantics=sem, vmem_limit_bytes=VMEM_LIMIT)


def _rms(x, g):
    return x * lax.rsqrt(jnp.mean(x * x, axis=-1, keepdims=True) + RMS_EPS) * g


class _CastJob(NamedTuple):
    src: jax.Array
    row_axis: int
    col_tile: Optional[int] = None

    def specs(self, grid):
        rows, cols = self.src.shape
        ra, ca = self.row_axis, 1 - self.row_axis
        assert rows % grid[ra] == 0 and cols % grid[ca] == 0
        br, bc = rows // grid[ra], cols // grid[ca]
        in_spec = pl.BlockSpec((br, bc), lambda *g: (g[ra], g[ca]))
        if self.col_tile is None:
            return in_spec, in_spec, jax.ShapeDtypeStruct((rows, cols), BF16)
        assert self.col_tile % bc == 0 and cols % self.col_tile == 0
        per = self.col_tile // bc
        out_spec = pl.BlockSpec((None, br, bc), lambda *g: (g[ca] // per, g[ra], g[ca] % per))
        return in_spec, out_spec, jax.ShapeDtypeStruct((cols // self.col_tile, rows, self.col_tile), BF16)


class _SideJob(NamedTuple):
    in_specs: list
    out_specs: list
    out_shape: list
    operands: tuple
    body: Callable


def _with_side_work(kernel, n_in, n_out, n_cast, side):
    n_side_in = len(side.in_specs) if side else 0
    n_side_out = len(side.out_specs) if side else 0

    def wrapped(*refs):
        ins, rest = refs[:n_in], refs[n_in:]
        srcs, rest = rest[:n_cast], rest[n_cast:]
        side_ins, rest = rest[:n_side_in], rest[n_side_in:]
        outs, rest = rest[:n_out], rest[n_out:]
        dsts, rest = rest[:n_cast], rest[n_cast:]
        side_outs, scratch = rest[:n_side_out], rest[n_side_out:]
        for src, dst in zip(srcs, dsts):
            dst[...] = src[...].astype(dst.dtype)
        kernel(*ins, *outs, *scratch)
        if side:
            side.body(*side_ins, *side_outs)
    return wrapped


def _call_with_casts(kernel, casts, *, grid, in_specs, out_specs, out_shape, operands, side=None, **kwargs):
    casts = casts or {}
    specs = [job.specs(grid) for job in casts.values()]
    side_in = list(side.in_specs) if side else []
    side_out = list(side.out_specs) if side else []
    results = pl.pallas_call(
        _with_side_work(kernel, len(in_specs), len(out_specs), len(specs), side), grid=grid,
        in_specs=list(in_specs) + [s[0] for s in specs] + side_in,
        out_specs=list(out_specs) + [s[1] for s in specs] + side_out,
        out_shape=list(out_shape) + [s[2] for s in specs] + (list(side.out_shape) if side else []),
        **kwargs,
    )(*operands, *[job.src for job in casts.values()], *(side.operands if side else ()))
    n, c = len(out_specs), len(specs)
    return results[:n], dict(zip(casts.keys(), results[n:n + c])), results[n + c:]


def _norm_matmul_kernel(x_ref, g_ref, w_ref, o_ref, h_ref):
    @pl.when(pl.program_id(1) == 0)
    def _():
        h_ref[...] = _rms(x_ref[...], g_ref[...]).astype(BF16)

    o_ref[...] = jnp.dot(h_ref[...], w_ref[0], preferred_element_type=F32)


def _norm_matmul2_kernel(x_ref, g_ref, w_ref, ws_ref, o_ref, os_ref, h_ref):
    @pl.when(pl.program_id(1) == 0)
    def _():
        h = _rms(x_ref[...], g_ref[...]).astype(BF16)
        h_ref[...] = h
        os_ref[...] = jnp.dot(h, ws_ref[...], preferred_element_type=F32)

    o_ref[...] = jnp.dot(h_ref[...], w_ref[0], preferred_element_type=F32)


def _norm_matmul(x, g, w, w_small=None, casts=None, make_side=None, *, tm):
    t, d = x.shape
    nj, _, tn = w.shape
    n = nj * tn
    grid = (t // tm, n // tn)
    in_specs = [pl.BlockSpec((tm, d), lambda i, j: (i, 0)),
                pl.BlockSpec((1, d), lambda i, j: (0, 0)),
                pl.BlockSpec((1, d, tn), lambda i, j: (j, 0, 0))]
    out_specs = pl.BlockSpec((tm, tn), lambda i, j: (i, j))
    out_shape = jax.ShapeDtypeStruct((t, n), F32)
    scratch = [pltpu.VMEM((tm, d), BF16)]
    if w_small is None:
        return pl.pallas_call(_norm_matmul_kernel, grid=grid, in_specs=in_specs, out_specs=out_specs,
                              out_shape=out_shape, scratch_shapes=scratch,
                              compiler_params=_params("parallel", "arbitrary"),
                              name="norm_matmul")(x, g, w)
    ns = w_small.shape[1]
    in_specs.append(pl.BlockSpec((d, ns), lambda i, j: (0, 0)))
    (z, narrow), made, side_out = _call_with_casts(
        _norm_matmul2_kernel, casts, grid=grid, in_specs=in_specs,
        side=make_side(grid) if make_side else None,
        out_specs=[out_specs, pl.BlockSpec((tm, ns), lambda i, j: (i, 0))],
        out_shape=[out_shape, jax.ShapeDtypeStruct((t, ns), F32)],
        operands=(x, g, w, w_small), scratch_shapes=scratch,
        compiler_params=_params("arbitrary" if make_side else "parallel", "arbitrary"), name="in_proj")
    return z, narrow, made, side_out


def _s5_param_kernel(lr_ref, li_ref, ldt_ref, btr_ref, bti_ref, cr_ref, ci_ref,
                     w16_ref, cc16_ref, w8_ref, cc8_ref, ap_ref):
    p = S5_STATE
    half = S5_TC // 2
    lam_re, lam_im = lr_ref[0], li_ref[0]
    dt = jnp.exp(ldt_ref[0])
    mag = jnp.exp(lam_re * dt)
    a_re = mag * jnp.cos(lam_im * dt)
    a_im = mag * jnp.sin(lam_im * dt)
    den = lam_re * lam_re + lam_im * lam_im
    coef_re = ((a_re - 1.0) * lam_re + a_im * lam_im) / den
    coef_im = (a_im * lam_re - (a_re - 1.0) * lam_im) / den
    bt_re, bt_im = btr_ref[0], bti_ref[0]
    bb_re = coef_re * bt_re - coef_im * bt_im
    bb_im = coef_re * bt_im + coef_im * bt_re
    c_re, c_im = cr_ref[0], ci_ref[0]

    pw_re, pw_im = jnp.ones_like(a_re), jnp.zeros_like(a_re)
    ca_re, ca_im, ab_re, ab_im, powers = [], [], [], [], {}
    for j in range(S5_TC + 1):
        powers[j] = (pw_re, pw_im)
        ca_re.append(c_re * pw_re - c_im * pw_im)
        ca_im.append(c_re * pw_im + c_im * pw_re)
        ab_re.append(bb_re * pw_re - bb_im * pw_im)
        ab_im.append(bb_re * pw_im + bb_im * pw_re)
        pw_re, pw_im = pw_re * a_re - pw_im * a_im, pw_re * a_im + pw_im * a_re
    lag_re = jnp.concatenate(ca_re[:S5_TC], axis=0)
    lag_im = jnp.concatenate(ca_im[:S5_TC], axis=0)
    end_re = jnp.concatenate([ab_re[S5_TC - 1 - s] for s in range(S5_TC)], axis=0)
    end_im = jnp.concatenate([ab_im[S5_TC - 1 - s] for s in range(S5_TC)], axis=0)
    car_re = jnp.concatenate(ca_re[1:], axis=0).T
    car_im = -jnp.concatenate(ca_im[1:], axis=0).T

    def pair_blocks(x):
        first = lax.broadcasted_iota(jnp.int32, x.shape, 0) < p
        return jnp.concatenate([jnp.where(first, x, 0.0), jnp.where(first, 0.0, x)], axis=1)

    cc16_ref[0] = jnp.concatenate([pair_blocks(car_re), pair_blocks(car_im)], axis=0).astype(BF16)
    n8 = half * S5_GROUP
    cc8_ref[0] = jnp.concatenate([pair_blocks(car_re[:, :n8]), pair_blocks(car_im[:, :n8])], axis=0).astype(BF16)

    nt_dims = (((1,), (1,)), ((), ()))
    hi = lax.Precision.HIGHEST
    lane = lax.broadcasted_iota(jnp.int32, (S5_GROUP, LANES), 1)
    end_lane = lax.broadcasted_iota(jnp.int32, end_re.shape, 1)
    for gi in range(2):
        mine = (lane < p) if gi == 0 else (lane >= p)
        own_re, own_im = jnp.where(mine, bb_re, 0.0), jnp.where(mine, bb_im, 0.0)
        strip = (lax.dot_general(own_re, lag_re, nt_dims, precision=hi, preferred_element_type=F32)
                 - lax.dot_general(own_im, lag_im, nt_dims, precision=hi, preferred_element_type=F32))
        lo, up = strip[:, :LANES], strip[:, LANES:]
        blocks = [strip]
        for s in range(1, S5_TC):
            sh = (s % half) * S5_GROUP
            lo_r = pltpu.roll(lo, sh, 1) if sh else lo
            up_r = pltpu.roll(up, sh, 1) if sh else up
            if s < half:
                blocks.append(jnp.concatenate([jnp.where(lane >= sh, lo_r, 0.0),
                                               jnp.where(lane >= sh, up_r, lo_r)], axis=1))
            else:
                blocks.append(jnp.concatenate([jnp.zeros_like(lo), jnp.where(lane >= sh, lo_r, 0.0)], axis=1))
        toeplitz = jnp.concatenate(blocks, axis=0)
        own_end = (end_lane < p) if gi == 0 else (end_lane >= p)
        ends = jnp.concatenate([jnp.where(own_end, end_re, 0.0), jnp.where(own_end, end_im, 0.0)], axis=1)
        w16_ref[gi] = jnp.concatenate([toeplitz, ends], axis=1).astype(BF16)
        w8_ref[gi] = jnp.concatenate([toeplitz[:n8, :n8], ends[n8:, :]], axis=1).astype(BF16)
    ap_ref[0] = jnp.concatenate([powers[S5_TC][0], powers[S5_TC][1], powers[half][0], powers[half][1]], axis=0)


def _s5_params(lam_re, lam_im, log_dt, b_re, b_im, c_re, c_im, d_skip):
    g, p, c = S5_GROUPS, S5_STATE, S5_GROUP
    npair = g // 2
    w16, w8 = S5_TC * c, S5_TC // 2 * c

    def pair_lanes(x):
        return x.reshape(npair, 2, x.shape[1], p).transpose(0, 2, 1, 3).reshape(npair, x.shape[1], 2 * p)

    row = lambda x: pair_lanes(x.reshape(g, 1, p))
    blk = lambda *shape: pl.BlockSpec((1,) + shape, lambda i: (i, 0, 0))
    two = lambda *shape: pl.BlockSpec((2,) + shape, lambda i: (i, 0, 0))
    wt16, cc16, wt8, cc8, apow = pl.pallas_call(
        _s5_param_kernel, grid=(npair,),
        in_specs=[blk(1, 2 * p)] * 3 + [blk(c, 2 * p)] * 4,
        out_specs=[two(w16, w16 + 4 * p), blk(4 * p, 2 * w16), two(w8, w8 + 4 * p), blk(4 * p, 2 * w8),
                   blk(4, 2 * p)],
        out_shape=[jax.ShapeDtypeStruct((g, w16, w16 + 4 * p), BF16),
                   jax.ShapeDtypeStruct((npair, 4 * p, 2 * w16), BF16),
                   jax.ShapeDtypeStruct((g, w8, w8 + 4 * p), BF16),
                   jax.ShapeDtypeStruct((npair, 4 * p, 2 * w8), BF16),
                   jax.ShapeDtypeStruct((npair, 4, 2 * p), F32)],
        compiler_params=_params("parallel"),
        name="s5_params",
    )(row(lam_re), row(lam_im), row(jnp.broadcast_to(log_dt[:, None], (g, p))),
      pair_lanes(b_re.transpose(0, 2, 1)), pair_lanes(b_im.transpose(0, 2, 1)),
      pair_lanes(c_re), pair_lanes(c_im))
    d = d_skip.reshape(S5_NBLK, 1, S5_GB * c)

    def transition(r):
        return jnp.concatenate([apow[:, r].reshape(S5_NBLK, 1, S5_CH), apow[:, r + 1].reshape(S5_NBLK, 1, S5_CH)],
                               axis=-1)

    return {S5_TC: (wt16, cc16, transition(0), d), S5_TC // 2: (wt8, cc8, transition(2), d)}


def _unit_transpose(vs):
    unit = lax.broadcasted_iota(jnp.int32, vs[0].shape, 1) >> int(math.log2(S5_GROUP))
    for dist in (4, 2, 1):
        keep = (unit & dist) == 0
        nxt = list(vs)
        for i in range(8):
            if i & dist == 0:
                a, b = vs[i], vs[i + dist]
                nxt[i] = jnp.where(keep, a, pltpu.roll(b, dist * S5_GROUP, 1))
                nxt[i + dist] = jnp.where(keep, pltpu.roll(a, LANES - dist * S5_GROUP, 1), b)
        vs = nxt
    return vs


def _s5_chunk_kernel(z_ref, w_ref, cc_ref, a_ref, d_ref, h0r_ref, h0i_ref,
                     y_ref, hfr_ref, hfi_ref, ut_ref, ug_ref, yg_ref, e_ref, *, nsl, rps, tc, nseg):
    uw = tc * S5_GROUP
    nq = uw // LANES
    rows = nsl * rps
    nk = rps if nsl > 1 else 1
    ns = rows // nk
    npair = S5_GB // 2
    rchunk = min(rows, 64)

    for s in range(nsl):
        for t in range(tc):
            ut_ref[t, s * rps:(s + 1) * rps, :] = z_ref[pl.ds(s * rps * tc + t, rps, stride=tc), :]

    def row_chunk(rc):
        return pl.ds(pl.multiple_of(rc * rchunk, rchunk), rchunk)

    def to_groups(rc, carry):
        for q in range(nq):
            per_group = _unit_transpose([ut_ref[q * 8 + t, row_chunk(rc), :] for t in range(8)])
            for g in range(S5_GB):
                ug_ref[g, row_chunk(rc), q * LANES:(q + 1) * LANES] = per_group[g].astype(BF16)
        return carry

    lax.fori_loop(0, rows // rchunk, to_groups, 0)

    for pair in range(npair):
        ends = None
        for gi in range(2):
            g = 2 * pair + gi
            em = jnp.dot(ug_ref[g], w_ref[g], preferred_element_type=F32)
            yg_ref[g] = em[:, :uw]
            ends = em[:, uw:] if ends is None else ends + em[:, uw:]
        for part, blk in ((ends[:, :LANES], pair), (ends[:, LANES:], npair + pair)):
            if nk == 1:
                e_ref[blk] = part
            else:
                for s in range(nsl):
                    e_ref[blk, pl.ds(s, rps, stride=nsl), :] = part[s * rps:(s + 1) * rps, :]

    a_row = [(a_ref[0, :, i * LANES:(i + 1) * LANES], a_ref[0, :, S5_CH + i * LANES:S5_CH + (i + 1) * LANES])
             for i in range(npair)]
    a_full = [(jnp.broadcast_to(ar, (ns, LANES)), jnp.broadcast_to(ai, (ns, LANES))) for ar, ai in a_row]

    def slab(t):
        return pl.ds(pl.multiple_of(t * ns, ns), ns)

    def run_scan(h, store):
        def step(t, carry):
            out = []
            for i in range(npair):
                hr, hi = carry[i]
                ar, ai = a_full[i]
                er = e_ref[i, slab(t), :]
                ei = e_ref[npair + i, slab(t), :]
                if store:
                    e_ref[i, slab(t), :] = hr
                    e_ref[npair + i, slab(t), :] = hi
                out.append((ar * hr - ai * hi + er, ar * hi + ai * hr + ei))
            return tuple(out)

        if nk == 1:
            return step(0, h)
        return lax.fori_loop(0, nk, step, h, unroll=4)

    h0 = tuple((h0r_ref[:, i * LANES:(i + 1) * LANES], h0i_ref[:, i * LANES:(i + 1) * LANES]) for i in range(npair))
    if nseg == 1:
        final = run_scan(h0, store=True)
    else:
        zero = jnp.zeros((ns, LANES), F32)
        seg_end = run_scan(tuple((zero, zero) for _ in range(npair)), store=False)
        second = (lax.broadcasted_iota(jnp.int32, (ns, LANES), 0) & 1) == 1
        init, final = [], []
        for i in range(npair):
            p_re, p_im = a_row[i]
            for _ in range(int(math.log2(nk))):
                p_re, p_im = p_re * p_re - p_im * p_im, 2.0 * (p_re * p_im)
            (hr, hi), (er, ei) = h0[i], seg_end[i]
            i_re = jnp.where(second, p_re * hr - p_im * hi + pltpu.roll(er, 1, 0), hr)
            i_im = jnp.where(second, p_re * hi + p_im * hr + pltpu.roll(ei, 1, 0), hi)
            init.append((i_re, i_im))
            final.append((p_re * i_re - p_im * i_im + er, p_re * i_im + p_im * i_re + ei))
        run_scan(tuple(init), store=True)
    for i in range(npair):
        hfr_ref[:, i * LANES:(i + 1) * LANES] = final[i][0]
        hfi_ref[:, i * LANES:(i + 1) * LANES] = final[i][1]

    def chunk_rows(blk):
        if nk == 1:
            return e_ref[blk]
        return jnp.concatenate([e_ref[blk, pl.ds(s, rps, stride=nsl), :] for s in range(nsl)], axis=0)

    for pair in range(npair):
        h_in = jnp.concatenate([chunk_rows(pair), chunk_rows(npair + pair)], axis=1).astype(BF16)
        carried = jnp.dot(h_in, cc_ref[pair], preferred_element_type=F32)
        for gi in range(2):
            yg_ref[2 * pair + gi] += carried[:, gi * uw:(gi + 1) * uw]

    d_row = d_ref[0]

    def to_tokens(rc, carry):
        for q in range(nq):
            per_tau = _unit_transpose([yg_ref[g, row_chunk(rc), q * LANES:(q + 1) * LANES] for g in range(S5_GB)])
            for t in range(8):
                y = per_tau[t] + d_row * ut_ref[q * 8 + t, row_chunk(rc), :]
                ut_ref[q * 8 + t, row_chunk(rc), :] = jax.nn.gelu(y, approximate=True)
        return carry

    lax.fori_loop(0, rows // rchunk, to_tokens, 0)
    for s in range(nsl):
        for t in range(tc):
            y_ref[pl.ds(s * rps * tc + t, rps, stride=tc), :] = ut_ref[t, s * rps:(s + 1) * rps, :]


def _s5_branch(z, weights, h0_re, h0_im, *, batch, seq):
    tokens = batch * seq
    if seq > S5_TC:
        tc, nseg = S5_TC, 2
        nsl, rps = batch * nseg, seq // (nseg * tc)
        assert rps & (rps - 1) == 0
        h0_re, h0_im = jnp.repeat(h0_re, nseg, axis=0), jnp.repeat(h0_im, nseg, axis=0)
    else:
        tc, nseg, nsl, rps = seq, 1, 1, batch
    w, cc, a, d = weights[tc]
    uw = tc * S5_GROUP
    rows = nsl * rps
    ns = h0_re.shape[0]
    kern = functools.partial(_s5_chunk_kernel, nsl=nsl, rps=rps, tc=tc, nseg=nseg)
    state_spec = pl.BlockSpec((ns, S5_CH), lambda k: (0, k))
    state_shape = jax.ShapeDtypeStruct((ns, S5_GROUPS * S5_STATE), F32)
    y, hf_re, hf_im = pl.pallas_call(
        kern, grid=(S5_NBLK,),
        in_specs=[pl.BlockSpec((tokens, LANES), lambda k: (0, COL_U // LANES + k)),
                  pl.BlockSpec((S5_GB, uw, uw + 4 * S5_STATE), lambda k: (k, 0, 0)),
                  pl.BlockSpec((S5_GB // 2, 4 * S5_STATE, 2 * uw), lambda k: (k, 0, 0)),
                  pl.BlockSpec((1, 1, 2 * S5_CH), lambda k: (k, 0, 0)),
                  pl.BlockSpec((1, 1, LANES), lambda k: (k, 0, 0)),
                  state_spec, state_spec],
        out_specs=[pl.BlockSpec((tokens, LANES), lambda k: (0, k)), state_spec, state_spec],
        out_shape=[jax.ShapeDtypeStruct((tokens, S5_WIDTH), F32), state_shape, state_shape],
        scratch_shapes=[pltpu.VMEM((tc, rows, LANES), F32), pltpu.VMEM((S5_GB, rows, uw), BF16),
                        pltpu.VMEM((S5_GB, rows, uw), F32), pltpu.VMEM((S5_GB, rows, LANES), F32)],
        compiler_params=_params("parallel"),
        name="s5_chunked",
    )(z, w, cc, a, d, h0_re, h0_im)
    if nseg == 2:
        hf_re, hf_im = hf_re[1::2], hf_im[1::2]
    return y, hf_re, hf_im


def _gla_kernel(q_ref, k_ref, v_ref, r_ref, al_ref, wa_ref, ba_ref, gn_ref, s0_ref,
                y_ref, sf_ref, s_scr, *, ntile, groups, chained, width):
    rt = GLA_TILE
    c = rt // groups
    shift = int(math.log2(c))
    row_g = lax.broadcasted_iota(jnp.int32, (rt, rt), 0)
    col_g = lax.broadcasted_iota(jnp.int32, (rt, rt), 1)
    same = (row_g >> shift) == (col_g >> shift)
    causal = same & (row_g >= col_g)
    cum_w = jnp.concatenate([causal.astype(BF16), same.astype(BF16)], axis=0)
    tn_dims = (((0,), (0,)), ((), ()))
    nt_dims = (((1,), (1,)), ((), ()))

    def split3(x):
        hi = x.astype(BF16)
        r1 = x - hi.astype(F32)
        mid = r1.astype(BF16)
        lo = (r1 - mid.astype(F32)).astype(BF16)
        return jnp.concatenate([hi, mid, lo], axis=1)

    if chained:
        s_scr[...] = s0_ref[0, 0]

    def body(it, carry):
        tiles = [it * width + u for u in range(width)]
        rows = [pl.ds(pl.multiple_of(t * rt, rt), rt) for t in tiles]
        log_a = []
        for u in range(width):
            x = jnp.dot(al_ref[rows[u], :].astype(BF16), wa_ref[...], preferred_element_type=F32) + ba_ref[...]
            log_a.append((jnp.minimum(x, 0.0) - jnp.log1p(jnp.exp(-jnp.abs(x)))) * (1.0 / GLA_TAU))
        la3 = [split3(la) for la in log_a]
        cums, e_col = [], []
        for u in range(width):
            cs = jnp.dot(cum_w, la3[u], preferred_element_type=F32)
            cums.append(cs[:, :LANES] + cs[:, LANES:2 * LANES] + cs[:, 2 * LANES:])
            e_col.append(jnp.exp(cums[u].T))
        qb, kd, v, att, upd = [], [], [], [], []
        for u in range(width):
            b = cums[u][:rt]
            b_end = cums[u][rt:]
            k = k_ref[rows[u], :]
            v.append(v_ref[rows[u], :])
            mid = 0.5 * b_end
            q_mid = q_ref[rows[u], :] * (GLA_DK ** -0.5) * jnp.exp(b - mid)
            k_mid = k * jnp.exp(mid - b)
            e_mid = jnp.exp(mid)
            qb.append(q_mid * e_mid)
            kd.append(k_mid * e_mid)
            a = lax.dot_general(q_mid.astype(BF16), k_mid.astype(BF16), nt_dims, preferred_element_type=F32)
            att.append(jnp.where(causal, a, 0.0).astype(BF16))
            upd.append([lax.dot_general(kd[u][g * c:(g + 1) * c].astype(BF16), v[u][g * c:(g + 1) * c].astype(BF16),
                                        tn_dims, preferred_element_type=F32) for g in range(groups)])
        o = [jnp.dot(att[u], v[u].astype(BF16), preferred_element_type=F32) for u in range(width)]
        for u in range(width):
            o_state = []
            for g in range(groups):
                s = s_scr[...] if chained else s0_ref[tiles[u] * groups + g, 0]
                o_state.append(jnp.dot(qb[u][g * c:(g + 1) * c].astype(BF16), s.astype(BF16),
                                       preferred_element_type=F32))
                s_new = s * e_col[u][:, rt + g * c:rt + g * c + 1] + upd[u][g]
                if chained:
                    s_scr[...] = s_new
                else:
                    sf_ref[tiles[u] * groups + g, 0] = s_new
            o[u] = o[u] + (o_state[0] if groups == 1 else jnp.concatenate(o_state, axis=0))
        for u in range(width):
            y = o[u] * lax.rsqrt(jnp.mean(o[u] * o[u], axis=-1, keepdims=True) + RMS_EPS)
            y = y * gn_ref[...]
            r = r_ref[rows[u], :]
            y_ref[rows[u], :] = (y * (r * jax.nn.sigmoid(r))).astype(y_ref.dtype)
        return carry

    lax.fori_loop(0, ntile // width, body, 0)
    if chained:
        sf_ref[0, 0] = s_scr[...]


def _gla(z, alow, w_a2, b_a, g_norm, s0, casts=None, *, batch, seq, chained):
    chunk = math.gcd(seq, GLA_CHUNK)
    if chained:
        assert chunk == GLA_TILE
        nb, groups, width = 1, 1, 8
    else:
        assert GLA_TILE % seq == 0 and chunk == seq
        nb, groups, width = 32, GLA_TILE // seq, 4
    rows = nb * seq
    kern = functools.partial(_gla_kernel, ntile=rows // GLA_TILE, groups=groups, chained=chained, width=width)
    state_spec = pl.BlockSpec((nb, 1, GLA_DK, GLA_DV), lambda i, h: (i, h, 0, 0))
    (y, s_fin), made, _ = _call_with_casts(
        kern, casts, grid=(batch // nb, GLA_HEADS),
        in_specs=[pl.BlockSpec((rows, GLA_DK), lambda i, h: (i, COL_Q // GLA_DK + h)),
                  pl.BlockSpec((rows, GLA_DK), lambda i, h: (i, COL_K // GLA_DK + h)),
                  pl.BlockSpec((rows, GLA_DV), lambda i, h: (i, COL_V // GLA_DV + h)),
                  pl.BlockSpec((rows, GLA_DV), lambda i, h: (i, COL_R // GLA_DV + h)),
                  pl.BlockSpec((rows, LANES), lambda i, h: (i, 0)),
                  pl.BlockSpec((LANES, GLA_DK), lambda i, h: (0, h)),
                  pl.BlockSpec((1, GLA_DK), lambda i, h: (0, h)),
                  pl.BlockSpec((1, GLA_DV), lambda i, h: (0, h)),
                  state_spec],
        out_specs=[pl.BlockSpec((rows, GLA_DV), lambda i, h: (i, h)), state_spec],
        out_shape=[jax.ShapeDtypeStruct((batch * seq, GLA_HEADS * GLA_DV), BF16),
                   jax.ShapeDtypeStruct((batch, GLA_HEADS, GLA_DK, GLA_DV), F32)],
        operands=(z, z, z, z, alow, w_a2, b_a, g_norm, s0),
        scratch_shapes=[pltpu.VMEM((GLA_DK, GLA_DV), F32)],
        compiler_params=_params("parallel", "parallel"),
        name="gla")
    return y, s_fin, made


def _attn_kernel(q_ref, k_ref, v_ref, o_ref):
    nt_dims = (((1,), (1,)), ((), ()))
    cols = [slice(h * XA_HEAD_DIM, (h + 1) * XA_HEAD_DIM) for h in range(XA_HEADS)]
    scores = [lax.dot_general(q_ref[:, c].astype(BF16), k_ref[0, :, c].astype(BF16), nt_dims,
                              preferred_element_type=F32) * (XA_HEAD_DIM ** -0.5) for c in cols]
    probs = []
    for s in scores:
        p = jnp.exp(s - jnp.max(s, axis=-1, keepdims=True))
        probs.append((p / jnp.sum(p, axis=-1, keepdims=True)).astype(BF16))
    for c, p in zip(cols, probs):
        o_ref[:, c] = jnp.dot(p, v_ref[0, :, c].astype(BF16), preferred_element_type=F32).astype(o_ref.dtype)


def _attn_prompt(z, mem_k, mem_v, *, batch, seq, rc):
    nrc = seq // rc
    kv_spec = pl.BlockSpec((1, MEM_LEN, XA_WIDTH), lambda b, c: (b, 0, 0))
    return pl.pallas_call(
        _attn_kernel, grid=(batch, nrc),
        in_specs=[pl.BlockSpec((rc, XA_WIDTH), lambda b, c: (b * nrc + c, COL_QX // XA_WIDTH)), kv_spec, kv_spec],
        out_specs=pl.BlockSpec((rc, XA_WIDTH), lambda b, c: (b * nrc + c, 0)),
        out_shape=jax.ShapeDtypeStruct((batch * seq, XA_WIDTH), BF16),
        compiler_params=_params("parallel", "parallel"),
        name="attn_prompt",
    )(z, mem_k, mem_v)


XA_HALF = XA_HEAD_DIM // 2
XA_ROWS = 2 * XA_HEADS


def _attn_cache_body(q_ref, k_ref, v_ref, o_ref, *, nseq, seq, nblk, grid):
    nt_dims = (((1,), (1,)), ((), ()))

    def half_rows(ref, bi, h, half):
        return ref[bi, pl.ds(half * XA_HEADS + h, MEM_LEN, stride=XA_ROWS), :].astype(BF16)

    @pl.when(pl.program_id(0) * grid[1] + pl.program_id(1) < nblk)
    def _():
        scores = []
        for bi in range(nseq):
            rows = slice(bi * seq, (bi + 1) * seq)
            for h in range(XA_HEADS):
                lo = h * XA_HEAD_DIM
                s = lax.dot_general(q_ref[rows, lo:lo + XA_HALF].astype(BF16), half_rows(k_ref, bi, h, 0),
                                    nt_dims, preferred_element_type=F32)
                s = s + lax.dot_general(q_ref[rows, lo + XA_HALF:lo + XA_HEAD_DIM].astype(BF16),
                                        half_rows(k_ref, bi, h, 1), nt_dims, preferred_element_type=F32)
                scores.append(s * (XA_HEAD_DIM ** -0.5))
        probs = []
        for s in scores:
            p = jnp.exp(s - jnp.max(s, axis=-1, keepdims=True))
            probs.append((p / jnp.sum(p, axis=-1, keepdims=True)).astype(BF16))
        for h in range(XA_HEADS):
            for half in range(2):
                lo = h * XA_HEAD_DIM + half * XA_HALF
                out = [jnp.dot(probs[bi * XA_HEADS + h], half_rows(v_ref, bi, h, half),
                               preferred_element_type=F32) for bi in range(nseq)]
                o_ref[:, lo:lo + XA_HALF] = jnp.concatenate(out, axis=0).astype(o_ref.dtype)


def _cache_rows(cache):
    bs = cache.shape[0]
    c = cache.reshape(bs, MEM_LEN, XA_HEADS, 2, XA_HALF).transpose(0, 1, 3, 2, 4)
    return c.reshape(bs, MEM_LEN * XA_ROWS, XA_HALF)


def _attn_sample_side(z, mem_k, mem_v, *, batch, seq, grid, per_step=2):
    nblk = batch // per_step
    assert nblk <= grid[0] * grid[1] and (per_step * seq) % 16 == 0
    rows = per_step * seq

    def blk(*g):
        return jnp.minimum(g[0] * grid[1] + g[1], nblk - 1)

    kv_spec = pl.BlockSpec((per_step, MEM_LEN * XA_ROWS, XA_HALF), lambda *g: (blk(*g), 0, 0))
    return _SideJob(
        in_specs=[pl.BlockSpec((rows, XA_WIDTH), lambda *g: (blk(*g), COL_QX // XA_WIDTH)), kv_spec, kv_spec],
        out_specs=[pl.BlockSpec((rows, XA_WIDTH), lambda *g: (blk(*g), 0))],
        out_shape=[jax.ShapeDtypeStruct((batch * seq, XA_WIDTH), BF16)],
        operands=(z, mem_k, mem_v),
        body=functools.partial(_attn_cache_body, nseq=per_step, seq=seq, nblk=nblk, grid=grid))


def _merge_kernel(ys_ref, yg_ref, yx_ref, g0_ref, g1_ref, g2_ref, wglu_ref, bglu_ref,
                  w0_ref, w1_ref, w2_ref, o_ref, s5_scr):
    @pl.when(pl.program_id(1) == 0)
    def _():
        y = ys_ref[...]
        lin = jnp.dot(y.astype(BF16), wglu_ref[...], preferred_element_type=F32) + bglu_ref[...]
        s5_scr[...] = (y * jax.nn.sigmoid(lin)).astype(BF16)

    j = pl.program_id(1)
    m = jax.nn.sigmoid(g0_ref[...]) * jnp.dot(s5_scr[...], w0_ref[j], preferred_element_type=F32)
    m = m + jax.nn.sigmoid(g1_ref[...]) * jnp.dot(yg_ref[...], w1_ref[j], preferred_element_type=F32)
    m = m + jax.nn.sigmoid(g2_ref[...]) * jnp.dot(yx_ref[...], w2_ref[j], preferred_element_type=F32)
    o_ref[...] = m.astype(o_ref.dtype)


def _merge(y_s5, y_gla, y_x, z, w_glu, b_glu, w_br_s5, w_br_gla, w_br_x, *, tm, tn):
    t = y_s5.shape[0]
    nj = D_MODEL // tn
    once = pl.Buffered(1)
    wide = pl.BlockSpec((tm, S5_WIDTH), lambda i, j: (i, 0))
    gate = lambda b: pl.BlockSpec((tm, tn), lambda i, j: (i, (COL_GATE + b * D_MODEL) // tn + j))
    w_br = pl.BlockSpec((nj, S5_WIDTH, tn), lambda i, j: (0, 0, 0), pipeline_mode=once)
    return pl.pallas_call(
        _merge_kernel, grid=(t // tm, nj),
        in_specs=[wide, wide, wide, gate(0), gate(1), gate(2),
                  pl.BlockSpec((S5_WIDTH, S5_WIDTH), lambda i, j: (0, 0), pipeline_mode=once),
                  pl.BlockSpec((1, S5_WIDTH), lambda i, j: (0, 0)),
                  w_br, w_br, w_br],
        out_specs=pl.BlockSpec((tm, tn), lambda i, j: (i, j)),
        out_shape=jax.ShapeDtypeStruct((t, D_MODEL), BF16),
        scratch_shapes=[pltpu.VMEM((tm, S5_WIDTH), BF16)],
        compiler_params=_params("parallel", "arbitrary"),
        name="merge",
    )(y_s5, y_gla, y_x, z, z, z, w_glu, b_glu, w_br_s5, w_br_gla, w_br_x)


def _out_ffn_kernel(x_ref, m_ref, wo_ref, gf_ref, wg_ref, wu_ref, wd_ref, gl_ref, o_ref, h_scr, acc_scr):
    k = pl.program_id(1)

    @pl.when(k == 0)
    def _():
        acc_scr[...] = x_ref[...] + jnp.dot(m_ref[...], wo_ref[...], preferred_element_type=F32)
        h_scr[...] = _rms(acc_scr[...], gf_ref[...]).astype(BF16)

    h = h_scr[...]
    gate = jnp.dot(h, wg_ref[...], preferred_element_type=F32)
    up = jnp.dot(h, wu_ref[...], preferred_element_type=F32)
    act = (gate * jax.nn.sigmoid(gate) * up).astype(BF16)
    acc_scr[...] += jnp.dot(act, wd_ref[...], preferred_element_type=F32)

    @pl.when(k == pl.num_programs(1) - 1)
    def _():
        o_ref[...] = _rms(acc_scr[...], gl_ref[...])


def _out_ffn(x, merged, w_out, g_ffn, w_gate, w_up, w_down, g_final, *, tm, th):
    t = x.shape[0]
    row = pl.BlockSpec((tm, D_MODEL), lambda i, k: (i, 0))
    vec = pl.BlockSpec((1, D_MODEL), lambda i, k: (0, 0))
    w_in = pl.BlockSpec((D_MODEL, th), lambda i, k: (0, k))
    return pl.pallas_call(
        _out_ffn_kernel, grid=(t // tm, FFN_HIDDEN // th),
        in_specs=[row, row,
                  pl.BlockSpec((D_MODEL, D_MODEL), lambda i, k: (0, 0), pipeline_mode=pl.Buffered(1)),
                  vec, w_in, w_in, pl.BlockSpec((th, D_MODEL), lambda i, k: (k, 0)), vec],
        out_specs=row,
        out_shape=jax.ShapeDtypeStruct((t, D_MODEL), F32),
        scratch_shapes=[pltpu.VMEM((tm, D_MODEL), BF16), pltpu.VMEM((tm, D_MODEL), F32)],
        compiler_params=pltpu.CompilerParams(dimension_semantics=("parallel", "arbitrary"),
                                             vmem_limit_bytes=OUT_FFN_VMEM),
        name="out_ffn",
    )(x, merged, w_out, g_ffn, w_gate, w_up, w_down, g_final)


def _first_in_proj_kernel(x_ref, g_ref, wt_ref, wa_ref, o_ref, os_ref, main_ref, alow_ref, h_ref):
    @pl.when(pl.program_id(0) == 0)
    def _():
        h = _rms(x_ref[...], g_ref[...]).astype(BF16)
        h_ref[...] = h
        lane = lax.broadcasted_iota(jnp.int32, alow_ref.shape, 1)
        w_low = jnp.where(lane < GLA_RANK, wa_ref[...].T, 0.0).astype(BF16)
        alow_ref[...] = w_low
        os_ref[...] = jnp.dot(h, w_low, preferred_element_type=F32)

    w = wt_ref[...].T.astype(BF16)
    main_ref[...] = w
    o_ref[...] = jnp.dot(h_ref[...], w, preferred_element_type=F32)


def _first_in_proj(x, g, w_in):
    t, _ = x.shape
    d, n = w_in.shape
    wt = w_in.T
    tc = 512
    per_tile = IN_TN // tc
    n_first, n_gate = COL_GATE // tc, (COL_QX - COL_GATE) // tc
    gate_start, qx_start = 4096 + GLA_RANK + XA_WIDTH, 4096 + GLA_RANK

    def src_row(j):
        t8, g8, q8 = tc // SUBLANES, gate_start // SUBLANES, qx_start // SUBLANES
        r8 = jnp.where(j < n_first, j * t8,
                       jnp.where(j < n_first + n_gate, g8 + (j - n_first) * t8,
                                 q8 + (j - n_first - n_gate) * t8))
        return r8 * SUBLANES

    once = pl.Buffered(1)
    return pl.pallas_call(
        _first_in_proj_kernel, grid=(Z_WIDTH // tc,),
        in_specs=[pl.BlockSpec((pl.Element(t), pl.Element(d)), lambda j: (0, 0), pipeline_mode=once),
                  pl.BlockSpec((pl.Element(1), pl.Element(d)), lambda j: (0, 0)),
                  pl.BlockSpec((pl.Element(tc), pl.Element(d)), lambda j: (src_row(j), 0)),
                  pl.BlockSpec((pl.Element(LANES), pl.Element(d)), lambda j: (COL_GATE, 0), pipeline_mode=once)],
        out_specs=[pl.BlockSpec((t, tc), lambda j: (0, j)),
                   pl.BlockSpec((t, LANES), lambda j: (0, 0)),
                   pl.BlockSpec((None, d, tc), lambda j: (j // per_tile, 0, j % per_tile)),
                   pl.BlockSpec((d, LANES), lambda j: (0, 0))],
        out_shape=[jax.ShapeDtypeStruct((t, Z_WIDTH), F32), jax.ShapeDtypeStruct((t, LANES), F32),
                   jax.ShapeDtypeStruct((Z_WIDTH // IN_TN, d, IN_TN), BF16),
                   jax.ShapeDtypeStruct((d, LANES), BF16)],
        scratch_shapes=[pltpu.VMEM((t, d), BF16)],
        compiler_params=_params("arbitrary"),
        name="first_in_proj",
    )(x, g, wt, wt)


def _layer(x, z, alow, y_x, s5_re0, s5_im0, gla_s0, w, *, batch, seq, chained, gla_casts=None):
    w = dict(w)
    y_s5, hf_re, hf_im = _s5_branch(z, w['s5'], s5_re0, s5_im0, batch=batch, seq=seq)
    y_gla, gla_s, made = _gla(z, alow, w['gla_w_a2'], w['gla_b_a'], w['gla_norm'], gla_s0, gla_casts,
                              batch=batch, seq=seq, chained=chained)
    w.update(made)
    merged = _merge(y_s5, y_gla, y_x, z, w['s5_w_glu'], w['s5_b_glu'],
                    w['w_br_s5'], w['w_br_gla'], w['w_br_xattn'], tm=512, tn=MERGE_TN)
    y = _out_ffn(x, merged, w['w_out'], w['norm_ffn'], w['w_ffn_gate'], w['w_ffn_up'], w['w_ffn_down'],
                 w['norm_final'], tm=512, th=FFN_TH)
    return y, hf_re, hf_im, gla_s, w


def kernel(x_prompt, x_sample, mem_prompt, state_s5_re, state_s5_im, state_gla, cache_mem_k, cache_mem_v,
           norm_mix, w_in, s5_lam_re, s5_lam_im, s5_log_dt, s5_b_re, s5_b_im, s5_c_re, s5_c_im,
           s5_d, s5_w_glu, s5_b_glu, gla_w_a2, gla_b_a, gla_norm, mem_norm, w_mem_k, w_mem_v,
           w_br_s5, w_br_gla, w_br_xattn, w_out, norm_ffn, w_ffn_gate, w_ffn_up, w_ffn_down, norm_final):
    depth = w_in.shape[0]
    assert depth == 1
    bp, sp, d = x_prompt.shape
    bs, ss, _ = x_sample.shape
    n_state = S5_GROUPS * S5_STATE
    row = lambda v: v.reshape(1, -1)

    l = 0
    s5_w = _s5_params(s5_lam_re[l], s5_lam_im[l], s5_log_dt[l], s5_b_re[l], s5_b_im[l],
                      s5_c_re[l], s5_c_im[l], s5_d[l])
    w = {
        'norm_mix': row(norm_mix[l]),
        's5': s5_w,
        's5_b_glu': row(s5_b_glu[l]),
        'gla_w_a2': jnp.pad(gla_w_a2[l], ((0, LANES - GLA_RANK), (0, 0))).astype(BF16),
        'gla_b_a': row(gla_b_a[l]), 'gla_norm': row(gla_norm[l]),
        'norm_ffn': row(norm_ffn[l]), 'norm_final': row(norm_final),
    }
    ffn_casts = {'w_ffn_gate': _CastJob(w_ffn_gate[l], row_axis=0), 'w_ffn_up': _CastJob(w_ffn_up[l], row_axis=0),
                 'w_ffn_down': _CastJob(w_ffn_down[l], row_axis=1)}
    mix_casts = {'w_out': _CastJob(w_out[l], row_axis=0), 's5_w_glu': _CastJob(s5_w_glu[l], row_axis=0),
                 'w_br_s5': _CastJob(w_br_s5[l], row_axis=0, col_tile=MERGE_TN),
                 'w_br_gla': _CastJob(w_br_gla[l], row_axis=0, col_tile=MERGE_TN),
                 'w_br_xattn': _CastJob(w_br_xattn[l], row_axis=0, col_tile=MERGE_TN)}

    w_mem = jnp.stack([w_mem_k[l], w_mem_v[l]]).astype(BF16)
    mem_kv = _norm_matmul(mem_prompt.reshape(bp * MEM_LEN, d), row(mem_norm[l]), w_mem, tm=512)
    mk = mem_kv[:, :XA_WIDTH].reshape(bp, MEM_LEN, XA_WIDTH)
    mv = mem_kv[:, XA_WIDTH:].reshape(bp, MEM_LEN, XA_WIDTH)
    zero_s5 = jnp.zeros((bp, n_state), F32)
    zero_gla = jnp.zeros((bp, GLA_HEADS, GLA_DK, GLA_DV), F32)
    xp, xs = x_prompt.reshape(bp * sp, d), x_sample.reshape(bs * ss, d)

    z_s, alow_s, w['w_main'], w['w_alow'] = _first_in_proj(xs, w['norm_mix'], w_in[l])
    cache_k, cache_v = _cache_rows(cache_mem_k[l]), _cache_rows(cache_mem_v[l])
    z_p, alow_p, made, (yx_s,) = _norm_matmul(
        xp, w['norm_mix'], w['w_main'], w['w_alow'], ffn_casts,
        lambda grid: _attn_sample_side(z_s, cache_k, cache_v, batch=bs, seq=ss, grid=grid), tm=1024)
    w.update(made)
    yx_p = _attn_prompt(z_p, mk, mv, batch=bp, seq=sp, rc=1024)
    yp, p_re, p_im, p_gla, w = _layer(xp, z_p, alow_p, yx_p, zero_s5, zero_s5, zero_gla, w,
                                      batch=bp, seq=sp, chained=True, gla_casts=mix_casts)

    ys, s_re, s_im, s_gla, _ = _layer(xs, z_s, alow_s, yx_s,
                                      state_s5_re[l].reshape(bs, n_state), state_s5_im[l].reshape(bs, n_state),
                                      state_gla[l], w, batch=bs, seq=ss, chained=False)

    s5_shape_p = (1, bp, S5_GROUPS, S5_STATE)
    s5_shape_s = (1, bs, S5_GROUPS, S5_STATE)
    kv_shape = (1, bp, MEM_LEN, XA_HEADS, XA_HEAD_DIM)
    return (yp.reshape(bp, sp, d), ys.reshape(bs, ss, d),
            p_re.reshape(s5_shape_p), p_im.reshape(s5_shape_p), p_gla[None],
            mk.reshape(kv_shape), mv.reshape(kv_shape),
            s_re.reshape(s5_shape_s), s_im.reshape(s5_shape_s), s_gla[None])
```

```python
import functools
import math
from typing import Callable, NamedTuple, Optional

import jax
import jax.numpy as jnp
from jax import lax
from jax.experimental import pallas as pl
from jax.experimental.pallas import tpu as pltpu

F32 = jnp.float32
BF16 = jnp.bfloat16

D_MODEL = 2048
S5_WIDTH = 1024
S5_GROUP = 16
S5_GROUPS = 64
S5_STATE = 64
GLA_HEADS = 4
GLA_DK = 128
GLA_DV = 256
GLA_RANK = 16
GLA_TAU = 16.0
GLA_CHUNK = 64
XA_HEADS = 4
XA_HEAD_DIM = 256
XA_WIDTH = 1024
MEM_LEN = 256
FFN_HIDDEN = 5632
RMS_EPS = 1e-6

LANES = 128
SUBLANES = 8
VMEM_LIMIT = 56 * 1024 * 1024
OUT_FFN_VMEM = 52 * 1024 * 1024

COL_U = 0
COL_Q = 1024
COL_K = 1536
COL_V = 2048
COL_R = 3072
COL_GATE = 4096
COL_QX = 10240
Z_WIDTH = 11264

S5_GB = 8
S5_CH = S5_GB * S5_STATE
S5_NBLK = S5_GROUPS // S5_GB
S5_TC = 16
GLA_TILE = 64
MERGE_TN = 1024
IN_TN = 1024
FFN_TH = 512


def _params(*sem):
    return pltpu.CompilerParams(dimension_semantics=sem, vmem_limit_bytes=VMEM_LIMIT)


def _rms(x, g):
    return x * lax.rsqrt(jnp.mean(x * x, axis=-1, keepdims=True) + RMS_EPS) * g


class _CastJob(NamedTuple):
    src: jax.Array
    row_axis: int
    col_tile: Optional[int] = None

    def specs(self, grid):
        rows, cols = self.src.shape
        ra, ca = self.row_axis, 1 - self.row_axis
        assert rows % grid[ra] == 0 and cols % grid[ca] == 0
        br, bc = rows // grid[ra], cols // grid[ca]
        in_spec = pl.BlockSpec((br, bc), lambda *g: (g[ra], g[ca]))
        if self.col_tile is None:
            return in_spec, in_spec, jax.ShapeDtypeStruct((rows, cols), BF16)
        assert self.col_tile % bc == 0 and cols % self.col_tile == 0
        per = self.col_tile // bc
        out_spec = pl.BlockSpec((None, br, bc), lambda *g: (g[ca] // per, g[ra], g[ca] % per))
        return in_spec, out_spec, jax.ShapeDtypeStruct((cols // self.col_tile, rows, self.col_tile), BF16)


class _SideJob(NamedTuple):
    in_specs: list
    out_specs: list
    out_shape: list
    operands: tuple
    body: Callable


def _with_side_work(kernel, n_in, n_out, n_cast, side):
    n_side_in = len(side.in_specs) if side else 0
    n_side_out = len(side.out_specs) if side else 0

    def wrapped(*refs):
        ins, rest = refs[:n_in], refs[n_in:]
        srcs, rest = rest[:n_cast], rest[n_cast:]
        side_ins, rest = rest[:n_side_in], rest[n_side_in:]
        outs, rest = rest[:n_out], rest[n_out:]
        dsts, rest = rest[:n_cast], rest[n_cast:]
        side_outs, scratch = rest[:n_side_out], rest[n_side_out:]
        for src, dst in zip(srcs, dsts):
            dst[...] = src[...].astype(dst.dtype)
        kernel(*ins, *outs, *scratch)
        if side:
            side.body(*side_ins, *side_outs)
    return wrapped


def _call_with_casts(kernel, casts, *, grid, in_specs, out_specs, out_shape, operands, side=None, **kwargs):
    casts = casts or {}
    specs = [job.specs(grid) for job in casts.values()]
    side_in = list(side.in_specs) if side else []
    side_out = list(side.out_specs) if side else []
    results = pl.pallas_call(
        _with_side_work(kernel, len(in_specs), len(out_specs), len(specs), side), grid=grid,
        in_specs=list(in_specs) + [s[0] for s in specs] + side_in,
        out_specs=list(out_specs) + [s[1] for s in specs] + side_out,
        out_shape=list(out_shape) + [s[2] for s in specs] + (list(side.out_shape) if side else []),
        **kwargs,
    )(*operands, *[job.src for job in casts.values()], *(side.operands if side else ()))
    n, c = len(out_specs), len(specs)
    return results[:n], dict(zip(casts.keys(), results[n:n + c])), results[n + c:]


def _norm_matmul_kernel(x_ref, g_ref, w_ref, o_ref, h_ref):
    @pl.when(pl.program_id(1) == 0)
    def _():
        h_ref[...] = _rms(x_ref[...], g_ref[...]).astype(BF16)

    o_ref[...] = jnp.dot(h_ref[...], w_ref[0], preferred_element_type=F32)


def _norm_matmul2_kernel(x_ref, g_ref, w_ref, ws_ref, o_ref, os_ref, h_ref):
    @pl.when(pl.program_id(1) == 0)
    def _():
        h = _rms(x_ref[...], g_ref[...]).astype(BF16)
        h_ref[...] = h
        os_ref[...] = jnp.dot(h, ws_ref[...], preferred_element_type=F32)

    o_ref[...] = jnp.dot(h_ref[...], w_ref[0], preferred_element_type=F32)


def _norm_matmul(x, g, w, w_small=None, casts=None, make_side=None, *, tm):
    t, d = x.shape
    nj, _, tn = w.shape
    n = nj * tn
    grid = (t // tm, n // tn)
    in_specs = [pl.BlockSpec((tm, d), lambda i, j: (i, 0)),
                pl.BlockSpec((1, d), lambda i, j: (0, 0)),
                pl.BlockSpec((1, d, tn), lambda i, j: (j, 0, 0))]
    out_specs = pl.BlockSpec((tm, tn), lambda i, j: (i, j))
    out_shape = jax.ShapeDtypeStruct((t, n), F32)
    scratch = [pltpu.VMEM((tm, d), BF16)]
    if w_small is None:
        return pl.pallas_call(_norm_matmul_kernel, grid=grid, in_specs=in_specs, out_specs=out_specs,
                              out_shape=out_shape, scratch_shapes=scratch,
                              compiler_params=_params("parallel", "arbitrary"),
                              name="norm_matmul")(x, g, w)
    ns = w_small.shape[1]
    in_specs.append(pl.BlockSpec((d, ns), lambda i, j: (0, 0)))
    (z, narrow), made, side_out = _call_with_casts(
        _norm_matmul2_kernel, casts, grid=grid, in_specs=in_specs,
        side=make_side(grid) if make_side else None,
        out_specs=[out_specs, pl.BlockSpec((tm, ns), lambda i, j: (i, 0))],
        out_shape=[out_shape, jax.ShapeDtypeStruct((t, ns), F32)],
        operands=(x, g, w, w_small), scratch_shapes=scratch,
        compiler_params=_params("arbitrary" if make_side else "parallel", "arbitrary"), name="in_proj")
    return z, narrow, made, side_out


def _s5_param_kernel(lr_ref, li_ref, ldt_ref, btr_ref, bti_ref, cr_ref, ci_ref,
                     w16_ref, cc16_ref, w8_ref, cc8_ref, ap_ref):
    p = S5_STATE
    half = S5_TC // 2
    lam_re, lam_im = lr_ref[0], li_ref[0]
    dt = jnp.exp(ldt_ref[0])
    mag = jnp.exp(lam_re * dt)
    a_re = mag * jnp.cos(lam_im * dt)
    a_im = mag * jnp.sin(lam_im * dt)
    den = lam_re * lam_re + lam_im * lam_im
    coef_re = ((a_re - 1.0) * lam_re + a_im * lam_im) / den
    coef_im = (a_im * lam_re - (a_re - 1.0) * lam_im) / den
    bt_re, bt_im = btr_ref[0], bti_ref[0]
    bb_re = coef_re * bt_re - coef_im * bt_im
    bb_im = coef_re * bt_im + coef_im * bt_re
    c_re, c_im = cr_ref[0], ci_ref[0]

    pw_re, pw_im = jnp.ones_like(a_re), jnp.zeros_like(a_re)
    ca_re, ca_im, ab_re, ab_im, powers = [], [], [], [], {}
    for j in range(S5_TC + 1):
        powers[j] = (pw_re, pw_im)
        ca_re.append(c_re * pw_re - c_im * pw_im)
        ca_im.append(c_re * pw_im + c_im * pw_re)
        ab_re.append(bb_re * pw_re - bb_im * pw_im)
        ab_im.append(bb_re * pw_im + bb_im * pw_re)
        pw_re, pw_im = pw_re * a_re - pw_im * a_im, pw_re * a_im + pw_im * a_re
    lag_re = jnp.concatenate(ca_re[:S5_TC], axis=0)
    lag_im = jnp.concatenate(ca_im[:S5_TC], axis=0)
    end_re = jnp.concatenate([ab_re[S5_TC - 1 - s] for s in range(S5_TC)], axis=0)
    end_im = jnp.concatenate([ab_im[S5_TC - 1 - s] for s in range(S5_TC)], axis=0)
    car_re = jnp.concatenate(ca_re[1:], axis=0).T
    car_im = -jnp.concatenate(ca_im[1:], axis=0).T

    def pair_blocks(x):
        first = lax.broadcasted_iota(jnp.int32, x.shape, 0) < p
        return jnp.concatenate([jnp.where(first, x, 0.0), jnp.where(first, 0.0, x)], axis=1)

    cc16_ref[0] = jnp.concatenate([pair_blocks(car_re), pair_blocks(car_im)], axis=0).astype(BF16)
    n8 = half * S5_GROUP
    cc8_ref[0] = jnp.concatenate([pair_blocks(car_re[:, :n8]), pair_blocks(car_im[:, :n8])], axis=0).astype(BF16)

    nt_dims = (((1,), (1,)), ((), ()))
    hi = lax.Precision.HIGHEST
    lane = lax.broadcasted_iota(jnp.int32, (S5_GROUP, LANES), 1)
    end_lane = lax.broadcasted_iota(jnp.int32, end_re.shape, 1)
    for gi in range(2):
        mine = (lane < p) if gi == 0 else (lane >= p)
        own_re, own_im = jnp.where(mine, bb_re, 0.0), jnp.where(mine, bb_im, 0.0)
        strip = (lax.dot_general(own_re, lag_re, nt_dims, precision=hi, preferred_element_type=F32)
                 - lax.dot_general(own_im, lag_im, nt_dims, precision=hi, preferred_element_type=F32))
        lo, up = strip[:, :LANES], strip[:, LANES:]
        blocks = [strip]
        for s in range(1, S5_TC):
            sh = (s % half) * S5_GROUP
            lo_r = pltpu.roll(lo, sh, 1) if sh else lo
            up_r = pltpu.roll(up, sh, 1) if sh else up
            if s < half:
                blocks.append(jnp.concatenate([jnp.where(lane >= sh, lo_r, 0.0),
                                               jnp.where(lane >= sh, up_r, lo_r)], axis=1))
            else:
                blocks.append(jnp.concatenate([jnp.zeros_like(lo), jnp.where(lane >= sh, lo_r, 0.0)], axis=1))
        toeplitz = jnp.concatenate(blocks, axis=0)
        own_end = (end_lane < p) if gi == 0 else (end_lane >= p)
        ends = jnp.concatenate([jnp.where(own_end, end_re, 0.0), jnp.where(own_end, end_im, 0.0)], axis=1)
        w16_ref[gi] = jnp.concatenate([toeplitz, ends], axis=1).astype(BF16)
        w8_ref[gi] = jnp.concatenate([toeplitz[:n8, :n8], ends[n8:, :]], axis=1).astype(BF16)
    ap_ref[0] = jnp.concatenate([powers[S5_TC][0], powers[S5_TC][1], powers[half][0], powers[half][1]], axis=0)


def _s5_params(lam_re, lam_im, log_dt, b_re, b_im, c_re, c_im, d_skip):
    g, p, c = S5_GROUPS, S5_STATE, S5_GROUP
    npair = g // 2
    w16, w8 = S5_TC * c, S5_TC // 2 * c

    def pair_lanes(x):
        return x.reshape(npair, 2, x.shape[1], p).transpose(0, 2, 1, 3).reshape(npair, x.shape[1], 2 * p)

    row = lambda x: pair_lanes(x.reshape(g, 1, p))
    blk = lambda *shape: pl.BlockSpec((1,) + shape, lambda i: (i, 0, 0))
    two = lambda *shape: pl.BlockSpec((2,) + shape, lambda i: (i, 0, 0))
    wt16, cc16, wt8, cc8, apow = pl.pallas_call(
        _s5_param_kernel, grid=(npair,),
        in_specs=[blk(1, 2 * p)] * 3 + [blk(c, 2 * p)] * 4,
        out_specs=[two(w16, w16 + 4 * p), blk(4 * p, 2 * w16), two(w8, w8 + 4 * p), blk(4 * p, 2 * w8),
                   blk(4, 2 * p)],
        out_shape=[jax.ShapeDtypeStruct((g, w16, w16 + 4 * p), BF16),
                   jax.ShapeDtypeStruct((npair, 4 * p, 2 * w16), BF16),
                   jax.ShapeDtypeStruct((g, w8, w8 + 4 * p), BF16),
                   jax.ShapeDtypeStruct((npair, 4 * p, 2 * w8), BF16),
                   jax.ShapeDtypeStruct((npair, 4, 2 * p), F32)],
        compiler_params=_params("parallel"),
        name="s5_params",
    )(row(lam_re), row(lam_im), row(jnp.broadcast_to(log_dt[:, None], (g, p))),
      pair_lanes(b_re.transpose(0, 2, 1)), pair_lanes(b_im.transpose(0, 2, 1)),
      pair_lanes(c_re), pair_lanes(c_im))
    d = d_skip.reshape(S5_NBLK, 1, S5_GB * c)

    def transition(r):
        return jnp.concatenate([apow[:, r].reshape(S5_NBLK, 1, S5_CH), apow[:, r + 1].reshape(S5_NBLK, 1, S5_CH)],
                               axis=-1)

    return {S5_TC: (wt16, cc16, transition(0), d), S5_TC // 2: (wt8, cc8, transition(2), d)}


def _unit_transpose(vs):
    unit = lax.broadcasted_iota(jnp.int32, vs[0].shape, 1) >> int(math.log2(S5_GROUP))
    for dist in (4, 2, 1):
        keep = (unit & dist) == 0
        nxt = list(vs)
        for i in range(8):
            if i & dist == 0:
                a, b = vs[i], vs[i + dist]
                nxt[i] = jnp.where(keep, a, pltpu.roll(b, dist * S5_GROUP, 1))
                nxt[i + dist] = jnp.where(keep, pltpu.roll(a, LANES - dist * S5_GROUP, 1), b)
        vs = nxt
    return vs


def _s5_chunk_kernel(z_ref, w_ref, cc_ref, a_ref, d_ref, h0r_ref, h0i_ref,
                     y_ref, hfr_ref, hfi_ref, ut_ref, ug_ref, yg_ref, e_ref, *, nsl, rps, tc, nseg):
    uw = tc * S5_GROUP
    nq = uw // LANES
    rows = nsl * rps
    nk = rps if nsl > 1 else 1
    ns = rows // nk
    npair = S5_GB // 2
    rchunk = min(rows, 64)

    for s in range(nsl):
        for t in range(tc):
            ut_ref[t, s * rps:(s + 1) * rps, :] = z_ref[pl.ds(s * rps * tc + t, rps, stride=tc), :]

    def row_chunk(rc):
        return pl.ds(pl.multiple_of(rc * rchunk, rchunk), rchunk)

    def to_groups(rc, carry):
        for q in range(nq):
            per_group = _unit_transpose([ut_ref[q * 8 + t, row_chunk(rc), :] for t in range(8)])
            for g in range(S5_GB):
                ug_ref[g, row_chunk(rc), q * LANES:(q + 1) * LANES] = per_group[g].astype(BF16)
        return carry

    lax.fori_loop(0, rows // rchunk, to_groups, 0)

    for pair in range(npair):
        ends = None
        for gi in range(2):
            g = 2 * pair + gi
            em = jnp.dot(ug_ref[g], w_ref[g], preferred_element_type=F32)
            yg_ref[g] = em[:, :uw]
            ends = em[:, uw:] if ends is None else ends + em[:, uw:]
        for part, blk in ((ends[:, :LANES], pair), (ends[:, LANES:], npair + pair)):
            if nk == 1:
                e_ref[blk] = part
            else:
                for s in range(nsl):
                    e_ref[blk, pl.ds(s, rps, stride=nsl), :] = part[s * rps:(s + 1) * rps, :]

    a_row = [(a_ref[0, :, i * LANES:(i + 1) * LANES], a_ref[0, :, S5_CH + i * LANES:S5_CH + (i + 1) * LANES])
             for i in range(npair)]
    a_full = [(jnp.broadcast_to(ar, (ns, LANES)), jnp.broadcast_to(ai, (ns, LANES))) for ar, ai in a_row]

    def slab(t):
        return pl.ds(pl.multiple_of(t * ns, ns), ns)

    def run_scan(h, store):
        def step(t, carry):
            out = []
            for i in range(npair):
                hr, hi = carry[i]
                ar, ai = a_full[i]
                er = e_ref[i, slab(t), :]
                ei = e_ref[npair + i, slab(t), :]
                if store:
                    e_ref[i, slab(t), :] = hr
                    e_ref[npair + i, slab(t), :] = hi
                out.append((ar * hr - ai * hi + er, ar * hi + ai * hr + ei))
            return tuple(out)

        if nk == 1:
            return step(0, h)
        return lax.fori_loop(0, nk, step, h, unroll=4)

    h0 = tuple((h0r_ref[:, i * LANES:(i + 1) * LANES], h0i_ref[:, i * LANES:(i + 1) * LANES]) for i in range(npair))
    if nseg == 1:
        final = run_scan(h0, store=True)
    else:
        zero = jnp.zeros((ns, LANES), F32)
        seg_end = run_scan(tuple((zero, zero) for _ in range(npair)), store=False)
        second = (lax.broadcasted_iota(jnp.int32, (ns, LANES), 0) & 1) == 1
        init, final = [], []
        for i in range(npair):
            p_re, p_im = a_row[i]
            for _ in range(int(math.log2(nk))):
                p_re, p_im = p_re * p_re - p_im * p_im, 2.0 * (p_re * p_im)
            (hr, hi), (er, ei) = h0[i], seg_end[i]
            i_re = jnp.where(second, p_re * hr - p_im * hi + pltpu.roll(er, 1, 0), hr)
            i_im = jnp.where(second, p_re * hi + p_im * hr + pltpu.roll(ei, 1, 0), hi)
            init.append((i_re, i_im))
            final.append((p_re * i_re - p_im * i_im + er, p_re * i_im + p_im * i_re + ei))
        run_scan(tuple(init), store=True)
    for i in range(npair):
        hfr_ref[:, i * LANES:(i + 1) * LANES] = final[i][0]
        hfi_ref[:, i * LANES:(i + 1) * LANES] = final[i][1]

    def chunk_rows(blk):
        if nk == 1:
            return e_ref[blk]
        return jnp.concatenate([e_ref[blk, pl.ds(s, rps, stride=nsl), :] for s in range(nsl)], axis=0)

    for pair in range(npair):
        h_in = jnp.concatenate([chunk_rows(pair), chunk_rows(npair + pair)], axis=1).astype(BF16)
        carried = jnp.dot(h_in, cc_ref[pair], preferred_element_type=F32)
        for gi in range(2):
            yg_ref[2 * pair + gi] += carried[:, gi * uw:(gi + 1) * uw]

    d_row = d_ref[0]

    def to_tokens(rc, carry):
        for q in range(nq):
            per_tau = _unit_transpose([yg_ref[g, row_chunk(rc), q * LANES:(q + 1) * LANES] for g in range(S5_GB)])
            for t in range(8):
                y = per_tau[t] + d_row * ut_ref[q * 8 + t, row_chunk(rc), :]
                ut_ref[q * 8 + t, row_chunk(rc), :] = jax.nn.gelu(y, approximate=True)
        return carry

    lax.fori_loop(0, rows // rchunk, to_tokens, 0)
    for s in range(nsl):
        for t in range(tc):
            y_ref[pl.ds(s * rps * tc + t, rps, stride=tc), :] = ut_ref[t, s * rps:(s + 1) * rps, :]


def _s5_branch(z, weights, h0_re, h0_im, *, batch, seq):
    tokens = batch * seq
    if seq > S5_TC:
        tc, nseg = S5_TC, 2
        nsl, rps = batch * nseg, seq // (nseg * tc)
        assert rps & (rps - 1) == 0
        h0_re, h0_im = jnp.repeat(h0_re, nseg, axis=0), jnp.repeat(h0_im, nseg, axis=0)
    else:
        tc, nseg, nsl, rps = seq, 1, 1, batch
    w, cc, a, d = weights[tc]
    uw = tc * S5_GROUP
    rows = nsl * rps
    ns = h0_re.shape[0]
    kern = functools.partial(_s5_chunk_kernel, nsl=nsl, rps=rps, tc=tc, nseg=nseg)
    state_spec = pl.BlockSpec((ns, S5_CH), lambda k: (0, k))
    state_shape = jax.ShapeDtypeStruct((ns, S5_GROUPS * S5_STATE), F32)
    y, hf_re, hf_im = pl.pallas_call(
        kern, grid=(S5_NBLK,),
        in_specs=[pl.BlockSpec((tokens, LANES), lambda k: (0, COL_U // LANES + k)),
                  pl.BlockSpec((S5_GB, uw, uw + 4 * S5_STATE), lambda k: (k, 0, 0)),
                  pl.BlockSpec((S5_GB // 2, 4 * S5_STATE, 2 * uw), lambda k: (k, 0, 0)),
                  pl.BlockSpec((1, 1, 2 * S5_CH), lambda k: (k, 0, 0)),
                  pl.BlockSpec((1, 1, LANES), lambda k: (k, 0, 0)),
                  state_spec, state_spec],
        out_specs=[pl.BlockSpec((tokens, LANES), lambda k: (0, k)), state_spec, state_spec],
        out_shape=[jax.ShapeDtypeStruct((tokens, S5_WIDTH), F32), state_shape, state_shape],
        scratch_shapes=[pltpu.VMEM((tc, rows, LANES), F32), pltpu.VMEM((S5_GB, rows, uw), BF16),
                        pltpu.VMEM((S5_GB, rows, uw), F32), pltpu.VMEM((S5_GB, rows, LANES), F32)],
        compiler_params=_params("parallel"),
        name="s5_chunked",
    )(z, w, cc, a, d, h0_re, h0_im)
    if nseg == 2:
        hf_re, hf_im = hf_re[1::2], hf_im[1::2]
    return y, hf_re, hf_im


def _gla_kernel(q_ref, k_ref, v_ref, r_ref, al_ref, wa_ref, ba_ref, gn_ref, s0_ref,
                y_ref, sf_ref, s_scr, *, ntile, groups, chained, width):
    rt = GLA_TILE
    c = rt // groups
    shift = int(math.log2(c))
    row_g = lax.broadcasted_iota(jnp.int32, (rt, rt), 0)
    col_g = lax.broadcasted_iota(jnp.int32, (rt, rt), 1)
    same = (row_g >> shift) == (col_g >> shift)
    causal = same & (row_g >= col_g)
    cum_w = jnp.concatenate([causal.astype(BF16), same.astype(BF16)], axis=0)
    tn_dims = (((0,), (0,)), ((), ()))
    nt_dims = (((1,), (1,)), ((), ()))

    def split3(x):
        hi = x.astype(BF16)
        r1 = x - hi.astype(F32)
        mid = r1.astype(BF16)
        lo = (r1 - mid.astype(F32)).astype(BF16)
        return jnp.concatenate([hi, mid, lo], axis=1)

    if chained:
        s_scr[...] = s0_ref[0, 0]

    def body(it, carry):
        tiles = [it * width + u for u in range(width)]
        rows = [pl.ds(pl.multiple_of(t * rt, rt), rt) for t in tiles]
        log_a = []
        for u in range(width):
            x = jnp.dot(al_ref[rows[u], :].astype(BF16), wa_ref[...], preferred_element_type=F32) + ba_ref[...]
            log_a.append((jnp.minimum(x, 0.0) - jnp.log1p(jnp.exp(-jnp.abs(x)))) * (1.0 / GLA_TAU))
        la3 = [split3(la) for la in log_a]
        cums, e_col = [], []
        for u in range(width):
            cs = jnp.dot(cum_w, la3[u], preferred_element_type=F32)
            cums.append(cs[:, :LANES] + cs[:, LANES:2 * LANES] + cs[:, 2 * LANES:])
            e_col.append(jnp.exp(cums[u].T))
        qb, kd, v, att, upd = [], [], [], [], []
        for u in range(width):
            b = cums[u][:rt]
            b_end = cums[u][rt:]
            k = k_ref[rows[u], :]
            v.append(v_ref[rows[u], :])
            mid = 0.5 * b_end
            q_mid = q_ref[rows[u], :] * (GLA_DK ** -0.5) * jnp.exp(b - mid)
            k_mid = k * jnp.exp(mid - b)
            e_mid = jnp.exp(mid)
            qb.append(q_mid * e_mid)
            kd.append(k_mid * e_mid)
            a = lax.dot_general(q_mid.astype(BF16), k_mid.astype(BF16), nt_dims, preferred_element_type=F32)
            att.append(jnp.where(causal, a, 0.0).astype(BF16))
            upd.append([lax.dot_general(kd[u][g * c:(g + 1) * c].astype(BF16), v[u][g * c:(g + 1) * c].astype(BF16),
                                        tn_dims, preferred_element_type=F32) for g in range(groups)])
        o = [jnp.dot(att[u], v[u].astype(BF16), preferred_element_type=F32) for u in range(width)]
        for u in range(width):
            o_state = []
            for g in range(groups):
                s = s_scr[...] if chained else s0_ref[tiles[u] * groups + g, 0]
                o_state.append(jnp.dot(qb[u][g * c:(g + 1) * c].astype(BF16), s.astype(BF16),
                                       preferred_element_type=F32))
                s_new = s * e_col[u][:, rt + g * c:rt + g * c + 1] + upd[u][g]
                if chained:
                    s_scr[...] = s_new
                else:
                    sf_ref[tiles[u] * groups + g, 0] = s_new
            o[u] = o[u] + (o_state[0] if groups == 1 else jnp.concatenate(o_state, axis=0))
        for u in range(width):
            y = o[u] * lax.rsqrt(jnp.mean(o[u] * o[u], axis=-1, keepdims=True) + RMS_EPS)
            y = y * gn_ref[...]
            r = r_ref[rows[u], :]
            y_ref[rows[u], :] = (y * (r * jax.nn.sigmoid(r))).astype(y_ref.dtype)
        return carry

    lax.fori_loop(0, ntile // width, body, 0)
    if chained:
        sf_ref[0, 0] = s_scr[...]


def _gla(z, alow, w_a2, b_a, g_norm, s0, casts=None, *, batch, seq, chained):
    chunk = math.gcd(seq, GLA_CHUNK)
    if chained:
        assert chunk == GLA_TILE
        nb, groups, width = 1, 1, 8
    else:
        assert GLA_TILE % seq == 0 and chunk == seq
        nb, groups, width = 32, GLA_TILE // seq, 4
    rows = nb * seq
    kern = functools.partial(_gla_kernel, ntile=rows // GLA_TILE, groups=groups, chained=chained, width=width)
    state_spec = pl.BlockSpec((nb, 1, GLA_DK, GLA_DV), lambda i, h: (i, h, 0, 0))
    (y, s_fin), made, _ = _call_with_casts(
        kern, casts, grid=(batch // nb, GLA_HEADS),
        in_specs=[pl.BlockSpec((rows, GLA_DK), lambda i, h: (i, COL_Q // GLA_DK + h)),
                  pl.BlockSpec((rows, GLA_DK), lambda i, h: (i, COL_K // GLA_DK + h)),
                  pl.BlockSpec((rows, GLA_DV), lambda i, h: (i, COL_V // GLA_DV + h)),
                  pl.BlockSpec((rows, GLA_DV), lambda i, h: (i, COL_R // GLA_DV + h)),
                  pl.BlockSpec((rows, LANES), lambda i, h: (i, 0)),
                  pl.BlockSpec((LANES, GLA_DK), lambda i, h: (0, h)),
                  pl.BlockSpec((1, GLA_DK), lambda i, h: (0, h)),
                  pl.BlockSpec((1, GLA_DV), lambda i, h: (0, h)),
                  state_spec],
        out_specs=[pl.BlockSpec((rows, GLA_DV), lambda i, h: (i, h)), state_spec],
        out_shape=[jax.ShapeDtypeStruct((batch * seq, GLA_HEADS * GLA_DV), BF16),
                   jax.ShapeDtypeStruct((batch, GLA_HEADS, GLA_DK, GLA_DV), F32)],
        operands=(z, z, z, z, alow, w_a2, b_a, g_norm, s0),
        scratch_shapes=[pltpu.VMEM((GLA_DK, GLA_DV), F32)],
        compiler_params=_params("parallel", "parallel"),
        name="gla")
    return y, s_fin, made


def _attn_kernel(q_ref, k_ref, v_ref, o_ref):
    nt_dims = (((1,), (1,)), ((), ()))
    cols = [slice(h * XA_HEAD_DIM, (h + 1) * XA_HEAD_DIM) for h in range(XA_HEADS)]
    scores = [lax.dot_general(q_ref[:, c].astype(BF16), k_ref[0, :, c].astype(BF16), nt_dims,
                              preferred_element_type=F32) * (XA_HEAD_DIM ** -0.5) for c in cols]
    probs = []
    for s in scores:
        p = jnp.exp(s - jnp.max(s, axis=-1, keepdims=True))
        probs.append((p / jnp.sum(p, axis=-1, keepdims=True)).astype(BF16))
    for c, p in zip(cols, probs):
        o_ref[:, c] = jnp.dot(p, v_ref[0, :, c].astype(BF16), preferred_element_type=F32).astype(o_ref.dtype)


def _attn_prompt(z, mem_k, mem_v, *, batch, seq, rc):
    nrc = seq // rc
    kv_spec = pl.BlockSpec((1, MEM_LEN, XA_WIDTH), lambda b, c: (b, 0, 0))
    return pl.pallas_call(
        _attn_kernel, grid=(batch, nrc),
        in_specs=[pl.BlockSpec((rc, XA_WIDTH), lambda b, c: (b * nrc + c, COL_QX // XA_WIDTH)), kv_spec, kv_spec],
        out_specs=pl.BlockSpec((rc, XA_WIDTH), lambda b, c: (b * nrc + c, 0)),
        out_shape=jax.ShapeDtypeStruct((batch * seq, XA_WIDTH), BF16),
        compiler_params=_params("parallel", "parallel"),
        name="attn_prompt",
    )(z, mem_k, mem_v)


XA_HALF = XA_HEAD_DIM // 2
XA_ROWS = 2 * XA_HEADS


def _attn_cache_body(q_ref, k_ref, v_ref, o_ref, *, nseq, seq, nblk, grid):
    nt_dims = (((1,), (1,)), ((), ()))

    def half_rows(ref, bi, h, half):
        return ref[bi, pl.ds(half * XA_HEADS + h, MEM_LEN, stride=XA_ROWS), :].astype(BF16)

    @pl.when(pl.program_id(0) * grid[1] + pl.program_id(1) < nblk)
    def _():
        scores = []
        for bi in range(nseq):
            rows = slice(bi * seq, (bi + 1) * seq)
            for h in range(XA_HEADS):
                lo = h * XA_HEAD_DIM
                s = lax.dot_general(q_ref[rows, lo:lo + XA_HALF].astype(BF16), half_rows(k_ref, bi, h, 0),
                                    nt_dims, preferred_element_type=F32)
                s = s + lax.dot_general(q_ref[rows, lo + XA_HALF:lo + XA_HEAD_DIM].astype(BF16),
                                        half_rows(k_ref, bi, h, 1), nt_dims, preferred_element_type=F32)
                scores.append(s * (XA_HEAD_DIM ** -0.5))
        probs = []
        for s in scores:
            p = jnp.exp(s - jnp.max(s, axis=-1, keepdims=True))
            probs.append((p / jnp.sum(p, axis=-1, keepdims=True)).astype(BF16))
        for h in range(XA_HEADS):
            for half in range(2):
                lo = h * XA_HEAD_DIM + half * XA_HALF
                out = [jnp.dot(probs[bi * XA_HEADS + h], half_rows(v_ref, bi, h, half),
                               preferred_element_type=F32) for bi in range(nseq)]
                o_ref[:, lo:lo + XA_HALF] = jnp.concatenate(out, axis=0).astype(o_ref.dtype)


def _cache_rows(cache):
    bs = cache.shape[0]
    c = cache.reshape(bs, MEM_LEN, XA_HEADS, 2, XA_HALF).transpose(0, 1, 3, 2, 4)
    return c.reshape(bs, MEM_LEN * XA_ROWS, XA_HALF)


def _attn_sample_side(z, mem_k, mem_v, *, batch, seq, grid, per_step=2):
    nblk = batch // per_step
    assert nblk <= grid[0] * grid[1] and (per_step * seq) % 16 == 0
    rows = per_step * seq

    def blk(*g):
        return jnp.minimum(g[0] * grid[1] + g[1], nblk - 1)

    kv_spec = pl.BlockSpec((per_step, MEM_LEN * XA_ROWS, XA_HALF), lambda *g: (blk(*g), 0, 0))
    return _SideJob(
        in_specs=[pl.BlockSpec((rows, XA_WIDTH), lambda *g: (blk(*g), COL_QX // XA_WIDTH)), kv_spec, kv_spec],
        out_specs=[pl.BlockSpec((rows, XA_WIDTH), lambda *g: (blk(*g), 0))],
        out_shape=[jax.ShapeDtypeStruct((batch * seq, XA_WIDTH), BF16)],
        operands=(z, mem_k, mem_v),
        body=functools.partial(_attn_cache_body, nseq=per_step, seq=seq, nblk=nblk, grid=grid))


def _merge_kernel(ys_ref, yg_ref, yx_ref, g0_ref, g1_ref, g2_ref, wglu_ref, bglu_ref,
                  w0_ref, w1_ref, w2_ref, o_ref, s5_scr):
    @pl.when(pl.program_id(1) == 0)
    def _():
        y = ys_ref[...]
        lin = jnp.dot(y.astype(BF16), wglu_ref[...], preferred_element_type=F32) + bglu_ref[...]
        s5_scr[...] = (y * jax.nn.sigmoid(lin)).astype(BF16)

    j = pl.program_id(1)
    m = jax.nn.sigmoid(g0_ref[...]) * jnp.dot(s5_scr[...], w0_ref[j], preferred_element_type=F32)
    m = m + jax.nn.sigmoid(g1_ref[...]) * jnp.dot(yg_ref[...], w1_ref[j], preferred_element_type=F32)
    m = m + jax.nn.sigmoid(g2_ref[...]) * jnp.dot(yx_ref[...], w2_ref[j], preferred_element_type=F32)
    o_ref[...] = m.astype(o_ref.dtype)


def _merge(y_s5, y_gla, y_x, z, w_glu, b_glu, w_br_s5, w_br_gla, w_br_x, *, tm, tn):
    t = y_s5.shape[0]
    nj = D_MODEL // tn
    once = pl.Buffered(1)
    wide = pl.BlockSpec((tm, S5_WIDTH), lambda i, j: (i, 0))
    gate = lambda b: pl.BlockSpec((tm, tn), lambda i, j: (i, (COL_GATE + b * D_MODEL) // tn + j))
    w_br = pl.BlockSpec((nj, S5_WIDTH, tn), lambda i, j: (0, 0, 0), pipeline_mode=once)
    return pl.pallas_call(
        _merge_kernel, grid=(t // tm, nj),
        in_specs=[wide, wide, wide, gate(0), gate(1), gate(2),
                  pl.BlockSpec((S5_WIDTH, S5_WIDTH), lambda i, j: (0, 0), pipeline_mode=once),
                  pl.BlockSpec((1, S5_WIDTH), lambda i, j: (0, 0)),
                  w_br, w_br, w_br],
        out_specs=pl.BlockSpec((tm, tn), lambda i, j: (i, j)),
        out_shape=jax.ShapeDtypeStruct((t, D_MODEL), BF16),
        scratch_shapes=[pltpu.VMEM((tm, S5_WIDTH), BF16)],
        compiler_params=_params("parallel", "arbitrary"),
        name="merge",
    )(y_s5, y_gla, y_x, z, z, z, w_glu, b_glu, w_br_s5, w_br_gla, w_br_x)


def _out_ffn_kernel(x_ref, m_ref, wo_ref, gf_ref, wg_ref, wu_ref, wd_ref, gl_ref, o_ref, h_scr, acc_scr):
    k = pl.program_id(1)

    @pl.when(k == 0)
    def _():
        acc_scr[...] = x_ref[...] + jnp.dot(m_ref[...], wo_ref[...], preferred_element_type=F32)
        h_scr[...] = _rms(acc_scr[...], gf_ref[...]).astype(BF16)

    h = h_scr[...]
    gate = jnp.dot(h, wg_ref[...], preferred_element_type=F32)
    up = jnp.dot(h, wu_ref[...], preferred_element_type=F32)
    act = (gate * jax.nn.sigmoid(gate) * up).astype(BF16)
    acc_scr[...] += jnp.dot(act, wd_ref[...], preferred_element_type=F32)

    @pl.when(k == pl.num_programs(1) - 1)
    def _():
        o_ref[...] = _rms(acc_scr[...], gl_ref[...])


def _out_ffn(x, merged, w_out, g_ffn, w_gate, w_up, w_down, g_final, *, tm, th):
    t = x.shape[0]
    row = pl.BlockSpec((tm, D_MODEL), lambda i, k: (i, 0))
    vec = pl.BlockSpec((1, D_MODEL), lambda i, k: (0, 0))
    w_in = pl.BlockSpec((D_MODEL, th), lambda i, k: (0, k))
    return pl.pallas_call(
        _out_ffn_kernel, grid=(t // tm, FFN_HIDDEN // th),
        in_specs=[row, row,
                  pl.BlockSpec((D_MODEL, D_MODEL), lambda i, k: (0, 0), pipeline_mode=pl.Buffered(1)),
                  vec, w_in, w_in, pl.BlockSpec((th, D_MODEL), lambda i, k: (k, 0)), vec],
        out_specs=row,
        out_shape=jax.ShapeDtypeStruct((t, D_MODEL), F32),
        scratch_shapes=[pltpu.VMEM((tm, D_MODEL), BF16), pltpu.VMEM((tm, D_MODEL), F32)],
        compiler_params=pltpu.CompilerParams(dimension_semantics=("parallel", "arbitrary"),
                                             vmem_limit_bytes=OUT_FFN_VMEM),
        name="out_ffn",
    )(x, merged, w_out, g_ffn, w_gate, w_up, w_down, g_final)


def _first_in_proj_kernel(x_ref, g_ref, wt_ref, wa_ref, o_ref, os_ref, main_ref, alow_ref, h_ref):
    @pl.when(pl.program_id(0) == 0)
    def _():
        h = _rms(x_ref[...], g_ref[...]).astype(BF16)
        h_ref[...] = h
        lane = lax.broadcasted_iota(jnp.int32, alow_ref.shape, 1)
        w_low = jnp.where(lane < GLA_RANK, wa_ref[...].T, 0.0).astype(BF16)
        alow_ref[...] = w_low
        os_ref[...] = jnp.dot(h, w_low, preferred_element_type=F32)

    w = wt_ref[...].T.astype(BF16)
    main_ref[...] = w
    o_ref[...] = jnp.dot(h_ref[...], w, preferred_element_type=F32)


def _first_in_proj(x, g, w_in):
    t, _ = x.shape
    d, n = w_in.shape
    wt = w_in.T
    tc = 512
    per_tile = IN_TN // tc
    n_first, n_gate = COL_GATE // tc, (COL_QX - COL_GATE) // tc
    gate_start, qx_start = 4096 + GLA_RANK + XA_WIDTH, 4096 + GLA_RANK

    def src_row(j):
        t8, g8, q8 = tc // SUBLANES, gate_start // SUBLANES, qx_start // SUBLANES
        r8 = jnp.where(j < n_first, j * t8,
                       jnp.where(j < n_first + n_gate, g8 + (j - n_first) * t8,
                                 q8 + (j - n_first - n_gate) * t8))
        return r8 * SUBLANES

    once = pl.Buffered(1)
    return pl.pallas_call(
        _first_in_proj_kernel, grid=(Z_WIDTH // tc,),
        in_specs=[pl.BlockSpec((pl.Element(t), pl.Element(d)), lambda j: (0, 0), pipeline_mode=once),
                  pl.BlockSpec((pl.Element(1), pl.Element(d)), lambda j: (0, 0)),
                  pl.BlockSpec((pl.Element(tc), pl.Element(d)), lambda j: (src_row(j), 0)),
                  pl.BlockSpec((pl.Element(LANES), pl.Element(d)), lambda j: (COL_GATE, 0), pipeline_mode=once)],
        out_specs=[pl.BlockSpec((t, tc), lambda j: (0, j)),
                   pl.BlockSpec((t, LANES), lambda j: (0, 0)),
                   pl.BlockSpec((None, d, tc), lambda j: (j // per_tile, 0, j % per_tile)),
                   pl.BlockSpec((d, LANES), lambda j: (0, 0))],
        out_shape=[jax.ShapeDtypeStruct((t, Z_WIDTH), F32), jax.ShapeDtypeStruct((t, LANES), F32),
                   jax.ShapeDtypeStruct((Z_WIDTH // IN_TN, d, IN_TN), BF16),
                   jax.ShapeDtypeStruct((d, LANES), BF16)],
        scratch_shapes=[pltpu.VMEM((t, d), BF16)],
        compiler_params=_params("arbitrary"),
        name="first_in_proj",
    )(x, g, wt, wt)


def _layer(x, z, alow, memory_attention, s5_re0, s5_im0, gla_s0, w, *, batch, seq, chained, gla_casts=None):
    w = dict(w)
    y_s5, hf_re, hf_im = _s5_branch(z, w['s5'], s5_re0, s5_im0, batch=batch, seq=seq)
    y_gla, gla_s, made = _gla(z, alow, w['gla_w_a2'], w['gla_b_a'], w['gla_norm'], gla_s0, gla_casts,
                              batch=batch, seq=seq, chained=chained)
    w.update(made)
    y_x = memory_attention(w)
    merged = _merge(y_s5, y_gla, y_x, z, w['s5_w_glu'], w['s5_b_glu'],
                    w['w_br_s5'], w['w_br_gla'], w['w_br_xattn'], tm=512, tn=MERGE_TN)
    y = _out_ffn(x, merged, w['w_out'], w['norm_ffn'], w['w_ffn_gate'], w['w_ffn_up'], w['w_ffn_down'],
                 w['norm_final'], tm=512, th=FFN_TH)
    return y, hf_re, hf_im, gla_s, w


def kernel(x_prompt, x_sample, mem_prompt, state_s5_re, state_s5_im, state_gla, cache_mem_k, cache_mem_v,
           norm_mix, w_in, s5_lam_re, s5_lam_im, s5_log_dt, s5_b_re, s5_b_im, s5_c_re, s5_c_im,
           s5_d, s5_w_glu, s5_b_glu, gla_w_a2, gla_b_a, gla_norm, mem_norm, w_mem_k, w_mem_v,
           w_br_s5, w_br_gla, w_br_xattn, w_out, norm_ffn, w_ffn_gate, w_ffn_up, w_ffn_down, norm_final):
    depth = w_in.shape[0]
    assert depth == 1
    bp, sp, d = x_prompt.shape
    bs, ss, _ = x_sample.shape
    n_state = S5_GROUPS * S5_STATE
    row = lambda v: v.reshape(1, -1)

    l = 0
    s5_w = _s5_params(s5_lam_re[l], s5_lam_im[l], s5_log_dt[l], s5_b_re[l], s5_b_im[l],
                      s5_c_re[l], s5_c_im[l], s5_d[l])
    w = {
        'norm_mix': row(norm_mix[l]),
        's5': s5_w,
        's5_b_glu': row(s5_b_glu[l]),
        'gla_w_a2': jnp.pad(gla_w_a2[l], ((0, LANES - GLA_RANK), (0, 0))).astype(BF16),
        'gla_b_a': row(gla_b_a[l]), 'gla_norm': row(gla_norm[l]),
        'norm_ffn': row(norm_ffn[l]), 'norm_final': row(norm_final),
    }
    ffn_casts = {'w_ffn_gate': _CastJob(w_ffn_gate[l], row_axis=0), 'w_ffn_up': _CastJob(w_ffn_up[l], row_axis=0),
                 'w_ffn_down': _CastJob(w_ffn_down[l], row_axis=1)}
    mix_casts = {'w_out': _CastJob(w_out[l], row_axis=0), 's5_w_glu': _CastJob(s5_w_glu[l], row_axis=0),
                 'w_br_s5': _CastJob(w_br_s5[l], row_axis=0, col_tile=MERGE_TN),
                 'w_br_gla': _CastJob(w_br_gla[l], row_axis=0, col_tile=MERGE_TN),
                 'w_br_xattn': _CastJob(w_br_xattn[l], row_axis=0, col_tile=MERGE_TN),
                 'w_mem_k': _CastJob(w_mem_k[l], row_axis=0, col_tile=IN_TN),
                 'w_mem_v': _CastJob(w_mem_v[l], row_axis=0, col_tile=IN_TN)}

    zero_s5 = jnp.zeros((bp, n_state), F32)
    zero_gla = jnp.zeros((bp, GLA_HEADS, GLA_DK, GLA_DV), F32)
    xp, xs = x_prompt.reshape(bp * sp, d), x_sample.reshape(bs * ss, d)

    z_s, alow_s, w['w_main'], w['w_alow'] = _first_in_proj(xs, w['norm_mix'], w_in[l])
    cache_k, cache_v = _cache_rows(cache_mem_k[l]), _cache_rows(cache_mem_v[l])
    z_p, alow_p, made, (yx_s,) = _norm_matmul(
        xp, w['norm_mix'], w['w_main'], w['w_alow'], ffn_casts,
        lambda grid: _attn_sample_side(z_s, cache_k, cache_v, batch=bs, seq=ss, grid=grid), tm=1024)
    w.update(made)

    mem_kv = {}

    def prompt_attention(w):
        mem = mem_prompt.reshape(bp * MEM_LEN, d)
        for name in ('w_mem_k', 'w_mem_v'):
            mem_kv[name] = _norm_matmul(mem, row(mem_norm[l]), w[name], tm=512).reshape(bp, MEM_LEN, XA_WIDTH)
        return _attn_prompt(z_p, mem_kv['w_mem_k'], mem_kv['w_mem_v'], batch=bp, seq=sp, rc=1024)

    yp, p_re, p_im, p_gla, w = _layer(xp, z_p, alow_p, prompt_attention, zero_s5, zero_s5, zero_gla, w,
                                      batch=bp, seq=sp, chained=True, gla_casts=mix_casts)
    mk, mv = mem_kv['w_mem_k'], mem_kv['w_mem_v']

    ys, s_re, s_im, s_gla, _ = _layer(xs, z_s, alow_s, lambda w: yx_s,
                                      state_s5_re[l].reshape(bs, n_state), state_s5_im[l].reshape(bs, n_state),
                                      state_gla[l], w, batch=bs, seq=ss, chained=False)

    s5_shape_p = (1, bp, S5_GROUPS, S5_STATE)
    s5_shape_s = (1, bs, S5_GROUPS, S5_STATE)
    kv_shape = (1, bp, MEM_LEN, XA_HEADS, XA_HEAD_DIM)
    return (yp.reshape(bp, sp, d), ys.reshape(bs, ss, d),
            p_re.reshape(s5_shape_p), p_im.reshape(s5_shape_p), p_gla[None],
            mk.reshape(kv_shape), mv.reshape(kv_shape),
            s_re.reshape(s5_shape_s), s_im.reshape(s5_shape_s), s_gla[None])
```

```python
import functools
import math
from typing import Callable, NamedTuple, Optional

import jax
import jax.numpy as jnp
from jax import lax
from jax.experimental import pallas as pl
from jax.experimental.pallas import tpu as pltpu

F32 = jnp.float32
BF16 = jnp.bfloat16

D_MODEL = 2048
S5_WIDTH = 1024
S5_GROUP = 16
S5_GROUPS = 64
S5_STATE = 64
GLA_HEADS = 4
GLA_DK = 128
GLA_DV = 256
GLA_RANK = 16
GLA_TAU = 16.0
GLA_CHUNK = 64
XA_HEADS = 4
XA_HEAD_DIM = 256
XA_WIDTH = 1024
MEM_LEN = 256
FFN_HIDDEN = 5632
RMS_EPS = 1e-6

LANES = 128
SUBLANES = 8
VMEM_LIMIT = 56 * 1024 * 1024
OUT_FFN_VMEM = 52 * 1024 * 1024

COL_U = 0
COL_Q = 1024
COL_K = 1536
COL_V = 2048
COL_R = 3072
COL_GATE = 4096
COL_QX = 10240
Z_WIDTH = 11264

S5_GB = 8
S5_CH = S5_GB * S5_STATE
S5_NBLK = S5_GROUPS // S5_GB
S5_TC = 16
S5_PAIRS_PER_STEP = 4
GLA_TILE = 64
MERGE_TN = 1024
IN_TN = 1024
FFN_TH = 512


def _params(*sem):
    return pltpu.CompilerParams(dimension_semantics=sem, vmem_limit_bytes=VMEM_LIMIT)


def _rms(x, g):
    return x * lax.rsqrt(jnp.mean(x * x, axis=-1, keepdims=True) + RMS_EPS) * g


class _CastJob(NamedTuple):
    src: jax.Array
    row_axis: int
    col_tile: Optional[int] = None

    def specs(self, grid):
        rows, cols = self.src.shape
        ra, ca = self.row_axis, 1 - self.row_axis
        assert rows % grid[ra] == 0 and cols % grid[ca] == 0
        br, bc = rows // grid[ra], cols // grid[ca]
        in_spec = pl.BlockSpec((br, bc), lambda *g: (g[ra], g[ca]))
        if self.col_tile is None:
            return in_spec, in_spec, jax.ShapeDtypeStruct((rows, cols), BF16)
        assert self.col_tile % bc == 0 and cols % self.col_tile == 0
        per = self.col_tile // bc
        out_spec = pl.BlockSpec((None, br, bc), lambda *g: (g[ca] // per, g[ra], g[ca] % per))
        return in_spec, out_spec, jax.ShapeDtypeStruct((cols // self.col_tile, rows, self.col_tile), BF16)


class _SideJob(NamedTuple):
    in_specs: list
    out_specs: list
    out_shape: list
    operands: tuple
    body: Callable


def _with_side_work(kernel, n_in, n_out, n_cast, side):
    n_side_in = len(side.in_specs) if side else 0
    n_side_out = len(side.out_specs) if side else 0

    def wrapped(*refs):
        ins, rest = refs[:n_in], refs[n_in:]
        srcs, rest = rest[:n_cast], rest[n_cast:]
        side_ins, rest = rest[:n_side_in], rest[n_side_in:]
        outs, rest = rest[:n_out], rest[n_out:]
        dsts, rest = rest[:n_cast], rest[n_cast:]
        side_outs, scratch = rest[:n_side_out], rest[n_side_out:]
        for src, dst in zip(srcs, dsts):
            dst[...] = src[...].astype(dst.dtype)
        kernel(*ins, *outs, *scratch)
        if side:
            side.body(*side_ins, *side_outs)
    return wrapped


def _call_with_casts(kernel, casts, *, grid, in_specs, out_specs, out_shape, operands, side=None, **kwargs):
    casts = casts or {}
    specs = [job.specs(grid) for job in casts.values()]
    side_in = list(side.in_specs) if side else []
    side_out = list(side.out_specs) if side else []
    results = pl.pallas_call(
        _with_side_work(kernel, len(in_specs), len(out_specs), len(specs), side), grid=grid,
        in_specs=list(in_specs) + [s[0] for s in specs] + side_in,
        out_specs=list(out_specs) + [s[1] for s in specs] + side_out,
        out_shape=list(out_shape) + [s[2] for s in specs] + (list(side.out_shape) if side else []),
        **kwargs,
    )(*operands, *[job.src for job in casts.values()], *(side.operands if side else ()))
    n, c = len(out_specs), len(specs)
    return results[:n], dict(zip(casts.keys(), results[n:n + c])), results[n + c:]


def _norm_matmul_kernel(x_ref, g_ref, w_ref, o_ref, h_ref):
    @pl.when(pl.program_id(1) == 0)
    def _():
        h_ref[...] = _rms(x_ref[...], g_ref[...]).astype(BF16)

    o_ref[...] = jnp.dot(h_ref[...], w_ref[0], preferred_element_type=F32)


def _norm_matmul2_kernel(x_ref, g_ref, w_ref, ws_ref, o_ref, os_ref, h_ref):
    @pl.when(pl.program_id(1) == 0)
    def _():
        h = _rms(x_ref[...], g_ref[...]).astype(BF16)
        h_ref[...] = h
        os_ref[...] = jnp.dot(h, ws_ref[...], preferred_element_type=F32)

    o_ref[...] = jnp.dot(h_ref[...], w_ref[0], preferred_element_type=F32)


def _norm_matmul(x, g, w, w_small=None, casts=None, make_side=None, *, tm):
    t, d = x.shape
    nj, _, tn = w.shape
    n = nj * tn
    grid = (t // tm, n // tn)
    in_specs = [pl.BlockSpec((tm, d), lambda i, j: (i, 0)),
                pl.BlockSpec((1, d), lambda i, j: (0, 0)),
                pl.BlockSpec((1, d, tn), lambda i, j: (j, 0, 0))]
    out_specs = pl.BlockSpec((tm, tn), lambda i, j: (i, j))
    out_shape = jax.ShapeDtypeStruct((t, n), F32)
    scratch = [pltpu.VMEM((tm, d), BF16)]
    if w_small is None:
        return pl.pallas_call(_norm_matmul_kernel, grid=grid, in_specs=in_specs, out_specs=out_specs,
                              out_shape=out_shape, scratch_shapes=scratch,
                              compiler_params=_params("parallel", "arbitrary"),
                              name="norm_matmul")(x, g, w)
    ns = w_small.shape[1]
    in_specs.append(pl.BlockSpec((d, ns), lambda i, j: (0, 0)))
    (z, narrow), made, side_out = _call_with_casts(
        _norm_matmul2_kernel, casts, grid=grid, in_specs=in_specs,
        side=make_side(grid) if make_side else None,
        out_specs=[out_specs, pl.BlockSpec((tm, ns), lambda i, j: (i, 0))],
        out_shape=[out_shape, jax.ShapeDtypeStruct((t, ns), F32)],
        operands=(x, g, w, w_small), scratch_shapes=scratch,
        compiler_params=_params("arbitrary" if make_side else "parallel", "arbitrary"), name="in_proj")
    return z, narrow, made, side_out


def _s5_param_kernel(*refs):
    for pi in range(S5_PAIRS_PER_STEP):
        _s5_pair_operands(pi, *refs)


def _s5_pair_operands(pi, lr_ref, li_ref, ldt_ref, btr_ref, bti_ref, cr_ref, ci_ref,
                      w16_ref, cc16_ref, w8_ref, cc8_ref, ap_ref):
    p = S5_STATE
    half = S5_TC // 2
    lam_re, lam_im = lr_ref[pi], li_ref[pi]
    dt = jnp.exp(ldt_ref[pi])
    mag = jnp.exp(lam_re * dt)
    a_re = mag * jnp.cos(lam_im * dt)
    a_im = mag * jnp.sin(lam_im * dt)
    den = lam_re * lam_re + lam_im * lam_im
    coef_re = ((a_re - 1.0) * lam_re + a_im * lam_im) / den
    coef_im = (a_im * lam_re - (a_re - 1.0) * lam_im) / den
    bt_re, bt_im = btr_ref[pi], bti_ref[pi]
    bb_re = coef_re * bt_re - coef_im * bt_im
    bb_im = coef_re * bt_im + coef_im * bt_re
    c_re, c_im = cr_ref[pi], ci_ref[pi]

    pw_re, pw_im = jnp.ones_like(a_re), jnp.zeros_like(a_re)
    ca_re, ca_im, ab_re, ab_im, powers = [], [], [], [], {}
    for j in range(S5_TC + 1):
        powers[j] = (pw_re, pw_im)
        ca_re.append(c_re * pw_re - c_im * pw_im)
        ca_im.append(c_re * pw_im + c_im * pw_re)
        ab_re.append(bb_re * pw_re - bb_im * pw_im)
        ab_im.append(bb_re * pw_im + bb_im * pw_re)
        pw_re, pw_im = pw_re * a_re - pw_im * a_im, pw_re * a_im + pw_im * a_re
    lag_re = jnp.concatenate(ca_re[:S5_TC], axis=0)
    lag_im = jnp.concatenate(ca_im[:S5_TC], axis=0)
    end_re = jnp.concatenate([ab_re[S5_TC - 1 - s] for s in range(S5_TC)], axis=0)
    end_im = jnp.concatenate([ab_im[S5_TC - 1 - s] for s in range(S5_TC)], axis=0)
    car_re = jnp.concatenate(ca_re[1:], axis=0).T
    car_im = -jnp.concatenate(ca_im[1:], axis=0).T

    def pair_blocks(x):
        first = lax.broadcasted_iota(jnp.int32, x.shape, 0) < p
        return jnp.concatenate([jnp.where(first, x, 0.0), jnp.where(first, 0.0, x)], axis=1)

    cc16_ref[pi] = jnp.concatenate([pair_blocks(car_re), pair_blocks(car_im)], axis=0).astype(BF16)
    n8 = half * S5_GROUP
    cc8_ref[pi] = jnp.concatenate([pair_blocks(car_re[:, :n8]), pair_blocks(car_im[:, :n8])], axis=0).astype(BF16)

    nt_dims = (((1,), (1,)), ((), ()))
    hi = lax.Precision.HIGHEST
    lane = lax.broadcasted_iota(jnp.int32, (S5_GROUP, LANES), 1)
    end_lane = lax.broadcasted_iota(jnp.int32, end_re.shape, 1)
    for gi in range(2):
        mine = (lane < p) if gi == 0 else (lane >= p)
        own_re, own_im = jnp.where(mine, bb_re, 0.0), jnp.where(mine, bb_im, 0.0)
        strip = (lax.dot_general(own_re, lag_re, nt_dims, precision=hi, preferred_element_type=F32)
                 - lax.dot_general(own_im, lag_im, nt_dims, precision=hi, preferred_element_type=F32))
        lo, up = strip[:, :LANES], strip[:, LANES:]
        blocks = [strip]
        for s in range(1, S5_TC):
            sh = (s % half) * S5_GROUP
            lo_r = pltpu.roll(lo, sh, 1) if sh else lo
            up_r = pltpu.roll(up, sh, 1) if sh else up
            if s < half:
                blocks.append(jnp.concatenate([jnp.where(lane >= sh, lo_r, 0.0),
                                               jnp.where(lane >= sh, up_r, lo_r)], axis=1))
            else:
                blocks.append(jnp.concatenate([jnp.zeros_like(lo), jnp.where(lane >= sh, lo_r, 0.0)], axis=1))
        toeplitz = jnp.concatenate(blocks, axis=0)
        own_end = (end_lane < p) if gi == 0 else (end_lane >= p)
        ends = jnp.concatenate([jnp.where(own_end, end_re, 0.0), jnp.where(own_end, end_im, 0.0)], axis=1)
        w16_ref[2 * pi + gi] = jnp.concatenate([toeplitz, ends], axis=1).astype(BF16)
        w8_ref[2 * pi + gi] = jnp.concatenate([toeplitz[:n8, :n8], ends[n8:, :]], axis=1).astype(BF16)
    ap_ref[pi] = jnp.concatenate([powers[S5_TC][0], powers[S5_TC][1], powers[half][0], powers[half][1]], axis=0)


def _s5_params(lam_re, lam_im, log_dt, b_re, b_im, c_re, c_im, d_skip):
    g, p, c = S5_GROUPS, S5_STATE, S5_GROUP
    npair = g // 2
    w16, w8 = S5_TC * c, S5_TC // 2 * c

    def pair_lanes(x):
        return x.reshape(npair, 2, x.shape[1], p).transpose(0, 2, 1, 3).reshape(npair, x.shape[1], 2 * p)

    row = lambda x: pair_lanes(x.reshape(g, 1, p))
    pp = S5_PAIRS_PER_STEP
    blk = lambda *shape: pl.BlockSpec((pp,) + shape, lambda i: (i, 0, 0))
    two = lambda *shape: pl.BlockSpec((2 * pp,) + shape, lambda i: (i, 0, 0))
    wt16, cc16, wt8, cc8, apow = pl.pallas_call(
        _s5_param_kernel, grid=(npair // pp,),
        in_specs=[blk(1, 2 * p)] * 3 + [blk(c, 2 * p)] * 4,
        out_specs=[two(w16, w16 + 4 * p), blk(4 * p, 2 * w16), two(w8, w8 + 4 * p), blk(4 * p, 2 * w8),
                   blk(4, 2 * p)],
        out_shape=[jax.ShapeDtypeStruct((g, w16, w16 + 4 * p), BF16),
                   jax.ShapeDtypeStruct((npair, 4 * p, 2 * w16), BF16),
                   jax.ShapeDtypeStruct((g, w8, w8 + 4 * p), BF16),
                   jax.ShapeDtypeStruct((npair, 4 * p, 2 * w8), BF16),
                   jax.ShapeDtypeStruct((npair, 4, 2 * p), F32)],
        compiler_params=_params("parallel"),
        name="s5_params",
    )(row(lam_re), row(lam_im), row(jnp.broadcast_to(log_dt[:, None], (g, p))),
      pair_lanes(b_re.transpose(0, 2, 1)), pair_lanes(b_im.transpose(0, 2, 1)),
      pair_lanes(c_re), pair_lanes(c_im))
    d = d_skip.reshape(S5_NBLK, 1, S5_GB * c)

    def transition(r):
        return jnp.concatenate([apow[:, r].reshape(S5_NBLK, 1, S5_CH), apow[:, r + 1].reshape(S5_NBLK, 1, S5_CH)],
                               axis=-1)

    return {S5_TC: (wt16, cc16, transition(0), d), S5_TC // 2: (wt8, cc8, transition(2), d)}


def _unit_transpose(vs):
    unit = lax.broadcasted_iota(jnp.int32, vs[0].shape, 1) >> int(math.log2(S5_GROUP))
    for dist in (4, 2, 1):
        keep = (unit & dist) == 0
        nxt = list(vs)
        for i in range(8):
            if i & dist == 0:
                a, b = vs[i], vs[i + dist]
                nxt[i] = jnp.where(keep, a, pltpu.roll(b, dist * S5_GROUP, 1))
                nxt[i + dist] = jnp.where(keep, pltpu.roll(a, LANES - dist * S5_GROUP, 1), b)
        vs = nxt
    return vs


def _s5_chunk_kernel(z_ref, w_ref, cc_ref, a_ref, d_ref, h0r_ref, h0i_ref,
                     y_ref, hfr_ref, hfi_ref, ut_ref, ug_ref, yg_ref, e_ref, *, nsl, rps, tc, nseg):
    uw = tc * S5_GROUP
    nq = uw // LANES
    rows = nsl * rps
    nk = rps if nsl > 1 else 1
    ns = rows // nk
    npair = S5_GB // 2
    rchunk = min(rows, 64)

    for s in range(nsl):
        for t in range(tc):
            ut_ref[t, s * rps:(s + 1) * rps, :] = z_ref[pl.ds(s * rps * tc + t, rps, stride=tc), :]

    def row_chunk(rc):
        return pl.ds(pl.multiple_of(rc * rchunk, rchunk), rchunk)

    def to_groups(rc, carry):
        for q in range(nq):
            per_group = _unit_transpose([ut_ref[q * 8 + t, row_chunk(rc), :] for t in range(8)])
            for g in range(S5_GB):
                ug_ref[g, row_chunk(rc), q * LANES:(q + 1) * LANES] = per_group[g].astype(BF16)
        return carry

    lax.fori_loop(0, rows // rchunk, to_groups, 0)

    for pair in range(npair):
        ends = None
        for gi in range(2):
            g = 2 * pair + gi
            em = jnp.dot(ug_ref[g], w_ref[g], preferred_element_type=F32)
            yg_ref[g] = em[:, :uw]
            ends = em[:, uw:] if ends is None else ends + em[:, uw:]
        for part, blk in ((ends[:, :LANES], pair), (ends[:, LANES:], npair + pair)):
            if nk == 1:
                e_ref[blk] = part
            else:
                for s in range(nsl):
                    e_ref[blk, pl.ds(s, rps, stride=nsl), :] = part[s * rps:(s + 1) * rps, :]

    a_row = [(a_ref[0, :, i * LANES:(i + 1) * LANES], a_ref[0, :, S5_CH + i * LANES:S5_CH + (i + 1) * LANES])
             for i in range(npair)]
    a_full = [(jnp.broadcast_to(ar, (ns, LANES)), jnp.broadcast_to(ai, (ns, LANES))) for ar, ai in a_row]

    def slab(t):
        return pl.ds(pl.multiple_of(t * ns, ns), ns)

    def run_scan(h, store):
        def step(t, carry):
            out = []
            for i in range(npair):
                hr, hi = carry[i]
                ar, ai = a_full[i]
                er = e_ref[i, slab(t), :]
                ei = e_ref[npair + i, slab(t), :]
                if store:
                    e_ref[i, slab(t), :] = hr
                    e_ref[npair + i, slab(t), :] = hi
                out.append((ar * hr - ai * hi + er, ar * hi + ai * hr + ei))
            return tuple(out)

        if nk == 1:
            return step(0, h)
        return lax.fori_loop(0, nk, step, h, unroll=4)

    h0 = tuple((h0r_ref[:, i * LANES:(i + 1) * LANES], h0i_ref[:, i * LANES:(i + 1) * LANES]) for i in range(npair))
    if nseg == 1:
        final = run_scan(h0, store=True)
    else:
        zero = jnp.zeros((ns, LANES), F32)
        seg_end = run_scan(tuple((zero, zero) for _ in range(npair)), store=False)
        second = (lax.broadcasted_iota(jnp.int32, (ns, LANES), 0) & 1) == 1
        init, final = [], []
        for i in range(npair):
            p_re, p_im = a_row[i]
            for _ in range(int(math.log2(nk))):
                p_re, p_im = p_re * p_re - p_im * p_im, 2.0 * (p_re * p_im)
            (hr, hi), (er, ei) = h0[i], seg_end[i]
            i_re = jnp.where(second, p_re * hr - p_im * hi + pltpu.roll(er, 1, 0), hr)
            i_im = jnp.where(second, p_re * hi + p_im * hr + pltpu.roll(ei, 1, 0), hi)
            init.append((i_re, i_im))
            final.append((p_re * i_re - p_im * i_im + er, p_re * i_im + p_im * i_re + ei))
        run_scan(tuple(init), store=True)
    for i in range(npair):
        hfr_ref[:, i * LANES:(i + 1) * LANES] = final[i][0]
        hfi_ref[:, i * LANES:(i + 1) * LANES] = final[i][1]

    def chunk_rows(blk):
        if nk == 1:
            return e_ref[blk]
        return jnp.concatenate([e_ref[blk, pl.ds(s, rps, stride=nsl), :] for s in range(nsl)], axis=0)

    for pair in range(npair):
        h_in = jnp.concatenate([chunk_rows(pair), chunk_rows(npair + pair)], axis=1).astype(BF16)
        carried = jnp.dot(h_in, cc_ref[pair], preferred_element_type=F32)
        for gi in range(2):
            yg_ref[2 * pair + gi] += carried[:, gi * uw:(gi + 1) * uw]

    d_row = d_ref[0]

    def to_tokens(rc, carry):
        for q in range(nq):
            per_tau = _unit_transpose([yg_ref[g, row_chunk(rc), q * LANES:(q + 1) * LANES] for g in range(S5_GB)])
            for t in range(8):
                y = per_tau[t] + d_row * ut_ref[q * 8 + t, row_chunk(rc), :]
                ut_ref[q * 8 + t, row_chunk(rc), :] = jax.nn.gelu(y, approximate=True)
        return carry

    lax.fori_loop(0, rows // rchunk, to_tokens, 0)
    for s in range(nsl):
        for t in range(tc):
            y_ref[pl.ds(s * rps * tc + t, rps, stride=tc), :] = ut_ref[t, s * rps:(s + 1) * rps, :]


def _s5_branch(z, weights, h0_re, h0_im, *, batch, seq):
    tokens = batch * seq
    if seq > S5_TC:
        tc, nseg = S5_TC, 2
        nsl, rps = batch * nseg, seq // (nseg * tc)
        assert rps & (rps - 1) == 0
        h0_re, h0_im = jnp.repeat(h0_re, nseg, axis=0), jnp.repeat(h0_im, nseg, axis=0)
    else:
        tc, nseg, nsl, rps = seq, 1, 1, batch
    w, cc, a, d = weights[tc]
    uw = tc * S5_GROUP
    rows = nsl * rps
    ns = h0_re.shape[0]
    kern = functools.partial(_s5_chunk_kernel, nsl=nsl, rps=rps, tc=tc, nseg=nseg)
    state_spec = pl.BlockSpec((ns, S5_CH), lambda k: (0, k))
    state_shape = jax.ShapeDtypeStruct((ns, S5_GROUPS * S5_STATE), F32)
    y, hf_re, hf_im = pl.pallas_call(
        kern, grid=(S5_NBLK,),
        in_specs=[pl.BlockSpec((tokens, LANES), lambda k: (0, COL_U // LANES + k)),
                  pl.BlockSpec((S5_GB, uw, uw + 4 * S5_STATE), lambda k: (k, 0, 0)),
                  pl.BlockSpec((S5_GB // 2, 4 * S5_STATE, 2 * uw), lambda k: (k, 0, 0)),
                  pl.BlockSpec((1, 1, 2 * S5_CH), lambda k: (k, 0, 0)),
                  pl.BlockSpec((1, 1, LANES), lambda k: (k, 0, 0)),
                  state_spec, state_spec],
        out_specs=[pl.BlockSpec((tokens, LANES), lambda k: (0, k)), state_spec, state_spec],
        out_shape=[jax.ShapeDtypeStruct((tokens, S5_WIDTH), F32), state_shape, state_shape],
        scratch_shapes=[pltpu.VMEM((tc, rows, LANES), F32), pltpu.VMEM((S5_GB, rows, uw), BF16),
                        pltpu.VMEM((S5_GB, rows, uw), F32), pltpu.VMEM((S5_GB, rows, LANES), F32)],
        compiler_params=_params("parallel"),
        name="s5_chunked",
    )(z, w, cc, a, d, h0_re, h0_im)
    if nseg == 2:
        hf_re, hf_im = hf_re[1::2], hf_im[1::2]
    return y, hf_re, hf_im


def _gla_kernel(q_ref, k_ref, v_ref, r_ref, al_ref, wa_ref, ba_ref, gn_ref, s0_ref,
                y_ref, sf_ref, s_scr, *, ntile, groups, chained, width):
    rt = GLA_TILE
    c = rt // groups
    shift = int(math.log2(c))
    row_g = lax.broadcasted_iota(jnp.int32, (rt, rt), 0)
    col_g = lax.broadcasted_iota(jnp.int32, (rt, rt), 1)
    same = (row_g >> shift) == (col_g >> shift)
    causal = same & (row_g >= col_g)
    cum_w = jnp.concatenate([causal.astype(BF16), same.astype(BF16)], axis=0)
    tn_dims = (((0,), (0,)), ((), ()))
    nt_dims = (((1,), (1,)), ((), ()))

    def split3(x):
        hi = x.astype(BF16)
        r1 = x - hi.astype(F32)
        mid = r1.astype(BF16)
        lo = (r1 - mid.astype(F32)).astype(BF16)
        return jnp.concatenate([hi, mid, lo], axis=1)

    if chained:
        s_scr[...] = s0_ref[0, 0]

    def body(it, carry):
        tiles = [it * width + u for u in range(width)]
        rows = [pl.ds(pl.multiple_of(t * rt, rt), rt) for t in tiles]
        log_a = []
        for u in range(width):
            x = jnp.dot(al_ref[rows[u], :].astype(BF16), wa_ref[...], preferred_element_type=F32) + ba_ref[...]
            log_a.append((jnp.minimum(x, 0.0) - jnp.log1p(jnp.exp(-jnp.abs(x)))) * (1.0 / GLA_TAU))
        la3 = [split3(la) for la in log_a]
        cums, e_col = [], []
        for u in range(width):
            cs = jnp.dot(cum_w, la3[u], preferred_element_type=F32)
            cums.append(cs[:, :LANES] + cs[:, LANES:2 * LANES] + cs[:, 2 * LANES:])
            e_col.append(jnp.exp(cums[u].T))
        qb, kd, v, att, upd = [], [], [], [], []
        for u in range(width):
            b = cums[u][:rt]
            b_end = cums[u][rt:]
            k = k_ref[rows[u], :]
            v.append(v_ref[rows[u], :])
            mid = 0.5 * b_end
            q_mid = q_ref[rows[u], :] * (GLA_DK ** -0.5) * jnp.exp(b - mid)
            k_mid = k * jnp.exp(mid - b)
            e_mid = jnp.exp(mid)
            qb.append(q_mid * e_mid)
            kd.append(k_mid * e_mid)
            a = lax.dot_general(q_mid.astype(BF16), k_mid.astype(BF16), nt_dims, preferred_element_type=F32)
            att.append(jnp.where(causal, a, 0.0).astype(BF16))
            upd.append([lax.dot_general(kd[u][g * c:(g + 1) * c].astype(BF16), v[u][g * c:(g + 1) * c].astype(BF16),
                                        tn_dims, preferred_element_type=F32) for g in range(groups)])
        o = [jnp.dot(att[u], v[u].astype(BF16), preferred_element_type=F32) for u in range(width)]
        for u in range(width):
            o_state = []
            for g in range(groups):
                s = s_scr[...] if chained else s0_ref[tiles[u] * groups + g, 0]
                o_state.append(jnp.dot(qb[u][g * c:(g + 1) * c].astype(BF16), s.astype(BF16),
                                       preferred_element_type=F32))
                s_new = s * e_col[u][:, rt + g * c:rt + g * c + 1] + upd[u][g]
                if chained:
                    s_scr[...] = s_new
                else:
                    sf_ref[tiles[u] * groups + g, 0] = s_new
            o[u] = o[u] + (o_state[0] if groups == 1 else jnp.concatenate(o_state, axis=0))
        for u in range(width):
            y = o[u] * lax.rsqrt(jnp.mean(o[u] * o[u], axis=-1, keepdims=True) + RMS_EPS)
            y = y * gn_ref[...]
            r = r_ref[rows[u], :]
            y_ref[rows[u], :] = (y * (r * jax.nn.sigmoid(r))).astype(y_ref.dtype)
        return carry

    lax.fori_loop(0, ntile // width, body, 0)
    if chained:
        sf_ref[0, 0] = s_scr[...]


def _gla(z, alow, w_a2, b_a, g_norm, s0, casts=None, *, batch, seq, chained):
    chunk = math.gcd(seq, GLA_CHUNK)
    if chained:
        assert chunk == GLA_TILE
        nb, groups, width = 1, 1, 8
    else:
        assert GLA_TILE % seq == 0 and chunk == seq
        nb, groups, width = 32, GLA_TILE // seq, 4
    rows = nb * seq
    kern = functools.partial(_gla_kernel, ntile=rows // GLA_TILE, groups=groups, chained=chained, width=width)
    state_spec = pl.BlockSpec((nb, 1, GLA_DK, GLA_DV), lambda i, h: (i, h, 0, 0))
    (y, s_fin), made, _ = _call_with_casts(
        kern, casts, grid=(batch // nb, GLA_HEADS),
        in_specs=[pl.BlockSpec((rows, GLA_DK), lambda i, h: (i, COL_Q // GLA_DK + h)),
                  pl.BlockSpec((rows, GLA_DK), lambda i, h: (i, COL_K // GLA_DK + h)),
                  pl.BlockSpec((rows, GLA_DV), lambda i, h: (i, COL_V // GLA_DV + h)),
                  pl.BlockSpec((rows, GLA_DV), lambda i, h: (i, COL_R // GLA_DV + h)),
                  pl.BlockSpec((rows, LANES), lambda i, h: (i, 0)),
                  pl.BlockSpec((LANES, GLA_DK), lambda i, h: (0, h)),
                  pl.BlockSpec((1, GLA_DK), lambda i, h: (0, h)),
                  pl.BlockSpec((1, GLA_DV), lambda i, h: (0, h)),
                  state_spec],
        out_specs=[pl.BlockSpec((rows, GLA_DV), lambda i, h: (i, h)), state_spec],
        out_shape=[jax.ShapeDtypeStruct((batch * seq, GLA_HEADS * GLA_DV), BF16),
                   jax.ShapeDtypeStruct((batch, GLA_HEADS, GLA_DK, GLA_DV), F32)],
        operands=(z, z, z, z, alow, w_a2, b_a, g_norm, s0),
        scratch_shapes=[pltpu.VMEM((GLA_DK, GLA_DV), F32)],
        compiler_params=_params("parallel", "parallel"),
        name="gla")
    return y, s_fin, made


def _attn_kernel(q_ref, k_ref, v_ref, o_ref):
    nt_dims = (((1,), (1,)), ((), ()))
    cols = [slice(h * XA_HEAD_DIM, (h + 1) * XA_HEAD_DIM) for h in range(XA_HEADS)]
    scores = [lax.dot_general(q_ref[:, c].astype(BF16), k_ref[0, :, c].astype(BF16), nt_dims,
                              preferred_element_type=F32) * (XA_HEAD_DIM ** -0.5) for c in cols]
    probs = []
    for s in scores:
        p = jnp.exp(s - jnp.max(s, axis=-1, keepdims=True))
        probs.append((p / jnp.sum(p, axis=-1, keepdims=True)).astype(BF16))
    for c, p in zip(cols, probs):
        o_ref[:, c] = jnp.dot(p, v_ref[0, :, c].astype(BF16), preferred_element_type=F32).astype(o_ref.dtype)


def _attn_prompt(z, mem_k, mem_v, *, batch, seq, rc):
    nrc = seq // rc
    kv_spec = pl.BlockSpec((1, MEM_LEN, XA_WIDTH), lambda b, c: (b, 0, 0))
    return pl.pallas_call(
        _attn_kernel, grid=(batch, nrc),
        in_specs=[pl.BlockSpec((rc, XA_WIDTH), lambda b, c: (b * nrc + c, COL_QX // XA_WIDTH)), kv_spec, kv_spec],
        out_specs=pl.BlockSpec((rc, XA_WIDTH), lambda b, c: (b * nrc + c, 0)),
        out_shape=jax.ShapeDtypeStruct((batch * seq, XA_WIDTH), BF16),
        compiler_params=_params("parallel", "parallel"),
        name="attn_prompt",
    )(z, mem_k, mem_v)


XA_HALF = XA_HEAD_DIM // 2
XA_ROWS = 2 * XA_HEADS


def _attn_cache_body(q_ref, k_ref, v_ref, o_ref, *, nseq, seq, nblk, grid):
    nt_dims = (((1,), (1,)), ((), ()))

    def head_rows(ref, bi, h):
        halves = [ref[bi, pl.ds(half * XA_HEADS + h, MEM_LEN, stride=XA_ROWS), :] for half in range(2)]
        return jnp.concatenate(halves, axis=1).astype(BF16)

    @pl.when(pl.program_id(0) * grid[1] + pl.program_id(1) < nblk)
    def _():
        scores = []
        for bi in range(nseq):
            rows = slice(bi * seq, (bi + 1) * seq)
            for h in range(XA_HEADS):
                cols = slice(h * XA_HEAD_DIM, (h + 1) * XA_HEAD_DIM)
                s = lax.dot_general(q_ref[rows, cols].astype(BF16), head_rows(k_ref, bi, h), nt_dims,
                                    preferred_element_type=F32)
                scores.append(s * (XA_HEAD_DIM ** -0.5))
        probs = []
        for s in scores:
            p = jnp.exp(s - jnp.max(s, axis=-1, keepdims=True))
            probs.append((p / jnp.sum(p, axis=-1, keepdims=True)).astype(BF16))
        for h in range(XA_HEADS):
            out = [jnp.dot(probs[bi * XA_HEADS + h], head_rows(v_ref, bi, h), preferred_element_type=F32)
                   for bi in range(nseq)]
            o_ref[:, h * XA_HEAD_DIM:(h + 1) * XA_HEAD_DIM] = jnp.concatenate(out, axis=0).astype(o_ref.dtype)


def _cache_rows(cache):
    bs = cache.shape[0]
    c = cache.reshape(bs, MEM_LEN, XA_HEADS, 2, XA_HALF).transpose(0, 1, 3, 2, 4)
    return c.reshape(bs, MEM_LEN * XA_ROWS, XA_HALF)


def _attn_sample_side(z, mem_k, mem_v, *, batch, seq, grid, per_step=2):
    nblk = batch // per_step
    assert nblk <= grid[0] * grid[1] and (per_step * seq) % 16 == 0
    rows = per_step * seq

    def blk(*g):
        return jnp.minimum(g[0] * grid[1] + g[1], nblk - 1)

    kv_spec = pl.BlockSpec((per_step, MEM_LEN * XA_ROWS, XA_HALF), lambda *g: (blk(*g), 0, 0))
    return _SideJob(
        in_specs=[pl.BlockSpec((rows, XA_WIDTH), lambda *g: (blk(*g), COL_QX // XA_WIDTH)), kv_spec, kv_spec],
        out_specs=[pl.BlockSpec((rows, XA_WIDTH), lambda *g: (blk(*g), 0))],
        out_shape=[jax.ShapeDtypeStruct((batch * seq, XA_WIDTH), BF16)],
        operands=(z, mem_k, mem_v),
        body=functools.partial(_attn_cache_body, nseq=per_step, seq=seq, nblk=nblk, grid=grid))


def _merge_kernel(ys_ref, yg_ref, yx_ref, g0_ref, g1_ref, g2_ref, wglu_ref, bglu_ref,
                  w0_ref, w1_ref, w2_ref, o_ref, s5_scr):
    @pl.when(pl.program_id(1) == 0)
    def _():
        y = ys_ref[...]
        lin = jnp.dot(y.astype(BF16), wglu_ref[...], preferred_element_type=F32) + bglu_ref[...]
        s5_scr[...] = (y * jax.nn.sigmoid(lin)).astype(BF16)

    j = pl.program_id(1)
    m = jax.nn.sigmoid(g0_ref[...]) * jnp.dot(s5_scr[...], w0_ref[j], preferred_element_type=F32)
    m = m + jax.nn.sigmoid(g1_ref[...]) * jnp.dot(yg_ref[...], w1_ref[j], preferred_element_type=F32)
    m = m + jax.nn.sigmoid(g2_ref[...]) * jnp.dot(yx_ref[...], w2_ref[j], preferred_element_type=F32)
    o_ref[...] = m.astype(o_ref.dtype)


def _merge(y_s5, y_gla, y_x, z, w_glu, b_glu, w_br_s5, w_br_gla, w_br_x, *, tm, tn):
    t = y_s5.shape[0]
    nj = D_MODEL // tn
    once = pl.Buffered(1)
    wide = pl.BlockSpec((tm, S5_WIDTH), lambda i, j: (i, 0))
    gate = lambda b: pl.BlockSpec((tm, tn), lambda i, j: (i, (COL_GATE + b * D_MODEL) // tn + j))
    w_br = pl.BlockSpec((nj, S5_WIDTH, tn), lambda i, j: (0, 0, 0), pipeline_mode=once)
    return pl.pallas_call(
        _merge_kernel, grid=(t // tm, nj),
        in_specs=[wide, wide, wide, gate(0), gate(1), gate(2),
                  pl.BlockSpec((S5_WIDTH, S5_WIDTH), lambda i, j: (0, 0), pipeline_mode=once),
                  pl.BlockSpec((1, S5_WIDTH), lambda i, j: (0, 0)),
                  w_br, w_br, w_br],
        out_specs=pl.BlockSpec((tm, tn), lambda i, j: (i, j)),
        out_shape=jax.ShapeDtypeStruct((t, D_MODEL), BF16),
        scratch_shapes=[pltpu.VMEM((tm, S5_WIDTH), BF16)],
        compiler_params=_params("parallel", "arbitrary"),
        name="merge",
    )(y_s5, y_gla, y_x, z, z, z, w_glu, b_glu, w_br_s5, w_br_gla, w_br_x)


def _out_ffn_kernel(x_ref, m_ref, wo_ref, gf_ref, wg_ref, wu_ref, wd_ref, gl_ref, o_ref, h_scr, acc_scr):
    k = pl.program_id(1)

    @pl.when(k == 0)
    def _():
        acc_scr[...] = x_ref[...] + jnp.dot(m_ref[...], wo_ref[...], preferred_element_type=F32)
        h_scr[...] = _rms(acc_scr[...], gf_ref[...]).astype(BF16)

    h = h_scr[...]
    gate = jnp.dot(h, wg_ref[...], preferred_element_type=F32)
    up = jnp.dot(h, wu_ref[...], preferred_element_type=F32)
    act = (gate * jax.nn.sigmoid(gate) * up).astype(BF16)
    acc_scr[...] += jnp.dot(act, wd_ref[...], preferred_element_type=F32)

    @pl.when(k == pl.num_programs(1) - 1)
    def _():
        o_ref[...] = _rms(acc_scr[...], gl_ref[...])


def _out_ffn(x, merged, w_out, g_ffn, w_gate, w_up, w_down, g_final, *, tm, th):
    t = x.shape[0]
    row = pl.BlockSpec((tm, D_MODEL), lambda i, k: (i, 0))
    vec = pl.BlockSpec((1, D_MODEL), lambda i, k: (0, 0))
    w_in = pl.BlockSpec((D_MODEL, th), lambda i, k: (0, k))
    return pl.pallas_call(
        _out_ffn_kernel, grid=(t // tm, FFN_HIDDEN // th),
        in_specs=[row, row,
                  pl.BlockSpec((D_MODEL, D_MODEL), lambda i, k: (0, 0), pipeline_mode=pl.Buffered(1)),
                  vec, w_in, w_in, pl.BlockSpec((th, D_MODEL), lambda i, k: (k, 0)), vec],
        out_specs=row,
        out_shape=jax.ShapeDtypeStruct((t, D_MODEL), F32),
        scratch_shapes=[pltpu.VMEM((tm, D_MODEL), BF16), pltpu.VMEM((tm, D_MODEL), F32)],
        compiler_params=pltpu.CompilerParams(dimension_semantics=("parallel", "arbitrary"),
                                             vmem_limit_bytes=OUT_FFN_VMEM),
        name="out_ffn",
    )(x, merged, w_out, g_ffn, w_gate, w_up, w_down, g_final)


def _first_in_proj_kernel(x_ref, g_ref, wt_ref, wa_ref, o_ref, os_ref, main_ref, alow_ref, h_ref):
    @pl.when(pl.program_id(0) == 0)
    def _():
        h = _rms(x_ref[...], g_ref[...]).astype(BF16)
        h_ref[...] = h
        lane = lax.broadcasted_iota(jnp.int32, alow_ref.shape, 1)
        w_low = jnp.where(lane < GLA_RANK, wa_ref[...].T, 0.0).astype(BF16)
        alow_ref[...] = w_low
        os_ref[...] = jnp.dot(h, w_low, preferred_element_type=F32)

    w = wt_ref[...].T.astype(BF16)
    main_ref[...] = w
    o_ref[...] = jnp.dot(h_ref[...], w, preferred_element_type=F32)


def _first_in_proj(x, g, w_in):
    t, _ = x.shape
    d, n = w_in.shape
    wt = w_in.T
    tc = 512
    per_tile = IN_TN // tc
    n_first, n_gate = COL_GATE // tc, (COL_QX - COL_GATE) // tc
    gate_start, qx_start = 4096 + GLA_RANK + XA_WIDTH, 4096 + GLA_RANK

    def src_row(j):
        t8, g8, q8 = tc // SUBLANES, gate_start // SUBLANES, qx_start // SUBLANES
        r8 = jnp.where(j < n_first, j * t8,
                       jnp.where(j < n_first + n_gate, g8 + (j - n_first) * t8,
                                 q8 + (j - n_first - n_gate) * t8))
        return r8 * SUBLANES

    once = pl.Buffered(1)
    return pl.pallas_call(
        _first_in_proj_kernel, grid=(Z_WIDTH // tc,),
        in_specs=[pl.BlockSpec((pl.Element(t), pl.Element(d)), lambda j: (0, 0), pipeline_mode=once),
                  pl.BlockSpec((pl.Element(1), pl.Element(d)), lambda j: (0, 0)),
                  pl.BlockSpec((pl.Element(tc), pl.Element(d)), lambda j: (src_row(j), 0)),
                  pl.BlockSpec((pl.Element(LANES), pl.Element(d)), lambda j: (COL_GATE, 0), pipeline_mode=once)],
        out_specs=[pl.BlockSpec((t, tc), lambda j: (0, j)),
                   pl.BlockSpec((t, LANES), lambda j: (0, 0)),
                   pl.BlockSpec((None, d, tc), lambda j: (j // per_tile, 0, j % per_tile)),
                   pl.BlockSpec((d, LANES), lambda j: (0, 0))],
        out_shape=[jax.ShapeDtypeStruct((t, Z_WIDTH), F32), jax.ShapeDtypeStruct((t, LANES), F32),
                   jax.ShapeDtypeStruct((Z_WIDTH // IN_TN, d, IN_TN), BF16),
                   jax.ShapeDtypeStruct((d, LANES), BF16)],
        scratch_shapes=[pltpu.VMEM((t, d), BF16)],
        compiler_params=_params("arbitrary"),
        name="first_in_proj",
    )(x, g, wt, wt)


def _layer(x, z, alow, memory_attention, s5_re0, s5_im0, gla_s0, w, *, batch, seq, chained, gla_casts=None):
    w = dict(w)
    y_s5, hf_re, hf_im = _s5_branch(z, w['s5'], s5_re0, s5_im0, batch=batch, seq=seq)
    y_gla, gla_s, made = _gla(z, alow, w['gla_w_a2'], w['gla_b_a'], w['gla_norm'], gla_s0, gla_casts,
                              batch=batch, seq=seq, chained=chained)
    w.update(made)
    y_x = memory_attention(w)
    merged = _merge(y_s5, y_gla, y_x, z, w['s5_w_glu'], w['s5_b_glu'],
                    w['w_br_s5'], w['w_br_gla'], w['w_br_xattn'], tm=512, tn=MERGE_TN)
    y = _out_ffn(x, merged, w['w_out'], w['norm_ffn'], w['w_ffn_gate'], w['w_ffn_up'], w['w_ffn_down'],
                 w['norm_final'], tm=512, th=FFN_TH)
    return y, hf_re, hf_im, gla_s, w


def kernel(x_prompt, x_sample, mem_prompt, state_s5_re, state_s5_im, state_gla, cache_mem_k, cache_mem_v,
           norm_mix, w_in, s5_lam_re, s5_lam_im, s5_log_dt, s5_b_re, s5_b_im, s5_c_re, s5_c_im,
           s5_d, s5_w_glu, s5_b_glu, gla_w_a2, gla_b_a, gla_norm, mem_norm, w_mem_k, w_mem_v,
           w_br_s5, w_br_gla, w_br_xattn, w_out, norm_ffn, w_ffn_gate, w_ffn_up, w_ffn_down, norm_final):
    depth = w_in.shape[0]
    assert depth == 1
    bp, sp, d = x_prompt.shape
    bs, ss, _ = x_sample.shape
    n_state = S5_GROUPS * S5_STATE
    row = lambda v: v.reshape(1, -1)

    l = 0
    s5_w = _s5_params(s5_lam_re[l], s5_lam_im[l], s5_log_dt[l], s5_b_re[l], s5_b_im[l],
                      s5_c_re[l], s5_c_im[l], s5_d[l])
    w = {
        'norm_mix': row(norm_mix[l]),
        's5': s5_w,
        's5_b_glu': row(s5_b_glu[l]),
        'gla_w_a2': jnp.pad(gla_w_a2[l], ((0, LANES - GLA_RANK), (0, 0))).astype(BF16),
        'gla_b_a': row(gla_b_a[l]), 'gla_norm': row(gla_norm[l]),
        'norm_ffn': row(norm_ffn[l]), 'norm_final': row(norm_final),
    }
    ffn_casts = {'w_ffn_gate': _CastJob(w_ffn_gate[l], row_axis=0), 'w_ffn_up': _CastJob(w_ffn_up[l], row_axis=0),
                 'w_ffn_down': _CastJob(w_ffn_down[l], row_axis=1)}
    mix_casts = {'w_out': _CastJob(w_out[l], row_axis=0), 's5_w_glu': _CastJob(s5_w_glu[l], row_axis=0),
                 'w_br_s5': _CastJob(w_br_s5[l], row_axis=0, col_tile=MERGE_TN),
                 'w_br_gla': _CastJob(w_br_gla[l], row_axis=0, col_tile=MERGE_TN),
                 'w_br_xattn': _CastJob(w_br_xattn[l], row_axis=0, col_tile=MERGE_TN),
                 'w_mem_k': _CastJob(w_mem_k[l], row_axis=0, col_tile=IN_TN),
                 'w_mem_v': _CastJob(w_mem_v[l], row_axis=0, col_tile=IN_TN)}

    zero_s5 = jnp.zeros((bp, n_state), F32)
    zero_gla = jnp.zeros((bp, GLA_HEADS, GLA_DK, GLA_DV), F32)
    xp, xs = x_prompt.reshape(bp * sp, d), x_sample.reshape(bs * ss, d)

    z_s, alow_s, w['w_main'], w['w_alow'] = _first_in_proj(xs, w['norm_mix'], w_in[l])
    cache_k, cache_v = _cache_rows(cache_mem_k[l]), _cache_rows(cache_mem_v[l])
    z_p, alow_p, made, (yx_s,) = _norm_matmul(
        xp, w['norm_mix'], w['w_main'], w['w_alow'], ffn_casts,
        lambda grid: _attn_sample_side(z_s, cache_k, cache_v, batch=bs, seq=ss, grid=grid), tm=1024)
    w.update(made)

    mem_kv = {}

    def prompt_attention(w):
        mem = mem_prompt.reshape(bp * MEM_LEN, d)
        for name in ('w_mem_k', 'w_mem_v'):
            mem_kv[name] = _norm_matmul(mem, row(mem_norm[l]), w[name], tm=512).reshape(bp, MEM_LEN, XA_WIDTH)
        return _attn_prompt(z_p, mem_kv['w_mem_k'], mem_kv['w_mem_v'], batch=bp, seq=sp, rc=1024)

    yp, p_re, p_im, p_gla, w = _layer(xp, z_p, alow_p, prompt_attention, zero_s5, zero_s5, zero_gla, w,
                                      batch=bp, seq=sp, chained=True, gla_casts=mix_casts)
    mk, mv = mem_kv['w_mem_k'], mem_kv['w_mem_v']

    ys, s_re, s_im, s_gla, _ = _layer(xs, z_s, alow_s, lambda w: yx_s,
                                      state_s5_re[l].reshape(bs, n_state), state_s5_im[l].reshape(bs, n_state),
                                      state_gla[l], w, batch=bs, seq=ss, chained=False)

    s5_shape_p = (1, bp, S5_GROUPS, S5_STATE)
    s5_shape_s = (1, bs, S5_GROUPS, S5_STATE)
    kv_shape = (1, bp, MEM_LEN, XA_HEADS, XA_HEAD_DIM)
    return (yp.reshape(bp, sp, d), ys.reshape(bs, ss, d),
            p_re.reshape(s5_shape_p), p_im.reshape(s5_shape_p), p_gla[None],
            mk.reshape(kv_shape), mv.reshape(kv_shape),
            s_re.reshape(s5_shape_s), s_im.reshape(s5_shape_s), s_gla[None])
```

```python
import functools
import math
from typing import Callable, NamedTuple, Optional

import jax
import jax.numpy as jnp
from jax import lax
from jax.experimental import pallas as pl
from jax.experimental.pallas import tpu as pltpu

F32 = jnp.float32
BF16 = jnp.bfloat16

D_MODEL = 2048
S5_WIDTH = 1024
S5_GROUP = 16
S5_GROUPS = 64
S5_STATE = 64
GLA_HEADS = 4
GLA_DK = 128
GLA_DV = 256
GLA_RANK = 16
GLA_TAU = 16.0
GLA_CHUNK = 64
XA_HEADS = 4
XA_HEAD_DIM = 256
XA_WIDTH = 1024
MEM_LEN = 256
FFN_HIDDEN = 5632
RMS_EPS = 1e-6

LANES = 128
SUBLANES = 8
VMEM_LIMIT = 56 * 1024 * 1024
OUT_FFN_VMEM = 52 * 1024 * 1024

COL_U = 0
COL_Q = 1024
COL_K = 1536
COL_V = 2048
COL_R = 3072
COL_GATE = 4096
COL_QX = 10240
Z_WIDTH = 11264

S5_GB = 8
S5_CH = S5_GB * S5_STATE
S5_NBLK = S5_GROUPS // S5_GB
S5_TC = 16
S5_PAIRS_PER_STEP = 4
GLA_TILE = 64
GLA_SUB = 16
MERGE_TN = 1024
IN_TN = 1024
FFN_TH = 512


def _params(*sem):
    return pltpu.CompilerParams(dimension_semantics=sem, vmem_limit_bytes=VMEM_LIMIT)


def _rms(x, g):
    return x * lax.rsqrt(jnp.mean(x * x, axis=-1, keepdims=True) + RMS_EPS) * g


class _CastJob(NamedTuple):
    src: jax.Array
    row_axis: int
    col_tile: Optional[int] = None

    def specs(self, grid):
        rows, cols = self.src.shape
        ra, ca = self.row_axis, 1 - self.row_axis
        assert rows % grid[ra] == 0 and cols % grid[ca] == 0
        br, bc = rows // grid[ra], cols // grid[ca]
        in_spec = pl.BlockSpec((br, bc), lambda *g: (g[ra], g[ca]))
        if self.col_tile is None:
            return in_spec, in_spec, jax.ShapeDtypeStruct((rows, cols), BF16)
        assert self.col_tile % bc == 0 and cols % self.col_tile == 0
        per = self.col_tile // bc
        out_spec = pl.BlockSpec((None, br, bc), lambda *g: (g[ca] // per, g[ra], g[ca] % per))
        return in_spec, out_spec, jax.ShapeDtypeStruct((cols // self.col_tile, rows, self.col_tile), BF16)


class _SideJob(NamedTuple):
    in_specs: list
    out_specs: list
    out_shape: list
    operands: tuple
    body: Callable


def _with_side_work(kernel, n_in, n_out, n_cast, side):
    n_side_in = len(side.in_specs) if side else 0
    n_side_out = len(side.out_specs) if side else 0

    def wrapped(*refs):
        ins, rest = refs[:n_in], refs[n_in:]
        srcs, rest = rest[:n_cast], rest[n_cast:]
        side_ins, rest = rest[:n_side_in], rest[n_side_in:]
        outs, rest = rest[:n_out], rest[n_out:]
        dsts, rest = rest[:n_cast], rest[n_cast:]
        side_outs, scratch = rest[:n_side_out], rest[n_side_out:]
        for src, dst in zip(srcs, dsts):
            dst[...] = src[...].astype(dst.dtype)
        kernel(*ins, *outs, *scratch)
        if side:
            side.body(*side_ins, *side_outs)
    return wrapped


def _call_with_casts(kernel, casts, *, grid, in_specs, out_specs, out_shape, operands, side=None, **kwargs):
    casts = casts or {}
    specs = [job.specs(grid) for job in casts.values()]
    side_in = list(side.in_specs) if side else []
    side_out = list(side.out_specs) if side else []
    results = pl.pallas_call(
        _with_side_work(kernel, len(in_specs), len(out_specs), len(specs), side), grid=grid,
        in_specs=list(in_specs) + [s[0] for s in specs] + side_in,
        out_specs=list(out_specs) + [s[1] for s in specs] + side_out,
        out_shape=list(out_shape) + [s[2] for s in specs] + (list(side.out_shape) if side else []),
        **kwargs,
    )(*operands, *[job.src for job in casts.values()], *(side.operands if side else ()))
    n, c = len(out_specs), len(specs)
    return results[:n], dict(zip(casts.keys(), results[n:n + c])), results[n + c:]


def _norm_matmul_kernel(x_ref, g_ref, w_ref, o_ref, h_ref):
    @pl.when(pl.program_id(1) == 0)
    def _():
        h_ref[...] = _rms(x_ref[...], g_ref[...]).astype(BF16)

    o_ref[...] = jnp.dot(h_ref[...], w_ref[0], preferred_element_type=F32)


def _norm_matmul2_kernel(x_ref, g_ref, w_ref, ws_ref, o_ref, os_ref, h_ref):
    @pl.when(pl.program_id(1) == 0)
    def _():
        h = _rms(x_ref[...], g_ref[...]).astype(BF16)
        h_ref[...] = h
        os_ref[...] = jnp.dot(h, ws_ref[...], preferred_element_type=F32)

    o_ref[...] = jnp.dot(h_ref[...], w_ref[0], preferred_element_type=F32)


def _norm_matmul(x, g, w, w_small=None, casts=None, make_side=None, *, tm):
    t, d = x.shape
    nj, _, tn = w.shape
    n = nj * tn
    grid = (t // tm, n // tn)
    in_specs = [pl.BlockSpec((tm, d), lambda i, j: (i, 0)),
                pl.BlockSpec((1, d), lambda i, j: (0, 0)),
                pl.BlockSpec((1, d, tn), lambda i, j: (j, 0, 0))]
    out_specs = pl.BlockSpec((tm, tn), lambda i, j: (i, j))
    out_shape = jax.ShapeDtypeStruct((t, n), F32)
    scratch = [pltpu.VMEM((tm, d), BF16)]
    if w_small is None:
        return pl.pallas_call(_norm_matmul_kernel, grid=grid, in_specs=in_specs, out_specs=out_specs,
                              out_shape=out_shape, scratch_shapes=scratch,
                              compiler_params=_params("parallel", "arbitrary"),
                              name="norm_matmul")(x, g, w)
    ns = w_small.shape[1]
    in_specs.append(pl.BlockSpec((d, ns), lambda i, j: (0, 0)))
    (z, narrow), made, side_out = _call_with_casts(
        _norm_matmul2_kernel, casts, grid=grid, in_specs=in_specs,
        side=make_side(grid) if make_side else None,
        out_specs=[out_specs, pl.BlockSpec((tm, ns), lambda i, j: (i, 0))],
        out_shape=[out_shape, jax.ShapeDtypeStruct((t, ns), F32)],
        operands=(x, g, w, w_small), scratch_shapes=scratch,
        compiler_params=_params("arbitrary" if make_side else "parallel", "arbitrary"), name="in_proj")
    return z, narrow, made, side_out


def _s5_param_kernel(*refs):
    for pi in range(S5_PAIRS_PER_STEP):
        _s5_pair_operands(pi, *refs)


def _s5_pair_operands(pi, lr_ref, li_ref, ldt_ref, btr_ref, bti_ref, cr_ref, ci_ref,
                      w16_ref, cc16_ref, w8_ref, cc8_ref, ap_ref):
    p = S5_STATE
    half = S5_TC // 2
    lam_re, lam_im = lr_ref[pi], li_ref[pi]
    dt = jnp.exp(ldt_ref[pi])
    mag = jnp.exp(lam_re * dt)
    a_re = mag * jnp.cos(lam_im * dt)
    a_im = mag * jnp.sin(lam_im * dt)
    den = lam_re * lam_re + lam_im * lam_im
    coef_re = ((a_re - 1.0) * lam_re + a_im * lam_im) / den
    coef_im = (a_im * lam_re - (a_re - 1.0) * lam_im) / den
    bt_re, bt_im = btr_ref[pi], bti_ref[pi]
    bb_re = coef_re * bt_re - coef_im * bt_im
    bb_im = coef_re * bt_im + coef_im * bt_re
    c_re, c_im = cr_ref[pi], ci_ref[pi]

    pw_re, pw_im = jnp.ones_like(a_re), jnp.zeros_like(a_re)
    ca_re, ca_im, ab_re, ab_im, powers = [], [], [], [], {}
    for j in range(S5_TC + 1):
        powers[j] = (pw_re, pw_im)
        ca_re.append(c_re * pw_re - c_im * pw_im)
        ca_im.append(c_re * pw_im + c_im * pw_re)
        ab_re.append(bb_re * pw_re - bb_im * pw_im)
        ab_im.append(bb_re * pw_im + bb_im * pw_re)
        pw_re, pw_im = pw_re * a_re - pw_im * a_im, pw_re * a_im + pw_im * a_re
    lag_re = jnp.concatenate(ca_re[:S5_TC], axis=0)
    lag_im = jnp.concatenate(ca_im[:S5_TC], axis=0)
    end_re = jnp.concatenate([ab_re[S5_TC - 1 - s] for s in range(S5_TC)], axis=0)
    end_im = jnp.concatenate([ab_im[S5_TC - 1 - s] for s in range(S5_TC)], axis=0)
    car_re = jnp.concatenate(ca_re[1:], axis=0).T
    car_im = -jnp.concatenate(ca_im[1:], axis=0).T

    def pair_blocks(x):
        first = lax.broadcasted_iota(jnp.int32, x.shape, 0) < p
        return jnp.concatenate([jnp.where(first, x, 0.0), jnp.where(first, 0.0, x)], axis=1)

    cc16_ref[pi] = jnp.concatenate([pair_blocks(car_re), pair_blocks(car_im)], axis=0).astype(BF16)
    n8 = half * S5_GROUP
    cc8_ref[pi] = jnp.concatenate([pair_blocks(car_re[:, :n8]), pair_blocks(car_im[:, :n8])], axis=0).astype(BF16)

    nt_dims = (((1,), (1,)), ((), ()))
    hi = lax.Precision.HIGHEST
    lane = lax.broadcasted_iota(jnp.int32, (S5_GROUP, LANES), 1)
    end_lane = lax.broadcasted_iota(jnp.int32, end_re.shape, 1)
    for gi in range(2):
        mine = (lane < p) if gi == 0 else (lane >= p)
        own_re, own_im = jnp.where(mine, bb_re, 0.0), jnp.where(mine, bb_im, 0.0)
        strip = (lax.dot_general(own_re, lag_re, nt_dims, precision=hi, preferred_element_type=F32)
                 - lax.dot_general(own_im, lag_im, nt_dims, precision=hi, preferred_element_type=F32))
        lo, up = strip[:, :LANES], strip[:, LANES:]
        blocks = [strip]
        for s in range(1, S5_TC):
            sh = (s % half) * S5_GROUP
            lo_r = pltpu.roll(lo, sh, 1) if sh else lo
            up_r = pltpu.roll(up, sh, 1) if sh else up
            if s < half:
                blocks.append(jnp.concatenate([jnp.where(lane >= sh, lo_r, 0.0),
                                               jnp.where(lane >= sh, up_r, lo_r)], axis=1))
            else:
                blocks.append(jnp.concatenate([jnp.zeros_like(lo), jnp.where(lane >= sh, lo_r, 0.0)], axis=1))
        toeplitz = jnp.concatenate(blocks, axis=0)
        own_end = (end_lane < p) if gi == 0 else (end_lane >= p)
        ends = jnp.concatenate([jnp.where(own_end, end_re, 0.0), jnp.where(own_end, end_im, 0.0)], axis=1)
        w16_ref[2 * pi + gi] = jnp.concatenate([toeplitz, ends], axis=1).astype(BF16)
        w8_ref[2 * pi + gi] = jnp.concatenate([toeplitz[:n8, :n8], ends[n8:, :]], axis=1).astype(BF16)
    ap_ref[pi] = jnp.concatenate([powers[S5_TC][0], powers[S5_TC][1], powers[half][0], powers[half][1]], axis=0)


def _s5_params(lam_re, lam_im, log_dt, b_re, b_im, c_re, c_im, d_skip):
    g, p, c = S5_GROUPS, S5_STATE, S5_GROUP
    npair = g // 2
    w16, w8 = S5_TC * c, S5_TC // 2 * c

    def pair_lanes(x):
        return x.reshape(npair, 2, x.shape[1], p).transpose(0, 2, 1, 3).reshape(npair, x.shape[1], 2 * p)

    row = lambda x: pair_lanes(x.reshape(g, 1, p))
    pp = S5_PAIRS_PER_STEP
    blk = lambda *shape: pl.BlockSpec((pp,) + shape, lambda i: (i, 0, 0))
    two = lambda *shape: pl.BlockSpec((2 * pp,) + shape, lambda i: (i, 0, 0))
    wt16, cc16, wt8, cc8, apow = pl.pallas_call(
        _s5_param_kernel, grid=(npair // pp,),
        in_specs=[blk(1, 2 * p)] * 3 + [blk(c, 2 * p)] * 4,
        out_specs=[two(w16, w16 + 4 * p), blk(4 * p, 2 * w16), two(w8, w8 + 4 * p), blk(4 * p, 2 * w8),
                   blk(4, 2 * p)],
        out_shape=[jax.ShapeDtypeStruct((g, w16, w16 + 4 * p), BF16),
                   jax.ShapeDtypeStruct((npair, 4 * p, 2 * w16), BF16),
                   jax.ShapeDtypeStruct((g, w8, w8 + 4 * p), BF16),
                   jax.ShapeDtypeStruct((npair, 4 * p, 2 * w8), BF16),
                   jax.ShapeDtypeStruct((npair, 4, 2 * p), F32)],
        compiler_params=_params("parallel"),
        name="s5_params",
    )(row(lam_re), row(lam_im), row(jnp.broadcast_to(log_dt[:, None], (g, p))),
      pair_lanes(b_re.transpose(0, 2, 1)), pair_lanes(b_im.transpose(0, 2, 1)),
      pair_lanes(c_re), pair_lanes(c_im))
    d = d_skip.reshape(S5_NBLK, 1, S5_GB * c)

    def transition(r):
        return jnp.concatenate([apow[:, r].reshape(S5_NBLK, 1, S5_CH), apow[:, r + 1].reshape(S5_NBLK, 1, S5_CH)],
                               axis=-1)

    return {S5_TC: (wt16, cc16, transition(0), d), S5_TC // 2: (wt8, cc8, transition(2), d)}


def _unit_transpose(vs):
    unit = lax.broadcasted_iota(jnp.int32, vs[0].shape, 1) >> int(math.log2(S5_GROUP))
    for dist in (4, 2, 1):
        keep = (unit & dist) == 0
        nxt = list(vs)
        for i in range(8):
            if i & dist == 0:
                a, b = vs[i], vs[i + dist]
                nxt[i] = jnp.where(keep, a, pltpu.roll(b, dist * S5_GROUP, 1))
                nxt[i + dist] = jnp.where(keep, pltpu.roll(a, LANES - dist * S5_GROUP, 1), b)
        vs = nxt
    return vs


def _s5_chunk_kernel(z_ref, w_ref, cc_ref, a_ref, d_ref, h0r_ref, h0i_ref,
                     y_ref, hfr_ref, hfi_ref, ut_ref, ug_ref, yg_ref, e_ref, *, nsl, rps, tc, nseg):
    uw = tc * S5_GROUP
    nq = uw // LANES
    rows = nsl * rps
    nk = rps if nsl > 1 else 1
    ns = rows // nk
    npair = S5_GB // 2
    rchunk = min(rows, 64)

    for s in range(nsl):
        for t in range(tc):
            ut_ref[t, s * rps:(s + 1) * rps, :] = z_ref[pl.ds(s * rps * tc + t, rps, stride=tc), :]

    def row_chunk(rc):
        return pl.ds(pl.multiple_of(rc * rchunk, rchunk), rchunk)

    def to_groups(rc, carry):
        for q in range(nq):
            per_group = _unit_transpose([ut_ref[q * 8 + t, row_chunk(rc), :] for t in range(8)])
            for g in range(S5_GB):
                ug_ref[g, row_chunk(rc), q * LANES:(q + 1) * LANES] = per_group[g].astype(BF16)
        return carry

    lax.fori_loop(0, rows // rchunk, to_groups, 0)

    for pair in range(npair):
        ends = None
        for gi in range(2):
            g = 2 * pair + gi
            em = jnp.dot(ug_ref[g], w_ref[g], preferred_element_type=F32)
            yg_ref[g] = em[:, :uw]
            ends = em[:, uw:] if ends is None else ends + em[:, uw:]
        for part, blk in ((ends[:, :LANES], pair), (ends[:, LANES:], npair + pair)):
            if nk == 1:
                e_ref[blk] = part
            else:
                for s in range(nsl):
                    e_ref[blk, pl.ds(s, rps, stride=nsl), :] = part[s * rps:(s + 1) * rps, :]

    a_row = [(a_ref[0, :, i * LANES:(i + 1) * LANES], a_ref[0, :, S5_CH + i * LANES:S5_CH + (i + 1) * LANES])
             for i in range(npair)]
    a_full = [(jnp.broadcast_to(ar, (ns, LANES)), jnp.broadcast_to(ai, (ns, LANES))) for ar, ai in a_row]

    def slab(t):
        return pl.ds(pl.multiple_of(t * ns, ns), ns)

    def run_scan(h, store):
        def step(t, carry):
            out = []
            for i in range(npair):
                hr, hi = carry[i]
                ar, ai = a_full[i]
                er = e_ref[i, slab(t), :]
                ei = e_ref[npair + i, slab(t), :]
                if store:
                    e_ref[i, slab(t), :] = hr
                    e_ref[npair + i, slab(t), :] = hi
                out.append((ar * hr - ai * hi + er, ar * hi + ai * hr + ei))
            return tuple(out)

        if nk == 1:
            return step(0, h)
        return lax.fori_loop(0, nk, step, h, unroll=4)

    h0 = tuple((h0r_ref[:, i * LANES:(i + 1) * LANES], h0i_ref[:, i * LANES:(i + 1) * LANES]) for i in range(npair))
    if nseg == 1:
        final = run_scan(h0, store=True)
    else:
        zero = jnp.zeros((ns, LANES), F32)
        seg_end = run_scan(tuple((zero, zero) for _ in range(npair)), store=False)
        second = (lax.broadcasted_iota(jnp.int32, (ns, LANES), 0) & 1) == 1
        init, final = [], []
        for i in range(npair):
            p_re, p_im = a_row[i]
            for _ in range(int(math.log2(nk))):
                p_re, p_im = p_re * p_re - p_im * p_im, 2.0 * (p_re * p_im)
            (hr, hi), (er, ei) = h0[i], seg_end[i]
            i_re = jnp.where(second, p_re * hr - p_im * hi + pltpu.roll(er, 1, 0), hr)
            i_im = jnp.where(second, p_re * hi + p_im * hr + pltpu.roll(ei, 1, 0), hi)
            init.append((i_re, i_im))
            final.append((p_re * i_re - p_im * i_im + er, p_re * i_im + p_im * i_re + ei))
        run_scan(tuple(init), store=True)
    for i in range(npair):
        hfr_ref[:, i * LANES:(i + 1) * LANES] = final[i][0]
        hfi_ref[:, i * LANES:(i + 1) * LANES] = final[i][1]

    def chunk_rows(blk):
        if nk == 1:
            return e_ref[blk]
        return jnp.concatenate([e_ref[blk, pl.ds(s, rps, stride=nsl), :] for s in range(nsl)], axis=0)

    for pair in range(npair):
        h_in = jnp.concatenate([chunk_rows(pair), chunk_rows(npair + pair)], axis=1).astype(BF16)
        carried = jnp.dot(h_in, cc_ref[pair], preferred_element_type=F32)
        for gi in range(2):
            yg_ref[2 * pair + gi] += carried[:, gi * uw:(gi + 1) * uw]

    d_row = d_ref[0]

    def to_tokens(rc, carry):
        for q in range(nq):
            per_tau = _unit_transpose([yg_ref[g, row_chunk(rc), q * LANES:(q + 1) * LANES] for g in range(S5_GB)])
            for t in range(8):
                y = per_tau[t] + d_row * ut_ref[q * 8 + t, row_chunk(rc), :]
                ut_ref[q * 8 + t, row_chunk(rc), :] = jax.nn.gelu(y, approximate=True)
        return carry

    lax.fori_loop(0, rows // rchunk, to_tokens, 0)
    for s in range(nsl):
        for t in range(tc):
            y_ref[pl.ds(s * rps * tc + t, rps, stride=tc), :] = ut_ref[t, s * rps:(s + 1) * rps, :]


def _s5_branch(z, weights, h0_re, h0_im, *, batch, seq):
    tokens = batch * seq
    if seq > S5_TC:
        tc, nseg = S5_TC, 2
        nsl, rps = batch * nseg, seq // (nseg * tc)
        assert rps & (rps - 1) == 0
        h0_re, h0_im = jnp.repeat(h0_re, nseg, axis=0), jnp.repeat(h0_im, nseg, axis=0)
    else:
        tc, nseg, nsl, rps = seq, 1, 1, batch
    w, cc, a, d = weights[tc]
    uw = tc * S5_GROUP
    rows = nsl * rps
    ns = h0_re.shape[0]
    kern = functools.partial(_s5_chunk_kernel, nsl=nsl, rps=rps, tc=tc, nseg=nseg)
    state_spec = pl.BlockSpec((ns, S5_CH), lambda k: (0, k))
    state_shape = jax.ShapeDtypeStruct((ns, S5_GROUPS * S5_STATE), F32)
    y, hf_re, hf_im = pl.pallas_call(
        kern, grid=(S5_NBLK,),
        in_specs=[pl.BlockSpec((tokens, LANES), lambda k: (0, COL_U // LANES + k)),
                  pl.BlockSpec((S5_GB, uw, uw + 4 * S5_STATE), lambda k: (k, 0, 0)),
                  pl.BlockSpec((S5_GB // 2, 4 * S5_STATE, 2 * uw), lambda k: (k, 0, 0)),
                  pl.BlockSpec((1, 1, 2 * S5_CH), lambda k: (k, 0, 0)),
                  pl.BlockSpec((1, 1, LANES), lambda k: (k, 0, 0)),
                  state_spec, state_spec],
        out_specs=[pl.BlockSpec((tokens, LANES), lambda k: (0, k)), state_spec, state_spec],
        out_shape=[jax.ShapeDtypeStruct((tokens, S5_WIDTH), F32), state_shape, state_shape],
        scratch_shapes=[pltpu.VMEM((tc, rows, LANES), F32), pltpu.VMEM((S5_GB, rows, uw), BF16),
                        pltpu.VMEM((S5_GB, rows, uw), F32), pltpu.VMEM((S5_GB, rows, LANES), F32)],
        compiler_params=_params("parallel"),
        name="s5_chunked",
    )(z, w, cc, a, d, h0_re, h0_im)
    if nseg == 2:
        hf_re, hf_im = hf_re[1::2], hf_im[1::2]
    return y, hf_re, hf_im


def _gla_kernel(q_ref, k_ref, v_ref, r_ref, al_ref, wa_ref, ba_ref, gn_ref, s0_ref,
                y_ref, sf_ref, s_scr, *, ntile, groups, chained, width):
    rt = GLA_TILE
    c = rt // groups
    shift = int(math.log2(c))
    row_g = lax.broadcasted_iota(jnp.int32, (rt, rt), 0)
    col_g = lax.broadcasted_iota(jnp.int32, (rt, rt), 1)
    same = (row_g >> shift) == (col_g >> shift)
    causal = same & (row_g >= col_g)
    cum_w = jnp.concatenate([causal.astype(BF16), same.astype(BF16)], axis=0)
    tn_dims = (((0,), (0,)), ((), ()))
    nt_dims = (((1,), (1,)), ((), ()))

    def split3(x):
        hi = x.astype(BF16)
        r1 = x - hi.astype(F32)
        mid = r1.astype(BF16)
        lo = (r1 - mid.astype(F32)).astype(BF16)
        return jnp.concatenate([hi, mid, lo], axis=1)

    if chained:
        s_scr[...] = s0_ref[0, 0]

    def body(it, carry):
        tiles = [it * width + u for u in range(width)]
        rows = [pl.ds(pl.multiple_of(t * rt, rt), rt) for t in tiles]
        log_a = []
        for u in range(width):
            x = jnp.dot(al_ref[rows[u], :].astype(BF16), wa_ref[...], preferred_element_type=F32) + ba_ref[...]
            log_a.append((jnp.minimum(x, 0.0) - jnp.log1p(jnp.exp(-jnp.abs(x)))) * (1.0 / GLA_TAU))
        la3 = [split3(la) for la in log_a]
        cums, e_col = [], []
        for u in range(width):
            cs = jnp.dot(cum_w, la3[u], preferred_element_type=F32)
            cums.append(cs[:, :LANES] + cs[:, LANES:2 * LANES] + cs[:, 2 * LANES:])
            e_col.append(jnp.exp(cums[u].T))
        qb, kd, v, att, upd = [], [], [], [], []
        for u in range(width):
            b = cums[u][:rt]
            b_end = cums[u][rt:]
            k = k_ref[rows[u], :]
            v.append(v_ref[rows[u], :])
            q = q_ref[rows[u], :] * (GLA_DK ** -0.5)
            if c <= GLA_SUB:
                mid = 0.5 * b_end
                q_mid = q * jnp.exp(b - mid)
                k_mid = k * jnp.exp(mid - b)
                e_mid = jnp.exp(mid)
                qb.append(q_mid * e_mid)
                kd.append(k_mid * e_mid)
                a = lax.dot_general(q_mid.astype(BF16), k_mid.astype(BF16), nt_dims, preferred_element_type=F32)
            else:
                qb.append(q * jnp.exp(b))
                kd.append(k * jnp.exp(b_end - b))
                key_row = lax.broadcasted_iota(jnp.int32, (rt, GLA_DK), 0)
                blocks = []
                for lo in range(0, rt, GLA_SUB):
                    hi = lo + GLA_SUB
                    top = b[lo - 1:lo] if lo else jnp.zeros_like(b[0:1])
                    mid = 0.5 * (top + b[hi - 1:hi])
                    q_blk = q[lo:hi] * jnp.exp(b[lo:hi] - mid)
                    k_blk = jnp.where(key_row < hi, k * jnp.exp(mid - b), 0.0)
                    blocks.append(lax.dot_general(q_blk.astype(BF16), k_blk.astype(BF16), nt_dims,
                                                  preferred_element_type=F32))
                a = jnp.concatenate(blocks, axis=0)
            att.append(jnp.where(causal, a, 0.0).astype(BF16))
            upd.append([lax.dot_general(kd[u][g * c:(g + 1) * c].astype(BF16), v[u][g * c:(g + 1) * c].astype(BF16),
                                        tn_dims, preferred_element_type=F32) for g in range(groups)])
        o = [jnp.dot(att[u], v[u].astype(BF16), preferred_element_type=F32) for u in range(width)]
        for u in range(width):
            o_state = []
            for g in range(groups):
                s = s_scr[...] if chained else s0_ref[tiles[u] * groups + g, 0]
                o_state.append(jnp.dot(qb[u][g * c:(g + 1) * c].astype(BF16), s.astype(BF16),
                                       preferred_element_type=F32))
                s_new = s * e_col[u][:, rt + g * c:rt + g * c + 1] + upd[u][g]
                if chained:
                    s_scr[...] = s_new
                else:
                    sf_ref[tiles[u] * groups + g, 0] = s_new
            o[u] = o[u] + (o_state[0] if groups == 1 else jnp.concatenate(o_state, axis=0))
        for u in range(width):
            y = o[u] * lax.rsqrt(jnp.mean(o[u] * o[u], axis=-1, keepdims=True) + RMS_EPS)
            y = y * gn_ref[...]
            r = r_ref[rows[u], :]
            y_ref[rows[u], :] = (y * (r * jax.nn.sigmoid(r))).astype(y_ref.dtype)
        return carry

    lax.fori_loop(0, ntile // width, body, 0)
    if chained:
        sf_ref[0, 0] = s_scr[...]


def _gla(z, alow, w_a2, b_a, g_norm, s0, casts=None, *, batch, seq, chained):
    chunk = math.gcd(seq, GLA_CHUNK)
    if chained:
        assert chunk == GLA_TILE
        nb, groups, width = 1, 1, 8
    else:
        assert GLA_TILE % seq == 0 and chunk == seq
        nb, groups, width = 32, GLA_TILE // seq, 4
    rows = nb * seq
    kern = functools.partial(_gla_kernel, ntile=rows // GLA_TILE, groups=groups, chained=chained, width=width)
    state_spec = pl.BlockSpec((nb, 1, GLA_DK, GLA_DV), lambda i, h: (i, h, 0, 0))
    (y, s_fin), made, _ = _call_with_casts(
        kern, casts, grid=(batch // nb, GLA_HEADS),
        in_specs=[pl.BlockSpec((rows, GLA_DK), lambda i, h: (i, COL_Q // GLA_DK + h)),
                  pl.BlockSpec((rows, GLA_DK), lambda i, h: (i, COL_K // GLA_DK + h)),
                  pl.BlockSpec((rows, GLA_DV), lambda i, h: (i, COL_V // GLA_DV + h)),
                  pl.BlockSpec((rows, GLA_DV), lambda i, h: (i, COL_R // GLA_DV + h)),
                  pl.BlockSpec((rows, LANES), lambda i, h: (i, 0)),
                  pl.BlockSpec((LANES, GLA_DK), lambda i, h: (0, h)),
                  pl.BlockSpec((1, GLA_DK), lambda i, h: (0, h)),
                  pl.BlockSpec((1, GLA_DV), lambda i, h: (0, h)),
                  state_spec],
        out_specs=[pl.BlockSpec((rows, GLA_DV), lambda i, h: (i, h)), state_spec],
        out_shape=[jax.ShapeDtypeStruct((batch * seq, GLA_HEADS * GLA_DV), BF16),
                   jax.ShapeDtypeStruct((batch, GLA_HEADS, GLA_DK, GLA_DV), F32)],
        operands=(z, z, z, z, alow, w_a2, b_a, g_norm, s0),
        scratch_shapes=[pltpu.VMEM((GLA_DK, GLA_DV), F32)],
        compiler_params=_params("parallel", "parallel"),
        name="gla")
    return y, s_fin, made


def _attn_kernel(q_ref, k_ref, v_ref, o_ref):
    nt_dims = (((1,), (1,)), ((), ()))
    cols = [slice(h * XA_HEAD_DIM, (h + 1) * XA_HEAD_DIM) for h in range(XA_HEADS)]
    scores = [lax.dot_general(q_ref[:, c].astype(BF16), k_ref[0, :, c].astype(BF16), nt_dims,
                              preferred_element_type=F32) * (XA_HEAD_DIM ** -0.5) for c in cols]
    probs = []
    for s in scores:
        p = jnp.exp(s - jnp.max(s, axis=-1, keepdims=True))
        probs.append((p / jnp.sum(p, axis=-1, keepdims=True)).astype(BF16))
    for c, p in zip(cols, probs):
        o_ref[:, c] = jnp.dot(p, v_ref[0, :, c].astype(BF16), preferred_element_type=F32).astype(o_ref.dtype)


def _attn_prompt(z, mem_k, mem_v, *, batch, seq, rc):
    nrc = seq // rc
    kv_spec = pl.BlockSpec((1, MEM_LEN, XA_WIDTH), lambda b, c: (b, 0, 0))
    return pl.pallas_call(
        _attn_kernel, grid=(batch, nrc),
        in_specs=[pl.BlockSpec((rc, XA_WIDTH), lambda b, c: (b * nrc + c, COL_QX // XA_WIDTH)), kv_spec, kv_spec],
        out_specs=pl.BlockSpec((rc, XA_WIDTH), lambda b, c: (b * nrc + c, 0)),
        out_shape=jax.ShapeDtypeStruct((batch * seq, XA_WIDTH), BF16),
        compiler_params=_params("parallel", "parallel"),
        name="attn_prompt",
    )(z, mem_k, mem_v)


XA_HALF = XA_HEAD_DIM // 2
XA_ROWS = 2 * XA_HEADS


def _attn_cache_body(q_ref, k_ref, v_ref, o_ref, *, nseq, seq, nblk, grid):
    nt_dims = (((1,), (1,)), ((), ()))

    def head_rows(ref, bi, h):
        halves = [ref[bi, pl.ds(half * XA_HEADS + h, MEM_LEN, stride=XA_ROWS), :] for half in range(2)]
        return jnp.concatenate(halves, axis=1).astype(BF16)

    @pl.when(pl.program_id(0) * grid[1] + pl.program_id(1) < nblk)
    def _():
        scores = []
        for bi in range(nseq):
            rows = slice(bi * seq, (bi + 1) * seq)
            for h in range(XA_HEADS):
                cols = slice(h * XA_HEAD_DIM, (h + 1) * XA_HEAD_DIM)
                s = lax.dot_general(q_ref[rows, cols].astype(BF16), head_rows(k_ref, bi, h), nt_dims,
                                    preferred_element_type=F32)
                scores.append(s * (XA_HEAD_DIM ** -0.5))
        probs = []
        for s in scores:
            p = jnp.exp(s - jnp.max(s, axis=-1, keepdims=True))
            probs.append((p / jnp.sum(p, axis=-1, keepdims=True)).astype(BF16))
        for h in range(XA_HEADS):
            out = [jnp.dot(probs[bi * XA_HEADS + h], head_rows(v_ref, bi, h), preferred_element_type=F32)
                   for bi in range(nseq)]
            o_ref[:, h * XA_HEAD_DIM:(h + 1) * XA_HEAD_DIM] = jnp.concatenate(out, axis=0).astype(o_ref.dtype)


def _cache_rows(cache):
    bs = cache.shape[0]
    c = cache.reshape(bs, MEM_LEN, XA_HEADS, 2, XA_HALF).transpose(0, 1, 3, 2, 4)
    return c.reshape(bs, MEM_LEN * XA_ROWS, XA_HALF)


def _attn_sample_side(z, mem_k, mem_v, *, batch, seq, grid, per_step=2):
    nblk = batch // per_step
    assert nblk <= grid[0] * grid[1] and (per_step * seq) % 16 == 0
    rows = per_step * seq

    def blk(*g):
        return jnp.minimum(g[0] * grid[1] + g[1], nblk - 1)

    kv_spec = pl.BlockSpec((per_step, MEM_LEN * XA_ROWS, XA_HALF), lambda *g: (blk(*g), 0, 0))
    return _SideJob(
        in_specs=[pl.BlockSpec((rows, XA_WIDTH), lambda *g: (blk(*g), COL_QX // XA_WIDTH)), kv_spec, kv_spec],
        out_specs=[pl.BlockSpec((rows, XA_WIDTH), lambda *g: (blk(*g), 0))],
        out_shape=[jax.ShapeDtypeStruct((batch * seq, XA_WIDTH), BF16)],
        operands=(z, mem_k, mem_v),
        body=functools.partial(_attn_cache_body, nseq=per_step, seq=seq, nblk=nblk, grid=grid))


def _merge_kernel(ys_ref, yg_ref, yx_ref, g0_ref, g1_ref, g2_ref, wglu_ref, bglu_ref,
                  w0_ref, w1_ref, w2_ref, o_ref, s5_scr):
    @pl.when(pl.program_id(1) == 0)
    def _():
        y = ys_ref[...]
        lin = jnp.dot(y.astype(BF16), wglu_ref[...], preferred_element_type=F32) + bglu_ref[...]
        s5_scr[...] = (y * jax.nn.sigmoid(lin)).astype(BF16)

    j = pl.program_id(1)
    m = jax.nn.sigmoid(g0_ref[...]) * jnp.dot(s5_scr[...], w0_ref[j], preferred_element_type=F32)
    m = m + jax.nn.sigmoid(g1_ref[...]) * jnp.dot(yg_ref[...], w1_ref[j], preferred_element_type=F32)
    m = m + jax.nn.sigmoid(g2_ref[...]) * jnp.dot(yx_ref[...], w2_ref[j], preferred_element_type=F32)
    o_ref[...] = m.astype(o_ref.dtype)


def _merge(y_s5, y_gla, y_x, z, w_glu, b_glu, w_br_s5, w_br_gla, w_br_x, *, tm, tn):
    t = y_s5.shape[0]
    nj = D_MODEL // tn
    once = pl.Buffered(1)
    wide = pl.BlockSpec((tm, S5_WIDTH), lambda i, j: (i, 0))
    gate = lambda b: pl.BlockSpec((tm, tn), lambda i, j: (i, (COL_GATE + b * D_MODEL) // tn + j))
    w_br = pl.BlockSpec((nj, S5_WIDTH, tn), lambda i, j: (0, 0, 0), pipeline_mode=once)
    return pl.pallas_call(
        _merge_kernel, grid=(t // tm, nj),
        in_specs=[wide, wide, wide, gate(0), gate(1), gate(2),
                  pl.BlockSpec((S5_WIDTH, S5_WIDTH), lambda i, j: (0, 0), pipeline_mode=once),
                  pl.BlockSpec((1, S5_WIDTH), lambda i, j: (0, 0)),
                  w_br, w_br, w_br],
        out_specs=pl.BlockSpec((tm, tn), lambda i, j: (i, j)),
        out_shape=jax.ShapeDtypeStruct((t, D_MODEL), BF16),
        scratch_shapes=[pltpu.VMEM((tm, S5_WIDTH), BF16)],
        compiler_params=_params("parallel", "arbitrary"),
        name="merge",
    )(y_s5, y_gla, y_x, z, z, z, w_glu, b_glu, w_br_s5, w_br_gla, w_br_x)


def _out_ffn_kernel(x_ref, m_ref, wo_ref, gf_ref, wg_ref, wu_ref, wd_ref, gl_ref, o_ref, h_scr, acc_scr):
    k = pl.program_id(1)

    @pl.when(k == 0)
    def _():
        acc_scr[...] = x_ref[...] + jnp.dot(m_ref[...], wo_ref[...], preferred_element_type=F32)
        h_scr[...] = _rms(acc_scr[...], gf_ref[...]).astype(BF16)

    h = h_scr[...]
    gate = jnp.dot(h, wg_ref[...], preferred_element_type=F32)
    up = jnp.dot(h, wu_ref[...], preferred_element_type=F32)
    act = (gate * jax.nn.sigmoid(gate) * up).astype(BF16)
    acc_scr[...] += jnp.dot(act, wd_ref[...], preferred_element_type=F32)

    @pl.when(k == pl.num_programs(1) - 1)
    def _():
        o_ref[...] = _rms(acc_scr[...], gl_ref[...])


def _out_ffn(x, merged, w_out, g_ffn, w_gate, w_up, w_down, g_final, *, tm, th):
    t = x.shape[0]
    row = pl.BlockSpec((tm, D_MODEL), lambda i, k: (i, 0))
    vec = pl.BlockSpec((1, D_MODEL), lambda i, k: (0, 0))
    w_in = pl.BlockSpec((D_MODEL, th), lambda i, k: (0, k))
    return pl.pallas_call(
        _out_ffn_kernel, grid=(t // tm, FFN_HIDDEN // th),
        in_specs=[row, row,
                  pl.BlockSpec((D_MODEL, D_MODEL), lambda i, k: (0, 0), pipeline_mode=pl.Buffered(1)),
                  vec, w_in, w_in, pl.BlockSpec((th, D_MODEL), lambda i, k: (k, 0)), vec],
        out_specs=row,
        out_shape=jax.ShapeDtypeStruct((t, D_MODEL), F32),
        scratch_shapes=[pltpu.VMEM((tm, D_MODEL), BF16), pltpu.VMEM((tm, D_MODEL), F32)],
        compiler_params=pltpu.CompilerParams(dimension_semantics=("parallel", "arbitrary"),
                                             vmem_limit_bytes=OUT_FFN_VMEM),
        name="out_ffn",
    )(x, merged, w_out, g_ffn, w_gate, w_up, w_down, g_final)


def _first_in_proj_kernel(x_ref, g_ref, wt_ref, wa_ref, o_ref, os_ref, main_ref, alow_ref, h_ref):
    @pl.when(pl.program_id(0) == 0)
    def _():
        h = _rms(x_ref[...], g_ref[...]).astype(BF16)
        h_ref[...] = h
        lane = lax.broadcasted_iota(jnp.int32, alow_ref.shape, 1)
        w_low = jnp.where(lane < GLA_RANK, wa_ref[...].T, 0.0).astype(BF16)
        alow_ref[...] = w_low
        os_ref[...] = jnp.dot(h, w_low, preferred_element_type=F32)

    w = wt_ref[...].T.astype(BF16)
    main_ref[...] = w
    o_ref[...] = jnp.dot(h_ref[...], w, preferred_element_type=F32)


def _first_in_proj(x, g, w_in):
    t, _ = x.shape
    d, n = w_in.shape
    wt = w_in.T
    tc = 512
    per_tile = IN_TN // tc
    n_first, n_gate = COL_GATE // tc, (COL_QX - COL_GATE) // tc
    gate_start, qx_start = 4096 + GLA_RANK + XA_WIDTH, 4096 + GLA_RANK

    def src_row(j):
        t8, g8, q8 = tc // SUBLANES, gate_start // SUBLANES, qx_start // SUBLANES
        r8 = jnp.where(j < n_first, j * t8,
                       jnp.where(j < n_first + n_gate, g8 + (j - n_first) * t8,
                                 q8 + (j - n_first - n_gate) * t8))
        return r8 * SUBLANES

    once = pl.Buffered(1)
    return pl.pallas_call(
        _first_in_proj_kernel, grid=(Z_WIDTH // tc,),
        in_specs=[pl.BlockSpec((pl.Element(t), pl.Element(d)), lambda j: (0, 0), pipeline_mode=once),
                  pl.BlockSpec((pl.Element(1), pl.Element(d)), lambda j: (0, 0)),
                  pl.BlockSpec((pl.Element(tc), pl.Element(d)), lambda j: (src_row(j), 0)),
                  pl.BlockSpec((pl.Element(LANES), pl.Element(d)), lambda j: (COL_GATE, 0), pipeline_mode=once)],
        out_specs=[pl.BlockSpec((t, tc), lambda j: (0, j)),
                   pl.BlockSpec((t, LANES), lambda j: (0, 0)),
                   pl.BlockSpec((None, d, tc), lambda j: (j // per_tile, 0, j % per_tile)),
                   pl.BlockSpec((d, LANES), lambda j: (0, 0))],
        out_shape=[jax.ShapeDtypeStruct((t, Z_WIDTH), F32), jax.ShapeDtypeStruct((t, LANES), F32),
                   jax.ShapeDtypeStruct((Z_WIDTH // IN_TN, d, IN_TN), BF16),
                   jax.ShapeDtypeStruct((d, LANES), BF16)],
        scratch_shapes=[pltpu.VMEM((t, d), BF16)],
        compiler_params=_params("arbitrary"),
        name="first_in_proj",
    )(x, g, wt, wt)


def _layer(x, z, alow, memory_attention, s5_re0, s5_im0, gla_s0, w, *, batch, seq, chained, gla_casts=None):
    w = dict(w)
    y_s5, hf_re, hf_im = _s5_branch(z, w['s5'], s5_re0, s5_im0, batch=batch, seq=seq)
    y_gla, gla_s, made = _gla(z, alow, w['gla_w_a2'], w['gla_b_a'], w['gla_norm'], gla_s0, gla_casts,
                              batch=batch, seq=seq, chained=chained)
    w.update(made)
    y_x = memory_attention(w)
    merged = _merge(y_s5, y_gla, y_x, z, w['s5_w_glu'], w['s5_b_glu'],
                    w['w_br_s5'], w['w_br_gla'], w['w_br_xattn'], tm=512, tn=MERGE_TN)
    y = _out_ffn(x, merged, w['w_out'], w['norm_ffn'], w['w_ffn_gate'], w['w_ffn_up'], w['w_ffn_down'],
                 w['norm_final'], tm=512, th=FFN_TH)
    return y, hf_re, hf_im, gla_s, w


def kernel(x_prompt, x_sample, mem_prompt, state_s5_re, state_s5_im, state_gla, cache_mem_k, cache_mem_v,
           norm_mix, w_in, s5_lam_re, s5_lam_im, s5_log_dt, s5_b_re, s5_b_im, s5_c_re, s5_c_im,
           s5_d, s5_w_glu, s5_b_glu, gla_w_a2, gla_b_a, gla_norm, mem_norm, w_mem_k, w_mem_v,
           w_br_s5, w_br_gla, w_br_xattn, w_out, norm_ffn, w_ffn_gate, w_ffn_up, w_ffn_down, norm_final):
    depth = w_in.shape[0]
    assert depth == 1
    bp, sp, d = x_prompt.shape
    bs, ss, _ = x_sample.shape
    n_state = S5_GROUPS * S5_STATE
    row = lambda v: v.reshape(1, -1)

    l = 0
    s5_w = _s5_params(s5_lam_re[l], s5_lam_im[l], s5_log_dt[l], s5_b_re[l], s5_b_im[l],
                      s5_c_re[l], s5_c_im[l], s5_d[l])
    w = {
        'norm_mix': row(norm_mix[l]),
        's5': s5_w,
        's5_b_glu': row(s5_b_glu[l]),
        'gla_w_a2': jnp.pad(gla_w_a2[l], ((0, LANES - GLA_RANK), (0, 0))).astype(BF16),
        'gla_b_a': row(gla_b_a[l]), 'gla_norm': row(gla_norm[l]),
        'norm_ffn': row(norm_ffn[l]), 'norm_final': row(norm_final),
    }
    ffn_casts = {'w_ffn_gate': _CastJob(w_ffn_gate[l], row_axis=0), 'w_ffn_up': _CastJob(w_ffn_up[l], row_axis=0),
                 'w_ffn_down': _CastJob(w_ffn_down[l], row_axis=1)}
    mix_casts = {'w_out': _CastJob(w_out[l], row_axis=0), 's5_w_glu': _CastJob(s5_w_glu[l], row_axis=0),
                 'w_br_s5': _CastJob(w_br_s5[l], row_axis=0, col_tile=MERGE_TN),
                 'w_br_gla': _CastJob(w_br_gla[l], row_axis=0, col_tile=MERGE_TN),
                 'w_br_xattn': _CastJob(w_br_xattn[l], row_axis=0, col_tile=MERGE_TN),
                 'w_mem_k': _CastJob(w_mem_k[l], row_axis=0, col_tile=IN_TN),
                 'w_mem_v': _CastJob(w_mem_v[l], row_axis=0, col_tile=IN_TN)}

    zero_s5 = jnp.zeros((bp, n_state), F32)
    zero_gla = jnp.zeros((bp, GLA_HEADS, GLA_DK, GLA_DV), F32)
    xp, xs = x_prompt.reshape(bp * sp, d), x_sample.reshape(bs * ss, d)

    z_s, alow_s, w['w_main'], w['w_alow'] = _first_in_proj(xs, w['norm_mix'], w_in[l])
    cache_k, cache_v = _cache_rows(cache_mem_k[l]), _cache_rows(cache_mem_v[l])
    z_p, alow_p, made, (yx_s,) = _norm_matmul(
        xp, w['norm_mix'], w['w_main'], w['w_alow'], ffn_casts,
        lambda grid: _attn_sample_side(z_s, cache_k, cache_v, batch=bs, seq=ss, grid=grid), tm=1024)
    w.update(made)

    mem_kv = {}

    def prompt_attention(w):
        mem = mem_prompt.reshape(bp * MEM_LEN, d)
        for name in ('w_mem_k', 'w_mem_v'):
            mem_kv[name] = _norm_matmul(mem, row(mem_norm[l]), w[name], tm=512).reshape(bp, MEM_LEN, XA_WIDTH)
        return _attn_prompt(z_p, mem_kv['w_mem_k'], mem_kv['w_mem_v'], batch=bp, seq=sp, rc=1024)

    yp, p_re, p_im, p_gla, w = _layer(xp, z_p, alow_p, prompt_attention, zero_s5, zero_s5, zero_gla, w,
                                      batch=bp, seq=sp, chained=True, gla_casts=mix_casts)
    mk, mv = mem_kv['w_mem_k'], mem_kv['w_mem_v']

    ys, s_re, s_im, s_gla, _ = _layer(xs, z_s, alow_s, lambda w: yx_s,
                                      state_s5_re[l].reshape(bs, n_state), state_s5_im[l].reshape(bs, n_state),
                                      state_gla[l], w, batch=bs, seq=ss, chained=False)

    s5_shape_p = (1, bp, S5_GROUPS, S5_STATE)
    s5_shape_s = (1, bs, S5_GROUPS, S5_STATE)
    kv_shape = (1, bp, MEM_LEN, XA_HEADS, XA_HEAD_DIM)
    return (yp.reshape(bp, sp, d), ys.reshape(bs, ss, d),
            p_re.reshape(s5_shape_p), p_im.reshape(s5_shape_p), p_gla[None],
            mk.reshape(kv_shape), mv.reshape(kv_shape),
            s_re.reshape(s5_shape_s), s_im.reshape(s5_shape_s), s_gla[None])
```

```python
import functools
import math
from typing import Callable, NamedTuple, Optional

import jax
import jax.numpy as jnp
from jax import lax
from jax.experimental import pallas as pl
from jax.experimental.pallas import tpu as pltpu

F32 = jnp.float32
BF16 = jnp.bfloat16

D_MODEL = 2048
S5_WIDTH = 1024
S5_GROUP = 16
S5_GROUPS = 64
S5_STATE = 64
GLA_HEADS = 4
GLA_DK = 128
GLA_DV = 256
GLA_RANK = 16
GLA_TAU = 16.0
GLA_CHUNK = 64
XA_HEADS = 4
XA_HEAD_DIM = 256
XA_WIDTH = 1024
MEM_LEN = 256
FFN_HIDDEN = 5632
RMS_EPS = 1e-6

LANES = 128
SUBLANES = 8
VMEM_LIMIT = 56 * 1024 * 1024
OUT_FFN_VMEM = 52 * 1024 * 1024

COL_U = 0
COL_Q = 1024
COL_K = 1536
COL_V = 2048
COL_R = 3072
COL_GATE = 4096
COL_QX = 10240
Z_WIDTH = 11264

S5_GB = 8
S5_CH = S5_GB * S5_STATE
S5_NBLK = S5_GROUPS // S5_GB
S5_TC = 16
S5_PAIRS_PER_STEP = 4
GLA_TILE = 64
GLA_SUB = 16
MERGE_TN = 1024
IN_TN = 1024
FFN_TH = 512


def _params(*sem):
    return pltpu.CompilerParams(dimension_semantics=sem, vmem_limit_bytes=VMEM_LIMIT)


def _rms(x, g):
    return x * lax.rsqrt(jnp.mean(x * x, axis=-1, keepdims=True) + RMS_EPS) * g


class _CastJob(NamedTuple):
    src: jax.Array
    row_axis: int
    col_tile: Optional[int] = None

    def specs(self, grid):
        rows, cols = self.src.shape
        ra, ca = self.row_axis, 1 - self.row_axis
        assert rows % grid[ra] == 0 and cols % grid[ca] == 0
        br, bc = rows // grid[ra], cols // grid[ca]
        in_spec = pl.BlockSpec((br, bc), lambda *g: (g[ra], g[ca]))
        if self.col_tile is None:
            return in_spec, in_spec, jax.ShapeDtypeStruct((rows, cols), BF16)
        assert self.col_tile % bc == 0 and cols % self.col_tile == 0
        per = self.col_tile // bc
        out_spec = pl.BlockSpec((None, br, bc), lambda *g: (g[ca] // per, g[ra], g[ca] % per))
        return in_spec, out_spec, jax.ShapeDtypeStruct((cols // self.col_tile, rows, self.col_tile), BF16)


class _SideJob(NamedTuple):
    in_specs: list
    out_specs: list
    out_shape: list
    operands: tuple
    body: Callable


def _with_side_work(kernel, n_in, n_out, n_cast, side):
    n_side_in = len(side.in_specs) if side else 0
    n_side_out = len(side.out_specs) if side else 0

    def wrapped(*refs):
        ins, rest = refs[:n_in], refs[n_in:]
        srcs, rest = rest[:n_cast], rest[n_cast:]
        side_ins, rest = rest[:n_side_in], rest[n_side_in:]
        outs, rest = rest[:n_out], rest[n_out:]
        dsts, rest = rest[:n_cast], rest[n_cast:]
        side_outs, scratch = rest[:n_side_out], rest[n_side_out:]
        for src, dst in zip(srcs, dsts):
            dst[...] = src[...].astype(dst.dtype)
        kernel(*ins, *outs, *scratch)
        if side:
            side.body(*side_ins, *side_outs)
    return wrapped


def _call_with_casts(kernel, casts, *, grid, in_specs, out_specs, out_shape, operands, side=None, **kwargs):
    casts = casts or {}
    specs = [job.specs(grid) for job in casts.values()]
    side_in = list(side.in_specs) if side else []
    side_out = list(side.out_specs) if side else []
    results = pl.pallas_call(
        _with_side_work(kernel, len(in_specs), len(out_specs), len(specs), side), grid=grid,
        in_specs=list(in_specs) + [s[0] for s in specs] + side_in,
        out_specs=list(out_specs) + [s[1] for s in specs] + side_out,
        out_shape=list(out_shape) + [s[2] for s in specs] + (list(side.out_shape) if side else []),
        **kwargs,
    )(*operands, *[job.src for job in casts.values()], *(side.operands if side else ()))
    n, c = len(out_specs), len(specs)
    return results[:n], dict(zip(casts.keys(), results[n:n + c])), results[n + c:]


def _norm_matmul_kernel(x_ref, g_ref, w_ref, o_ref, h_ref):
    @pl.when(pl.program_id(1) == 0)
    def _():
        h_ref[...] = _rms(x_ref[...], g_ref[...]).astype(BF16)

    o_ref[...] = jnp.dot(h_ref[...], w_ref[0], preferred_element_type=F32)


def _norm_matmul2_kernel(x_ref, g_ref, w_ref, ws_ref, o_ref, os_ref, h_ref):
    @pl.when(pl.program_id(1) == 0)
    def _():
        h = _rms(x_ref[...], g_ref[...]).astype(BF16)
        h_ref[...] = h
        os_ref[...] = jnp.dot(h, ws_ref[...], preferred_element_type=F32)

    o_ref[...] = jnp.dot(h_ref[...], w_ref[0], preferred_element_type=F32)


def _norm_matmul(x, g, w, w_small=None, casts=None, make_side=None, *, tm):
    t, d = x.shape
    nj, _, tn = w.shape
    n = nj * tn
    grid = (t // tm, n // tn)
    in_specs = [pl.BlockSpec((tm, d), lambda i, j: (i, 0)),
                pl.BlockSpec((1, d), lambda i, j: (0, 0)),
                pl.BlockSpec((1, d, tn), lambda i, j: (j, 0, 0))]
    out_specs = pl.BlockSpec((tm, tn), lambda i, j: (i, j))
    out_shape = jax.ShapeDtypeStruct((t, n), F32)
    scratch = [pltpu.VMEM((tm, d), BF16)]
    if w_small is None:
        return pl.pallas_call(_norm_matmul_kernel, grid=grid, in_specs=in_specs, out_specs=out_specs,
                              out_shape=out_shape, scratch_shapes=scratch,
                              compiler_params=_params("parallel", "arbitrary"),
                              name="norm_matmul")(x, g, w)
    ns = w_small.shape[1]
    in_specs.append(pl.BlockSpec((d, ns), lambda i, j: (0, 0)))
    (z, narrow), made, side_out = _call_with_casts(
        _norm_matmul2_kernel, casts, grid=grid, in_specs=in_specs,
        side=make_side(grid) if make_side else None,
        out_specs=[out_specs, pl.BlockSpec((tm, ns), lambda i, j: (i, 0))],
        out_shape=[out_shape, jax.ShapeDtypeStruct((t, ns), F32)],
        operands=(x, g, w, w_small), scratch_shapes=scratch,
        compiler_params=_params("arbitrary" if make_side else "parallel", "arbitrary"), name="in_proj")
    return z, narrow, made, side_out


def _s5_param_kernel(*refs):
    for pi in range(S5_PAIRS_PER_STEP):
        _s5_pair_operands(pi, *refs)


def _s5_pair_operands(pi, lr_ref, li_ref, ldt_ref, btr_ref, bti_ref, cr_ref, ci_ref,
                      w16_ref, cc16_ref, w8_ref, cc8_ref, ap_ref):
    p = S5_STATE
    half = S5_TC // 2
    lam_re, lam_im = lr_ref[pi], li_ref[pi]
    dt = jnp.exp(ldt_ref[pi])
    mag = jnp.exp(lam_re * dt)
    a_re = mag * jnp.cos(lam_im * dt)
    a_im = mag * jnp.sin(lam_im * dt)
    den = lam_re * lam_re + lam_im * lam_im
    coef_re = ((a_re - 1.0) * lam_re + a_im * lam_im) / den
    coef_im = (a_im * lam_re - (a_re - 1.0) * lam_im) / den
    bt_re, bt_im = btr_ref[pi], bti_ref[pi]
    bb_re = coef_re * bt_re - coef_im * bt_im
    bb_im = coef_re * bt_im + coef_im * bt_re
    c_re, c_im = cr_ref[pi], ci_ref[pi]

    pw_re, pw_im = jnp.ones_like(a_re), jnp.zeros_like(a_re)
    ca_re, ca_im, ab_re, ab_im, powers = [], [], [], [], {}
    for j in range(S5_TC + 1):
        powers[j] = (pw_re, pw_im)
        ca_re.append(c_re * pw_re - c_im * pw_im)
        ca_im.append(c_re * pw_im + c_im * pw_re)
        ab_re.append(bb_re * pw_re - bb_im * pw_im)
        ab_im.append(bb_re * pw_im + bb_im * pw_re)
        pw_re, pw_im = pw_re * a_re - pw_im * a_im, pw_re * a_im + pw_im * a_re
    lag_re = jnp.concatenate(ca_re[:S5_TC], axis=0)
    lag_im = jnp.concatenate(ca_im[:S5_TC], axis=0)
    end_re = jnp.concatenate([ab_re[S5_TC - 1 - s] for s in range(S5_TC)], axis=0)
    end_im = jnp.concatenate([ab_im[S5_TC - 1 - s] for s in range(S5_TC)], axis=0)
    car_re = jnp.concatenate(ca_re[1:], axis=0).T
    car_im = -jnp.concatenate(ca_im[1:], axis=0).T

    def pair_blocks(x):
        first = lax.broadcasted_iota(jnp.int32, x.shape, 0) < p
        return jnp.concatenate([jnp.where(first, x, 0.0), jnp.where(first, 0.0, x)], axis=1)

    cc16_ref[pi] = jnp.concatenate([pair_blocks(car_re), pair_blocks(car_im)], axis=0).astype(BF16)
    n8 = half * S5_GROUP
    cc8_ref[pi] = jnp.concatenate([pair_blocks(car_re[:, :n8]), pair_blocks(car_im[:, :n8])], axis=0).astype(BF16)

    nt_dims = (((1,), (1,)), ((), ()))
    hi = lax.Precision.HIGHEST
    lane = lax.broadcasted_iota(jnp.int32, (S5_GROUP, LANES), 1)
    end_lane = lax.broadcasted_iota(jnp.int32, end_re.shape, 1)
    for gi in range(2):
        mine = (lane < p) if gi == 0 else (lane >= p)
        own_re, own_im = jnp.where(mine, bb_re, 0.0), jnp.where(mine, bb_im, 0.0)
        strip = (lax.dot_general(own_re, lag_re, nt_dims, precision=hi, preferred_element_type=F32)
                 - lax.dot_general(own_im, lag_im, nt_dims, precision=hi, preferred_element_type=F32))
        lo, up = strip[:, :LANES], strip[:, LANES:]
        blocks = [strip]
        for s in range(1, S5_TC):
            sh = (s % half) * S5_GROUP
            lo_r = pltpu.roll(lo, sh, 1) if sh else lo
            up_r = pltpu.roll(up, sh, 1) if sh else up
            if s < half:
                blocks.append(jnp.concatenate([jnp.where(lane >= sh, lo_r, 0.0),
                                               jnp.where(lane >= sh, up_r, lo_r)], axis=1))
            else:
                blocks.append(jnp.concatenate([jnp.zeros_like(lo), jnp.where(lane >= sh, lo_r, 0.0)], axis=1))
        toeplitz = jnp.concatenate(blocks, axis=0)
        own_end = (end_lane < p) if gi == 0 else (end_lane >= p)
        ends = jnp.concatenate([jnp.where(own_end, end_re, 0.0), jnp.where(own_end, end_im, 0.0)], axis=1)
        w16_ref[2 * pi + gi] = jnp.concatenate([toeplitz, ends], axis=1).astype(BF16)
        w8_ref[2 * pi + gi] = jnp.concatenate([toeplitz[:n8, :n8], ends[n8:, :]], axis=1).astype(BF16)
    ap_ref[pi] = jnp.concatenate([powers[S5_TC][0], powers[S5_TC][1], powers[half][0], powers[half][1]], axis=0)


def _s5_params(lam_re, lam_im, log_dt, b_re, b_im, c_re, c_im, d_skip):
    g, p, c = S5_GROUPS, S5_STATE, S5_GROUP
    npair = g // 2
    w16, w8 = S5_TC * c, S5_TC // 2 * c

    def pair_lanes(x):
        return x.reshape(npair, 2, x.shape[1], p).transpose(0, 2, 1, 3).reshape(npair, x.shape[1], 2 * p)

    row = lambda x: pair_lanes(x.reshape(g, 1, p))
    pp = S5_PAIRS_PER_STEP
    blk = lambda *shape: pl.BlockSpec((pp,) + shape, lambda i: (i, 0, 0))
    two = lambda *shape: pl.BlockSpec((2 * pp,) + shape, lambda i: (i, 0, 0))
    wt16, cc16, wt8, cc8, apow = pl.pallas_call(
        _s5_param_kernel, grid=(npair // pp,),
        in_specs=[blk(1, 2 * p)] * 3 + [blk(c, 2 * p)] * 4,
        out_specs=[two(w16, w16 + 4 * p), blk(4 * p, 2 * w16), two(w8, w8 + 4 * p), blk(4 * p, 2 * w8),
                   blk(4, 2 * p)],
        out_shape=[jax.ShapeDtypeStruct((g, w16, w16 + 4 * p), BF16),
                   jax.ShapeDtypeStruct((npair, 4 * p, 2 * w16), BF16),
                   jax.ShapeDtypeStruct((g, w8, w8 + 4 * p), BF16),
                   jax.ShapeDtypeStruct((npair, 4 * p, 2 * w8), BF16),
                   jax.ShapeDtypeStruct((npair, 4, 2 * p), F32)],
        compiler_params=_params("parallel"),
        name="s5_params",
    )(row(lam_re), row(lam_im), row(jnp.broadcast_to(log_dt[:, None], (g, p))),
      pair_lanes(b_re.transpose(0, 2, 1)), pair_lanes(b_im.transpose(0, 2, 1)),
      pair_lanes(c_re), pair_lanes(c_im))
    d = d_skip.reshape(S5_NBLK, 1, S5_GB * c)

    def transition(r):
        return jnp.concatenate([apow[:, r].reshape(S5_NBLK, 1, S5_CH), apow[:, r + 1].reshape(S5_NBLK, 1, S5_CH)],
                               axis=-1)

    return {S5_TC: (wt16, cc16, transition(0), d), S5_TC // 2: (wt8, cc8, transition(2), d)}


def _unit_transpose(vs):
    unit = lax.broadcasted_iota(jnp.int32, vs[0].shape, 1) >> int(math.log2(S5_GROUP))
    for dist in (4, 2, 1):
        keep = (unit & dist) == 0
        nxt = list(vs)
        for i in range(8):
            if i & dist == 0:
                a, b = vs[i], vs[i + dist]
                nxt[i] = jnp.where(keep, a, pltpu.roll(b, dist * S5_GROUP, 1))
                nxt[i + dist] = jnp.where(keep, pltpu.roll(a, LANES - dist * S5_GROUP, 1), b)
        vs = nxt
    return vs


def _s5_chunk_kernel(z_ref, w_ref, cc_ref, a_ref, d_ref, h0r_ref, h0i_ref,
                     y_ref, hfr_ref, hfi_ref, ut_ref, ug_ref, yg_ref, e_ref, *, nsl, rps, tc, nseg):
    uw = tc * S5_GROUP
    nq = uw // LANES
    rows = nsl * rps
    nk = rps if nsl > 1 else 1
    ns = rows // nk
    npair = S5_GB // 2
    rchunk = min(rows, 128)

    for s in range(nsl):
        for t in range(tc):
            ut_ref[t, s * rps:(s + 1) * rps, :] = z_ref[pl.ds(s * rps * tc + t, rps, stride=tc), :]

    def row_chunk(rc):
        return pl.ds(pl.multiple_of(rc * rchunk, rchunk), rchunk)

    def to_groups(rc, carry):
        for q in range(nq):
            per_group = _unit_transpose([ut_ref[q * 8 + t, row_chunk(rc), :] for t in range(8)])
            for g in range(S5_GB):
                ug_ref[g, row_chunk(rc), q * LANES:(q + 1) * LANES] = per_group[g].astype(BF16)
        return carry

    lax.fori_loop(0, rows // rchunk, to_groups, 0)

    for pair in range(npair):
        ends = None
        for gi in range(2):
            g = 2 * pair + gi
            em = jnp.dot(ug_ref[g], w_ref[g], preferred_element_type=F32)
            yg_ref[g] = em[:, :uw]
            ends = em[:, uw:] if ends is None else ends + em[:, uw:]
        for part, blk in ((ends[:, :LANES], pair), (ends[:, LANES:], npair + pair)):
            if nk == 1:
                e_ref[blk] = part
            else:
                for s in range(nsl):
                    e_ref[blk, pl.ds(s, rps, stride=nsl), :] = part[s * rps:(s + 1) * rps, :]

    a_row = [(a_ref[0, :, i * LANES:(i + 1) * LANES], a_ref[0, :, S5_CH + i * LANES:S5_CH + (i + 1) * LANES])
             for i in range(npair)]
    a_full = [(jnp.broadcast_to(ar, (ns, LANES)), jnp.broadcast_to(ai, (ns, LANES))) for ar, ai in a_row]

    def slab(t):
        return pl.ds(pl.multiple_of(t * ns, ns), ns)

    def run_scan(h, store):
        def step(t, carry):
            out = []
            for i in range(npair):
                hr, hi = carry[i]
                ar, ai = a_full[i]
                er = e_ref[i, slab(t), :]
                ei = e_ref[npair + i, slab(t), :]
                if store:
                    e_ref[i, slab(t), :] = hr
                    e_ref[npair + i, slab(t), :] = hi
                out.append((ar * hr - ai * hi + er, ar * hi + ai * hr + ei))
            return tuple(out)

        if nk == 1:
            return step(0, h)
        return lax.fori_loop(0, nk, step, h, unroll=4)

    h0 = tuple((h0r_ref[:, i * LANES:(i + 1) * LANES], h0i_ref[:, i * LANES:(i + 1) * LANES]) for i in range(npair))
    if nseg == 1:
        final = run_scan(h0, store=True)
    else:
        zero = jnp.zeros((ns, LANES), F32)
        seg_end = run_scan(tuple((zero, zero) for _ in range(npair)), store=False)
        second = (lax.broadcasted_iota(jnp.int32, (ns, LANES), 0) & 1) == 1
        init, final = [], []
        for i in range(npair):
            p_re, p_im = a_row[i]
            for _ in range(int(math.log2(nk))):
                p_re, p_im = p_re * p_re - p_im * p_im, 2.0 * (p_re * p_im)
            (hr, hi), (er, ei) = h0[i], seg_end[i]
            i_re = jnp.where(second, p_re * hr - p_im * hi + pltpu.roll(er, 1, 0), hr)
            i_im = jnp.where(second, p_re * hi + p_im * hr + pltpu.roll(ei, 1, 0), hi)
            init.append((i_re, i_im))
            final.append((p_re * i_re - p_im * i_im + er, p_re * i_im + p_im * i_re + ei))
        run_scan(tuple(init), store=True)
    for i in range(npair):
        hfr_ref[:, i * LANES:(i + 1) * LANES] = final[i][0]
        hfi_ref[:, i * LANES:(i + 1) * LANES] = final[i][1]

    def chunk_rows(blk):
        if nk == 1:
            return e_ref[blk]
        return jnp.concatenate([e_ref[blk, pl.ds(s, rps, stride=nsl), :] for s in range(nsl)], axis=0)

    for pair in range(npair):
        h_in = jnp.concatenate([chunk_rows(pair), chunk_rows(npair + pair)], axis=1).astype(BF16)
        carried = jnp.dot(h_in, cc_ref[pair], preferred_element_type=F32)
        for gi in range(2):
            yg_ref[2 * pair + gi] += carried[:, gi * uw:(gi + 1) * uw]

    d_row = d_ref[0]

    def to_tokens(rc, carry):
        for q in range(nq):
            per_tau = _unit_transpose([yg_ref[g, row_chunk(rc), q * LANES:(q + 1) * LANES] for g in range(S5_GB)])
            for t in range(8):
                y = per_tau[t] + d_row * ut_ref[q * 8 + t, row_chunk(rc), :]
                ut_ref[q * 8 + t, row_chunk(rc), :] = jax.nn.gelu(y, approximate=True)
        return carry

    lax.fori_loop(0, rows // rchunk, to_tokens, 0)
    for s in range(nsl):
        for t in range(tc):
            y_ref[pl.ds(s * rps * tc + t, rps, stride=tc), :] = ut_ref[t, s * rps:(s + 1) * rps, :]


def _s5_branch(z, weights, h0_re, h0_im, *, batch, seq):
    tokens = batch * seq
    if seq > S5_TC:
        tc, nseg = S5_TC, 2
        nsl, rps = batch * nseg, seq // (nseg * tc)
        assert rps & (rps - 1) == 0
        h0_re, h0_im = jnp.repeat(h0_re, nseg, axis=0), jnp.repeat(h0_im, nseg, axis=0)
    else:
        tc, nseg, nsl, rps = seq, 1, 1, batch
    w, cc, a, d = weights[tc]
    uw = tc * S5_GROUP
    rows = nsl * rps
    ns = h0_re.shape[0]
    kern = functools.partial(_s5_chunk_kernel, nsl=nsl, rps=rps, tc=tc, nseg=nseg)
    state_spec = pl.BlockSpec((ns, S5_CH), lambda k: (0, k))
    state_shape = jax.ShapeDtypeStruct((ns, S5_GROUPS * S5_STATE), F32)
    y, hf_re, hf_im = pl.pallas_call(
        kern, grid=(S5_NBLK,),
        in_specs=[pl.BlockSpec((tokens, LANES), lambda k: (0, COL_U // LANES + k)),
                  pl.BlockSpec((S5_GB, uw, uw + 4 * S5_STATE), lambda k: (k, 0, 0)),
                  pl.BlockSpec((S5_GB // 2, 4 * S5_STATE, 2 * uw), lambda k: (k, 0, 0)),
                  pl.BlockSpec((1, 1, 2 * S5_CH), lambda k: (k, 0, 0)),
                  pl.BlockSpec((1, 1, LANES), lambda k: (k, 0, 0)),
                  state_spec, state_spec],
        out_specs=[pl.BlockSpec((tokens, LANES), lambda k: (0, k)), state_spec, state_spec],
        out_shape=[jax.ShapeDtypeStruct((tokens, S5_WIDTH), F32), state_shape, state_shape],
        scratch_shapes=[pltpu.VMEM((tc, rows, LANES), F32), pltpu.VMEM((S5_GB, rows, uw), BF16),
                        pltpu.VMEM((S5_GB, rows, uw), F32), pltpu.VMEM((S5_GB, rows, LANES), F32)],
        compiler_params=_params("parallel"),
        name="s5_chunked",
    )(z, w, cc, a, d, h0_re, h0_im)
    if nseg == 2:
        hf_re, hf_im = hf_re[1::2], hf_im[1::2]
    return y, hf_re, hf_im


def _gla_kernel(q_ref, k_ref, v_ref, r_ref, al_ref, wa_ref, ba_ref, gn_ref, s0_ref,
                y_ref, sf_ref, s_scr, *, ntile, groups, chained, width):
    rt = GLA_TILE
    c = rt // groups
    shift = int(math.log2(c))
    row_g = lax.broadcasted_iota(jnp.int32, (rt, rt), 0)
    col_g = lax.broadcasted_iota(jnp.int32, (rt, rt), 1)
    same = (row_g >> shift) == (col_g >> shift)
    causal = same & (row_g >= col_g)
    cum_w = jnp.concatenate([causal.astype(BF16), same.astype(BF16)], axis=0)
    tn_dims = (((0,), (0,)), ((), ()))
    nt_dims = (((1,), (1,)), ((), ()))

    def split3(x):
        hi = x.astype(BF16)
        r1 = x - hi.astype(F32)
        mid = r1.astype(BF16)
        lo = (r1 - mid.astype(F32)).astype(BF16)
        return jnp.concatenate([hi, mid, lo], axis=1)

    if chained:
        s_scr[...] = s0_ref[0, 0]

    def body(it, carry):
        tiles = [it * width + u for u in range(width)]
        rows = [pl.ds(pl.multiple_of(t * rt, rt), rt) for t in tiles]
        log_a = []
        for u in range(width):
            x = jnp.dot(al_ref[rows[u], :].astype(BF16), wa_ref[...], preferred_element_type=F32) + ba_ref[...]
            log_a.append((jnp.minimum(x, 0.0) - jnp.log1p(jnp.exp(-jnp.abs(x)))) * (1.0 / GLA_TAU))
        la3 = [split3(la) for la in log_a]
        cums, e_col = [], []
        for u in range(width):
            cs = jnp.dot(cum_w, la3[u], preferred_element_type=F32)
            cums.append(cs[:, :LANES] + cs[:, LANES:2 * LANES] + cs[:, 2 * LANES:])
            e_col.append(jnp.exp(cums[u].T))
        qb, kd, v, att, upd = [], [], [], [], []
        for u in range(width):
            b = cums[u][:rt]
            b_end = cums[u][rt:]
            k = k_ref[rows[u], :]
            v.append(v_ref[rows[u], :])
            q = q_ref[rows[u], :] * (GLA_DK ** -0.5)
            if c <= GLA_SUB:
                mid = 0.5 * b_end
                q_mid = q * jnp.exp(b - mid)
                k_mid = k * jnp.exp(mid - b)
                e_mid = jnp.exp(mid)
                qb.append(q_mid * e_mid)
                kd.append(k_mid * e_mid)
                a = lax.dot_general(q_mid.astype(BF16), k_mid.astype(BF16), nt_dims, preferred_element_type=F32)
            else:
                qb.append(q * jnp.exp(b))
                kd.append(k * jnp.exp(b_end - b))
                key_row = lax.broadcasted_iota(jnp.int32, (rt, GLA_DK), 0)
                blocks = []
                for lo in range(0, rt, GLA_SUB):
                    hi = lo + GLA_SUB
                    top = b[lo - 1:lo] if lo else jnp.zeros_like(b[0:1])
                    mid = 0.5 * (top + b[hi - 1:hi])
                    q_blk = q[lo:hi] * jnp.exp(b[lo:hi] - mid)
                    k_blk = jnp.where(key_row < hi, k * jnp.exp(mid - b), 0.0)
                    blocks.append(lax.dot_general(q_blk.astype(BF16), k_blk.astype(BF16), nt_dims,
                                                  preferred_element_type=F32))
                a = jnp.concatenate(blocks, axis=0)
            att.append(jnp.where(causal, a, 0.0).astype(BF16))
            upd.append([lax.dot_general(kd[u][g * c:(g + 1) * c].astype(BF16), v[u][g * c:(g + 1) * c].astype(BF16),
                                        tn_dims, preferred_element_type=F32) for g in range(groups)])
        o = [jnp.dot(att[u], v[u].astype(BF16), preferred_element_type=F32) for u in range(width)]
        for u in range(width):
            o_state = []
            for g in range(groups):
                s = s_scr[...] if chained else s0_ref[tiles[u] * groups + g, 0]
                o_state.append(jnp.dot(qb[u][g * c:(g + 1) * c].astype(BF16), s.astype(BF16),
                                       preferred_element_type=F32))
                s_new = s * e_col[u][:, rt + g * c:rt + g * c + 1] + upd[u][g]
                if chained:
                    s_scr[...] = s_new
                else:
                    sf_ref[tiles[u] * groups + g, 0] = s_new
            o[u] = o[u] + (o_state[0] if groups == 1 else jnp.concatenate(o_state, axis=0))
        for u in range(width):
            y = o[u] * lax.rsqrt(jnp.mean(o[u] * o[u], axis=-1, keepdims=True) + RMS_EPS)
            y = y * gn_ref[...]
            r = r_ref[rows[u], :]
            y_ref[rows[u], :] = (y * (r * jax.nn.sigmoid(r))).astype(y_ref.dtype)
        return carry

    lax.fori_loop(0, ntile // width, body, 0)
    if chained:
        sf_ref[0, 0] = s_scr[...]


def _gla(z, alow, w_a2, b_a, g_norm, s0, casts=None, *, batch, seq, chained):
    chunk = math.gcd(seq, GLA_CHUNK)
    if chained:
        assert chunk == GLA_TILE
        nb, groups, width = 1, 1, 8
    else:
        assert GLA_TILE % seq == 0 and chunk == seq
        nb, groups, width = 32, GLA_TILE // seq, 4
    rows = nb * seq
    kern = functools.partial(_gla_kernel, ntile=rows // GLA_TILE, groups=groups, chained=chained, width=width)
    state_spec = pl.BlockSpec((nb, 1, GLA_DK, GLA_DV), lambda i, h: (i, h, 0, 0))
    (y, s_fin), made, _ = _call_with_casts(
        kern, casts, grid=(batch // nb, GLA_HEADS),
        in_specs=[pl.BlockSpec((rows, GLA_DK), lambda i, h: (i, COL_Q // GLA_DK + h)),
                  pl.BlockSpec((rows, GLA_DK), lambda i, h: (i, COL_K // GLA_DK + h)),
                  pl.BlockSpec((rows, GLA_DV), lambda i, h: (i, COL_V // GLA_DV + h)),
                  pl.BlockSpec((rows, GLA_DV), lambda i, h: (i, COL_R // GLA_DV + h)),
                  pl.BlockSpec((rows, LANES), lambda i, h: (i, 0)),
                  pl.BlockSpec((LANES, GLA_DK), lambda i, h: (0, h)),
                  pl.BlockSpec((1, GLA_DK), lambda i, h: (0, h)),
                  pl.BlockSpec((1, GLA_DV), lambda i, h: (0, h)),
                  state_spec],
        out_specs=[pl.BlockSpec((rows, GLA_DV), lambda i, h: (i, h)), state_spec],
        out_shape=[jax.ShapeDtypeStruct((batch * seq, GLA_HEADS * GLA_DV), BF16),
                   jax.ShapeDtypeStruct((batch, GLA_HEADS, GLA_DK, GLA_DV), F32)],
        operands=(z, z, z, z, alow, w_a2, b_a, g_norm, s0),
        scratch_shapes=[pltpu.VMEM((GLA_DK, GLA_DV), F32)],
        compiler_params=_params("parallel", "parallel"),
        name="gla")
    return y, s_fin, made


def _attn_kernel(q_ref, k_ref, v_ref, o_ref):
    nt_dims = (((1,), (1,)), ((), ()))
    cols = [slice(h * XA_HEAD_DIM, (h + 1) * XA_HEAD_DIM) for h in range(XA_HEADS)]
    scores = [lax.dot_general(q_ref[:, c].astype(BF16), k_ref[0, :, c].astype(BF16), nt_dims,
                              preferred_element_type=F32) * (XA_HEAD_DIM ** -0.5) for c in cols]
    probs = []
    for s in scores:
        p = jnp.exp(s - jnp.max(s, axis=-1, keepdims=True))
        probs.append((p / jnp.sum(p, axis=-1, keepdims=True)).astype(BF16))
    for c, p in zip(cols, probs):
        o_ref[:, c] = jnp.dot(p, v_ref[0, :, c].astype(BF16), preferred_element_type=F32).astype(o_ref.dtype)


def _attn_prompt(z, mem_k, mem_v, *, batch, seq, rc):
    nrc = seq // rc
    kv_spec = pl.BlockSpec((1, MEM_LEN, XA_WIDTH), lambda b, c: (b, 0, 0))
    return pl.pallas_call(
        _attn_kernel, grid=(batch, nrc),
        in_specs=[pl.BlockSpec((rc, XA_WIDTH), lambda b, c: (b * nrc + c, COL_QX // XA_WIDTH)), kv_spec, kv_spec],
        out_specs=pl.BlockSpec((rc, XA_WIDTH), lambda b, c: (b * nrc + c, 0)),
        out_shape=jax.ShapeDtypeStruct((batch * seq, XA_WIDTH), BF16),
        compiler_params=_params("parallel", "parallel"),
        name="attn_prompt",
    )(z, mem_k, mem_v)


XA_HALF = XA_HEAD_DIM // 2
XA_ROWS = 2 * XA_HEADS


def _attn_cache_body(q_ref, k_ref, v_ref, o_ref, *, nseq, seq, nblk, grid):
    nt_dims = (((1,), (1,)), ((), ()))

    def head_rows(ref, bi, h):
        halves = [ref[bi, pl.ds(half * XA_HEADS + h, MEM_LEN, stride=XA_ROWS), :] for half in range(2)]
        return jnp.concatenate(halves, axis=1).astype(BF16)

    @pl.when(pl.program_id(0) * grid[1] + pl.program_id(1) < nblk)
    def _():
        scores = []
        for bi in range(nseq):
            rows = slice(bi * seq, (bi + 1) * seq)
            for h in range(XA_HEADS):
                cols = slice(h * XA_HEAD_DIM, (h + 1) * XA_HEAD_DIM)
                s = lax.dot_general(q_ref[rows, cols].astype(BF16), head_rows(k_ref, bi, h), nt_dims,
                                    preferred_element_type=F32)
                scores.append(s * (XA_HEAD_DIM ** -0.5))
        probs = []
        for s in scores:
            p = jnp.exp(s - jnp.max(s, axis=-1, keepdims=True))
            probs.append((p / jnp.sum(p, axis=-1, keepdims=True)).astype(BF16))
        for h in range(XA_HEADS):
            out = [jnp.dot(probs[bi * XA_HEADS + h], head_rows(v_ref, bi, h), preferred_element_type=F32)
                   for bi in range(nseq)]
            o_ref[:, h * XA_HEAD_DIM:(h + 1) * XA_HEAD_DIM] = jnp.concatenate(out, axis=0).astype(o_ref.dtype)


def _cache_rows(cache):
    bs = cache.shape[0]
    c = cache.reshape(bs, MEM_LEN, XA_HEADS, 2, XA_HALF).transpose(0, 1, 3, 2, 4)
    return c.reshape(bs, MEM_LEN * XA_ROWS, XA_HALF)


def _attn_sample_side(z, mem_k, mem_v, *, batch, seq, grid, per_step=2):
    nblk = batch // per_step
    assert nblk <= grid[0] * grid[1] and (per_step * seq) % 16 == 0
    rows = per_step * seq

    def blk(*g):
        return jnp.minimum(g[0] * grid[1] + g[1], nblk - 1)

    kv_spec = pl.BlockSpec((per_step, MEM_LEN * XA_ROWS, XA_HALF), lambda *g: (blk(*g), 0, 0))
    return _SideJob(
        in_specs=[pl.BlockSpec((rows, XA_WIDTH), lambda *g: (blk(*g), COL_QX // XA_WIDTH)), kv_spec, kv_spec],
        out_specs=[pl.BlockSpec((rows, XA_WIDTH), lambda *g: (blk(*g), 0))],
        out_shape=[jax.ShapeDtypeStruct((batch * seq, XA_WIDTH), BF16)],
        operands=(z, mem_k, mem_v),
        body=functools.partial(_attn_cache_body, nseq=per_step, seq=seq, nblk=nblk, grid=grid))


def _merge_kernel(ys_ref, yg_ref, yx_ref, g0_ref, g1_ref, g2_ref, wglu_ref, bglu_ref,
                  w0_ref, w1_ref, w2_ref, o_ref, s5_scr):
    @pl.when(pl.program_id(1) == 0)
    def _():
        y = ys_ref[...]
        lin = jnp.dot(y.astype(BF16), wglu_ref[...], preferred_element_type=F32) + bglu_ref[...]
        s5_scr[...] = (y * jax.nn.sigmoid(lin)).astype(BF16)

    j = pl.program_id(1)
    m = jax.nn.sigmoid(g0_ref[...]) * jnp.dot(s5_scr[...], w0_ref[j], preferred_element_type=F32)
    m = m + jax.nn.sigmoid(g1_ref[...]) * jnp.dot(yg_ref[...], w1_ref[j], preferred_element_type=F32)
    m = m + jax.nn.sigmoid(g2_ref[...]) * jnp.dot(yx_ref[...], w2_ref[j], preferred_element_type=F32)
    o_ref[...] = m.astype(o_ref.dtype)


def _merge(y_s5, y_gla, y_x, z, w_glu, b_glu, w_br_s5, w_br_gla, w_br_x, *, tm, tn):
    t = y_s5.shape[0]
    nj = D_MODEL // tn
    once = pl.Buffered(1)
    wide = pl.BlockSpec((tm, S5_WIDTH), lambda i, j: (i, 0))
    gate = lambda b: pl.BlockSpec((tm, tn), lambda i, j: (i, (COL_GATE + b * D_MODEL) // tn + j))
    w_br = pl.BlockSpec((nj, S5_WIDTH, tn), lambda i, j: (0, 0, 0), pipeline_mode=once)
    return pl.pallas_call(
        _merge_kernel, grid=(t // tm, nj),
        in_specs=[wide, wide, wide, gate(0), gate(1), gate(2),
                  pl.BlockSpec((S5_WIDTH, S5_WIDTH), lambda i, j: (0, 0), pipeline_mode=once),
                  pl.BlockSpec((1, S5_WIDTH), lambda i, j: (0, 0)),
                  w_br, w_br, w_br],
        out_specs=pl.BlockSpec((tm, tn), lambda i, j: (i, j)),
        out_shape=jax.ShapeDtypeStruct((t, D_MODEL), BF16),
        scratch_shapes=[pltpu.VMEM((tm, S5_WIDTH), BF16)],
        compiler_params=_params("parallel", "arbitrary"),
        name="merge",
    )(y_s5, y_gla, y_x, z, z, z, w_glu, b_glu, w_br_s5, w_br_gla, w_br_x)


def _out_ffn_kernel(x_ref, m_ref, wo_ref, gf_ref, wg_ref, wu_ref, wd_ref, gl_ref, o_ref, h_scr, acc_scr):
    k = pl.program_id(1)

    @pl.when(k == 0)
    def _():
        acc_scr[...] = x_ref[...] + jnp.dot(m_ref[...], wo_ref[...], preferred_element_type=F32)
        h_scr[...] = _rms(acc_scr[...], gf_ref[...]).astype(BF16)

    h = h_scr[...]
    gate = jnp.dot(h, wg_ref[...], preferred_element_type=F32)
    up = jnp.dot(h, wu_ref[...], preferred_element_type=F32)
    act = (gate * jax.nn.sigmoid(gate) * up).astype(BF16)
    acc_scr[...] += jnp.dot(act, wd_ref[...], preferred_element_type=F32)

    @pl.when(k == pl.num_programs(1) - 1)
    def _():
        o_ref[...] = _rms(acc_scr[...], gl_ref[...])


def _out_ffn(x, merged, w_out, g_ffn, w_gate, w_up, w_down, g_final, *, tm, th):
    t = x.shape[0]
    row = pl.BlockSpec((tm, D_MODEL), lambda i, k: (i, 0))
    vec = pl.BlockSpec((1, D_MODEL), lambda i, k: (0, 0))
    w_in = pl.BlockSpec((D_MODEL, th), lambda i, k: (0, k))
    return pl.pallas_call(
        _out_ffn_kernel, grid=(t // tm, FFN_HIDDEN // th),
        in_specs=[row, row,
                  pl.BlockSpec((D_MODEL, D_MODEL), lambda i, k: (0, 0), pipeline_mode=pl.Buffered(1)),
                  vec, w_in, w_in, pl.BlockSpec((th, D_MODEL), lambda i, k: (k, 0)), vec],
        out_specs=row,
        out_shape=jax.ShapeDtypeStruct((t, D_MODEL), F32),
        scratch_shapes=[pltpu.VMEM((tm, D_MODEL), BF16), pltpu.VMEM((tm, D_MODEL), F32)],
        compiler_params=pltpu.CompilerParams(dimension_semantics=("parallel", "arbitrary"),
                                             vmem_limit_bytes=OUT_FFN_VMEM),
        name="out_ffn",
    )(x, merged, w_out, g_ffn, w_gate, w_up, w_down, g_final)


def _first_in_proj_kernel(x_ref, g_ref, wt_ref, wa_ref, o_ref, os_ref, main_ref, alow_ref, h_ref):
    @pl.when(pl.program_id(0) == 0)
    def _():
        h = _rms(x_ref[...], g_ref[...]).astype(BF16)
        h_ref[...] = h
        lane = lax.broadcasted_iota(jnp.int32, alow_ref.shape, 1)
        w_low = jnp.where(lane < GLA_RANK, wa_ref[...].T, 0.0).astype(BF16)
        alow_ref[...] = w_low
        os_ref[...] = jnp.dot(h, w_low, preferred_element_type=F32)

    w = wt_ref[...].T.astype(BF16)
    main_ref[...] = w
    o_ref[...] = jnp.dot(h_ref[...], w, preferred_element_type=F32)


def _first_in_proj(x, g, w_in):
    t, _ = x.shape
    d, n = w_in.shape
    wt = w_in.T
    tc = 512
    per_tile = IN_TN // tc
    n_first, n_gate = COL_GATE // tc, (COL_QX - COL_GATE) // tc
    gate_start, qx_start = 4096 + GLA_RANK + XA_WIDTH, 4096 + GLA_RANK

    def src_row(j):
        t8, g8, q8 = tc // SUBLANES, gate_start // SUBLANES, qx_start // SUBLANES
        r8 = jnp.where(j < n_first, j * t8,
                       jnp.where(j < n_first + n_gate, g8 + (j - n_first) * t8,
                                 q8 + (j - n_first - n_gate) * t8))
        return r8 * SUBLANES

    once = pl.Buffered(1)
    return pl.pallas_call(
        _first_in_proj_kernel, grid=(Z_WIDTH // tc,),
        in_specs=[pl.BlockSpec((pl.Element(t), pl.Element(d)), lambda j: (0, 0), pipeline_mode=once),
                  pl.BlockSpec((pl.Element(1), pl.Element(d)), lambda j: (0, 0)),
                  pl.BlockSpec((pl.Element(tc), pl.Element(d)), lambda j: (src_row(j), 0)),
                  pl.BlockSpec((pl.Element(LANES), pl.Element(d)), lambda j: (COL_GATE, 0), pipeline_mode=once)],
        out_specs=[pl.BlockSpec((t, tc), lambda j: (0, j)),
                   pl.BlockSpec((t, LANES), lambda j: (0, 0)),
                   pl.BlockSpec((None, d, tc), lambda j: (j // per_tile, 0, j % per_tile)),
                   pl.BlockSpec((d, LANES), lambda j: (0, 0))],
        out_shape=[jax.ShapeDtypeStruct((t, Z_WIDTH), F32), jax.ShapeDtypeStruct((t, LANES), F32),
                   jax.ShapeDtypeStruct((Z_WIDTH // IN_TN, d, IN_TN), BF16),
                   jax.ShapeDtypeStruct((d, LANES), BF16)],
        scratch_shapes=[pltpu.VMEM((t, d), BF16)],
        compiler_params=_params("arbitrary"),
        name="first_in_proj",
    )(x, g, wt, wt)


def _layer(x, z, alow, memory_attention, s5_re0, s5_im0, gla_s0, w, *, batch, seq, chained, gla_casts=None):
    w = dict(w)
    y_s5, hf_re, hf_im = _s5_branch(z, w['s5'], s5_re0, s5_im0, batch=batch, seq=seq)
    y_gla, gla_s, made = _gla(z, alow, w['gla_w_a2'], w['gla_b_a'], w['gla_norm'], gla_s0, gla_casts,
                              batch=batch, seq=seq, chained=chained)
    w.update(made)
    y_x = memory_attention(w)
    merged = _merge(y_s5, y_gla, y_x, z, w['s5_w_glu'], w['s5_b_glu'],
                    w['w_br_s5'], w['w_br_gla'], w['w_br_xattn'], tm=512, tn=MERGE_TN)
    y = _out_ffn(x, merged, w['w_out'], w['norm_ffn'], w['w_ffn_gate'], w['w_ffn_up'], w['w_ffn_down'],
                 w['norm_final'], tm=512, th=FFN_TH)
    return y, hf_re, hf_im, gla_s, w


def kernel(x_prompt, x_sample, mem_prompt, state_s5_re, state_s5_im, state_gla, cache_mem_k, cache_mem_v,
           norm_mix, w_in, s5_lam_re, s5_lam_im, s5_log_dt, s5_b_re, s5_b_im, s5_c_re, s5_c_im,
           s5_d, s5_w_glu, s5_b_glu, gla_w_a2, gla_b_a, gla_norm, mem_norm, w_mem_k, w_mem_v,
           w_br_s5, w_br_gla, w_br_xattn, w_out, norm_ffn, w_ffn_gate, w_ffn_up, w_ffn_down, norm_final):
    depth = w_in.shape[0]
    assert depth == 1
    bp, sp, d = x_prompt.shape
    bs, ss, _ = x_sample.shape
    n_state = S5_GROUPS * S5_STATE
    row = lambda v: v.reshape(1, -1)

    l = 0
    s5_w = _s5_params(s5_lam_re[l], s5_lam_im[l], s5_log_dt[l], s5_b_re[l], s5_b_im[l],
                      s5_c_re[l], s5_c_im[l], s5_d[l])
    w = {
        'norm_mix': row(norm_mix[l]),
        's5': s5_w,
        's5_b_glu': row(s5_b_glu[l]),
        'gla_w_a2': jnp.pad(gla_w_a2[l], ((0, LANES - GLA_RANK), (0, 0))).astype(BF16),
        'gla_b_a': row(gla_b_a[l]), 'gla_norm': row(gla_norm[l]),
        'norm_ffn': row(norm_ffn[l]), 'norm_final': row(norm_final),
    }
    ffn_casts = {'w_ffn_gate': _CastJob(w_ffn_gate[l], row_axis=0), 'w_ffn_up': _CastJob(w_ffn_up[l], row_axis=0),
                 'w_ffn_down': _CastJob(w_ffn_down[l], row_axis=1)}
    mix_casts = {'w_out': _CastJob(w_out[l], row_axis=0), 's5_w_glu': _CastJob(s5_w_glu[l], row_axis=0),
                 'w_br_s5': _CastJob(w_br_s5[l], row_axis=0, col_tile=MERGE_TN),
                 'w_br_gla': _CastJob(w_br_gla[l], row_axis=0, col_tile=MERGE_TN),
                 'w_br_xattn': _CastJob(w_br_xattn[l], row_axis=0, col_tile=MERGE_TN),
                 'w_mem_k': _CastJob(w_mem_k[l], row_axis=0, col_tile=IN_TN),
                 'w_mem_v': _CastJob(w_mem_v[l], row_axis=0, col_tile=IN_TN)}

    zero_s5 = jnp.zeros((bp, n_state), F32)
    zero_gla = jnp.zeros((bp, GLA_HEADS, GLA_DK, GLA_DV), F32)
    xp, xs = x_prompt.reshape(bp * sp, d), x_sample.reshape(bs * ss, d)

    z_s, alow_s, w['w_main'], w['w_alow'] = _first_in_proj(xs, w['norm_mix'], w_in[l])
    cache_k, cache_v = _cache_rows(cache_mem_k[l]), _cache_rows(cache_mem_v[l])
    z_p, alow_p, made, (yx_s,) = _norm_matmul(
        xp, w['norm_mix'], w['w_main'], w['w_alow'], None,
        lambda grid: _attn_sample_side(z_s, cache_k, cache_v, batch=bs, seq=ss, grid=grid), tm=1024)
    w.update(made)

    mem_kv = {}

    def prompt_attention(w):
        mem = mem_prompt.reshape(bp * MEM_LEN, d)
        for name in ('w_mem_k', 'w_mem_v'):
            mem_kv[name] = _norm_matmul(mem, row(mem_norm[l]), w[name], tm=512).reshape(bp, MEM_LEN, XA_WIDTH)
        return _attn_prompt(z_p, mem_kv['w_mem_k'], mem_kv['w_mem_v'], batch=bp, seq=sp, rc=1024)

    yp, p_re, p_im, p_gla, w = _layer(xp, z_p, alow_p, prompt_attention, zero_s5, zero_s5, zero_gla, w,
                                      batch=bp, seq=sp, chained=True, gla_casts={**mix_casts, **ffn_casts})
    mk, mv = mem_kv['w_mem_k'], mem_kv['w_mem_v']

    ys, s_re, s_im, s_gla, _ = _layer(xs, z_s, alow_s, lambda w: yx_s,
                                      state_s5_re[l].reshape(bs, n_state), state_s5_im[l].reshape(bs, n_state),
                                      state_gla[l], w, batch=bs, seq=ss, chained=False)

    s5_shape_p = (1, bp, S5_GROUPS, S5_STATE)
    s5_shape_s = (1, bs, S5_GROUPS, S5_STATE)
    kv_shape = (1, bp, MEM_LEN, XA_HEADS, XA_HEAD_DIM)
    return (yp.reshape(bp, sp, d), ys.reshape(bs, ss, d),
            p_re.reshape(s5_shape_p), p_im.reshape(s5_shape_p), p_gla[None],
            mk.reshape(kv_shape), mv.reshape(kv_shape),
            s_re.reshape(s5_shape_s), s_im.reshape(s5_shape_s), s_gla[None])
```
